```python
import jax, jax.numpy as jnp
from jax import lax
import numpy as np

D_MODEL = 2048
BATCH = 4
SEQ = 4096
DEPTH = 4

GRID_W = 64
CTX_LEN = 256
N_MOD = 9
D_FF = ((8 * D_MODEL // 3 + 127) // 128) * 128
MLA_HEADS = 4
MLA_Q_LORA = 448
MLA_KV_LORA = 128
MLA_NOPE_DIM = 128
MLA_ROPE_DIM = 64
MLA_QK_DIM = MLA_NOPE_DIM + MLA_ROPE_DIM
MLA_V_DIM = 128
GQA_HEADS = 4
GQA_KV_HEADS = 2
GQA_GROUP = GQA_HEADS // GQA_KV_HEADS
GQA_HEAD_DIM = 128
CONV_WIDTH = 512
CONV_K = 3
FOURIER_GROUPS = 4
FOURIER_GROUP_DIM = 128
FOURIER_WIDTH = FOURIER_GROUPS * FOURIER_GROUP_DIM
BRANCH_WIDTH = 512
N_BRANCHES = 4
IN_SPLITS = (MLA_Q_LORA, MLA_KV_LORA, MLA_ROPE_DIM,
             GQA_HEADS * GQA_HEAD_DIM, GQA_KV_HEADS * GQA_HEAD_DIM, GQA_KV_HEADS * GQA_HEAD_DIM,
             CONV_WIDTH, CONV_WIDTH, CONV_WIDTH, FOURIER_WIDTH)
D_IN = sum(IN_SPLITS)
Q_BLOCK = 128
ROPE_THETA = 10000.0
NORM_EPS = 1e-6

kernel_name = 'hybrid_mla_gqa_conv_fourier_dit_block'


def rms_norm(x, g):
    xf = x.astype(jnp.float32)
    y = xf * lax.rsqrt(jnp.mean(xf * xf, axis=-1, keepdims=True) + NORM_EPS)
    return (y * g.astype(jnp.float32)).astype(x.dtype)


def modulate(x, shift, scale):
    return x * (1 + scale) + shift


def ada_modulation(cond, w_ada, b_ada):
    m = jax.nn.silu(cond) @ w_ada + b_ada
    return m.reshape(cond.shape[0], 1, N_MOD, -1)


def swiglu(xn, wi, wo):
    g, u = jnp.split(xn @ wi, 2, axis=-1)
    return (jax.nn.silu(g) * u) @ wo


def axial_rope_tables(pos_row, pos_col, rot_dim):
    n = rot_dim // 4
    inv_freq = ROPE_THETA ** (-jnp.arange(n, dtype=jnp.float32) / n)
    ang = jnp.concatenate([pos_row[:, None] * inv_freq, pos_col[:, None] * inv_freq], axis=-1)
    return jnp.cos(ang), jnp.sin(ang)


def apply_rope(x, cos, sin):
    half = x.shape[-1] // 2
    x1, x2 = x[..., :half], x[..., half:]
    c = cos[:, None, :].astype(x.dtype)
    s = sin[:, None, :].astype(x.dtype)
    return jnp.concatenate([x1 * c - x2 * s, x2 * c + x1 * s], axis=-1)


def split_cols(z):
    idx = [int(i) for i in np.cumsum(IN_SPLITS)[:-1]]
    return jnp.split(z, idx, axis=-1)


def mla_project(zq, zkv, zpe, g_cq, w_uq, g_ckv, w_ukv, g_q, g_k, rope):
    B_, T = zq.shape[:2]
    q = (rms_norm(zq, g_cq) @ w_uq).reshape(B_, T, MLA_HEADS, MLA_QK_DIM)
    kv = (rms_norm(zkv, g_ckv) @ w_ukv).reshape(B_, T, MLA_HEADS, MLA_NOPE_DIM + MLA_V_DIM)
    k_nope, v = kv[..., :MLA_NOPE_DIM], kv[..., MLA_NOPE_DIM:]
    k_pe = jnp.broadcast_to(zpe[:, :, None, :], (B_, T, MLA_HEADS, MLA_ROPE_DIM))
    k = jnp.concatenate([k_nope, k_pe], axis=-1)
    q = rms_norm(q, g_q)
    k = rms_norm(k, g_k)
    if rope is not None:
        q = jnp.concatenate([q[..., :MLA_NOPE_DIM], apply_rope(q[..., MLA_NOPE_DIM:], *rope)], axis=-1)
        k = jnp.concatenate([k[..., :MLA_NOPE_DIM], apply_rope(k[..., MLA_NOPE_DIM:], *rope)], axis=-1)
    return q[:, :, :, None, :], k, v


def gqa_project(zq, zk, zv, g_q, g_k, rope):
    B_, T = zq.shape[:2]
    q = rms_norm(zq.reshape(B_, T, GQA_HEADS, GQA_HEAD_DIM), g_q)
    k = rms_norm(zk.reshape(B_, T, GQA_KV_HEADS, GQA_HEAD_DIM), g_k)
    v = zv.reshape(B_, T, GQA_KV_HEADS, GQA_HEAD_DIM)
    if rope is not None:
        q = apply_rope(q, *rope)
        k = apply_rope(k, *rope)
    return q.reshape(B_, T, GQA_KV_HEADS, GQA_GROUP, GQA_HEAD_DIM), k, v


def joint_attention(q, k, v, k_ctx, v_ctx, scale):
    B_, S, KH, G, Dk = q.shape
    n_blocks = S // Q_BLOCK
    qb = q.reshape(B_, n_blocks, Q_BLOCK, KH, G, Dk).swapaxes(0, 1)

    def attend_block(q_blk):
        s = jnp.concatenate([jnp.einsum('bqhgd,bkhd->bhgqk', q_blk, k),
                             jnp.einsum('bqhgd,bkhd->bhgqk', q_blk, k_ctx)], axis=-1)
        p = jax.nn.softmax(s.astype(jnp.float32) * scale, axis=-1).astype(v.dtype)
        return (jnp.einsum('bhgqk,bkhd->bqhgd', p[..., :S], v)
                + jnp.einsum('bhgqk,bkhd->bqhgd', p[..., S:], v_ctx))

    o = lax.map(attend_block, qb)
    return o.swapaxes(0, 1).reshape(B_, S, -1)


def context_attention(q, k, v, scale):
    B_, T = q.shape[:2]
    s = jnp.einsum('bqhgd,bkhd->bhgqk', q, k)
    p = jax.nn.softmax(s.astype(jnp.float32) * scale, axis=-1).astype(v.dtype)
    return jnp.einsum('bhgqk,bkhd->bqhgd', p, v).reshape(B_, T, -1)


def short_conv_mix(zb, zc, zx, w, b):
    u = zc * zx
    up = jnp.pad(u, ((0, 0), (1, 1), (0, 0)))
    y = up[:, :-2] * w[0] + up[:, 1:-1] * w[1] + up[:, 2:] * w[2] + b
    return zb * y


def fourier_mix(zf):
    B_, T = zf.shape[:2]
    f = zf.reshape(B_, T, FOURIER_GROUPS, FOURIER_GROUP_DIM).astype(jnp.float32)
    spec = jnp.fft.fft2(f, axes=(1, 3), norm='ortho').real
    return spec.reshape(B_, T, FOURIER_WIDTH).astype(zf.dtype)


def merge_branches(xn, branches, w_branch, w_gate, b_gate, w_o):
    merged = jax.nn.sigmoid(xn @ w_gate[0] + b_gate[0]) * (branches[0] @ w_branch[0])
    for i in range(1, N_BRANCHES):
        merged = merged + jax.nn.sigmoid(xn @ w_gate[i] + b_gate[i]) * (branches[i] @ w_branch[i])
    return merged @ w_o


def setup_inputs(seed: int = 0) -> dict:
    key = jax.random.key(seed)
    ks = jax.random.split(key, 32)
    L, D, F = DEPTH, D_MODEL, D_FF

    def nrm(k, shape, scale):
        return jax.random.normal(k, shape, jnp.float32) * scale

    def gain(k, shape):
        return 1.0 + 0.01 * jax.random.normal(k, shape, jnp.float32)

    return {
        'x': nrm(ks[0], (BATCH, SEQ, D), 1.0),
        'c': nrm(ks[1], (BATCH, D), 1.0),
        'ctx': nrm(ks[2], (BATCH, CTX_LEN, D), 1.0),
        'c_ctx': nrm(ks[3], (D,), 1.0),
        'w_ada': nrm(ks[4], (L, D, N_MOD * D), 0.5 * D ** -0.5),
        'b_ada': nrm(ks[5], (L, N_MOD * D), 0.01),
        'norm_ffn1': gain(ks[6], (L, D)),
        'ffn1_wi': nrm(ks[7], (L, D, 2 * F), D ** -0.5),
        'ffn1_wo': nrm(ks[8], (L, F, D), F ** -0.5),
        'norm_mix': gain(ks[9], (L, D)),
        'w_in': nrm(ks[10], (L, D, D_IN), D ** -0.5),
        'g_cq': gain(ks[11], (L, MLA_Q_LORA)),
        'w_uq': nrm(ks[12], (L, MLA_Q_LORA, MLA_HEADS * MLA_QK_DIM), MLA_Q_LORA ** -0.5),
        'g_ckv': gain(ks[13], (L, MLA_KV_LORA)),
        'w_ukv': nrm(ks[14], (L, MLA_KV_LORA, MLA_HEADS * (MLA_NOPE_DIM + MLA_V_DIM)), MLA_KV_LORA ** -0.5),
        'g_qa': gain(ks[15], (L, MLA_QK_DIM)),
        'g_ka': gain(ks[16], (L, MLA_QK_DIM)),
        'g_qb': gain(ks[17], (L, GQA_HEAD_DIM)),
        'g_kb': gain(ks[18], (L, GQA_HEAD_DIM)),
        'conv_w': nrm(ks[19], (L, CONV_K, CONV_WIDTH), CONV_K ** -0.5),
        'conv_b': nrm(ks[20], (L, CONV_WIDTH), 0.01),
        'w_branch': nrm(ks[21], (L, N_BRANCHES, BRANCH_WIDTH, D), BRANCH_WIDTH ** -0.5),
        'w_gate': nrm(ks[22], (L, N_BRANCHES, D, D), D ** -0.5),
        'b_gate': nrm(ks[23], (L, N_BRANCHES, D), 0.01),
        'w_o': nrm(ks[24], (L, D, D), D ** -0.5),
        'norm_ffn2': gain(ks[25], (L, D)),
        'ffn2_wi': nrm(ks[26], (L, D, 2 * F), D ** -0.5),
        'ffn2_wo': nrm(ks[27], (L, F, D), F ** -0.5),
    }


def reference(x, c, ctx, c_ctx, w_ada, b_ada, norm_ffn1, ffn1_wi, ffn1_wo, norm_mix, w_in,
              g_cq, w_uq, g_ckv, w_ukv, g_qa, g_ka, g_qb, g_kb, conv_w, conv_b,
              w_branch, w_gate, b_gate, w_o, norm_ffn2, ffn2_wi, ffn2_wo):
    S = x.shape[1]
    ROWS = S // GRID_W
    pos_row = jnp.broadcast_to(jnp.arange(ROWS, dtype=jnp.float32)[:, None], (ROWS, GRID_W)).reshape(-1)
    pos_col = jnp.broadcast_to(jnp.arange(GRID_W, dtype=jnp.float32)[None, :], (ROWS, GRID_W)).reshape(-1)
    rope_a = axial_rope_tables(pos_row, pos_col, MLA_ROPE_DIM)
    rope_b = axial_rope_tables(pos_row, pos_col, GQA_HEAD_DIM)
    scale_a = MLA_QK_DIM ** -0.5
    scale_b = GQA_HEAD_DIM ** -0.5

    h, hc = x, ctx
    for l in range(DEPTH):
        last = l == DEPTH - 1
        m = ada_modulation(c, w_ada[l], b_ada[l])
        mc = ada_modulation(c_ctx[None, :], w_ada[l], b_ada[l])

        h = h + 0.5 * m[:, :, 2] * swiglu(
            modulate(rms_norm(h, norm_ffn1[l]), m[:, :, 0], m[:, :, 1]), ffn1_wi[l], ffn1_wo[l])
        hc = hc + 0.5 * mc[:, :, 2] * swiglu(
            modulate(rms_norm(hc, norm_ffn1[l]), mc[:, :, 0], mc[:, :, 1]), ffn1_wi[l], ffn1_wo[l])

        xn = modulate(rms_norm(h, norm_mix[l]), m[:, :, 3], m[:, :, 4])
        xnc = modulate(rms_norm(hc, norm_mix[l]), mc[:, :, 3], mc[:, :, 4])
        aq, akv, ape, bq, bk, bv, cb, cg, cx, fz = split_cols(xn @ w_in[l])
        aqc, akvc, apec, bqc, bkc, bvc, cbc, cgc, cxc, fzc = split_cols(xnc @ w_in[l])

        qa, ka, va = mla_project(aq, akv, ape, g_cq[l], w_uq[l], g_ckv[l], w_ukv[l], g_qa[l], g_ka[l], rope_a)
        qac, kac, vac = mla_project(aqc, akvc, apec, g_cq[l], w_uq[l], g_ckv[l], w_ukv[l], g_qa[l], g_ka[l], None)
        qb, kb, vb = gqa_project(bq, bk, bv, g_qb[l], g_kb[l], rope_b)
        qbc, kbc, vbc = gqa_project(bqc, bkc, bvc, g_qb[l], g_kb[l], None)

        branches = [joint_attention(qa, ka, va, kac, vac, scale_a),
                    joint_attention(qb, kb, vb, kbc, vbc, scale_b),
                    short_conv_mix(cb, cg, cx, conv_w[l], conv_b[l]),
                    fourier_mix(fz)]
        h = h + m[:, :, 5] * merge_branches(xn, branches, w_branch[l], w_gate[l], b_gate[l], w_o[l])
        if not last:
            branches_c = [context_attention(qac, kac, vac, scale_a),
                          context_attention(qbc, kbc, vbc, scale_b),
                          short_conv_mix(cbc, cgc, cxc, conv_w[l], conv_b[l]),
                          fourier_mix(fzc)]
            hc = hc + mc[:, :, 5] * merge_branches(xnc, branches_c, w_branch[l], w_gate[l], b_gate[l], w_o[l])

        h = h + 0.5 * m[:, :, 8] * swiglu(
            modulate(rms_norm(h, norm_ffn2[l]), m[:, :, 6], m[:, :, 7]), ffn2_wi[l], ffn2_wo[l])
        if not last:
            hc = hc + 0.5 * mc[:, :, 8] * swiglu(
                modulate(rms_norm(hc, norm_ffn2[l]), mc[:, :, 6], mc[:, :, 7]), ffn2_wi[l], ffn2_wo[l])
    return h
```

```python
import functools
import math

import jax
import jax.numpy as jnp
from jax import lax
from jax.experimental import pallas as pl
from jax.experimental.pallas import tpu as pltpu

F32 = jnp.float32
BF16 = jnp.bfloat16

D = 2048
BATCH = 4
SEQ = 4096
DEPTH = 4
GRID_W = 64
CTX = 256
N_MOD = 9
F_FF = 5504
F_PAD = 5632
HEADS = 4
Q_LORA = 448
Q_LORA_PAD = 512
KV_LORA = 128
NOPE = 128
ROPE_A = 64
QK_A = NOPE + ROPE_A
QK_A_PAD = 256
V_A = 128
HD_B = 128
KVH_B = 2
GROUP_B = 2
CONV_W = 512
FG = 4
FGD = 128
BR_W = 512
N_BR = 4
THETA = 10000.0
EPS = 1e-6

T_LAT = BATCH * SEQ
T_CTX = BATCH * CTX
T_ALL = T_LAT + T_CTX

W_IN_PAD = 3840
OFF_A = 0
OFF_B = 768
OFF_C = 1792
OFF_F = 3328

VMEM_PHYS_V7X = 64 * 1024 * 1024
VMEM_LIMIT = 60 * 1024 * 1024

TM_FFN = 512
TF_FFN = 512
TM_PROJ = 512
TM_MERGE = 512
TN_MERGE = 512
TQ_ATT = 512
TR_DFT = 512
TN_ADA = 1024


def _cparams(sem):
    return pltpu.CompilerParams(dimension_semantics=sem, vmem_limit_bytes=VMEM_LIMIT)


def _mod_row(i, tiles_per_batch):
    return jnp.minimum(i // tiles_per_batch, BATCH)


def _norm_mod(x, gain, shift, scale):
    ms = jnp.mean(x * x, axis=-1, keepdims=True)
    y = x * lax.rsqrt(ms + EPS) * gain
    return y * (1.0 + scale) + shift


def _silu(x):
    return x * jax.nn.sigmoid(x)


def _ada_kernel(cond_ref, w_ref, b_ref, o_ref):
    s = _silu(cond_ref[...]).astype(BF16)
    o_ref[...] = jnp.dot(s, w_ref[...].astype(BF16), preferred_element_type=F32) + b_ref[...]


def _ada(cond8, w_ada, b_ada):
    n = N_MOD * D
    return pl.pallas_call(
        _ada_kernel,
        grid=(DEPTH, n // TN_ADA),
        in_specs=[
            pl.BlockSpec((8, D), lambda l, j: (0, 0)),
            pl.BlockSpec((None, D, TN_ADA), lambda l, j: (l, 0, j)),
            pl.BlockSpec((None, 1, TN_ADA), lambda l, j: (l, 0, j)),
        ],
        out_specs=pl.BlockSpec((None, 8, TN_ADA), lambda l, j: (l, 0, j)),
        out_shape=jax.ShapeDtypeStruct((DEPTH, 8, n), F32),
        compiler_params=_cparams(("arbitrary", "arbitrary")),
        name="ada",
    )(cond8, w_ada, b_ada.reshape(DEPTH, 1, n))


def _mod_spec(layer, j, tiles_per_batch):
    return pl.BlockSpec((None, None, None, 1, D),
                        lambda i, *_: (layer, _mod_row(i, tiles_per_batch), j, 0, 0))


def _ffn_kernel(x_ref, gain_ref, sh_ref, sc_ref, gt_ref, wg_ref, wu_ref, wo_ref, o_ref, xn_ref):
    f = pl.program_id(1)

    @pl.when(f == 0)
    def _():
        xn_ref[...] = _norm_mod(x_ref[...], gain_ref[...], sh_ref[...], sc_ref[...]).astype(BF16)
        o_ref[...] = jnp.zeros_like(o_ref)

    xn = xn_ref[...]
    g = jnp.dot(xn, wg_ref[...], preferred_element_type=F32)
    u = jnp.dot(xn, wu_ref[...], preferred_element_type=F32)
    a = (_silu(g) * u).astype(BF16)
    o_ref[...] += jnp.dot(a, wo_ref[...], preferred_element_type=F32)

    @pl.when(f == pl.num_programs(1) - 1)
    def _():
        o_ref[...] = x_ref[...] + (0.5 * gt_ref[...]) * o_ref[...]


def _ffn(h, mod, layer, j0, gain, wi2, wo, n_rows):
    tm, tf = TM_FFN, TF_FFN
    tpb = SEQ // tm
    return pl.pallas_call(
        _ffn_kernel,
        grid=(n_rows // tm, F_PAD // tf),
        in_specs=[
            pl.BlockSpec((tm, D), lambda i, f: (i, 0)),
            pl.BlockSpec((1, D), lambda i, f: (0, 0)),
            _mod_spec(layer, j0, tpb),
            _mod_spec(layer, j0 + 1, tpb),
            _mod_spec(layer, j0 + 2, tpb),
            pl.BlockSpec((None, D, tf), lambda i, f: (0, 0, f)),
            pl.BlockSpec((None, D, tf), lambda i, f: (1, 0, f)),
            pl.BlockSpec((tf, D), lambda i, f: (f, 0)),
        ],
        out_specs=pl.BlockSpec((tm, D), lambda i, f: (i, 0)),
        out_shape=jax.ShapeDtypeStruct((n_rows, D), F32),
        scratch_shapes=[pltpu.VMEM((tm, D), BF16)],
        compiler_params=_cparams(("arbitrary", "arbitrary")),
        name="ffn",
    )(h, gain, mod, mod, mod, wi2, wi2, wo)


def _rope_a(v, c, s1, s2):
    return v * c + pltpu.roll(v, 96, 1) * s1 + pltpu.roll(v, 32, 1) * s2


def _rope_b(v, c, s):
    return v * c + pltpu.roll(v, 64, 1) * s


def _proj_kernel(h_ref, gain_ref, sh_ref, sc_ref, w_ref, gcq_ref, wuq_ref, gckv_ref, wukv_ref,
                 gqa_ref, gkan_ref, gkap_ref, gqb_ref, gkb_ref, ca_ref, s1a_ref, s2a_ref,
                 cb_ref, sb_ref, ccs_ref,
                 qa_ref, ka_ref, va_ref, qb_ref, kb_ref, vb_ref, cbo_ref, u_ref, pc_ref, ps_ref):
    xn = _norm_mod(h_ref[...], gain_ref[...], sh_ref[...], sc_ref[...]).astype(BF16)
    ca, s1a, s2a = ca_ref[...], s1a_ref[...], s2a_ref[...]
    cb, sb = cb_ref[...], sb_ref[...]
    scale_a = QK_A ** -0.5
    scale_b = HD_B ** -0.5

    za = jnp.dot(xn, w_ref[:, OFF_A:OFF_B], preferred_element_type=F32)
    aq, akv, ape = za[:, :Q_LORA_PAD], za[:, Q_LORA_PAD:Q_LORA_PAD + KV_LORA], za[:, Q_LORA_PAD + KV_LORA:]
    r = lax.rsqrt(jnp.sum(aq * aq, axis=-1, keepdims=True) * (1.0 / Q_LORA) + EPS)
    cq = (aq * r * gcq_ref[...]).astype(BF16)
    q = jnp.dot(cq, wuq_ref[...], preferred_element_type=F32)
    gqa = gqa_ref[...]
    for hd in range(HEADS):
        lo = hd * QK_A_PAD
        qh = q[:, lo:lo + QK_A_PAD]
        r = lax.rsqrt(jnp.sum(qh * qh, axis=-1, keepdims=True) * (1.0 / QK_A) + EPS)
        qh = qh * r * gqa
        qa_ref[:, lo:lo + NOPE] = (qh[:, :NOPE] * scale_a).astype(BF16)
        qa_ref[:, lo + NOPE:lo + QK_A_PAD] = (_rope_a(qh[:, NOPE:], ca, s1a, s2a) * scale_a).astype(BF16)
    r = lax.rsqrt(jnp.mean(akv * akv, axis=-1, keepdims=True) + EPS)
    ckv = (akv * r * gckv_ref[...]).astype(BF16)
    kv = jnp.dot(ckv, wukv_ref[...], preferred_element_type=F32)
    ss_pe = jnp.sum(ape * ape, axis=-1, keepdims=True)
    gkan, gkap = gkan_ref[...], gkap_ref[...]
    for hd in range(HEADS):
        kn = kv[:, hd * NOPE:(hd + 1) * NOPE]
        r = lax.rsqrt((jnp.sum(kn * kn, axis=-1, keepdims=True) + ss_pe) * (1.0 / QK_A) + EPS)
        lo = hd * QK_A_PAD
        ka_ref[:, lo:lo + NOPE] = (kn * r * gkan).astype(BF16)
        ka_ref[:, lo + NOPE:lo + QK_A_PAD] = _rope_a(ape * r * gkap, ca, s1a, s2a).astype(BF16)
        va_ref[:, hd * V_A:(hd + 1) * V_A] = kv[:, HEADS * NOPE + hd * V_A:HEADS * NOPE + (hd + 1) * V_A].astype(BF16)

    zb = jnp.dot(xn, w_ref[:, OFF_B:OFF_C], preferred_element_type=F32)
    gqb, gkb = gqb_ref[...], gkb_ref[...]
    for hd in range(HEADS):
        qh = zb[:, hd * HD_B:(hd + 1) * HD_B]
        r = lax.rsqrt(jnp.mean(qh * qh, axis=-1, keepdims=True) + EPS)
        qb_ref[:, hd * HD_B:(hd + 1) * HD_B] = (_rope_b(qh * r * gqb, cb, sb) * scale_b).astype(BF16)
    for hd in range(KVH_B):
        lo = HEADS * HD_B + hd * HD_B
        kh = zb[:, lo:lo + HD_B]
        r = lax.rsqrt(jnp.mean(kh * kh, axis=-1, keepdims=True) + EPS)
        kb_ref[:, hd * HD_B:(hd + 1) * HD_B] = _rope_b(kh * r * gkb, cb, sb).astype(BF16)
    vb_ref[...] = zb[:, (HEADS + KVH_B) * HD_B:].astype(BF16)

    zc = jnp.dot(xn, w_ref[:, OFF_C:OFF_F], preferred_element_type=F32)
    cbo_ref[...] = zc[:, :CONV_W]
    u_ref[...] = zc[:, CONV_W:2 * CONV_W] * zc[:, 2 * CONV_W:]

    zf = jnp.dot(xn, w_ref[:, OFF_F:], preferred_element_type=F32).astype(BF16)
    ccs = ccs_ref[...]
    for g in range(FG):
        pp = jnp.dot(zf[:, g * FGD:(g + 1) * FGD], ccs, preferred_element_type=F32)
        pc_ref[:, g * FGD:(g + 1) * FGD] = pp[:, :FGD].astype(BF16)
        ps_ref[:, g * FGD:(g + 1) * FGD] = pp[:, FGD:].astype(BF16)


def _const_spec(shape):
    nd = len(shape)
    return pl.BlockSpec(shape, lambda i, *_: (0,) * nd, pipeline_mode=pl.Buffered(1))


def _proj(h, mod, layer, gain, lw, tabs, ccs):
    tm = TM_PROJ
    tpb = SEQ // tm
    n_lat = T_LAT // tm

    def tab_spec():
        return pl.BlockSpec((tm, 128), lambda i: (jnp.where(i < n_lat, i % tpb, tpb), 0))

    def row_spec(w):
        return pl.BlockSpec((tm, w), lambda i: (i, 0))

    widths = (HEADS * QK_A_PAD, HEADS * QK_A_PAD, HEADS * V_A, HEADS * HD_B, KVH_B * HD_B,
              KVH_B * HD_B, CONV_W, CONV_W, FG * FGD, FG * FGD)
    dtypes = (BF16, BF16, BF16, BF16, BF16, BF16, F32, F32, BF16, BF16)
    return pl.pallas_call(
        _proj_kernel,
        grid=(T_ALL // tm,),
        in_specs=[
            row_spec(D),
            _const_spec((1, D)),
            _mod_spec(layer, 3, tpb),
            _mod_spec(layer, 4, tpb),
            _const_spec((D, W_IN_PAD)),
            _const_spec((1, Q_LORA_PAD)),
            _const_spec((Q_LORA_PAD, HEADS * QK_A_PAD)),
            _const_spec((1, KV_LORA)),
            _const_spec((KV_LORA, HEADS * (NOPE + V_A))),
            _const_spec((1, QK_A_PAD)),
            _const_spec((1, NOPE)),
            _const_spec((1, 128)),
            _const_spec((1, HD_B)),
            _const_spec((1, HD_B)),
            tab_spec(), tab_spec(), tab_spec(), tab_spec(), tab_spec(),
            _const_spec((FGD, 2 * FGD)),
        ],
        out_specs=[row_spec(w) for w in widths],
        out_shape=[jax.ShapeDtypeStruct((T_ALL, w), dt) for w, dt in zip(widths, dtypes)],
        compiler_params=_cparams(("arbitrary",)),
        name="proj",
    )(h, gain, mod, mod, lw["w_in"], lw["g_cq"], lw["w_uq"], lw["g_ckv"], lw["w_ukv"],
      lw["g_qa"], lw["g_ka_nope"], lw["g_ka_pe"], lw["g_qb"], lw["g_kb"], *tabs, ccs)


def _att_lat_kernel(q_ref, k_ref, kc_ref, v_ref, vc_ref, o_ref):
    q = q_ref[...]
    dn = (((1,), (1,)), ((), ()))
    s = lax.dot_general(q, k_ref[...], dn, preferred_element_type=F32)
    sc = lax.dot_general(q, kc_ref[...], dn, preferred_element_type=F32)
    m = jnp.maximum(jnp.max(s, axis=-1, keepdims=True), jnp.max(sc, axis=-1, keepdims=True))
    p = jnp.exp(s - m)
    pc = jnp.exp(sc - m)
    l = jnp.sum(p, axis=-1, keepdims=True) + jnp.sum(pc, axis=-1, keepdims=True)
    o = (jnp.dot(p.astype(BF16), v_ref[...], preferred_element_type=F32)
         + jnp.dot(pc.astype(BF16), vc_ref[...], preferred_element_type=F32))
    o_ref[...] = (o / l).astype(o_ref.dtype)


def _att_ctx_kernel(alias_ref, q_ref, k_ref, v_ref, o_ref):
    del alias_ref
    s = lax.dot_general(q_ref[...], k_ref[...], (((1,), (1,)), ((), ())), preferred_element_type=F32)
    m = jnp.max(s, axis=-1, keepdims=True)
    p = jnp.exp(s - m)
    l = jnp.sum(p, axis=-1, keepdims=True)
    o = jnp.dot(p.astype(BF16), v_ref[...], preferred_element_type=F32)
    o_ref[...] = (o / l).astype(o_ref.dtype)


def _attention(q, k, v, dq, dv, group, with_ctx):
    tq = TQ_ATT
    nq = SEQ // tq
    ctx0 = T_LAT // CTX
    out = pl.pallas_call(
        _att_lat_kernel,
        grid=(BATCH, HEADS, nq),
        in_specs=[
            pl.BlockSpec((tq, dq), lambda b, h, i: (b * nq + i, h)),
            pl.BlockSpec((SEQ, dq), lambda b, h, i: (b, h // group)),
            pl.BlockSpec((CTX, dq), lambda b, h, i: (ctx0 + b, h // group)),
            pl.BlockSpec((SEQ, dv), lambda b, h, i: (b, h // group)),
            pl.BlockSpec((CTX, dv), lambda b, h, i: (ctx0 + b, h // group)),
        ],
        out_specs=pl.BlockSpec((tq, dv), lambda b, h, i: (b * nq + i, h)),
        out_shape=jax.ShapeDtypeStruct((T_ALL, HEADS * dv), BF16),
        compiler_params=_cparams(("arbitrary", "arbitrary", "arbitrary")),
        name="att_lat",
    )(q, k, k, v, v)
    if not with_ctx:
        return out
    return pl.pallas_call(
        _att_ctx_kernel,
        grid=(BATCH, HEADS),
        in_specs=[
            pl.BlockSpec(memory_space=pl.ANY),
            pl.BlockSpec((CTX, dq), lambda b, h: (ctx0 + b, h)),
            pl.BlockSpec((CTX, dq), lambda b, h: (ctx0 + b, h // group)),
            pl.BlockSpec((CTX, dv), lambda b, h: (ctx0 + b, h // group)),
        ],
        out_specs=pl.BlockSpec((CTX, dv), lambda b, h: (ctx0 + b, h)),
        out_shape=jax.ShapeDtypeStruct((T_ALL, HEADS * dv), BF16),
        input_output_aliases={0: 0},
        compiler_params=_cparams(("arbitrary", "arbitrary")),
        name="att_ctx",
    )(out, q, k, v)


def _dft_kernel(scale, c_ref, s_ref, pc_ref, ps_ref, o_ref):
    y = (jnp.dot(c_ref[...], pc_ref[...], preferred_element_type=F32)
         - jnp.dot(s_ref[...], ps_ref[...], preferred_element_type=F32))
    o_ref[...] = (y * scale).astype(o_ref.dtype)


def _dft_ctx_kernel(scale, alias_ref, c_ref, s_ref, pc_ref, ps_ref, o_ref):
    del alias_ref
    _dft_kernel(scale, c_ref, s_ref, pc_ref, ps_ref, o_ref)


def _fourier(pc, ps, dft_lat, dft_ctx, with_ctx):
    tr = TR_DFT
    nr = SEQ // tr
    w = FG * FGD
    out = pl.pallas_call(
        functools.partial(_dft_kernel, (SEQ * FGD) ** -0.5),
        grid=(BATCH, nr),
        in_specs=[
            pl.BlockSpec((tr, SEQ), lambda b, r: (r, 0)),
            pl.BlockSpec((tr, SEQ), lambda b, r: (r, 0)),
            pl.BlockSpec((SEQ, w), lambda b, r: (b, 0)),
            pl.BlockSpec((SEQ, w), lambda b, r: (b, 0)),
        ],
        out_specs=pl.BlockSpec((tr, w), lambda b, r: (b * nr + r, 0)),
        out_shape=jax.ShapeDtypeStruct((T_ALL, w), BF16),
        compiler_params=_cparams(("arbitrary", "arbitrary")),
        name="dft_lat",
    )(dft_lat[0], dft_lat[1], pc, ps)
    if not with_ctx:
        return out
    ctx0 = T_LAT // CTX
    return pl.pallas_call(
        functools.partial(_dft_ctx_kernel, (CTX * FGD) ** -0.5),
        grid=(BATCH,),
        in_specs=[
            pl.BlockSpec(memory_space=pl.ANY),
            pl.BlockSpec((CTX, CTX), lambda b: (0, 0)),
            pl.BlockSpec((CTX, CTX), lambda b: (0, 0)),
            pl.BlockSpec((CTX, w), lambda b: (ctx0 + b, 0)),
            pl.BlockSpec((CTX, w), lambda b: (ctx0 + b, 0)),
        ],
        out_specs=pl.BlockSpec((CTX, w), lambda b: (ctx0 + b, 0)),
        out_shape=jax.ShapeDtypeStruct((T_ALL, w), BF16),
        input_output_aliases={0: 0},
        compiler_params=_cparams(("arbitrary",)),
        name="dft_ctx",
    )(out, dft_ctx[0], dft_ctx[1], pc, ps)


def _merge_kernel(n_lat_tiles, h_ref, gain_ref, sh_ref, sc_ref, gt_ref, oa_ref, ob_ref, cbo_ref,
                  u_ref, up_ref, un_ref, of_ref, cw_ref, cbias_ref, wg_ref, bg_ref, wb_ref, wo_ref,
                  o_ref, xn_ref, conv_ref):
    i = pl.program_id(0)
    n = pl.program_id(1)
    tm = h_ref.shape[0]

    @pl.when(n == 0)
    def _():
        xn_ref[...] = _norm_mod(h_ref[...], gain_ref[...], sh_ref[...], sc_ref[...]).astype(BF16)
        o_ref[...] = jnp.zeros_like(o_ref)
        u = u_ref[...]
        seq_len = jnp.where(i < n_lat_tiles, SEQ, CTX)
        ridx = lax.broadcasted_iota(jnp.int32, (tm, 1), 0)
        pos = (i * tm + ridx) & (seq_len - 1)
        prev = jnp.where(ridx == 0, up_ref[7:8, :], pltpu.roll(u, 1, 0))
        prev = jnp.where(pos == 0, 0.0, prev)
        nxt = jnp.where(ridx == tm - 1, un_ref[0:1, :], pltpu.roll(u, tm - 1, 0))
        nxt = jnp.where(pos == seq_len - 1, 0.0, nxt)
        y = prev * cw_ref[0:1, :] + u * cw_ref[1:2, :] + nxt * cw_ref[2:3, :] + cbias_ref[...]
        conv_ref[...] = (cbo_ref[...] * y).astype(BF16)

    xn = xn_ref[...]
    merged = None
    for k, br in enumerate((oa_ref, ob_ref, conv_ref, of_ref)):
        gate = jax.nn.sigmoid(jnp.dot(xn, wg_ref[k], preferred_element_type=F32) + bg_ref[k])
        term = gate * jnp.dot(br[...], wb_ref[k], preferred_element_type=F32)
        merged = term if merged is None else merged + term
    o_ref[...] += jnp.dot(merged.astype(BF16), wo_ref[...], preferred_element_type=F32)

    @pl.when(n == pl.num_programs(1) - 1)
    def _():
        o_ref[...] = h_ref[...] + gt_ref[...] * o_ref[...]


def _merge(h, mod, layer, gain, br, lw, n_rows):
    tm, tn = TM_MERGE, TN_MERGE
    tpb = SEQ // tm
    oa, ob, cbo, u, of = br
    last_blk = T_ALL // 8 - 1

    def row_spec(w):
        return pl.BlockSpec((tm, w), lambda i, n: (i, 0))

    return pl.pallas_call(
        functools.partial(_merge_kernel, T_LAT // tm),
        grid=(n_rows // tm, D // tn),
        in_specs=[
            row_spec(D),
            pl.BlockSpec((1, D), lambda i, n: (0, 0)),
            _mod_spec(layer, 3, tpb),
            _mod_spec(layer, 4, tpb),
            _mod_spec(layer, 5, tpb),
            row_spec(BR_W), row_spec(BR_W), row_spec(CONV_W), row_spec(CONV_W),
            pl.BlockSpec((8, CONV_W), lambda i, n: (jnp.maximum(i * (tm // 8) - 1, 0), 0)),
            pl.BlockSpec((8, CONV_W), lambda i, n: (jnp.minimum((i + 1) * (tm // 8), last_blk), 0)),
            row_spec(BR_W),
            pl.BlockSpec((3, CONV_W), lambda i, n: (0, 0)),
            pl.BlockSpec((1, CONV_W), lambda i, n: (0, 0)),
            pl.BlockSpec((N_BR, D, tn), lambda i, n: (0, 0, n)),
            pl.BlockSpec((N_BR, 1, tn), lambda i, n: (0, 0, n)),
            pl.BlockSpec((N_BR, BR_W, tn), lambda i, n: (0, 0, n)),
            pl.BlockSpec((tn, D), lambda i, n: (n, 0)),
        ],
        out_specs=row_spec(D),
        out_shape=jax.ShapeDtypeStruct((n_rows, D), F32),
        scratch_shapes=[pltpu.VMEM((tm, D), BF16), pltpu.VMEM((tm, CONV_W), BF16)],
        compiler_params=_cparams(("arbitrary", "arbitrary")),
        name="merge",
    )(h, gain, mod, mod, mod, oa, ob, cbo, u, u, u, of, lw["conv_w"], lw["conv_b"],
      lw["w_gate"], lw["b_gate"], lw["w_branch"], lw["w_o"])


def _rope_tables():
    t = jnp.arange(SEQ, dtype=jnp.int32)
    pos_row = (t // GRID_W).astype(F32)
    pos_col = (t % GRID_W).astype(F32)

    def ang(rot_dim):
        n = rot_dim // 4
        inv_freq = THETA ** (-jnp.arange(n, dtype=F32) / n)
        a = jnp.concatenate([pos_row[:, None] * inv_freq, pos_col[:, None] * inv_freq], axis=-1)
        return jnp.cos(a), jnp.sin(a)

    ca, sa = ang(ROPE_A)
    z32 = jnp.zeros_like(ca)
    z64 = jnp.zeros((SEQ, 64), F32)
    tab_c = jnp.concatenate([ca, ca, z64], axis=-1)
    tab_s1 = jnp.concatenate([-sa, z32, z64], axis=-1)
    tab_s2 = jnp.concatenate([z32, sa, z64], axis=-1)
    cb, sb = ang(HD_B)
    tab_cb = jnp.concatenate([cb, cb], axis=-1)
    tab_sb = jnp.concatenate([-sb, sb], axis=-1)
    ones = jnp.ones((TM_PROJ, 128), F32)
    zeros = jnp.zeros((TM_PROJ, 128), F32)
    return tuple(jnp.concatenate([tab, ident], axis=0)
                 for tab, ident in ((tab_c, ones), (tab_s1, zeros), (tab_s2, zeros),
                                    (tab_cb, ones), (tab_sb, zeros)))


def _dft_mats(n):
    j = jnp.arange(n, dtype=jnp.int32)
    a = ((j[:, None] * j[None, :]) % n).astype(F32) * (2.0 * math.pi / n)
    return jnp.cos(a), jnp.sin(a)


def _pad_to(a, axis, size):
    pad = [(0, 0)] * a.ndim
    pad[axis] = (0, size - a.shape[axis])
    return jnp.pad(a, pad)


def _layer_weights(l, p):
    w_in = p["w_in"][l]
    idx, off = [], 0
    for wd in (Q_LORA, KV_LORA, ROPE_A, HEADS * HD_B, KVH_B * HD_B, KVH_B * HD_B, CONV_W, CONV_W,
               CONV_W, FG * FGD):
        idx.append((off, wd))
        off += wd
    segs = [lax.slice_in_dim(w_in, o, o + wd, axis=1) for o, wd in idx]
    segs[0] = _pad_to(segs[0], 1, Q_LORA_PAD)
    segs[2] = _pad_to(segs[2], 1, 128)
    w_in_p = jnp.concatenate(segs, axis=1).astype(BF16)

    w_uq = p["w_uq"][l].reshape(Q_LORA, HEADS, QK_A)
    w_uq = _pad_to(_pad_to(w_uq, 2, QK_A_PAD), 0, Q_LORA_PAD).reshape(Q_LORA_PAD, HEADS * QK_A_PAD)
    w_ukv = p["w_ukv"][l].reshape(KV_LORA, HEADS, NOPE + V_A)
    w_ukv = jnp.concatenate([w_ukv[:, :, :NOPE].reshape(KV_LORA, HEADS * NOPE),
                             w_ukv[:, :, NOPE:].reshape(KV_LORA, HEADS * V_A)], axis=1)
    g_ka = p["g_ka"][l]

    def ffn_w(wi, wo):
        wi2 = jnp.stack([_pad_to(wi[:, :F_FF], 1, F_PAD), _pad_to(wi[:, F_FF:], 1, F_PAD)]).astype(BF16)
        return wi2, _pad_to(wo, 0, F_PAD).astype(BF16)

    ffn1_wi, ffn1_wo = ffn_w(p["ffn1_wi"][l], p["ffn1_wo"][l])
    ffn2_wi, ffn2_wo = ffn_w(p["ffn2_wi"][l], p["ffn2_wo"][l])
    return {
        "w_in": w_in_p,
        "g_cq": _pad_to(p["g_cq"][l], 0, Q_LORA_PAD)[None, :],
        "w_uq": w_uq.astype(BF16),
        "g_ckv": p["g_ckv"][l][None, :],
        "w_ukv": w_ukv.astype(BF16),
        "g_qa": _pad_to(p["g_qa"][l], 0, QK_A_PAD)[None, :],
        "g_ka_nope": g_ka[None, :NOPE],
        "g_ka_pe": _pad_to(g_ka[NOPE:], 0, 128)[None, :],
        "g_qb": p["g_qb"][l][None, :],
        "g_kb": p["g_kb"][l][None, :],
        "conv_w": p["conv_w"][l],
        "conv_b": p["conv_b"][l][None, :],
        "w_gate": p["w_gate"][l].astype(BF16),
        "b_gate": p["b_gate"][l][:, None, :],
        "w_branch": p["w_branch"][l].astype(BF16),
        "w_o": p["w_o"][l].astype(BF16),
        "norm_ffn1": p["norm_ffn1"][l][None, :],
        "norm_mix": p["norm_mix"][l][None, :],
        "norm_ffn2": p["norm_ffn2"][l][None, :],
        "ffn1_wi": ffn1_wi, "ffn1_wo": ffn1_wo, "ffn2_wi": ffn2_wi, "ffn2_wo": ffn2_wo,
    }


def kernel(x, c, ctx, c_ctx, w_ada, b_ada, norm_ffn1, ffn1_wi, ffn1_wo, norm_mix, w_in, g_cq, w_uq, g_ckv, w_ukv, g_qa, g_ka, g_qb, g_kb, conv_w, conv_b, w_branch, w_gate, b_gate, w_o, norm_ffn2, ffn2_wi, ffn2_wo):
    p = dict(norm_ffn1=norm_ffn1, ffn1_wi=ffn1_wi, ffn1_wo=ffn1_wo, norm_mix=norm_mix, w_in=w_in,
             g_cq=g_cq, w_uq=w_uq, g_ckv=g_ckv, w_ukv=w_ukv, g_qa=g_qa, g_ka=g_ka, g_qb=g_qb,
             g_kb=g_kb, conv_w=conv_w, conv_b=conv_b, w_branch=w_branch, w_gate=w_gate,
             b_gate=b_gate, w_o=w_o, norm_ffn2=norm_ffn2, ffn2_wi=ffn2_wi, ffn2_wo=ffn2_wo)

    cond8 = jnp.concatenate([c, c_ctx[None, :], jnp.zeros((8 - BATCH - 1, D), F32)], axis=0)
    mod = _ada(cond8, w_ada, b_ada).reshape(DEPTH, 8, N_MOD, 1, D)

    tabs = _rope_tables()
    cc, sc = _dft_mats(FGD)
    ccs = jnp.concatenate([cc, sc], axis=1).astype(BF16)
    dft_lat = tuple(m.astype(BF16) for m in _dft_mats(SEQ))
    dft_ctx = tuple(m.astype(BF16) for m in _dft_mats(CTX))

    h = jnp.concatenate([x.reshape(T_LAT, D), ctx.reshape(T_CTX, D)], axis=0)
    for l in range(DEPTH):
        last = l == DEPTH - 1
        lw = _layer_weights(l, p)
        h = _ffn(h, mod, l, 0, lw["norm_ffn1"], lw["ffn1_wi"], lw["ffn1_wo"], T_ALL)
        qa, ka, va, qb, kb, vb, cbo, u, pc, ps = _proj(h, mod, l, lw["norm_mix"], lw, tabs, ccs)
        oa = _attention(qa, ka, va, QK_A_PAD, V_A, 1, not last)
        ob = _attention(qb, kb, vb, HD_B, HD_B, GROUP_B, not last)
        of = _fourier(pc, ps, dft_lat, dft_ctx, not last)
        n_rows = T_LAT if last else T_ALL
        h = _merge(h, mod, l, lw["norm_mix"], (oa, ob, cbo, u, of), lw, n_rows)
        h = _ffn(h, mod, l, 6, lw["norm_ffn2"], lw["ffn2_wi"], lw["ffn2_wo"], n_rows)
    return h.reshape(BATCH, SEQ, D)
```

```python
import functools
import math

import jax
import jax.numpy as jnp
from jax import lax
from jax.experimental import pallas as pl
from jax.experimental.pallas import tpu as pltpu

F32 = jnp.float32
BF16 = jnp.bfloat16

D = 2048
BATCH = 4
SEQ = 4096
DEPTH = 4
GRID_W = 64
CTX = 256
N_MOD = 9
F_FF = 5504
F_PAD = 5632
HEADS = 4
Q_LORA = 448
Q_LORA_PAD = 512
KV_LORA = 128
NOPE = 128
ROPE_A = 64
QK_A = NOPE + ROPE_A
QK_A_PAD = 256
V_A = 128
HD_B = 128
KVH_B = 2
GROUP_B = 2
CONV_W = 512
FG = 4
FGD = 128
BR_W = 512
N_BR = 4
THETA = 10000.0
EPS = 1e-6
LOG2_E = math.log2(math.e)

T_LAT = BATCH * SEQ
T_CTX = BATCH * CTX
T_ALL = T_LAT + T_CTX

W_IN_PAD = 3840
OFF_A = 0
OFF_B = 768
OFF_C = 1792
OFF_F = 3328

VMEM_PHYS_V7X = 64 * 1024 * 1024
VMEM_LIMIT = 60 * 1024 * 1024

TM_FFN = 512
TF_FFN = 512
TM_PROJ = 512
TM_MERGE = 512
TN_MERGE = 512
TQ_ATT = 512
KC_ATT = 512
TR_DFT = 512
TN_ADA = 1024


def _cparams(sem):
    return pltpu.CompilerParams(dimension_semantics=sem, vmem_limit_bytes=VMEM_LIMIT)


def _mod_row(i, tiles_per_batch):
    return jnp.minimum(i // tiles_per_batch, BATCH)


def _norm_mod(x, gain, shift, scale):
    ms = jnp.mean(x * x, axis=-1, keepdims=True)
    y = x * lax.rsqrt(ms + EPS) * gain
    return y * (1.0 + scale) + shift


def _silu(x):
    return x * jax.nn.sigmoid(x)


def _ada_kernel(cond_ref, w_ref, b_ref, o_ref):
    s = _silu(cond_ref[...]).astype(BF16)
    o_ref[...] = jnp.dot(s, w_ref[...].astype(BF16), preferred_element_type=F32) + b_ref[...]


def _ada(cond8, w_ada, b_ada):
    n = N_MOD * D
    return pl.pallas_call(
        _ada_kernel,
        grid=(DEPTH, n // TN_ADA),
        in_specs=[
            pl.BlockSpec((8, D), lambda l, j: (0, 0)),
            pl.BlockSpec((None, D, TN_ADA), lambda l, j: (l, 0, j)),
            pl.BlockSpec((None, 1, TN_ADA), lambda l, j: (l, 0, j)),
        ],
        out_specs=pl.BlockSpec((None, 8, TN_ADA), lambda l, j: (l, 0, j)),
        out_shape=jax.ShapeDtypeStruct((DEPTH, 8, n), F32),
        compiler_params=_cparams(("arbitrary", "arbitrary")),
        name="ada",
    )(cond8, w_ada, b_ada.reshape(DEPTH, 1, n))


def _mod_spec(layer, j, tiles_per_batch):
    return pl.BlockSpec((None, None, None, 1, D),
                        lambda i, *_: (layer, _mod_row(i, tiles_per_batch), j, 0, 0))


def _ffn_kernel(x_ref, gain_ref, sh_ref, sc_ref, gt_ref, wg_ref, wu_ref, wo_ref, o_ref, xn_ref):
    f = pl.program_id(1)

    @pl.when(f == 0)
    def _():
        xn_ref[...] = _norm_mod(x_ref[...], gain_ref[...], sh_ref[...], sc_ref[...]).astype(BF16)
        o_ref[...] = jnp.zeros_like(o_ref)

    xn = xn_ref[...]
    g = jnp.dot(xn, wg_ref[...], preferred_element_type=F32)
    u = jnp.dot(xn, wu_ref[...], preferred_element_type=F32)
    a = (_silu(g) * u).astype(BF16)
    o_ref[...] += jnp.dot(a, wo_ref[...], preferred_element_type=F32)

    @pl.when(f == pl.num_programs(1) - 1)
    def _():
        o_ref[...] = x_ref[...] + (0.5 * gt_ref[...]) * o_ref[...]


def _ffn(h, mod, layer, j0, gain, wi2, wo, n_rows):
    tm, tf = TM_FFN, TF_FFN
    tpb = SEQ // tm
    return pl.pallas_call(
        _ffn_kernel,
        grid=(n_rows // tm, F_PAD // tf),
        in_specs=[
            pl.BlockSpec((tm, D), lambda i, f: (i, 0)),
            pl.BlockSpec((1, D), lambda i, f: (0, 0)),
            _mod_spec(layer, j0, tpb),
            _mod_spec(layer, j0 + 1, tpb),
            _mod_spec(layer, j0 + 2, tpb),
            pl.BlockSpec((None, D, tf), lambda i, f: (0, 0, f)),
            pl.BlockSpec((None, D, tf), lambda i, f: (1, 0, f)),
            pl.BlockSpec((tf, D), lambda i, f: (f, 0)),
        ],
        out_specs=pl.BlockSpec((tm, D), lambda i, f: (i, 0)),
        out_shape=jax.ShapeDtypeStruct((n_rows, D), F32),
        scratch_shapes=[pltpu.VMEM((tm, D), BF16)],
        compiler_params=_cparams(("arbitrary", "arbitrary")),
        name="ffn",
    )(h, gain, mod, mod, mod, wi2, wi2, wo)


def _rope_a(v, c, s1, s2):
    return v * c + pltpu.roll(v, 96, 1) * s1 + pltpu.roll(v, 32, 1) * s2


def _rope_b(v, c, s):
    return v * c + pltpu.roll(v, 64, 1) * s


def _proj_kernel(h_ref, gain_ref, sh_ref, sc_ref, w_ref, gcq_ref, wuq_ref, gckv_ref, wukv_ref,
                 gqa_ref, gkan_ref, gkap_ref, gqb_ref, gkb_ref, ca_ref, s1a_ref, s2a_ref,
                 cb_ref, sb_ref, ccs_ref,
                 qa_ref, ka_ref, va_ref, qb_ref, kb_ref, vb_ref, cbo_ref, u_ref, pc_ref, ps_ref):
    xn = _norm_mod(h_ref[...], gain_ref[...], sh_ref[...], sc_ref[...]).astype(BF16)
    ca, s1a, s2a = ca_ref[...], s1a_ref[...], s2a_ref[...]
    cb, sb = cb_ref[...], sb_ref[...]
    scale_a = QK_A ** -0.5 * LOG2_E
    scale_b = HD_B ** -0.5 * LOG2_E

    za = jnp.dot(xn, w_ref[:, OFF_A:OFF_B], preferred_element_type=F32)
    aq, akv, ape = za[:, :Q_LORA_PAD], za[:, Q_LORA_PAD:Q_LORA_PAD + KV_LORA], za[:, Q_LORA_PAD + KV_LORA:]
    r = lax.rsqrt(jnp.sum(aq * aq, axis=-1, keepdims=True) * (1.0 / Q_LORA) + EPS)
    cq = (aq * r * gcq_ref[...]).astype(BF16)
    q = jnp.dot(cq, wuq_ref[...], preferred_element_type=F32)
    gqa = gqa_ref[...]
    for hd in range(HEADS):
        lo = hd * QK_A_PAD
        qh = q[:, lo:lo + QK_A_PAD]
        r = lax.rsqrt(jnp.sum(qh * qh, axis=-1, keepdims=True) * (1.0 / QK_A) + EPS)
        qh = qh * r * gqa
        qa_ref[:, lo:lo + NOPE] = (qh[:, :NOPE] * scale_a).astype(BF16)
        qa_ref[:, lo + NOPE:lo + QK_A_PAD] = (_rope_a(qh[:, NOPE:], ca, s1a, s2a) * scale_a).astype(BF16)
    r = lax.rsqrt(jnp.mean(akv * akv, axis=-1, keepdims=True) + EPS)
    ckv = (akv * r * gckv_ref[...]).astype(BF16)
    kv = jnp.dot(ckv, wukv_ref[...], preferred_element_type=F32)
    ss_pe = jnp.sum(ape * ape, axis=-1, keepdims=True)
    gkan, gkap = gkan_ref[...], gkap_ref[...]
    for hd in range(HEADS):
        kn = kv[:, hd * NOPE:(hd + 1) * NOPE]
        r = lax.rsqrt((jnp.sum(kn * kn, axis=-1, keepdims=True) + ss_pe) * (1.0 / QK_A) + EPS)
        lo = hd * QK_A_PAD
        ka_ref[:, lo:lo + NOPE] = (kn * r * gkan).astype(BF16)
        ka_ref[:, lo + NOPE:lo + QK_A_PAD] = _rope_a(ape * r * gkap, ca, s1a, s2a).astype(BF16)
        va_ref[:, hd * V_A:(hd + 1) * V_A] = kv[:, HEADS * NOPE + hd * V_A:HEADS * NOPE + (hd + 1) * V_A].astype(BF16)

    zb = jnp.dot(xn, w_ref[:, OFF_B:OFF_C], preferred_element_type=F32)
    gqb, gkb = gqb_ref[...], gkb_ref[...]
    for hd in range(HEADS):
        qh = zb[:, hd * HD_B:(hd + 1) * HD_B]
        r = lax.rsqrt(jnp.mean(qh * qh, axis=-1, keepdims=True) + EPS)
        qb_ref[:, hd * HD_B:(hd + 1) * HD_B] = (_rope_b(qh * r * gqb, cb, sb) * scale_b).astype(BF16)
    for hd in range(KVH_B):
        lo = HEADS * HD_B + hd * HD_B
        kh = zb[:, lo:lo + HD_B]
        r = lax.rsqrt(jnp.mean(kh * kh, axis=-1, keepdims=True) + EPS)
        kb_ref[:, hd * HD_B:(hd + 1) * HD_B] = _rope_b(kh * r * gkb, cb, sb).astype(BF16)
    vb_ref[...] = zb[:, (HEADS + KVH_B) * HD_B:].astype(BF16)

    zc = jnp.dot(xn, w_ref[:, OFF_C:OFF_F], preferred_element_type=F32)
    cbo_ref[...] = zc[:, :CONV_W]
    u_ref[...] = zc[:, CONV_W:2 * CONV_W] * zc[:, 2 * CONV_W:]

    zf = jnp.dot(xn, w_ref[:, OFF_F:], preferred_element_type=F32).astype(BF16)
    ccs = ccs_ref[...]
    for g in range(FG):
        pp = jnp.dot(zf[:, g * FGD:(g + 1) * FGD], ccs, preferred_element_type=F32)
        pc_ref[:, g * FGD:(g + 1) * FGD] = pp[:, :FGD].astype(BF16)
        ps_ref[:, g * FGD:(g + 1) * FGD] = pp[:, FGD:].astype(BF16)


def _const_spec(shape):
    nd = len(shape)
    return pl.BlockSpec(shape, lambda i, *_: (0,) * nd, pipeline_mode=pl.Buffered(1))


def _proj(h, mod, layer, gain, lw, tabs, ccs):
    tm = TM_PROJ
    tpb = SEQ // tm
    n_lat = T_LAT // tm

    def tab_spec():
        return pl.BlockSpec((tm, 128), lambda i: (jnp.where(i < n_lat, i % tpb, tpb), 0))

    def row_spec(w):
        return pl.BlockSpec((tm, w), lambda i: (i, 0))

    widths = (HEADS * QK_A_PAD, HEADS * QK_A_PAD, HEADS * V_A, HEADS * HD_B, KVH_B * HD_B,
              KVH_B * HD_B, CONV_W, CONV_W, FG * FGD, FG * FGD)
    dtypes = (BF16, BF16, BF16, BF16, BF16, BF16, F32, F32, BF16, BF16)
    return pl.pallas_call(
        _proj_kernel,
        grid=(T_ALL // tm,),
        in_specs=[
            row_spec(D),
            _const_spec((1, D)),
            _mod_spec(layer, 3, tpb),
            _mod_spec(layer, 4, tpb),
            _const_spec((D, W_IN_PAD)),
            _const_spec((1, Q_LORA_PAD)),
            _const_spec((Q_LORA_PAD, HEADS * QK_A_PAD)),
            _const_spec((1, KV_LORA)),
            _const_spec((KV_LORA, HEADS * (NOPE + V_A))),
            _const_spec((1, QK_A_PAD)),
            _const_spec((1, NOPE)),
            _const_spec((1, 128)),
            _const_spec((1, HD_B)),
            _const_spec((1, HD_B)),
            tab_spec(), tab_spec(), tab_spec(), tab_spec(), tab_spec(),
            _const_spec((FGD, 2 * FGD)),
        ],
        out_specs=[row_spec(w) for w in widths],
        out_shape=[jax.ShapeDtypeStruct((T_ALL, w), dt) for w, dt in zip(widths, dtypes)],
        compiler_params=_cparams(("arbitrary",)),
        name="proj",
    )(h, gain, mod, mod, lw["w_in"], lw["g_cq"], lw["w_uq"], lw["g_ckv"], lw["w_ukv"],
      lw["g_qa"], lw["g_ka_nope"], lw["g_ka_pe"], lw["g_qb"], lw["g_kb"], *tabs, ccs)


_QK_DIMS = (((1,), (1,)), ((), ()))


def _att_lat_kernel(q_ref, k_ref, kc_ref, v_ref, vc_ref, o_ref):
    q = q_ref[...]
    chunks = [(k_ref, v_ref, j * KC_ATT, KC_ATT) for j in range(SEQ // KC_ATT)] + [(kc_ref, vc_ref, 0, CTX)]

    def scores(chunk):
        kr, _, lo, n = chunk
        return lax.dot_general(q, kr[lo:lo + n, :], _QK_DIMS, preferred_element_type=F32)

    s = scores(chunks[0])
    m = l = acc = None
    for idx, (_, vr, lo, n) in enumerate(chunks):
        s_next = scores(chunks[idx + 1]) if idx + 1 < len(chunks) else None
        m_chunk = jnp.max(s, axis=-1, keepdims=True)
        m_new = m_chunk if m is None else jnp.maximum(m, m_chunk)
        p = jnp.exp2(s - m_new)
        pv = jnp.dot(p.astype(BF16), vr[lo:lo + n, :], preferred_element_type=F32)
        if m is None:
            l = jnp.sum(p, axis=-1, keepdims=True)
            acc = pv
        else:
            alpha = jnp.exp2(m - m_new)
            l = alpha * l + jnp.sum(p, axis=-1, keepdims=True)
            acc = alpha * acc + pv
        m, s = m_new, s_next
    o_ref[...] = (acc / l).astype(o_ref.dtype)


def _att_ctx_kernel(alias_ref, q_ref, k_ref, v_ref, o_ref):
    del alias_ref
    s = lax.dot_general(q_ref[...], k_ref[...], _QK_DIMS, preferred_element_type=F32)
    m = jnp.max(s, axis=-1, keepdims=True)
    p = jnp.exp2(s - m)
    l = jnp.sum(p, axis=-1, keepdims=True)
    o = jnp.dot(p.astype(BF16), v_ref[...], preferred_element_type=F32)
    o_ref[...] = (o / l).astype(o_ref.dtype)


def _attention(q, k, v, dq, dv, group, with_ctx):
    tq = TQ_ATT
    nq = SEQ // tq
    ctx0 = T_LAT // CTX
    out = pl.pallas_call(
        _att_lat_kernel,
        grid=(BATCH, HEADS, nq),
        in_specs=[
            pl.BlockSpec((tq, dq), lambda b, h, i: (b * nq + i, h)),
            pl.BlockSpec((SEQ, dq), lambda b, h, i: (b, h // group)),
            pl.BlockSpec((CTX, dq), lambda b, h, i: (ctx0 + b, h // group)),
            pl.BlockSpec((SEQ, dv), lambda b, h, i: (b, h // group)),
            pl.BlockSpec((CTX, dv), lambda b, h, i: (ctx0 + b, h // group)),
        ],
        out_specs=pl.BlockSpec((tq, dv), lambda b, h, i: (b * nq + i, h)),
        out_shape=jax.ShapeDtypeStruct((T_ALL, HEADS * dv), BF16),
        compiler_params=_cparams(("arbitrary", "arbitrary", "arbitrary")),
        name="att_lat",
    )(q, k, k, v, v)
    if not with_ctx:
        return out
    return pl.pallas_call(
        _att_ctx_kernel,
        grid=(BATCH, HEADS),
        in_specs=[
            pl.BlockSpec(memory_space=pl.ANY),
            pl.BlockSpec((CTX, dq), lambda b, h: (ctx0 + b, h)),
            pl.BlockSpec((CTX, dq), lambda b, h: (ctx0 + b, h // group)),
            pl.BlockSpec((CTX, dv), lambda b, h: (ctx0 + b, h // group)),
        ],
        out_specs=pl.BlockSpec((CTX, dv), lambda b, h: (ctx0 + b, h)),
        out_shape=jax.ShapeDtypeStruct((T_ALL, HEADS * dv), BF16),
        input_output_aliases={0: 0},
        compiler_params=_cparams(("arbitrary", "arbitrary")),
        name="att_ctx",
    )(out, q, k, v)


def _dft_kernel(scale, c_ref, s_ref, pc_ref, ps_ref, o_ref):
    y = (jnp.dot(c_ref[...], pc_ref[...], preferred_element_type=F32)
         - jnp.dot(s_ref[...], ps_ref[...], preferred_element_type=F32))
    o_ref[...] = (y * scale).astype(o_ref.dtype)


def _dft_ctx_kernel(scale, alias_ref, c_ref, s_ref, pc_ref, ps_ref, o_ref):
    del alias_ref
    _dft_kernel(scale, c_ref, s_ref, pc_ref, ps_ref, o_ref)


def _fourier(pc, ps, dft_lat, dft_ctx, with_ctx):
    tr = TR_DFT
    nr = SEQ // tr
    w = FG * FGD
    out = pl.pallas_call(
        functools.partial(_dft_kernel, (SEQ * FGD) ** -0.5),
        grid=(BATCH, nr),
        in_specs=[
            pl.BlockSpec((tr, SEQ), lambda b, r: (r, 0)),
            pl.BlockSpec((tr, SEQ), lambda b, r: (r, 0)),
            pl.BlockSpec((SEQ, w), lambda b, r: (b, 0)),
            pl.BlockSpec((SEQ, w), lambda b, r: (b, 0)),
        ],
        out_specs=pl.BlockSpec((tr, w), lambda b, r: (b * nr + r, 0)),
        out_shape=jax.ShapeDtypeStruct((T_ALL, w), BF16),
        compiler_params=_cparams(("arbitrary", "arbitrary")),
        name="dft_lat",
    )(dft_lat[0], dft_lat[1], pc, ps)
    if not with_ctx:
        return out
    ctx0 = T_LAT // CTX
    return pl.pallas_call(
        functools.partial(_dft_ctx_kernel, (CTX * FGD) ** -0.5),
        grid=(BATCH,),
        in_specs=[
            pl.BlockSpec(memory_space=pl.ANY),
            pl.BlockSpec((CTX, CTX), lambda b: (0, 0)),
            pl.BlockSpec((CTX, CTX), lambda b: (0, 0)),
            pl.BlockSpec((CTX, w), lambda b: (ctx0 + b, 0)),
            pl.BlockSpec((CTX, w), lambda b: (ctx0 + b, 0)),
        ],
        out_specs=pl.BlockSpec((CTX, w), lambda b: (ctx0 + b, 0)),
        out_shape=jax.ShapeDtypeStruct((T_ALL, w), BF16),
        input_output_aliases={0: 0},
        compiler_params=_cparams(("arbitrary",)),
        name="dft_ctx",
    )(out, dft_ctx[0], dft_ctx[1], pc, ps)


def _merge_kernel(n_lat_tiles, h_ref, gain_ref, sh_ref, sc_ref, gt_ref, oa_ref, ob_ref, cbo_ref,
                  u_ref, up_ref, un_ref, of_ref, cw_ref, cbias_ref, wg_ref, bg_ref, wb_ref, wo_ref,
                  o_ref, xn_ref, conv_ref):
    i = pl.program_id(0)
    n = pl.program_id(1)
    tm = h_ref.shape[0]

    @pl.when(n == 0)
    def _():
        xn_ref[...] = _norm_mod(h_ref[...], gain_ref[...], sh_ref[...], sc_ref[...]).astype(BF16)
        o_ref[...] = jnp.zeros_like(o_ref)
        u = u_ref[...]
        seq_len = jnp.where(i < n_lat_tiles, SEQ, CTX)
        ridx = lax.broadcasted_iota(jnp.int32, (tm, 1), 0)
        pos = (i * tm + ridx) & (seq_len - 1)
        prev = jnp.where(ridx == 0, up_ref[7:8, :], pltpu.roll(u, 1, 0))
        prev = jnp.where(pos == 0, 0.0, prev)
        nxt = jnp.where(ridx == tm - 1, un_ref[0:1, :], pltpu.roll(u, tm - 1, 0))
        nxt = jnp.where(pos == seq_len - 1, 0.0, nxt)
        y = prev * cw_ref[0:1, :] + u * cw_ref[1:2, :] + nxt * cw_ref[2:3, :] + cbias_ref[...]
        conv_ref[...] = (cbo_ref[...] * y).astype(BF16)

    xn = xn_ref[...]
    merged = None
    for k, br in enumerate((oa_ref, ob_ref, conv_ref, of_ref)):
        gate = jax.nn.sigmoid(jnp.dot(xn, wg_ref[k], preferred_element_type=F32) + bg_ref[k])
        term = gate * jnp.dot(br[...], wb_ref[k], preferred_element_type=F32)
        merged = term if merged is None else merged + term
    o_ref[...] += jnp.dot(merged.astype(BF16), wo_ref[...], preferred_element_type=F32)

    @pl.when(n == pl.num_programs(1) - 1)
    def _():
        o_ref[...] = h_ref[...] + gt_ref[...] * o_ref[...]


def _merge(h, mod, layer, gain, br, lw, n_rows):
    tm, tn = TM_MERGE, TN_MERGE
    tpb = SEQ // tm
    oa, ob, cbo, u, of = br
    last_blk = T_ALL // 8 - 1

    def row_spec(w):
        return pl.BlockSpec((tm, w), lambda i, n: (i, 0))

    return pl.pallas_call(
        functools.partial(_merge_kernel, T_LAT // tm),
        grid=(n_rows // tm, D // tn),
        in_specs=[
            row_spec(D),
            pl.BlockSpec((1, D), lambda i, n: (0, 0)),
            _mod_spec(layer, 3, tpb),
            _mod_spec(layer, 4, tpb),
            _mod_spec(layer, 5, tpb),
            row_spec(BR_W), row_spec(BR_W), row_spec(CONV_W), row_spec(CONV_W),
            pl.BlockSpec((8, CONV_W), lambda i, n: (jnp.maximum(i * (tm // 8) - 1, 0), 0)),
            pl.BlockSpec((8, CONV_W), lambda i, n: (jnp.minimum((i + 1) * (tm // 8), last_blk), 0)),
            row_spec(BR_W),
            pl.BlockSpec((3, CONV_W), lambda i, n: (0, 0)),
            pl.BlockSpec((1, CONV_W), lambda i, n: (0, 0)),
            pl.BlockSpec((N_BR, D, tn), lambda i, n: (0, 0, n)),
            pl.BlockSpec((N_BR, 1, tn), lambda i, n: (0, 0, n)),
            pl.BlockSpec((N_BR, BR_W, tn), lambda i, n: (0, 0, n)),
            pl.BlockSpec((tn, D), lambda i, n: (n, 0)),
        ],
        out_specs=row_spec(D),
        out_shape=jax.ShapeDtypeStruct((n_rows, D), F32),
        scratch_shapes=[pltpu.VMEM((tm, D), BF16), pltpu.VMEM((tm, CONV_W), BF16)],
        compiler_params=_cparams(("arbitrary", "arbitrary")),
        name="merge",
    )(h, gain, mod, mod, mod, oa, ob, cbo, u, u, u, of, lw["conv_w"], lw["conv_b"],
      lw["w_gate"], lw["b_gate"], lw["w_branch"], lw["w_o"])


def _rope_tables():
    t = jnp.arange(SEQ, dtype=jnp.int32)
    pos_row = (t // GRID_W).astype(F32)
    pos_col = (t % GRID_W).astype(F32)

    def ang(rot_dim):
        n = rot_dim // 4
        inv_freq = THETA ** (-jnp.arange(n, dtype=F32) / n)
        a = jnp.concatenate([pos_row[:, None] * inv_freq, pos_col[:, None] * inv_freq], axis=-1)
        return jnp.cos(a), jnp.sin(a)

    ca, sa = ang(ROPE_A)
    z32 = jnp.zeros_like(ca)
    z64 = jnp.zeros((SEQ, 64), F32)
    tab_c = jnp.concatenate([ca, ca, z64], axis=-1)
    tab_s1 = jnp.concatenate([-sa, z32, z64], axis=-1)
    tab_s2 = jnp.concatenate([z32, sa, z64], axis=-1)
    cb, sb = ang(HD_B)
    tab_cb = jnp.concatenate([cb, cb], axis=-1)
    tab_sb = jnp.concatenate([-sb, sb], axis=-1)
    ones = jnp.ones((TM_PROJ, 128), F32)
    zeros = jnp.zeros((TM_PROJ, 128), F32)
    return tuple(jnp.concatenate([tab, ident], axis=0)
                 for tab, ident in ((tab_c, ones), (tab_s1, zeros), (tab_s2, zeros),
                                    (tab_cb, ones), (tab_sb, zeros)))


def _dft_mats(n):
    j = jnp.arange(n, dtype=jnp.int32)
    a = ((j[:, None] * j[None, :]) % n).astype(F32) * (2.0 * math.pi / n)
    return jnp.cos(a), jnp.sin(a)


def _dft_mats_seq():
    k = jnp.arange(SEQ, dtype=jnp.int32)[None, :]
    j = jnp.arange(GRID_W, dtype=jnp.int32)[:, None]
    a = ((j * k) % GRID_W).astype(F32) * (2.0 * math.pi / GRID_W)
    b = ((j * k) % SEQ).astype(F32) * (2.0 * math.pi / SEQ)
    ca, sa, cb, sb = jnp.cos(a)[:, None, :], jnp.sin(a)[:, None, :], jnp.cos(b)[None], jnp.sin(b)[None]
    return ((ca * cb - sa * sb).reshape(SEQ, SEQ), (sa * cb + ca * sb).reshape(SEQ, SEQ))


def _pad_to(a, axis, size):
    pad = [(0, 0)] * a.ndim
    pad[axis] = (0, size - a.shape[axis])
    return jnp.pad(a, pad)


def _layer_weights(l, p):
    w_in = p["w_in"][l]
    idx, off = [], 0
    for wd in (Q_LORA, KV_LORA, ROPE_A, HEADS * HD_B, KVH_B * HD_B, KVH_B * HD_B, CONV_W, CONV_W,
               CONV_W, FG * FGD):
        idx.append((off, wd))
        off += wd
    segs = [lax.slice_in_dim(w_in, o, o + wd, axis=1) for o, wd in idx]
    segs[0] = _pad_to(segs[0], 1, Q_LORA_PAD)
    segs[2] = _pad_to(segs[2], 1, 128)
    w_in_p = jnp.concatenate(segs, axis=1).astype(BF16)

    w_uq = p["w_uq"][l].reshape(Q_LORA, HEADS, QK_A)
    w_uq = _pad_to(_pad_to(w_uq, 2, QK_A_PAD), 0, Q_LORA_PAD).reshape(Q_LORA_PAD, HEADS * QK_A_PAD)
    w_ukv = p["w_ukv"][l].reshape(KV_LORA, HEADS, NOPE + V_A)
    w_ukv = jnp.concatenate([w_ukv[:, :, :NOPE].reshape(KV_LORA, HEADS * NOPE),
                             w_ukv[:, :, NOPE:].reshape(KV_LORA, HEADS * V_A)], axis=1)
    g_ka = p["g_ka"][l]

    def ffn_w(wi, wo):
        wi2 = jnp.stack([_pad_to(wi[:, :F_FF], 1, F_PAD), _pad_to(wi[:, F_FF:], 1, F_PAD)]).astype(BF16)
        return wi2, _pad_to(wo, 0, F_PAD).astype(BF16)

    ffn1_wi, ffn1_wo = ffn_w(p["ffn1_wi"][l], p["ffn1_wo"][l])
    ffn2_wi, ffn2_wo = ffn_w(p["ffn2_wi"][l], p["ffn2_wo"][l])
    return {
        "w_in": w_in_p,
        "g_cq": _pad_to(p["g_cq"][l], 0, Q_LORA_PAD)[None, :],
        "w_uq": w_uq.astype(BF16),
        "g_ckv": p["g_ckv"][l][None, :],
        "w_ukv": w_ukv.astype(BF16),
        "g_qa": _pad_to(p["g_qa"][l], 0, QK_A_PAD)[None, :],
        "g_ka_nope": g_ka[None, :NOPE],
        "g_ka_pe": _pad_to(g_ka[NOPE:], 0, 128)[None, :],
        "g_qb": p["g_qb"][l][None, :],
        "g_kb": p["g_kb"][l][None, :],
        "conv_w": p["conv_w"][l],
        "conv_b": p["conv_b"][l][None, :],
        "w_gate": p["w_gate"][l].astype(BF16),
        "b_gate": p["b_gate"][l][:, None, :],
        "w_branch": p["w_branch"][l].astype(BF16),
        "w_o": p["w_o"][l].astype(BF16),
        "norm_ffn1": p["norm_ffn1"][l][None, :],
        "norm_mix": p["norm_mix"][l][None, :],
        "norm_ffn2": p["norm_ffn2"][l][None, :],
        "ffn1_wi": ffn1_wi, "ffn1_wo": ffn1_wo, "ffn2_wi": ffn2_wi, "ffn2_wo": ffn2_wo,
    }


def kernel(x, c, ctx, c_ctx, w_ada, b_ada, norm_ffn1, ffn1_wi, ffn1_wo, norm_mix, w_in, g_cq, w_uq, g_ckv, w_ukv, g_qa, g_ka, g_qb, g_kb, conv_w, conv_b, w_branch, w_gate, b_gate, w_o, norm_ffn2, ffn2_wi, ffn2_wo):
    p = dict(norm_ffn1=norm_ffn1, ffn1_wi=ffn1_wi, ffn1_wo=ffn1_wo, norm_mix=norm_mix, w_in=w_in,
             g_cq=g_cq, w_uq=w_uq, g_ckv=g_ckv, w_ukv=w_ukv, g_qa=g_qa, g_ka=g_ka, g_qb=g_qb,
             g_kb=g_kb, conv_w=conv_w, conv_b=conv_b, w_branch=w_branch, w_gate=w_gate,
             b_gate=b_gate, w_o=w_o, norm_ffn2=norm_ffn2, ffn2_wi=ffn2_wi, ffn2_wo=ffn2_wo)

    cond8 = jnp.concatenate([c, c_ctx[None, :], jnp.zeros((8 - BATCH - 1, D), F32)], axis=0)
    mod = _ada(cond8, w_ada, b_ada).reshape(DEPTH, 8, N_MOD, 1, D)

    tabs = _rope_tables()
    cc, sc = _dft_mats(FGD)
    ccs = jnp.concatenate([cc, sc], axis=1).astype(BF16)
    dft_lat = tuple(m.astype(BF16) for m in _dft_mats_seq())
    dft_ctx = tuple(m.astype(BF16) for m in _dft_mats(CTX))

    h = jnp.concatenate([x.reshape(T_LAT, D), ctx.reshape(T_CTX, D)], axis=0)
    for l in range(DEPTH):
        last = l == DEPTH - 1
        lw = _layer_weights(l, p)
        h = _ffn(h, mod, l, 0, lw["norm_ffn1"], lw["ffn1_wi"], lw["ffn1_wo"], T_ALL)
        qa, ka, va, qb, kb, vb, cbo, u, pc, ps = _proj(h, mod, l, lw["norm_mix"], lw, tabs, ccs)
        oa = _attention(qa, ka, va, QK_A_PAD, V_A, 1, not last)
        ob = _attention(qb, kb, vb, HD_B, HD_B, GROUP_B, not last)
        of = _fourier(pc, ps, dft_lat, dft_ctx, not last)
        n_rows = T_LAT if last else T_ALL
        h = _merge(h, mod, l, lw["norm_mix"], (oa, ob, cbo, u, of), lw, n_rows)
        h = _ffn(h, mod, l, 6, lw["norm_ffn2"], lw["ffn2_wi"], lw["ffn2_wo"], n_rows)
    return h.reshape(BATCH, SEQ, D)
```

```python
import functools
import math

import jax
import jax.numpy as jnp
from jax import lax
from jax.experimental import pallas as pl
from jax.experimental.pallas import tpu as pltpu

F32 = jnp.float32
BF16 = jnp.bfloat16

D = 2048
BATCH = 4
SEQ = 4096
DEPTH = 4
GRID_W = 64
CTX = 256
N_MOD = 9
F_FF = 5504
F_PAD = 5632
HEADS = 4
Q_LORA = 448
Q_LORA_PAD = 512
KV_LORA = 128
NOPE = 128
ROPE_A = 64
QK_A = NOPE + ROPE_A
QK_A_PAD = 256
V_A = 128
HD_B = 128
KVH_B = 2
GROUP_B = 2
CONV_W = 512
FG = 4
FGD = 128
BR_W = 512
N_BR = 4
THETA = 10000.0
EPS = 1e-6
LOG2_E = math.log2(math.e)

T_LAT = BATCH * SEQ
T_CTX = BATCH * CTX
T_ALL = T_LAT + T_CTX

W_IN_PAD = 3840
OFF_A = 0
OFF_B = 768
OFF_C = 1792
OFF_F = 3328

VMEM_PHYS_V7X = 64 * 1024 * 1024
VMEM_LIMIT = 60 * 1024 * 1024

TM_FFN = 1024
TF_FFN = 512
TM_PROJ = 512
TM_MERGE = 512
TN_MERGE = 512
TQ_ATT = 512
KC_ATT = 512
TR_DFT = 512
TN_ADA = 1024


def _cparams(sem):
    return pltpu.CompilerParams(dimension_semantics=sem, vmem_limit_bytes=VMEM_LIMIT)


def _mod_row(i, tiles_per_batch):
    return jnp.minimum(i // tiles_per_batch, BATCH)


def _row_rsqrt(x):
    return lax.rsqrt(jnp.mean(x * x, axis=-1, keepdims=True) + EPS)


def _scale_mod(x, r, gain, shift, scale):
    return (x * r * gain) * (1.0 + scale) + shift


def _norm_mod(x, gain, shift, scale):
    return _scale_mod(x, _row_rsqrt(x), gain, shift, scale)


def _silu(x):
    return x * jax.nn.sigmoid(x)


def _ada_kernel(cond_ref, w_ref, b_ref, o_ref):
    s = _silu(cond_ref[...]).astype(BF16)
    o_ref[...] = jnp.dot(s, w_ref[...].astype(BF16), preferred_element_type=F32) + b_ref[...]


def _ada(cond8, w_ada, b_ada):
    n = N_MOD * D
    return pl.pallas_call(
        _ada_kernel,
        grid=(DEPTH, n // TN_ADA),
        in_specs=[
            pl.BlockSpec((8, D), lambda l, j: (0, 0)),
            pl.BlockSpec((None, D, TN_ADA), lambda l, j: (l, 0, j)),
            pl.BlockSpec((None, 1, TN_ADA), lambda l, j: (l, 0, j)),
        ],
        out_specs=pl.BlockSpec((None, 8, TN_ADA), lambda l, j: (l, 0, j)),
        out_shape=jax.ShapeDtypeStruct((DEPTH, 8, n), F32),
        compiler_params=_cparams(("arbitrary", "arbitrary")),
        name="ada",
    )(cond8, w_ada, b_ada.reshape(DEPTH, 1, n))


def _mod_spec(layer, j, tiles_per_batch):
    return pl.BlockSpec((None, None, None, 1, D),
                        lambda i, *_: (layer, _mod_row(i, tiles_per_batch), j, 0, 0))


def _ffn_kernel(x_ref, gain_ref, sh_ref, sc_ref, gt_ref, wg_ref, wu_ref, wo_ref, o_ref, r_ref):
    f = pl.program_id(1)

    @pl.when(f == 0)
    def _():
        r_ref[...] = _row_rsqrt(x_ref[...])
        o_ref[...] = jnp.zeros_like(o_ref)

    xn = _scale_mod(x_ref[...], r_ref[...], gain_ref[...], sh_ref[...], sc_ref[...]).astype(BF16)
    g = jnp.dot(xn, wg_ref[...], preferred_element_type=F32)
    u = jnp.dot(xn, wu_ref[...], preferred_element_type=F32)
    a = (_silu(g) * u).astype(BF16)
    o_ref[...] += jnp.dot(a, wo_ref[...], preferred_element_type=F32)

    @pl.when(f == pl.num_programs(1) - 1)
    def _():
        o_ref[...] = x_ref[...] + (0.5 * gt_ref[...]) * o_ref[...]


def _ffn(h, mod, layer, j0, gain, wi2, wo, n_rows):
    tm, tf = TM_FFN, TF_FFN
    tpb = SEQ // tm
    return pl.pallas_call(
        _ffn_kernel,
        grid=(n_rows // tm, F_PAD // tf),
        in_specs=[
            pl.BlockSpec((tm, D), lambda i, f: (i, 0)),
            pl.BlockSpec((1, D), lambda i, f: (0, 0)),
            _mod_spec(layer, j0, tpb),
            _mod_spec(layer, j0 + 1, tpb),
            _mod_spec(layer, j0 + 2, tpb),
            pl.BlockSpec((None, D, tf), lambda i, f: (0, 0, f)),
            pl.BlockSpec((None, D, tf), lambda i, f: (1, 0, f)),
            pl.BlockSpec((tf, D), lambda i, f: (f, 0)),
        ],
        out_specs=pl.BlockSpec((tm, D), lambda i, f: (i, 0)),
        out_shape=jax.ShapeDtypeStruct((n_rows, D), F32),
        scratch_shapes=[pltpu.VMEM((tm, 1), F32)],
        compiler_params=_cparams(("arbitrary", "arbitrary")),
        name="ffn",
    )(h, gain, mod, mod, mod, wi2, wi2, wo)


def _rope_a(v, c, s1, s2):
    return v * c + pltpu.roll(v, 96, 1) * s1 + pltpu.roll(v, 32, 1) * s2


def _rope_b(v, c, s):
    return v * c + pltpu.roll(v, 64, 1) * s


def _proj_kernel(h_ref, gain_ref, sh_ref, sc_ref, w_ref, gcq_ref, wuq_ref, gckv_ref, wukv_ref,
                 gqa_ref, gkan_ref, gkap_ref, gqb_ref, gkb_ref, ca_ref, s1a_ref, s2a_ref,
                 cb_ref, sb_ref, ccs_ref,
                 qa_ref, ka_ref, va_ref, qb_ref, kb_ref, vb_ref, cbo_ref, u_ref, pc_ref, ps_ref):
    xn = _norm_mod(h_ref[...], gain_ref[...], sh_ref[...], sc_ref[...]).astype(BF16)
    ca, s1a, s2a = ca_ref[...], s1a_ref[...], s2a_ref[...]
    cb, sb = cb_ref[...], sb_ref[...]
    scale_a = QK_A ** -0.5 * LOG2_E
    scale_b = HD_B ** -0.5 * LOG2_E

    za = jnp.dot(xn, w_ref[:, OFF_A:OFF_B], preferred_element_type=F32)
    aq, akv, ape = za[:, :Q_LORA_PAD], za[:, Q_LORA_PAD:Q_LORA_PAD + KV_LORA], za[:, Q_LORA_PAD + KV_LORA:]
    r = lax.rsqrt(jnp.sum(aq * aq, axis=-1, keepdims=True) * (1.0 / Q_LORA) + EPS)
    cq = (aq * r * gcq_ref[...]).astype(BF16)
    q = jnp.dot(cq, wuq_ref[...], preferred_element_type=F32)
    gqa = gqa_ref[...]
    for hd in range(HEADS):
        lo = hd * QK_A_PAD
        qh = q[:, lo:lo + QK_A_PAD]
        r = lax.rsqrt(jnp.sum(qh * qh, axis=-1, keepdims=True) * (1.0 / QK_A) + EPS)
        qh = qh * r * gqa
        qa_ref[:, lo:lo + NOPE] = (qh[:, :NOPE] * scale_a).astype(BF16)
        qa_ref[:, lo + NOPE:lo + QK_A_PAD] = (_rope_a(qh[:, NOPE:], ca, s1a, s2a) * scale_a).astype(BF16)
    r = lax.rsqrt(jnp.mean(akv * akv, axis=-1, keepdims=True) + EPS)
    ckv = (akv * r * gckv_ref[...]).astype(BF16)
    kv = jnp.dot(ckv, wukv_ref[...], preferred_element_type=F32)
    ss_pe = jnp.sum(ape * ape, axis=-1, keepdims=True)
    gkan, gkap = gkan_ref[...], gkap_ref[...]
    for hd in range(HEADS):
        kn = kv[:, hd * NOPE:(hd + 1) * NOPE]
        r = lax.rsqrt((jnp.sum(kn * kn, axis=-1, keepdims=True) + ss_pe) * (1.0 / QK_A) + EPS)
        lo = hd * QK_A_PAD
        ka_ref[:, lo:lo + NOPE] = (kn * r * gkan).astype(BF16)
        ka_ref[:, lo + NOPE:lo + QK_A_PAD] = _rope_a(ape * r * gkap, ca, s1a, s2a).astype(BF16)
        va_ref[:, hd * V_A:(hd + 1) * V_A] = kv[:, HEADS * NOPE + hd * V_A:HEADS * NOPE + (hd + 1) * V_A].astype(BF16)

    zb = jnp.dot(xn, w_ref[:, OFF_B:OFF_C], preferred_element_type=F32)
    gqb, gkb = gqb_ref[...], gkb_ref[...]
    for hd in range(HEADS):
        qh = zb[:, hd * HD_B:(hd + 1) * HD_B]
        r = lax.rsqrt(jnp.mean(qh * qh, axis=-1, keepdims=True) + EPS)
        qb_ref[:, hd * HD_B:(hd + 1) * HD_B] = (_rope_b(qh * r * gqb, cb, sb) * scale_b).astype(BF16)
    for hd in range(KVH_B):
        lo = HEADS * HD_B + hd * HD_B
        kh = zb[:, lo:lo + HD_B]
        r = lax.rsqrt(jnp.mean(kh * kh, axis=-1, keepdims=True) + EPS)
        kb_ref[:, hd * HD_B:(hd + 1) * HD_B] = _rope_b(kh * r * gkb, cb, sb).astype(BF16)
    vb_ref[...] = zb[:, (HEADS + KVH_B) * HD_B:].astype(BF16)

    zc = jnp.dot(xn, w_ref[:, OFF_C:OFF_F], preferred_element_type=F32)
    cbo_ref[...] = zc[:, :CONV_W]
    u_ref[...] = zc[:, CONV_W:2 * CONV_W] * zc[:, 2 * CONV_W:]

    zf = jnp.dot(xn, w_ref[:, OFF_F:], preferred_element_type=F32).astype(BF16)
    ccs = ccs_ref[...]
    for g in range(FG):
        pp = jnp.dot(zf[:, g * FGD:(g + 1) * FGD], ccs, preferred_element_type=F32)
        pc_ref[:, g * FGD:(g + 1) * FGD] = pp[:, :FGD].astype(BF16)
        ps_ref[:, g * FGD:(g + 1) * FGD] = pp[:, FGD:].astype(BF16)


def _const_spec(shape):
    nd = len(shape)
    return pl.BlockSpec(shape, lambda i, *_: (0,) * nd, pipeline_mode=pl.Buffered(1))


def _proj(h, mod, layer, gain, lw, tabs, ccs):
    tm = TM_PROJ
    tpb = SEQ // tm
    n_lat = T_LAT // tm

    def tab_spec():
        return pl.BlockSpec((tm, 128), lambda i: (jnp.where(i < n_lat, i % tpb, tpb), 0))

    def row_spec(w):
        return pl.BlockSpec((tm, w), lambda i: (i, 0))

    widths = (HEADS * QK_A_PAD, HEADS * QK_A_PAD, HEADS * V_A, HEADS * HD_B, KVH_B * HD_B,
              KVH_B * HD_B, CONV_W, CONV_W, FG * FGD, FG * FGD)
    dtypes = (BF16, BF16, BF16, BF16, BF16, BF16, F32, F32, BF16, BF16)
    return pl.pallas_call(
        _proj_kernel,
        grid=(T_ALL // tm,),
        in_specs=[
            row_spec(D),
            _const_spec((1, D)),
            _mod_spec(layer, 3, tpb),
            _mod_spec(layer, 4, tpb),
            _const_spec((D, W_IN_PAD)),
            _const_spec((1, Q_LORA_PAD)),
            _const_spec((Q_LORA_PAD, HEADS * QK_A_PAD)),
            _const_spec((1, KV_LORA)),
            _const_spec((KV_LORA, HEADS * (NOPE + V_A))),
            _const_spec((1, QK_A_PAD)),
            _const_spec((1, NOPE)),
            _const_spec((1, 128)),
            _const_spec((1, HD_B)),
            _const_spec((1, HD_B)),
            tab_spec(), tab_spec(), tab_spec(), tab_spec(), tab_spec(),
            _const_spec((FGD, 2 * FGD)),
        ],
        out_specs=[row_spec(w) for w in widths],
        out_shape=[jax.ShapeDtypeStruct((T_ALL, w), dt) for w, dt in zip(widths, dtypes)],
        compiler_params=_cparams(("arbitrary",)),
        name="proj",
    )(h, gain, mod, mod, lw["w_in"], lw["g_cq"], lw["w_uq"], lw["g_ckv"], lw["w_ukv"],
      lw["g_qa"], lw["g_ka_nope"], lw["g_ka_pe"], lw["g_qb"], lw["g_kb"], *tabs, ccs)


_QK_DIMS = (((1,), (1,)), ((), ()))


def _att_lat_kernel(q_ref, k_ref, kc_ref, v_ref, vc_ref, o_ref):
    q = q_ref[...]
    chunks = [(k_ref, v_ref, j * KC_ATT, KC_ATT) for j in range(SEQ // KC_ATT)] + [(kc_ref, vc_ref, 0, CTX)]

    def scores(chunk):
        kr, _, lo, n = chunk
        return lax.dot_general(q, kr[lo:lo + n, :], _QK_DIMS, preferred_element_type=F32)

    s = scores(chunks[0])
    m = l = acc = None
    for idx, (_, vr, lo, n) in enumerate(chunks):
        s_next = scores(chunks[idx + 1]) if idx + 1 < len(chunks) else None
        m_chunk = jnp.max(s, axis=-1, keepdims=True)
        m_new = m_chunk if m is None else jnp.maximum(m, m_chunk)
        p = jnp.exp2(s - m_new)
        pv = jnp.dot(p.astype(BF16), vr[lo:lo + n, :], preferred_element_type=F32)
        if m is None:
            l = jnp.sum(p, axis=-1, keepdims=True)
            acc = pv
        else:
            alpha = jnp.exp2(m - m_new)
            l = alpha * l + jnp.sum(p, axis=-1, keepdims=True)
            acc = alpha * acc + pv
        m, s = m_new, s_next
    o_ref[...] = (acc / l).astype(o_ref.dtype)


def _att_ctx_kernel(q_ref, k_ref, v_ref, o_ref):
    s = lax.dot_general(q_ref[...], k_ref[...], _QK_DIMS, preferred_element_type=F32)
    m = jnp.max(s, axis=-1, keepdims=True)
    p = jnp.exp2(s - m)
    l = jnp.sum(p, axis=-1, keepdims=True)
    o = jnp.dot(p.astype(BF16), v_ref[...], preferred_element_type=F32)
    o_ref[...] = (o / l).astype(o_ref.dtype)


def _attention(q, k, v, dq, dv, group, with_ctx):
    tq = TQ_ATT
    nq = SEQ // tq
    ctx0 = T_LAT // CTX
    out = pl.pallas_call(
        _att_lat_kernel,
        grid=(BATCH, HEADS, nq),
        in_specs=[
            pl.BlockSpec((tq, dq), lambda b, h, i: (b * nq + i, h)),
            pl.BlockSpec((SEQ, dq), lambda b, h, i: (b, h // group)),
            pl.BlockSpec((CTX, dq), lambda b, h, i: (ctx0 + b, h // group)),
            pl.BlockSpec((SEQ, dv), lambda b, h, i: (b, h // group)),
            pl.BlockSpec((CTX, dv), lambda b, h, i: (ctx0 + b, h // group)),
        ],
        out_specs=pl.BlockSpec((tq, dv), lambda b, h, i: (b * nq + i, h)),
        out_shape=jax.ShapeDtypeStruct((T_LAT, HEADS * dv), BF16),
        compiler_params=_cparams(("arbitrary", "arbitrary", "arbitrary")),
        name="att_lat",
    )(q, k, k, v, v)
    if not with_ctx:
        return out, None
    out_ctx = pl.pallas_call(
        _att_ctx_kernel,
        grid=(BATCH, HEADS),
        in_specs=[
            pl.BlockSpec((CTX, dq), lambda b, h: (ctx0 + b, h)),
            pl.BlockSpec((CTX, dq), lambda b, h: (ctx0 + b, h // group)),
            pl.BlockSpec((CTX, dv), lambda b, h: (ctx0 + b, h // group)),
        ],
        out_specs=pl.BlockSpec((CTX, dv), lambda b, h: (b, h)),
        out_shape=jax.ShapeDtypeStruct((T_CTX, HEADS * dv), BF16),
        compiler_params=_cparams(("arbitrary", "arbitrary")),
        name="att_ctx",
    )(q, k, v)
    return out, out_ctx


def _dft_kernel(scale, c_ref, s_ref, pc_ref, ps_ref, o_ref):
    y = (jnp.dot(c_ref[...], pc_ref[...], preferred_element_type=F32)
         - jnp.dot(s_ref[...], ps_ref[...], preferred_element_type=F32))
    o_ref[...] = (y * scale).astype(o_ref.dtype)


def _fourier(pc, ps, dft_lat, dft_ctx, with_ctx):
    tr = TR_DFT
    nr = SEQ // tr
    w = FG * FGD
    out = pl.pallas_call(
        functools.partial(_dft_kernel, (SEQ * FGD) ** -0.5),
        grid=(BATCH, nr),
        in_specs=[
            pl.BlockSpec((tr, SEQ), lambda b, r: (r, 0)),
            pl.BlockSpec((tr, SEQ), lambda b, r: (r, 0)),
            pl.BlockSpec((SEQ, w), lambda b, r: (b, 0)),
            pl.BlockSpec((SEQ, w), lambda b, r: (b, 0)),
        ],
        out_specs=pl.BlockSpec((tr, w), lambda b, r: (b * nr + r, 0)),
        out_shape=jax.ShapeDtypeStruct((T_LAT, w), BF16),
        compiler_params=_cparams(("arbitrary", "arbitrary")),
        name="dft_lat",
    )(dft_lat[0], dft_lat[1], pc, ps)
    if not with_ctx:
        return out, None
    ctx0 = T_LAT // CTX
    out_ctx = pl.pallas_call(
        functools.partial(_dft_kernel, (CTX * FGD) ** -0.5),
        grid=(BATCH,),
        in_specs=[
            pl.BlockSpec((CTX, CTX), lambda b: (0, 0)),
            pl.BlockSpec((CTX, CTX), lambda b: (0, 0)),
            pl.BlockSpec((CTX, w), lambda b: (ctx0 + b, 0)),
            pl.BlockSpec((CTX, w), lambda b: (ctx0 + b, 0)),
        ],
        out_specs=pl.BlockSpec((CTX, w), lambda b: (b, 0)),
        out_shape=jax.ShapeDtypeStruct((T_CTX, w), BF16),
        compiler_params=_cparams(("arbitrary",)),
        name="dft_ctx",
    )(dft_ctx[0], dft_ctx[1], pc, ps)
    return out, out_ctx


def _merge_kernel(n_lat_tiles, with_ctx, h_ref, gain_ref, sh_ref, sc_ref, gt_ref, *refs):
    lat_refs, refs = refs[:3], refs[3:]
    if with_ctx:
        ctx_refs, refs = refs[:3], refs[3:]
    (cbo_ref, u_ref, up_ref, un_ref, cw_ref, cbias_ref, wg_ref, bg_ref, wb_ref, wo_ref,
     o_ref, r_ref) = refs
    i = pl.program_id(0)
    n = pl.program_id(1)
    tm = h_ref.shape[0]

    @pl.when(n == 0)
    def _():
        r_ref[...] = _row_rsqrt(h_ref[...])
        o_ref[...] = jnp.zeros_like(o_ref)

    xn = _scale_mod(h_ref[...], r_ref[...], gain_ref[...], sh_ref[...], sc_ref[...]).astype(BF16)

    u = u_ref[...]
    is_lat = i < n_lat_tiles
    seq_len = jnp.where(is_lat, SEQ, CTX)
    ridx = lax.broadcasted_iota(jnp.int32, (tm, 1), 0)
    pos = (i * tm + ridx) & (seq_len - 1)
    prev = jnp.where(ridx == 0, up_ref[7:8, :], pltpu.roll(u, 1, 0))
    prev = jnp.where(pos == 0, 0.0, prev)
    nxt = jnp.where(ridx == tm - 1, un_ref[0:1, :], pltpu.roll(u, tm - 1, 0))
    nxt = jnp.where(pos == seq_len - 1, 0.0, nxt)
    y = prev * cw_ref[0:1, :] + u * cw_ref[1:2, :] + nxt * cw_ref[2:3, :] + cbias_ref[...]
    conv = (cbo_ref[...] * y).astype(BF16)

    if with_ctx:
        oa, ob, of = (jnp.where(is_lat, lr[...], cr[...]) for lr, cr in zip(lat_refs, ctx_refs))
    else:
        oa, ob, of = (lr[...] for lr in lat_refs)

    merged = None
    for k, br in enumerate((oa, ob, conv, of)):
        gate = jax.nn.sigmoid(jnp.dot(xn, wg_ref[k], preferred_element_type=F32) + bg_ref[k])
        term = gate * jnp.dot(br, wb_ref[k], preferred_element_type=F32)
        merged = term if merged is None else merged + term
    o_ref[...] += jnp.dot(merged.astype(BF16), wo_ref[...], preferred_element_type=F32)

    @pl.when(n == pl.num_programs(1) - 1)
    def _():
        o_ref[...] = h_ref[...] + gt_ref[...] * o_ref[...]


def _merge(h, mod, layer, gain, br_lat, br_ctx, cbo, u, lw, n_rows):
    tm, tn = TM_MERGE, TN_MERGE
    tpb = SEQ // tm
    n_lat = T_LAT // tm
    with_ctx = br_ctx is not None
    last_blk = T_ALL // 8 - 1
    br_specs = [pl.BlockSpec((tm, BR_W), lambda i, n: (jnp.minimum(i, n_lat - 1), 0))] * 3
    if with_ctx:
        br_specs += [pl.BlockSpec((tm, BR_W), lambda i, n: (jnp.maximum(i - n_lat, 0), 0))] * 3

    def row_spec(w):
        return pl.BlockSpec((tm, w), lambda i, n: (i, 0))

    return pl.pallas_call(
        functools.partial(_merge_kernel, n_lat, with_ctx),
        grid=(n_rows // tm, D // tn),
        in_specs=[
            row_spec(D),
            pl.BlockSpec((1, D), lambda i, n: (0, 0)),
            _mod_spec(layer, 3, tpb),
            _mod_spec(layer, 4, tpb),
            _mod_spec(layer, 5, tpb),
            *br_specs,
            row_spec(CONV_W), row_spec(CONV_W),
            pl.BlockSpec((8, CONV_W), lambda i, n: (jnp.maximum(i * (tm // 8) - 1, 0), 0)),
            pl.BlockSpec((8, CONV_W), lambda i, n: (jnp.minimum((i + 1) * (tm // 8), last_blk), 0)),
            pl.BlockSpec((3, CONV_W), lambda i, n: (0, 0)),
            pl.BlockSpec((1, CONV_W), lambda i, n: (0, 0)),
            pl.BlockSpec((N_BR, D, tn), lambda i, n: (0, 0, n)),
            pl.BlockSpec((N_BR, 1, tn), lambda i, n: (0, 0, n)),
            pl.BlockSpec((N_BR, BR_W, tn), lambda i, n: (0, 0, n)),
            pl.BlockSpec((tn, D), lambda i, n: (n, 0)),
        ],
        out_specs=row_spec(D),
        out_shape=jax.ShapeDtypeStruct((n_rows, D), F32),
        scratch_shapes=[pltpu.VMEM((tm, 1), F32)],
        compiler_params=_cparams(("arbitrary", "arbitrary")),
        name="merge",
    )(h, gain, mod, mod, mod, *br_lat, *(br_ctx or ()), cbo, u, u, u, lw["conv_w"], lw["conv_b"],
      lw["w_gate"], lw["b_gate"], lw["w_branch"], lw["w_o"])


def _rope_tables():
    t = jnp.arange(SEQ, dtype=jnp.int32)
    pos_row = (t // GRID_W).astype(F32)
    pos_col = (t % GRID_W).astype(F32)

    def ang(rot_dim):
        n = rot_dim // 4
        inv_freq = THETA ** (-jnp.arange(n, dtype=F32) / n)
        a = jnp.concatenate([pos_row[:, None] * inv_freq, pos_col[:, None] * inv_freq], axis=-1)
        return jnp.cos(a), jnp.sin(a)

    ca, sa = ang(ROPE_A)
    z32 = jnp.zeros_like(ca)
    z64 = jnp.zeros((SEQ, 64), F32)
    tab_c = jnp.concatenate([ca, ca, z64], axis=-1)
    tab_s1 = jnp.concatenate([-sa, z32, z64], axis=-1)
    tab_s2 = jnp.concatenate([z32, sa, z64], axis=-1)
    cb, sb = ang(HD_B)
    tab_cb = jnp.concatenate([cb, cb], axis=-1)
    tab_sb = jnp.concatenate([-sb, sb], axis=-1)
    ones = jnp.ones((TM_PROJ, 128), F32)
    zeros = jnp.zeros((TM_PROJ, 128), F32)
    return tuple(jnp.concatenate([tab, ident], axis=0)
                 for tab, ident in ((tab_c, ones), (tab_s1, zeros), (tab_s2, zeros),
                                    (tab_cb, ones), (tab_sb, zeros)))


def _dft_mats(n):
    j = jnp.arange(n, dtype=jnp.int32)
    a = ((j[:, None] * j[None, :]) % n).astype(F32) * (2.0 * math.pi / n)
    return jnp.cos(a), jnp.sin(a)


def _dft_mats_seq():
    k = jnp.arange(SEQ, dtype=jnp.int32)[None, :]
    j = jnp.arange(GRID_W, dtype=jnp.int32)[:, None]
    a = ((j * k) % GRID_W).astype(F32) * (2.0 * math.pi / GRID_W)
    b = ((j * k) % SEQ).astype(F32) * (2.0 * math.pi / SEQ)
    ca, sa, cb, sb = jnp.cos(a)[:, None, :], jnp.sin(a)[:, None, :], jnp.cos(b)[None], jnp.sin(b)[None]
    return ((ca * cb - sa * sb).reshape(SEQ, SEQ), (sa * cb + ca * sb).reshape(SEQ, SEQ))


def _pad_to(a, axis, size):
    pad = [(0, 0)] * a.ndim
    pad[axis] = (0, size - a.shape[axis])
    return jnp.pad(a, pad)


def _layer_weights(l, p):
    w_in = p["w_in"][l]
    idx, off = [], 0
    for wd in (Q_LORA, KV_LORA, ROPE_A, HEADS * HD_B, KVH_B * HD_B, KVH_B * HD_B, CONV_W, CONV_W,
               CONV_W, FG * FGD):
        idx.append((off, wd))
        off += wd
    segs = [lax.slice_in_dim(w_in, o, o + wd, axis=1) for o, wd in idx]
    segs[0] = _pad_to(segs[0], 1, Q_LORA_PAD)
    segs[2] = _pad_to(segs[2], 1, 128)
    w_in_p = jnp.concatenate(segs, axis=1).astype(BF16)

    w_uq = p["w_uq"][l].reshape(Q_LORA, HEADS, QK_A)
    w_uq = _pad_to(_pad_to(w_uq, 2, QK_A_PAD), 0, Q_LORA_PAD).reshape(Q_LORA_PAD, HEADS * QK_A_PAD)
    w_ukv = p["w_ukv"][l].reshape(KV_LORA, HEADS, NOPE + V_A)
    w_ukv = jnp.concatenate([w_ukv[:, :, :NOPE].reshape(KV_LORA, HEADS * NOPE),
                             w_ukv[:, :, NOPE:].reshape(KV_LORA, HEADS * V_A)], axis=1)
    g_ka = p["g_ka"][l]

    def ffn_w(wi, wo):
        wi2 = jnp.stack([_pad_to(wi[:, :F_FF], 1, F_PAD), _pad_to(wi[:, F_FF:], 1, F_PAD)]).astype(BF16)
        return wi2, _pad_to(wo, 0, F_PAD).astype(BF16)

    ffn1_wi, ffn1_wo = ffn_w(p["ffn1_wi"][l], p["ffn1_wo"][l])
    ffn2_wi, ffn2_wo = ffn_w(p["ffn2_wi"][l], p["ffn2_wo"][l])
    return {
        "w_in": w_in_p,
        "g_cq": _pad_to(p["g_cq"][l], 0, Q_LORA_PAD)[None, :],
        "w_uq": w_uq.astype(BF16),
        "g_ckv": p["g_ckv"][l][None, :],
        "w_ukv": w_ukv.astype(BF16),
        "g_qa": _pad_to(p["g_qa"][l], 0, QK_A_PAD)[None, :],
        "g_ka_nope": g_ka[None, :NOPE],
        "g_ka_pe": _pad_to(g_ka[NOPE:], 0, 128)[None, :],
        "g_qb": p["g_qb"][l][None, :],
        "g_kb": p["g_kb"][l][None, :],
        "conv_w": p["conv_w"][l],
        "conv_b": p["conv_b"][l][None, :],
        "w_gate": p["w_gate"][l].astype(BF16),
        "b_gate": p["b_gate"][l][:, None, :],
        "w_branch": p["w_branch"][l].astype(BF16),
        "w_o": p["w_o"][l].astype(BF16),
        "norm_ffn1": p["norm_ffn1"][l][None, :],
        "norm_mix": p["norm_mix"][l][None, :],
        "norm_ffn2": p["norm_ffn2"][l][None, :],
        "ffn1_wi": ffn1_wi, "ffn1_wo": ffn1_wo, "ffn2_wi": ffn2_wi, "ffn2_wo": ffn2_wo,
    }


def kernel(x, c, ctx, c_ctx, w_ada, b_ada, norm_ffn1, ffn1_wi, ffn1_wo, norm_mix, w_in, g_cq, w_uq, g_ckv, w_ukv, g_qa, g_ka, g_qb, g_kb, conv_w, conv_b, w_branch, w_gate, b_gate, w_o, norm_ffn2, ffn2_wi, ffn2_wo):
    p = dict(norm_ffn1=norm_ffn1, ffn1_wi=ffn1_wi, ffn1_wo=ffn1_wo, norm_mix=norm_mix, w_in=w_in,
             g_cq=g_cq, w_uq=w_uq, g_ckv=g_ckv, w_ukv=w_ukv, g_qa=g_qa, g_ka=g_ka, g_qb=g_qb,
             g_kb=g_kb, conv_w=conv_w, conv_b=conv_b, w_branch=w_branch, w_gate=w_gate,
             b_gate=b_gate, w_o=w_o, norm_ffn2=norm_ffn2, ffn2_wi=ffn2_wi, ffn2_wo=ffn2_wo)

    cond8 = jnp.concatenate([c, c_ctx[None, :], jnp.zeros((8 - BATCH - 1, D), F32)], axis=0)
    mod = _ada(cond8, w_ada, b_ada).reshape(DEPTH, 8, N_MOD, 1, D)

    tabs = _rope_tables()
    cc, sc = _dft_mats(FGD)
    ccs = jnp.concatenate([cc, sc], axis=1).astype(BF16)
    dft_lat = tuple(m.astype(BF16) for m in _dft_mats_seq())
    dft_ctx = tuple(m.astype(BF16) for m in _dft_mats(CTX))

    h = jnp.concatenate([x.reshape(T_LAT, D), ctx.reshape(T_CTX, D)], axis=0)
    for l in range(DEPTH):
        last = l == DEPTH - 1
        lw = _layer_weights(l, p)
        h = _ffn(h, mod, l, 0, lw["norm_ffn1"], lw["ffn1_wi"], lw["ffn1_wo"], T_ALL)
        qa, ka, va, qb, kb, vb, cbo, u, pc, ps = _proj(h, mod, l, lw["norm_mix"], lw, tabs, ccs)
        oa, oa_c = _attention(qa, ka, va, QK_A_PAD, V_A, 1, not last)
        ob, ob_c = _attention(qb, kb, vb, HD_B, HD_B, GROUP_B, not last)
        of, of_c = _fourier(pc, ps, dft_lat, dft_ctx, not last)
        n_rows = T_LAT if last else T_ALL
        h = _merge(h, mod, l, lw["norm_mix"], (oa, ob, of), None if last else (oa_c, ob_c, of_c),
                   cbo, u, lw, n_rows)
        h = _ffn(h, mod, l, 6, lw["norm_ffn2"], lw["ffn2_wi"], lw["ffn2_wo"], n_rows)
    return h.reshape(BATCH, SEQ, D)
```

```python
import functools
import math

import jax
import jax.numpy as jnp
from jax import lax
from jax.experimental import pallas as pl
from jax.experimental.pallas import tpu as pltpu

F32 = jnp.float32
BF16 = jnp.bfloat16

D = 2048
BATCH = 4
SEQ = 4096
DEPTH = 4
GRID_W = 64
CTX = 256
N_MOD = 9
F_FF = 5504
F_PAD = 5632
HEADS = 4
Q_LORA = 448
Q_LORA_PAD = 512
KV_LORA = 128
NOPE = 128
ROPE_A = 64
QK_A = NOPE + ROPE_A
QK_A_PAD = 256
V_A = 128
HD_B = 128
KVH_B = 2
GROUP_B = 2
CONV_W = 512
FG = 4
FGD = 128
BR_W = 512
N_BR = 4
THETA = 10000.0
EPS = 1e-6
LOG2_E = math.log2(math.e)

T_LAT = BATCH * SEQ
T_CTX = BATCH * CTX
T_ALL = T_LAT + T_CTX

W_IN_PAD = 3840
OFF_A = 0
OFF_B = 768
OFF_C = 1792
OFF_F = 3328

VMEM_PHYS_V7X = 64 * 1024 * 1024
VMEM_LIMIT = 60 * 1024 * 1024

TM_FFN = 1024
TF_FFN = 512
TM_PROJ = 512
TM_MERGE = 512
TN_MERGE = 512
TQ_ATT = 1024
KC_ATT = 512
TR_DFT = 512
TN_ADA = 1024


def _cparams(sem):
    return pltpu.CompilerParams(dimension_semantics=sem, vmem_limit_bytes=VMEM_LIMIT)


def _mod_row(i, tiles_per_batch):
    return jnp.minimum(i // tiles_per_batch, BATCH)


def _row_rsqrt(x):
    return lax.rsqrt(jnp.mean(x * x, axis=-1, keepdims=True) + EPS)


def _scale_mod(x, r, gain, shift, scale):
    return (x * r * gain) * (1.0 + scale) + shift


def _norm_mod(x, gain, shift, scale):
    return _scale_mod(x, _row_rsqrt(x), gain, shift, scale)


def _silu(x):
    return x * jax.nn.sigmoid(x)


def _ada_kernel(cond_ref, w_ref, b_ref, o_ref):
    s = _silu(cond_ref[...]).astype(BF16)
    o_ref[...] = jnp.dot(s, w_ref[...].astype(BF16), preferred_element_type=F32) + b_ref[...]


def _ada(cond8, w_ada, b_ada):
    n = N_MOD * D
    return pl.pallas_call(
        _ada_kernel,
        grid=(DEPTH, n // TN_ADA),
        in_specs=[
            pl.BlockSpec((8, D), lambda l, j: (0, 0)),
            pl.BlockSpec((None, D, TN_ADA), lambda l, j: (l, 0, j)),
            pl.BlockSpec((None, 1, TN_ADA), lambda l, j: (l, 0, j)),
        ],
        out_specs=pl.BlockSpec((None, 8, TN_ADA), lambda l, j: (l, 0, j)),
        out_shape=jax.ShapeDtypeStruct((DEPTH, 8, n), F32),
        compiler_params=_cparams(("arbitrary", "arbitrary")),
        name="ada",
    )(cond8, w_ada, b_ada.reshape(DEPTH, 1, n))


def _mod_spec(layer, j, tiles_per_batch):
    return pl.BlockSpec((None, None, None, 1, D),
                        lambda i, *_: (layer, _mod_row(i, tiles_per_batch), j, 0, 0))


def _ffn_kernel(x_ref, gain_ref, sh_ref, sc_ref, gt_ref, wg_ref, wu_ref, wo_ref, o_ref, r_ref):
    f = pl.program_id(1)
    last = pl.num_programs(1) - 1

    def chunk(r):
        xn = _scale_mod(x_ref[...], r, gain_ref[...], sh_ref[...], sc_ref[...]).astype(BF16)
        g = jnp.dot(xn, wg_ref[...], preferred_element_type=F32)
        u = jnp.dot(xn, wu_ref[...], preferred_element_type=F32)
        a = (_silu(g) * u).astype(BF16)
        return jnp.dot(a, wo_ref[...], preferred_element_type=F32)

    @pl.when(f == 0)
    def _():
        r = _row_rsqrt(x_ref[...])
        r_ref[...] = r
        o_ref[...] = chunk(r)

    @pl.when(jnp.logical_and(f > 0, f < last))
    def _():
        o_ref[...] += chunk(r_ref[...])

    @pl.when(f == last)
    def _():
        o_ref[...] = x_ref[...] + (0.5 * gt_ref[...]) * (o_ref[...] + chunk(r_ref[...]))


def _ffn(h, mod, layer, j0, gain, wi2, wo, n_rows):
    tm, tf = TM_FFN, TF_FFN
    tpb = SEQ // tm
    return pl.pallas_call(
        _ffn_kernel,
        grid=(n_rows // tm, F_PAD // tf),
        in_specs=[
            pl.BlockSpec((tm, D), lambda i, f: (i, 0)),
            pl.BlockSpec((1, D), lambda i, f: (0, 0)),
            _mod_spec(layer, j0, tpb),
            _mod_spec(layer, j0 + 1, tpb),
            _mod_spec(layer, j0 + 2, tpb),
            pl.BlockSpec((None, D, tf), lambda i, f: (0, 0, f)),
            pl.BlockSpec((None, D, tf), lambda i, f: (1, 0, f)),
            pl.BlockSpec((tf, D), lambda i, f: (f, 0)),
        ],
        out_specs=pl.BlockSpec((tm, D), lambda i, f: (i, 0)),
        out_shape=jax.ShapeDtypeStruct((n_rows, D), F32),
        scratch_shapes=[pltpu.VMEM((tm, 1), F32)],
        compiler_params=_cparams(("arbitrary", "arbitrary")),
        name="ffn",
    )(h, gain, mod, mod, mod, wi2, wi2, wo)


def _rope_a(v, c, s1, s2):
    return v * c + pltpu.roll(v, 96, 1) * s1 + pltpu.roll(v, 32, 1) * s2


def _rope_b(v, c, s):
    return v * c + pltpu.roll(v, 64, 1) * s


def _proj_kernel(h_ref, gain_ref, sh_ref, sc_ref, w_ref, gcq_ref, wuq_ref, gckv_ref, wukv_ref,
                 gqa_ref, gkan_ref, gkap_ref, gqb_ref, gkb_ref, ca_ref, s1a_ref, s2a_ref,
                 cb_ref, sb_ref, ccs_ref,
                 qa_ref, ka_ref, va_ref, qb_ref, kb_ref, vb_ref, cbo_ref, u_ref, pc_ref, ps_ref):
    xn = _norm_mod(h_ref[...], gain_ref[...], sh_ref[...], sc_ref[...]).astype(BF16)
    ca, s1a, s2a = ca_ref[...], s1a_ref[...], s2a_ref[...]
    cb, sb = cb_ref[...], sb_ref[...]
    scale_a = QK_A ** -0.5 * LOG2_E
    scale_b = HD_B ** -0.5 * LOG2_E

    za = jnp.dot(xn, w_ref[:, OFF_A:OFF_B], preferred_element_type=F32)
    aq, akv, ape = za[:, :Q_LORA_PAD], za[:, Q_LORA_PAD:Q_LORA_PAD + KV_LORA], za[:, Q_LORA_PAD + KV_LORA:]
    r = lax.rsqrt(jnp.sum(aq * aq, axis=-1, keepdims=True) * (1.0 / Q_LORA) + EPS)
    cq = (aq * r * gcq_ref[...]).astype(BF16)
    q = jnp.dot(cq, wuq_ref[...], preferred_element_type=F32)
    gqa = gqa_ref[...]
    for hd in range(HEADS):
        lo = hd * QK_A_PAD
        qh = q[:, lo:lo + QK_A_PAD]
        r = lax.rsqrt(jnp.sum(qh * qh, axis=-1, keepdims=True) * (1.0 / QK_A) + EPS)
        qh = qh * r * gqa
        qa_ref[:, lo:lo + NOPE] = (qh[:, :NOPE] * scale_a).astype(BF16)
        qa_ref[:, lo + NOPE:lo + QK_A_PAD] = (_rope_a(qh[:, NOPE:], ca, s1a, s2a) * scale_a).astype(BF16)
    r = lax.rsqrt(jnp.mean(akv * akv, axis=-1, keepdims=True) + EPS)
    ckv = (akv * r * gckv_ref[...]).astype(BF16)
    kv = jnp.dot(ckv, wukv_ref[...], preferred_element_type=F32)
    ss_pe = jnp.sum(ape * ape, axis=-1, keepdims=True)
    gkan, gkap = gkan_ref[...], gkap_ref[...]
    for hd in range(HEADS):
        kn = kv[:, hd * NOPE:(hd + 1) * NOPE]
        r = lax.rsqrt((jnp.sum(kn * kn, axis=-1, keepdims=True) + ss_pe) * (1.0 / QK_A) + EPS)
        lo = hd * QK_A_PAD
        ka_ref[:, lo:lo + NOPE] = (kn * r * gkan).astype(BF16)
        ka_ref[:, lo + NOPE:lo + QK_A_PAD] = _rope_a(ape * r * gkap, ca, s1a, s2a).astype(BF16)
        va_ref[:, hd * V_A:(hd + 1) * V_A] = kv[:, HEADS * NOPE + hd * V_A:HEADS * NOPE + (hd + 1) * V_A].astype(BF16)

    zb = jnp.dot(xn, w_ref[:, OFF_B:OFF_C], preferred_element_type=F32)
    gqb, gkb = gqb_ref[...], gkb_ref[...]
    for hd in range(HEADS):
        qh = zb[:, hd * HD_B:(hd + 1) * HD_B]
        r = lax.rsqrt(jnp.mean(qh * qh, axis=-1, keepdims=True) + EPS)
        qb_ref[:, hd * HD_B:(hd + 1) * HD_B] = (_rope_b(qh * r * gqb, cb, sb) * scale_b).astype(BF16)
    for hd in range(KVH_B):
        lo = HEADS * HD_B + hd * HD_B
        kh = zb[:, lo:lo + HD_B]
        r = lax.rsqrt(jnp.mean(kh * kh, axis=-1, keepdims=True) + EPS)
        kb_ref[:, hd * HD_B:(hd + 1) * HD_B] = _rope_b(kh * r * gkb, cb, sb).astype(BF16)
    vb_ref[...] = zb[:, (HEADS + KVH_B) * HD_B:].astype(BF16)

    zc = jnp.dot(xn, w_ref[:, OFF_C:OFF_F], preferred_element_type=F32)
    cbo_ref[...] = zc[:, :CONV_W]
    u_ref[...] = zc[:, CONV_W:2 * CONV_W] * zc[:, 2 * CONV_W:]

    zf = jnp.dot(xn, w_ref[:, OFF_F:], preferred_element_type=F32).astype(BF16)
    ccs = ccs_ref[...]
    for g in range(FG):
        pp = jnp.dot(zf[:, g * FGD:(g + 1) * FGD], ccs, preferred_element_type=F32)
        pc_ref[:, g * FGD:(g + 1) * FGD] = pp[:, :FGD].astype(BF16)
        ps_ref[:, g * FGD:(g + 1) * FGD] = pp[:, FGD:].astype(BF16)


def _const_spec(shape):
    nd = len(shape)
    return pl.BlockSpec(shape, lambda i, *_: (0,) * nd, pipeline_mode=pl.Buffered(1))


def _proj(h, mod, layer, gain, lw, tabs, ccs):
    tm = TM_PROJ
    tpb = SEQ // tm
    n_lat = T_LAT // tm

    def tab_spec():
        return pl.BlockSpec((tm, 128), lambda i: (jnp.where(i < n_lat, i % tpb, tpb), 0))

    def row_spec(w):
        return pl.BlockSpec((tm, w), lambda i: (i, 0))

    widths = (HEADS * QK_A_PAD, HEADS * QK_A_PAD, HEADS * V_A, HEADS * HD_B, KVH_B * HD_B,
              KVH_B * HD_B, CONV_W, CONV_W, FG * FGD, FG * FGD)
    dtypes = (BF16, BF16, BF16, BF16, BF16, BF16, F32, F32, BF16, BF16)
    return pl.pallas_call(
        _proj_kernel,
        grid=(T_ALL // tm,),
        in_specs=[
            row_spec(D),
            _const_spec((1, D)),
            _mod_spec(layer, 3, tpb),
            _mod_spec(layer, 4, tpb),
            _const_spec((D, W_IN_PAD)),
            _const_spec((1, Q_LORA_PAD)),
            _const_spec((Q_LORA_PAD, HEADS * QK_A_PAD)),
            _const_spec((1, KV_LORA)),
            _const_spec((KV_LORA, HEADS * (NOPE + V_A))),
            _const_spec((1, QK_A_PAD)),
            _const_spec((1, NOPE)),
            _const_spec((1, 128)),
            _const_spec((1, HD_B)),
            _const_spec((1, HD_B)),
            tab_spec(), tab_spec(), tab_spec(), tab_spec(), tab_spec(),
            _const_spec((FGD, 2 * FGD)),
        ],
        out_specs=[row_spec(w) for w in widths],
        out_shape=[jax.ShapeDtypeStruct((T_ALL, w), dt) for w, dt in zip(widths, dtypes)],
        compiler_params=_cparams(("arbitrary",)),
        name="proj",
    )(h, gain, mod, mod, lw["w_in"], lw["g_cq"], lw["w_uq"], lw["g_ckv"], lw["w_ukv"],
      lw["g_qa"], lw["g_ka_nope"], lw["g_ka_pe"], lw["g_qb"], lw["g_kb"], *tabs, ccs)


_QK_DIMS = (((1,), (1,)), ((), ()))
_TN_DIMS = (((0,), (0,)), ((), ()))


def _att_lat_kernel(q_ref, k_ref, kc_ref, v_ref, vc_ref, o_ref):
    q = q_ref[...]
    chunks = [(k_ref, v_ref, j * KC_ATT, KC_ATT) for j in range(SEQ // KC_ATT)] + [(kc_ref, vc_ref, 0, CTX)]

    def scores_t(chunk):
        kr, _, lo, n = chunk
        return lax.dot_general(kr[lo:lo + n, :], q, _QK_DIMS, preferred_element_type=F32)

    st = scores_t(chunks[0])
    m = l = acc = None
    for idx, (_, vr, lo, n) in enumerate(chunks):
        st_next = scores_t(chunks[idx + 1]) if idx + 1 < len(chunks) else None
        m_chunk = jnp.max(st, axis=0, keepdims=True)
        m_new = m_chunk if m is None else jnp.maximum(m, m_chunk)
        p = jnp.exp2(st - m_new)
        pv = lax.dot_general(vr[lo:lo + n, :], p.astype(BF16), _TN_DIMS, preferred_element_type=F32)
        if m is None:
            l = jnp.sum(p, axis=0, keepdims=True)
            acc = pv
        else:
            alpha = jnp.exp2(m - m_new)
            l = alpha * l + jnp.sum(p, axis=0, keepdims=True)
            acc = alpha * acc + pv
        m, st = m_new, st_next
    o_ref[...] = (acc / l).T.astype(o_ref.dtype)


def _att_ctx_kernel(q_ref, k_ref, v_ref, o_ref):
    s = lax.dot_general(q_ref[...], k_ref[...], _QK_DIMS, preferred_element_type=F32)
    m = jnp.max(s, axis=-1, keepdims=True)
    p = jnp.exp2(s - m)
    l = jnp.sum(p, axis=-1, keepdims=True)
    o = jnp.dot(p.astype(BF16), v_ref[...], preferred_element_type=F32)
    o_ref[...] = (o / l).astype(o_ref.dtype)


def _attention(q, k, v, dq, dv, group, with_ctx):
    tq = TQ_ATT
    nq = SEQ // tq
    ctx0 = T_LAT // CTX
    out = pl.pallas_call(
        _att_lat_kernel,
        grid=(BATCH, HEADS, nq),
        in_specs=[
            pl.BlockSpec((tq, dq), lambda b, h, i: (b * nq + i, h)),
            pl.BlockSpec((SEQ, dq), lambda b, h, i: (b, h // group)),
            pl.BlockSpec((CTX, dq), lambda b, h, i: (ctx0 + b, h // group)),
            pl.BlockSpec((SEQ, dv), lambda b, h, i: (b, h // group)),
            pl.BlockSpec((CTX, dv), lambda b, h, i: (ctx0 + b, h // group)),
        ],
        out_specs=pl.BlockSpec((tq, dv), lambda b, h, i: (b * nq + i, h)),
        out_shape=jax.ShapeDtypeStruct((T_LAT, HEADS * dv), BF16),
        compiler_params=_cparams(("arbitrary", "arbitrary", "arbitrary")),
        name="att_lat",
    )(q, k, k, v, v)
    if not with_ctx:
        return out, None
    out_ctx = pl.pallas_call(
        _att_ctx_kernel,
        grid=(BATCH, HEADS),
        in_specs=[
            pl.BlockSpec((CTX, dq), lambda b, h: (ctx0 + b, h)),
            pl.BlockSpec((CTX, dq), lambda b, h: (ctx0 + b, h // group)),
            pl.BlockSpec((CTX, dv), lambda b, h: (ctx0 + b, h // group)),
        ],
        out_specs=pl.BlockSpec((CTX, dv), lambda b, h: (b, h)),
        out_shape=jax.ShapeDtypeStruct((T_CTX, HEADS * dv), BF16),
        compiler_params=_cparams(("arbitrary", "arbitrary")),
        name="att_ctx",
    )(q, k, v)
    return out, out_ctx


def _dft_kernel(scale, c_ref, s_ref, pc_ref, ps_ref, o_ref):
    y = (jnp.dot(c_ref[...], pc_ref[...], preferred_element_type=F32)
         - jnp.dot(s_ref[...], ps_ref[...], preferred_element_type=F32))
    o_ref[...] = (y * scale).astype(o_ref.dtype)


def _fourier(pc, ps, dft_lat, dft_ctx, with_ctx):
    tr = TR_DFT
    nr = SEQ // tr
    w = FG * FGD
    out = pl.pallas_call(
        functools.partial(_dft_kernel, (SEQ * FGD) ** -0.5),
        grid=(BATCH, nr),
        in_specs=[
            pl.BlockSpec((tr, SEQ), lambda b, r: (r, 0)),
            pl.BlockSpec((tr, SEQ), lambda b, r: (r, 0)),
            pl.BlockSpec((SEQ, w), lambda b, r: (b, 0)),
            pl.BlockSpec((SEQ, w), lambda b, r: (b, 0)),
        ],
        out_specs=pl.BlockSpec((tr, w), lambda b, r: (b * nr + r, 0)),
        out_shape=jax.ShapeDtypeStruct((T_LAT, w), BF16),
        compiler_params=_cparams(("arbitrary", "arbitrary")),
        name="dft_lat",
    )(dft_lat[0], dft_lat[1], pc, ps)
    if not with_ctx:
        return out, None
    ctx0 = T_LAT // CTX
    out_ctx = pl.pallas_call(
        functools.partial(_dft_kernel, (CTX * FGD) ** -0.5),
        grid=(BATCH,),
        in_specs=[
            pl.BlockSpec((CTX, CTX), lambda b: (0, 0)),
            pl.BlockSpec((CTX, CTX), lambda b: (0, 0)),
            pl.BlockSpec((CTX, w), lambda b: (ctx0 + b, 0)),
            pl.BlockSpec((CTX, w), lambda b: (ctx0 + b, 0)),
        ],
        out_specs=pl.BlockSpec((CTX, w), lambda b: (b, 0)),
        out_shape=jax.ShapeDtypeStruct((T_CTX, w), BF16),
        compiler_params=_cparams(("arbitrary",)),
        name="dft_ctx",
    )(dft_ctx[0], dft_ctx[1], pc, ps)
    return out, out_ctx


def _merge_kernel(n_lat_tiles, with_ctx, h_ref, gain_ref, sh_ref, sc_ref, gt_ref, *refs):
    lat_refs, refs = refs[:3], refs[3:]
    if with_ctx:
        ctx_refs, refs = refs[:3], refs[3:]
    (cbo_ref, u_ref, up_ref, un_ref, cw_ref, cbias_ref, wg_ref, bg_ref, wb_ref, wo_ref,
     o_ref, r_ref) = refs
    i = pl.program_id(0)
    n = pl.program_id(1)
    last = pl.num_programs(1) - 1
    tm = h_ref.shape[0]

    def chunk(r):
        xn = _scale_mod(h_ref[...], r, gain_ref[...], sh_ref[...], sc_ref[...]).astype(BF16)

        u = u_ref[...]
        is_lat = i < n_lat_tiles
        seq_len = jnp.where(is_lat, SEQ, CTX)
        ridx = lax.broadcasted_iota(jnp.int32, (tm, 1), 0)
        pos = (i * tm + ridx) & (seq_len - 1)
        prev = jnp.where(ridx == 0, up_ref[7:8, :], pltpu.roll(u, 1, 0))
        prev = jnp.where(pos == 0, 0.0, prev)
        nxt = jnp.where(ridx == tm - 1, un_ref[0:1, :], pltpu.roll(u, tm - 1, 0))
        nxt = jnp.where(pos == seq_len - 1, 0.0, nxt)
        y = prev * cw_ref[0:1, :] + u * cw_ref[1:2, :] + nxt * cw_ref[2:3, :] + cbias_ref[...]
        conv = (cbo_ref[...] * y).astype(BF16)

        if with_ctx:
            oa, ob, of = (jnp.where(is_lat, lr[...], cr[...]) for lr, cr in zip(lat_refs, ctx_refs))
        else:
            oa, ob, of = (lr[...] for lr in lat_refs)

        merged = None
        for k, br in enumerate((oa, ob, conv, of)):
            gate = jax.nn.sigmoid(jnp.dot(xn, wg_ref[k], preferred_element_type=F32) + bg_ref[k])
            term = gate * jnp.dot(br, wb_ref[k], preferred_element_type=F32)
            merged = term if merged is None else merged + term
        return jnp.dot(merged.astype(BF16), wo_ref[...], preferred_element_type=F32)

    @pl.when(n == 0)
    def _():
        r = _row_rsqrt(h_ref[...])
        r_ref[...] = r
        o_ref[...] = chunk(r)

    @pl.when(jnp.logical_and(n > 0, n < last))
    def _():
        o_ref[...] += chunk(r_ref[...])

    @pl.when(n == last)
    def _():
        o_ref[...] = h_ref[...] + gt_ref[...] * (o_ref[...] + chunk(r_ref[...]))


def _merge(h, mod, layer, gain, br_lat, br_ctx, cbo, u, lw, n_rows):
    tm, tn = TM_MERGE, TN_MERGE
    tpb = SEQ // tm
    n_lat = T_LAT // tm
    with_ctx = br_ctx is not None
    last_blk = T_ALL // 8 - 1
    br_specs = [pl.BlockSpec((tm, BR_W), lambda i, n: (jnp.minimum(i, n_lat - 1), 0))] * 3
    if with_ctx:
        br_specs += [pl.BlockSpec((tm, BR_W), lambda i, n: (jnp.maximum(i - n_lat, 0), 0))] * 3

    def row_spec(w):
        return pl.BlockSpec((tm, w), lambda i, n: (i, 0))

    return pl.pallas_call(
        functools.partial(_merge_kernel, n_lat, with_ctx),
        grid=(n_rows // tm, D // tn),
        in_specs=[
            row_spec(D),
            pl.BlockSpec((1, D), lambda i, n: (0, 0)),
            _mod_spec(layer, 3, tpb),
            _mod_spec(layer, 4, tpb),
            _mod_spec(layer, 5, tpb),
            *br_specs,
            row_spec(CONV_W), row_spec(CONV_W),
            pl.BlockSpec((8, CONV_W), lambda i, n: (jnp.maximum(i * (tm // 8) - 1, 0), 0)),
            pl.BlockSpec((8, CONV_W), lambda i, n: (jnp.minimum((i + 1) * (tm // 8), last_blk), 0)),
            pl.BlockSpec((3, CONV_W), lambda i, n: (0, 0)),
            pl.BlockSpec((1, CONV_W), lambda i, n: (0, 0)),
            pl.BlockSpec((N_BR, D, tn), lambda i, n: (0, 0, n)),
            pl.BlockSpec((N_BR, 1, tn), lambda i, n: (0, 0, n)),
            pl.BlockSpec((N_BR, BR_W, tn), lambda i, n: (0, 0, n)),
            pl.BlockSpec((tn, D), lambda i, n: (n, 0)),
        ],
        out_specs=row_spec(D),
        out_shape=jax.ShapeDtypeStruct((n_rows, D), F32),
        scratch_shapes=[pltpu.VMEM((tm, 1), F32)],
        compiler_params=_cparams(("arbitrary", "arbitrary")),
        name="merge",
    )(h, gain, mod, mod, mod, *br_lat, *(br_ctx or ()), cbo, u, u, u, lw["conv_w"], lw["conv_b"],
      lw["w_gate"], lw["b_gate"], lw["w_branch"], lw["w_o"])


def _rope_tables():
    t = jnp.arange(SEQ, dtype=jnp.int32)
    pos_row = (t // GRID_W).astype(F32)
    pos_col = (t % GRID_W).astype(F32)

    def ang(rot_dim):
        n = rot_dim // 4
        inv_freq = THETA ** (-jnp.arange(n, dtype=F32) / n)
        a = jnp.concatenate([pos_row[:, None] * inv_freq, pos_col[:, None] * inv_freq], axis=-1)
        return jnp.cos(a), jnp.sin(a)

    ca, sa = ang(ROPE_A)
    z32 = jnp.zeros_like(ca)
    z64 = jnp.zeros((SEQ, 64), F32)
    tab_c = jnp.concatenate([ca, ca, z64], axis=-1)
    tab_s1 = jnp.concatenate([-sa, z32, z64], axis=-1)
    tab_s2 = jnp.concatenate([z32, sa, z64], axis=-1)
    cb, sb = ang(HD_B)
    tab_cb = jnp.concatenate([cb, cb], axis=-1)
    tab_sb = jnp.concatenate([-sb, sb], axis=-1)
    ones = jnp.ones((TM_PROJ, 128), F32)
    zeros = jnp.zeros((TM_PROJ, 128), F32)
    return tuple(jnp.concatenate([tab, ident], axis=0)
                 for tab, ident in ((tab_c, ones), (tab_s1, zeros), (tab_s2, zeros),
                                    (tab_cb, ones), (tab_sb, zeros)))


def _dft_mats(n):
    j = jnp.arange(n, dtype=jnp.int32)
    a = ((j[:, None] * j[None, :]) % n).astype(F32) * (2.0 * math.pi / n)
    return jnp.cos(a), jnp.sin(a)


def _dft_mats_seq():
    k = jnp.arange(SEQ, dtype=jnp.int32)[None, :]
    j = jnp.arange(GRID_W, dtype=jnp.int32)[:, None]
    a = ((j * k) % GRID_W).astype(F32) * (2.0 * math.pi / GRID_W)
    b = ((j * k) % SEQ).astype(F32) * (2.0 * math.pi / SEQ)
    ca, sa, cb, sb = jnp.cos(a)[:, None, :], jnp.sin(a)[:, None, :], jnp.cos(b)[None], jnp.sin(b)[None]
    return ((ca * cb - sa * sb).reshape(SEQ, SEQ), (sa * cb + ca * sb).reshape(SEQ, SEQ))


def _pad_to(a, axis, size):
    pad = [(0, 0)] * a.ndim
    pad[axis] = (0, size - a.shape[axis])
    return jnp.pad(a, pad)


def _layer_weights(l, p):
    w_in = p["w_in"][l]
    idx, off = [], 0
    for wd in (Q_LORA, KV_LORA, ROPE_A, HEADS * HD_B, KVH_B * HD_B, KVH_B * HD_B, CONV_W, CONV_W,
               CONV_W, FG * FGD):
        idx.append((off, wd))
        off += wd
    segs = [lax.slice_in_dim(w_in, o, o + wd, axis=1) for o, wd in idx]
    segs[0] = _pad_to(segs[0], 1, Q_LORA_PAD)
    segs[2] = _pad_to(segs[2], 1, 128)
    w_in_p = jnp.concatenate(segs, axis=1).astype(BF16)

    w_uq = p["w_uq"][l].reshape(Q_LORA, HEADS, QK_A)
    w_uq = _pad_to(_pad_to(w_uq, 2, QK_A_PAD), 0, Q_LORA_PAD).reshape(Q_LORA_PAD, HEADS * QK_A_PAD)
    w_ukv = p["w_ukv"][l].reshape(KV_LORA, HEADS, NOPE + V_A)
    w_ukv = jnp.concatenate([w_ukv[:, :, :NOPE].reshape(KV_LORA, HEADS * NOPE),
                             w_ukv[:, :, NOPE:].reshape(KV_LORA, HEADS * V_A)], axis=1)
    g_ka = p["g_ka"][l]

    def ffn_w(wi, wo):
        wi2 = jnp.stack([_pad_to(wi[:, :F_FF], 1, F_PAD), _pad_to(wi[:, F_FF:], 1, F_PAD)]).astype(BF16)
        return wi2, _pad_to(wo, 0, F_PAD).astype(BF16)

    ffn1_wi, ffn1_wo = ffn_w(p["ffn1_wi"][l], p["ffn1_wo"][l])
    ffn2_wi, ffn2_wo = ffn_w(p["ffn2_wi"][l], p["ffn2_wo"][l])
    return {
        "w_in": w_in_p,
        "g_cq": _pad_to(p["g_cq"][l], 0, Q_LORA_PAD)[None, :],
        "w_uq": w_uq.astype(BF16),
        "g_ckv": p["g_ckv"][l][None, :],
        "w_ukv": w_ukv.astype(BF16),
        "g_qa": _pad_to(p["g_qa"][l], 0, QK_A_PAD)[None, :],
        "g_ka_nope": g_ka[None, :NOPE],
        "g_ka_pe": _pad_to(g_ka[NOPE:], 0, 128)[None, :],
        "g_qb": p["g_qb"][l][None, :],
        "g_kb": p["g_kb"][l][None, :],
        "conv_w": p["conv_w"][l],
        "conv_b": p["conv_b"][l][None, :],
        "w_gate": p["w_gate"][l].astype(BF16),
        "b_gate": p["b_gate"][l][:, None, :],
        "w_branch": p["w_branch"][l].astype(BF16),
        "w_o": p["w_o"][l].astype(BF16),
        "norm_ffn1": p["norm_ffn1"][l][None, :],
        "norm_mix": p["norm_mix"][l][None, :],
        "norm_ffn2": p["norm_ffn2"][l][None, :],
        "ffn1_wi": ffn1_wi, "ffn1_wo": ffn1_wo, "ffn2_wi": ffn2_wi, "ffn2_wo": ffn2_wo,
    }


def kernel(x, c, ctx, c_ctx, w_ada, b_ada, norm_ffn1, ffn1_wi, ffn1_wo, norm_mix, w_in, g_cq, w_uq, g_ckv, w_ukv, g_qa, g_ka, g_qb, g_kb, conv_w, conv_b, w_branch, w_gate, b_gate, w_o, norm_ffn2, ffn2_wi, ffn2_wo):
    p = dict(norm_ffn1=norm_ffn1, ffn1_wi=ffn1_wi, ffn1_wo=ffn1_wo, norm_mix=norm_mix, w_in=w_in,
             g_cq=g_cq, w_uq=w_uq, g_ckv=g_ckv, w_ukv=w_ukv, g_qa=g_qa, g_ka=g_ka, g_qb=g_qb,
             g_kb=g_kb, conv_w=conv_w, conv_b=conv_b, w_branch=w_branch, w_gate=w_gate,
             b_gate=b_gate, w_o=w_o, norm_ffn2=norm_ffn2, ffn2_wi=ffn2_wi, ffn2_wo=ffn2_wo)

    cond8 = jnp.concatenate([c, c_ctx[None, :], jnp.zeros((8 - BATCH - 1, D), F32)], axis=0)
    mod = _ada(cond8, w_ada, b_ada).reshape(DEPTH, 8, N_MOD, 1, D)

    tabs = _rope_tables()
    cc, sc = _dft_mats(FGD)
    ccs = jnp.concatenate([cc, sc], axis=1).astype(BF16)
    dft_lat = tuple(m.astype(BF16) for m in _dft_mats_seq())
    dft_ctx = tuple(m.astype(BF16) for m in _dft_mats(CTX))

    h = jnp.concatenate([x.reshape(T_LAT, D), ctx.reshape(T_CTX, D)], axis=0)
    for l in range(DEPTH):
        last = l == DEPTH - 1
        lw = _layer_weights(l, p)
        h = _ffn(h, mod, l, 0, lw["norm_ffn1"], lw["ffn1_wi"], lw["ffn1_wo"], T_ALL)
        qa, ka, va, qb, kb, vb, cbo, u, pc, ps = _proj(h, mod, l, lw["norm_mix"], lw, tabs, ccs)
        oa, oa_c = _attention(qa, ka, va, QK_A_PAD, V_A, 1, not last)
        ob, ob_c = _attention(qb, kb, vb, HD_B, HD_B, GROUP_B, not last)
        of, of_c = _fourier(pc, ps, dft_lat, dft_ctx, not last)
        n_rows = T_LAT if last else T_ALL
        h = _merge(h, mod, l, lw["norm_mix"], (oa, ob, of), None if last else (oa_c, ob_c, of_c),
                   cbo, u, lw, n_rows)
        h = _ffn(h, mod, l, 6, lw["norm_ffn2"], lw["ffn2_wi"], lw["ffn2_wo"], n_rows)
    return h.reshape(BATCH, SEQ, D)
```

```python
import functools
import math

import jax
import jax.numpy as jnp
from jax import lax
from jax.experimental import pallas as pl
from jax.experimental.pallas import tpu as pltpu

F32 = jnp.float32
BF16 = jnp.bfloat16

D = 2048
BATCH = 4
SEQ = 4096
DEPTH = 4
GRID_W = 64
CTX = 256
N_MOD = 9
F_FF = 5504
HEADS = 4
Q_LORA = 448
Q_LORA_PAD = 512
KV_LORA = 128
NOPE = 128
ROPE_A = 64
QK_A = NOPE + ROPE_A
QK_A_PAD = 256
V_A = 128
HD_B = 128
KVH_B = 2
GROUP_B = 2
CONV_W = 512
FG = 4
FGD = 128
BR_W = 512
N_BR = 4
THETA = 10000.0
EPS = 1e-6
LOG2_E = math.log2(math.e)

T_LAT = BATCH * SEQ
T_CTX = BATCH * CTX
T_ALL = T_LAT + T_CTX

W_IN_PAD = 3840
OFF_A = 0
OFF_B = 768
OFF_C = 1792
OFF_F = 3328

LANE = 128
VMEM_PHYS_V7X = 64 * 1024 * 1024
VMEM_LIMIT = 60 * 1024 * 1024

TM_FFN = 1024
TF_FFN = 512
TM_PROJ = 512
TM_MERGE = 512
TN_MERGE = 512
TQ_ATT = 512
KC_ATT = 512
TR_DFT = 512
TN_ADA = 1024


def _cparams(sem):
    return pltpu.CompilerParams(dimension_semantics=sem, vmem_limit_bytes=VMEM_LIMIT)


def _mod_row(i, tiles_per_batch):
    return jnp.minimum(i // tiles_per_batch, BATCH)


def _row_rsqrt(x):
    return lax.rsqrt(jnp.mean(x * x, axis=-1, keepdims=True) + EPS)


def _scale_mod(x, r, gain, shift, scale):
    return (x * r * gain) * (1.0 + scale) + shift


def _norm_mod(x, gain, shift, scale):
    return _scale_mod(x, _row_rsqrt(x), gain, shift, scale)


def _silu(x):
    return x * jax.nn.sigmoid(x)


def _ada_kernel(cond_ref, w_ref, b_ref, o_ref):
    s = _silu(cond_ref[...]).astype(BF16)
    o_ref[...] = jnp.dot(s, w_ref[...].astype(BF16), preferred_element_type=F32) + b_ref[...]


def _ada(cond8, w_ada, b_ada):
    n = N_MOD * D
    return pl.pallas_call(
        _ada_kernel,
        grid=(DEPTH, n // TN_ADA),
        in_specs=[
            pl.BlockSpec((8, D), lambda l, j: (0, 0)),
            pl.BlockSpec((None, D, TN_ADA), lambda l, j: (l, 0, j)),
            pl.BlockSpec((None, 1, TN_ADA), lambda l, j: (l, 0, j)),
        ],
        out_specs=pl.BlockSpec((None, 8, TN_ADA), lambda l, j: (l, 0, j)),
        out_shape=jax.ShapeDtypeStruct((DEPTH, 8, n), F32),
        compiler_params=_cparams(("arbitrary", "arbitrary")),
        name="ada",
    )(cond8, w_ada, b_ada.reshape(DEPTH, 1, n))


def _mod_spec(layer, j, tiles_per_batch):
    return pl.BlockSpec((None, None, None, 1, D),
                        lambda i, *_: (layer, _mod_row(i, tiles_per_batch), j, 0, 0))


def _ffn_kernel(overlap, x_ref, gain_ref, sh_ref, sc_ref, gt_ref, wg_ref, wu_ref, wo_ref, o_ref, r_ref):
    f = pl.program_id(1)
    last = pl.num_programs(1) - 1

    def chunk(r, skip=0):
        xn = _scale_mod(x_ref[...], r, gain_ref[...], sh_ref[...], sc_ref[...]).astype(BF16)
        g = jnp.dot(xn, wg_ref[:, skip:], preferred_element_type=F32)
        u = jnp.dot(xn, wu_ref[:, skip:], preferred_element_type=F32)
        a = (_silu(g) * u).astype(BF16)
        return jnp.dot(a, wo_ref[skip:, :], preferred_element_type=F32)

    @pl.when(f == 0)
    def _():
        r = _row_rsqrt(x_ref[...])
        r_ref[...] = r
        o_ref[...] = chunk(r)

    @pl.when(jnp.logical_and(f > 0, f < last))
    def _():
        o_ref[...] += chunk(r_ref[...])

    @pl.when(f == last)
    def _():
        o_ref[...] = x_ref[...] + (0.5 * gt_ref[...]) * (o_ref[...] + chunk(r_ref[...], overlap))


def _ffn(h, mod, layer, j0, gain, wi, wo, n_rows):
    tm, tf = TM_FFN, TF_FFN
    tpb = SEQ // tm
    nf = pl.cdiv(F_FF, tf)

    def hid(f, base=0):
        return (base // LANE + jnp.minimum(f * (tf // LANE), (F_FF - tf) // LANE)) * LANE

    return pl.pallas_call(
        functools.partial(_ffn_kernel, nf * tf - F_FF),
        grid=(n_rows // tm, nf),
        in_specs=[
            pl.BlockSpec((tm, D), lambda i, f: (i, 0)),
            pl.BlockSpec((None, 1, D), lambda i, f: (layer, 0, 0)),
            _mod_spec(layer, j0, tpb),
            _mod_spec(layer, j0 + 1, tpb),
            _mod_spec(layer, j0 + 2, tpb),
            pl.BlockSpec((None, pl.Element(D), pl.Element(tf)), lambda i, f: (layer, 0, hid(f))),
            pl.BlockSpec((None, pl.Element(D), pl.Element(tf)), lambda i, f: (layer, 0, hid(f, F_FF))),
            pl.BlockSpec((None, pl.Element(tf), pl.Element(D)), lambda i, f: (layer, hid(f), 0)),
        ],
        out_specs=pl.BlockSpec((tm, D), lambda i, f: (i, 0)),
        out_shape=jax.ShapeDtypeStruct((n_rows, D), F32),
        scratch_shapes=[pltpu.VMEM((tm, 1), F32)],
        compiler_params=_cparams(("arbitrary", "arbitrary")),
        name="ffn",
    )(h, gain, mod, mod, mod, wi, wi, wo)


def _rope_a(v, c, s1, s2):
    return v * c + pltpu.roll(v, 96, 1) * s1 + pltpu.roll(v, 32, 1) * s2


def _rope_b(v, c, s):
    return v * c + pltpu.roll(v, 64, 1) * s


def _proj_kernel(h_ref, gain_ref, sh_ref, sc_ref, w_ref, gcq_ref, wuq_ref, gckv_ref, wukv_ref,
                 gqa_ref, gkan_ref, gkap_ref, gqb_ref, gkb_ref, ca_ref, s1a_ref, s2a_ref,
                 cb_ref, sb_ref, ccs_ref,
                 qa_ref, ka_ref, va_ref, qb_ref, kb_ref, vb_ref, cbo_ref, u_ref, pc_ref, ps_ref):
    xn = _norm_mod(h_ref[...], gain_ref[...], sh_ref[...], sc_ref[...]).astype(BF16)
    ca, s1a, s2a = ca_ref[...], s1a_ref[...], s2a_ref[...]
    cb, sb = cb_ref[...], sb_ref[...]
    scale_a = QK_A ** -0.5 * LOG2_E
    scale_b = HD_B ** -0.5 * LOG2_E

    za = jnp.dot(xn, w_ref[:, OFF_A:OFF_B], preferred_element_type=F32)
    aq, akv, ape = za[:, :Q_LORA_PAD], za[:, Q_LORA_PAD:Q_LORA_PAD + KV_LORA], za[:, Q_LORA_PAD + KV_LORA:]
    r = lax.rsqrt(jnp.sum(aq * aq, axis=-1, keepdims=True) * (1.0 / Q_LORA) + EPS)
    cq = (aq * r * gcq_ref[...]).astype(BF16)
    q = jnp.dot(cq, wuq_ref[...], preferred_element_type=F32)
    gqa = gqa_ref[...]
    for hd in range(HEADS):
        lo = hd * QK_A_PAD
        qh = q[:, lo:lo + QK_A_PAD]
        r = lax.rsqrt(jnp.sum(qh * qh, axis=-1, keepdims=True) * (1.0 / QK_A) + EPS)
        qh = qh * r * gqa
        qa_ref[:, lo:lo + NOPE] = (qh[:, :NOPE] * scale_a).astype(BF16)
        qa_ref[:, lo + NOPE:lo + QK_A_PAD] = (_rope_a(qh[:, NOPE:], ca, s1a, s2a) * scale_a).astype(BF16)
    r = lax.rsqrt(jnp.mean(akv * akv, axis=-1, keepdims=True) + EPS)
    ckv = (akv * r * gckv_ref[...]).astype(BF16)
    kv = jnp.dot(ckv, wukv_ref[...], preferred_element_type=F32)
    ss_pe = jnp.sum(ape * ape, axis=-1, keepdims=True)
    gkan, gkap = gkan_ref[...], gkap_ref[...]
    for hd in range(HEADS):
        kn = kv[:, hd * NOPE:(hd + 1) * NOPE]
        r = lax.rsqrt((jnp.sum(kn * kn, axis=-1, keepdims=True) + ss_pe) * (1.0 / QK_A) + EPS)
        lo = hd * QK_A_PAD
        ka_ref[:, lo:lo + NOPE] = (kn * r * gkan).astype(BF16)
        ka_ref[:, lo + NOPE:lo + QK_A_PAD] = _rope_a(ape * r * gkap, ca, s1a, s2a).astype(BF16)
        va_ref[:, hd * V_A:(hd + 1) * V_A] = kv[:, HEADS * NOPE + hd * V_A:HEADS * NOPE + (hd + 1) * V_A].astype(BF16)

    zb = jnp.dot(xn, w_ref[:, OFF_B:OFF_C], preferred_element_type=F32)
    gqb, gkb = gqb_ref[...], gkb_ref[...]
    for hd in range(HEADS):
        qh = zb[:, hd * HD_B:(hd + 1) * HD_B]
        r = lax.rsqrt(jnp.mean(qh * qh, axis=-1, keepdims=True) + EPS)
        qb_ref[:, hd * HD_B:(hd + 1) * HD_B] = (_rope_b(qh * r * gqb, cb, sb) * scale_b).astype(BF16)
    for hd in range(KVH_B):
        lo = HEADS * HD_B + hd * HD_B
        kh = zb[:, lo:lo + HD_B]
        r = lax.rsqrt(jnp.mean(kh * kh, axis=-1, keepdims=True) + EPS)
        kb_ref[:, hd * HD_B:(hd + 1) * HD_B] = _rope_b(kh * r * gkb, cb, sb).astype(BF16)
    vb_ref[...] = zb[:, (HEADS + KVH_B) * HD_B:].astype(BF16)

    zc = jnp.dot(xn, w_ref[:, OFF_C:OFF_F], preferred_element_type=F32)
    cbo_ref[...] = zc[:, :CONV_W]
    u_ref[...] = zc[:, CONV_W:2 * CONV_W] * zc[:, 2 * CONV_W:]

    zf = jnp.dot(xn, w_ref[:, OFF_F:], preferred_element_type=F32).astype(BF16)
    ccs = ccs_ref[...]
    for g in range(FG):
        pp = jnp.dot(zf[:, g * FGD:(g + 1) * FGD], ccs, preferred_element_type=F32)
        pc_ref[:, g * FGD:(g + 1) * FGD] = pp[:, :FGD].astype(BF16)
        ps_ref[:, g * FGD:(g + 1) * FGD] = pp[:, FGD:].astype(BF16)


def _const_spec(shape, layer=None):
    nd = len(shape)
    if layer is None:
        return pl.BlockSpec(shape, lambda i, *_: (0,) * nd, pipeline_mode=pl.Buffered(1))
    return pl.BlockSpec((None,) + shape, lambda i, *_: (layer,) + (0,) * nd, pipeline_mode=pl.Buffered(1))


def _proj(h, mod, layer, lw, tabs, ccs):
    tm = TM_PROJ
    tpb = SEQ // tm
    n_lat = T_LAT // tm

    def tab_spec():
        return pl.BlockSpec((tm, 128), lambda i: (jnp.where(i < n_lat, i % tpb, tpb), 0))

    def row_spec(w):
        return pl.BlockSpec((tm, w), lambda i: (i, 0))

    widths = (HEADS * QK_A_PAD, HEADS * QK_A_PAD, HEADS * V_A, HEADS * HD_B, KVH_B * HD_B,
              KVH_B * HD_B, CONV_W, CONV_W, FG * FGD, FG * FGD)
    dtypes = (BF16, BF16, BF16, BF16, BF16, BF16, F32, F32, BF16, BF16)
    return pl.pallas_call(
        _proj_kernel,
        grid=(T_ALL // tm,),
        in_specs=[
            row_spec(D),
            _const_spec((1, D), layer),
            _mod_spec(layer, 3, tpb),
            _mod_spec(layer, 4, tpb),
            _const_spec((D, W_IN_PAD), layer),
            _const_spec((1, Q_LORA_PAD), layer),
            _const_spec((Q_LORA_PAD, HEADS * QK_A_PAD), layer),
            _const_spec((1, KV_LORA), layer),
            _const_spec((KV_LORA, HEADS * (NOPE + V_A)), layer),
            _const_spec((1, QK_A_PAD), layer),
            _const_spec((1, NOPE), layer),
            _const_spec((1, 128), layer),
            _const_spec((1, HD_B), layer),
            _const_spec((1, HD_B), layer),
            tab_spec(), tab_spec(), tab_spec(), tab_spec(), tab_spec(),
            _const_spec((FGD, 2 * FGD)),
        ],
        out_specs=[row_spec(w) for w in widths],
        out_shape=[jax.ShapeDtypeStruct((T_ALL, w), dt) for w, dt in zip(widths, dtypes)],
        compiler_params=_cparams(("arbitrary",)),
        name="proj",
    )(h, lw["norm_mix"], mod, mod, lw["w_in"], lw["g_cq"], lw["w_uq"], lw["g_ckv"], lw["w_ukv"],
      lw["g_qa"], lw["g_ka_nope"], lw["g_ka_pe"], lw["g_qb"], lw["g_kb"], *tabs, ccs)


_QK_DIMS = (((1,), (1,)), ((), ()))


def _att_lat_kernel(q_ref, k_ref, kc_ref, v_ref, vc_ref, o_ref):
    q = q_ref[...]
    chunks = [(k_ref, v_ref, j * KC_ATT, KC_ATT) for j in range(SEQ // KC_ATT)] + [(kc_ref, vc_ref, 0, CTX)]

    def scores(chunk):
        kr, _, lo, n = chunk
        return lax.dot_general(q, kr[lo:lo + n, :], _QK_DIMS, preferred_element_type=F32)

    s = scores(chunks[0])
    m = l = acc = None
    for idx, (_, vr, lo, n) in enumerate(chunks):
        s_next = scores(chunks[idx + 1]) if idx + 1 < len(chunks) else None
        m_chunk = jnp.max(s, axis=-1, keepdims=True)
        m_new = m_chunk if m is None else jnp.maximum(m, m_chunk)
        p = jnp.exp2(s - m_new)
        pv = jnp.dot(p.astype(BF16), vr[lo:lo + n, :], preferred_element_type=F32)
        if m is None:
            l = jnp.sum(p, axis=-1, keepdims=True)
            acc = pv
        else:
            alpha = jnp.exp2(m - m_new)
            l = alpha * l + jnp.sum(p, axis=-1, keepdims=True)
            acc = alpha * acc + pv
        m, s = m_new, s_next
    o_ref[...] = (acc / l).astype(o_ref.dtype)


def _att_ctx_kernel(q_ref, k_ref, v_ref, o_ref):
    s = lax.dot_general(q_ref[...], k_ref[...], _QK_DIMS, preferred_element_type=F32)
    m = jnp.max(s, axis=-1, keepdims=True)
    p = jnp.exp2(s - m)
    l = jnp.sum(p, axis=-1, keepdims=True)
    o = jnp.dot(p.astype(BF16), v_ref[...], preferred_element_type=F32)
    o_ref[...] = (o / l).astype(o_ref.dtype)


def _attention(q, k, v, dq, dv, group, with_ctx):
    tq = TQ_ATT
    nq = SEQ // tq
    ctx0 = T_LAT // CTX
    out = pl.pallas_call(
        _att_lat_kernel,
        grid=(BATCH, HEADS, nq),
        in_specs=[
            pl.BlockSpec((tq, dq), lambda b, h, i: (b * nq + i, h)),
            pl.BlockSpec((SEQ, dq), lambda b, h, i: (b, h // group)),
            pl.BlockSpec((CTX, dq), lambda b, h, i: (ctx0 + b, h // group)),
            pl.BlockSpec((SEQ, dv), lambda b, h, i: (b, h // group)),
            pl.BlockSpec((CTX, dv), lambda b, h, i: (ctx0 + b, h // group)),
        ],
        out_specs=pl.BlockSpec((tq, dv), lambda b, h, i: (b * nq + i, h)),
        out_shape=jax.ShapeDtypeStruct((T_LAT, HEADS * dv), BF16),
        compiler_params=_cparams(("arbitrary", "arbitrary", "arbitrary")),
        name="att_lat",
    )(q, k, k, v, v)
    if not with_ctx:
        return out, None
    out_ctx = pl.pallas_call(
        _att_ctx_kernel,
        grid=(BATCH, HEADS),
        in_specs=[
            pl.BlockSpec((CTX, dq), lambda b, h: (ctx0 + b, h)),
            pl.BlockSpec((CTX, dq), lambda b, h: (ctx0 + b, h // group)),
            pl.BlockSpec((CTX, dv), lambda b, h: (ctx0 + b, h // group)),
        ],
        out_specs=pl.BlockSpec((CTX, dv), lambda b, h: (b, h)),
        out_shape=jax.ShapeDtypeStruct((T_CTX, HEADS * dv), BF16),
        compiler_params=_cparams(("arbitrary", "arbitrary")),
        name="att_ctx",
    )(q, k, v)
    return out, out_ctx


def _dft_kernel(scale, c_ref, s_ref, pc_ref, ps_ref, o_ref):
    y = (jnp.dot(c_ref[...], pc_ref[...], preferred_element_type=F32)
         - jnp.dot(s_ref[...], ps_ref[...], preferred_element_type=F32))
    o_ref[...] = (y * scale).astype(o_ref.dtype)


def _fourier(pc, ps, dft_lat, dft_ctx, with_ctx):
    tr = TR_DFT
    nr = SEQ // tr
    w = FG * FGD
    out = pl.pallas_call(
        functools.partial(_dft_kernel, (SEQ * FGD) ** -0.5),
        grid=(BATCH, nr),
        in_specs=[
            pl.BlockSpec((tr, SEQ), lambda b, r: (r, 0)),
            pl.BlockSpec((tr, SEQ), lambda b, r: (r, 0)),
            pl.BlockSpec((SEQ, w), lambda b, r: (b, 0)),
            pl.BlockSpec((SEQ, w), lambda b, r: (b, 0)),
        ],
        out_specs=pl.BlockSpec((tr, w), lambda b, r: (b * nr + r, 0)),
        out_shape=jax.ShapeDtypeStruct((T_LAT, w), BF16),
        compiler_params=_cparams(("arbitrary", "arbitrary")),
        name="dft_lat",
    )(dft_lat[0], dft_lat[1], pc, ps)
    if not with_ctx:
        return out, None
    ctx0 = T_LAT // CTX
    out_ctx = pl.pallas_call(
        functools.partial(_dft_kernel, (CTX * FGD) ** -0.5),
        grid=(BATCH,),
        in_specs=[
            pl.BlockSpec((CTX, CTX), lambda b: (0, 0)),
            pl.BlockSpec((CTX, CTX), lambda b: (0, 0)),
            pl.BlockSpec((CTX, w), lambda b: (ctx0 + b, 0)),
            pl.BlockSpec((CTX, w), lambda b: (ctx0 + b, 0)),
        ],
        out_specs=pl.BlockSpec((CTX, w), lambda b: (b, 0)),
        out_shape=jax.ShapeDtypeStruct((T_CTX, w), BF16),
        compiler_params=_cparams(("arbitrary",)),
        name="dft_ctx",
    )(dft_ctx[0], dft_ctx[1], pc, ps)
    return out, out_ctx


def _merge_kernel(n_lat_tiles, with_ctx, h_ref, gain_ref, sh_ref, sc_ref, gt_ref, *refs):
    lat_refs, refs = refs[:3], refs[3:]
    if with_ctx:
        ctx_refs, refs = refs[:3], refs[3:]
    (cbo_ref, u_ref, up_ref, un_ref, cw_ref, cbias_ref, wg_ref, bg_ref, wb_ref, wo_ref,
     o_ref, r_ref) = refs
    i = pl.program_id(0)
    n = pl.program_id(1)
    last = pl.num_programs(1) - 1
    tm = h_ref.shape[0]

    def chunk(r):
        xn = _scale_mod(h_ref[...], r, gain_ref[...], sh_ref[...], sc_ref[...]).astype(BF16)

        u = u_ref[...]
        is_lat = i < n_lat_tiles
        seq_len = jnp.where(is_lat, SEQ, CTX)
        ridx = lax.broadcasted_iota(jnp.int32, (tm, 1), 0)
        pos = (i * tm + ridx) & (seq_len - 1)
        prev = jnp.where(ridx == 0, up_ref[7:8, :], pltpu.roll(u, 1, 0))
        prev = jnp.where(pos == 0, 0.0, prev)
        nxt = jnp.where(ridx == tm - 1, un_ref[0:1, :], pltpu.roll(u, tm - 1, 0))
        nxt = jnp.where(pos == seq_len - 1, 0.0, nxt)
        y = prev * cw_ref[0:1, :] + u * cw_ref[1:2, :] + nxt * cw_ref[2:3, :] + cbias_ref[...]
        conv = (cbo_ref[...] * y).astype(BF16)

        if with_ctx:
            oa, ob, of = (jnp.where(is_lat, lr[...], cr[...]) for lr, cr in zip(lat_refs, ctx_refs))
        else:
            oa, ob, of = (lr[...] for lr in lat_refs)

        merged = None
        for k, br in enumerate((oa, ob, conv, of)):
            gate = jax.nn.sigmoid(jnp.dot(xn, wg_ref[k], preferred_element_type=F32) + bg_ref[k])
            term = gate * jnp.dot(br, wb_ref[k], preferred_element_type=F32)
            merged = term if merged is None else merged + term
        return jnp.dot(merged.astype(BF16), wo_ref[...], preferred_element_type=F32)

    @pl.when(n == 0)
    def _():
        r = _row_rsqrt(h_ref[...])
        r_ref[...] = r
        o_ref[...] = chunk(r)

    @pl.when(jnp.logical_and(n > 0, n < last))
    def _():
        o_ref[...] += chunk(r_ref[...])

    @pl.when(n == last)
    def _():
        o_ref[...] = h_ref[...] + gt_ref[...] * (o_ref[...] + chunk(r_ref[...]))


def _merge(h, mod, layer, br_lat, br_ctx, cbo, u, lw, n_rows):
    tm, tn = TM_MERGE, TN_MERGE
    tpb = SEQ // tm
    n_lat = T_LAT // tm
    with_ctx = br_ctx is not None
    last_blk = T_ALL // 8 - 1
    br_specs = [pl.BlockSpec((tm, BR_W), lambda i, n: (jnp.minimum(i, n_lat - 1), 0))] * 3
    if with_ctx:
        br_specs += [pl.BlockSpec((tm, BR_W), lambda i, n: (jnp.maximum(i - n_lat, 0), 0))] * 3

    def row_spec(w):
        return pl.BlockSpec((tm, w), lambda i, n: (i, 0))

    return pl.pallas_call(
        functools.partial(_merge_kernel, n_lat, with_ctx),
        grid=(n_rows // tm, D // tn),
        in_specs=[
            row_spec(D),
            pl.BlockSpec((None, 1, D), lambda i, n: (layer, 0, 0)),
            _mod_spec(layer, 3, tpb),
            _mod_spec(layer, 4, tpb),
            _mod_spec(layer, 5, tpb),
            *br_specs,
            row_spec(CONV_W), row_spec(CONV_W),
            pl.BlockSpec((8, CONV_W), lambda i, n: (jnp.maximum(i * (tm // 8) - 1, 0), 0)),
            pl.BlockSpec((8, CONV_W), lambda i, n: (jnp.minimum((i + 1) * (tm // 8), last_blk), 0)),
            pl.BlockSpec((None, 3, CONV_W), lambda i, n: (layer, 0, 0)),
            pl.BlockSpec((None, 1, CONV_W), lambda i, n: (layer, 0, 0)),
            pl.BlockSpec((None, N_BR, D, tn), lambda i, n: (layer, 0, 0, n)),
            pl.BlockSpec((None, N_BR, 1, tn), lambda i, n: (layer, 0, 0, n)),
            pl.BlockSpec((None, N_BR, BR_W, tn), lambda i, n: (layer, 0, 0, n)),
            pl.BlockSpec((None, tn, D), lambda i, n: (layer, n, 0)),
        ],
        out_specs=row_spec(D),
        out_shape=jax.ShapeDtypeStruct((n_rows, D), F32),
        scratch_shapes=[pltpu.VMEM((tm, 1), F32)],
        compiler_params=_cparams(("arbitrary", "arbitrary")),
        name="merge",
    )(h, lw["norm_mix"], mod, mod, mod, *br_lat, *(br_ctx or ()), cbo, u, u, u, lw["conv_w"],
      lw["conv_b"], lw["w_gate"], lw["b_gate"], lw["w_branch"], lw["w_o"])


def _rope_tables():
    t = jnp.arange(SEQ, dtype=jnp.int32)
    pos_row = (t // GRID_W).astype(F32)
    pos_col = (t % GRID_W).astype(F32)

    def ang(rot_dim):
        n = rot_dim // 4
        inv_freq = THETA ** (-jnp.arange(n, dtype=F32) / n)
        a = jnp.concatenate([pos_row[:, None] * inv_freq, pos_col[:, None] * inv_freq], axis=-1)
        return jnp.cos(a), jnp.sin(a)

    ca, sa = ang(ROPE_A)
    z32 = jnp.zeros_like(ca)
    z64 = jnp.zeros((SEQ, 64), F32)
    tab_c = jnp.concatenate([ca, ca, z64], axis=-1)
    tab_s1 = jnp.concatenate([-sa, z32, z64], axis=-1)
    tab_s2 = jnp.concatenate([z32, sa, z64], axis=-1)
    cb, sb = ang(HD_B)
    tab_cb = jnp.concatenate([cb, cb], axis=-1)
    tab_sb = jnp.concatenate([-sb, sb], axis=-1)
    ones = jnp.ones((TM_PROJ, 128), F32)
    zeros = jnp.zeros((TM_PROJ, 128), F32)
    return tuple(jnp.concatenate([tab, ident], axis=0)
                 for tab, ident in ((tab_c, ones), (tab_s1, zeros), (tab_s2, zeros),
                                    (tab_cb, ones), (tab_sb, zeros)))


def _dft_mats(n):
    j = jnp.arange(n, dtype=jnp.int32)
    a = ((j[:, None] * j[None, :]) % n).astype(F32) * (2.0 * math.pi / n)
    return jnp.cos(a), jnp.sin(a)


def _dft_mats_seq():
    k = jnp.arange(SEQ, dtype=jnp.int32)[None, :]
    j = jnp.arange(GRID_W, dtype=jnp.int32)[:, None]
    a = ((j * k) % GRID_W).astype(F32) * (2.0 * math.pi / GRID_W)
    b = ((j * k) % SEQ).astype(F32) * (2.0 * math.pi / SEQ)
    ca, sa, cb, sb = jnp.cos(a)[:, None, :], jnp.sin(a)[:, None, :], jnp.cos(b)[None], jnp.sin(b)[None]
    return ((ca * cb - sa * sb).reshape(SEQ, SEQ), (sa * cb + ca * sb).reshape(SEQ, SEQ))


def _pad_to(a, axis, size):
    pad = [(0, 0)] * a.ndim
    pad[axis] = (0, size - a.shape[axis])
    return jnp.pad(a, pad)


def _prep_weights(p):
    def cat_bf16(parts, axis):
        return jnp.concatenate([part.astype(BF16) for part in parts], axis=axis)

    w_in = p["w_in"]
    segs, off = [], 0
    for wd, wd_pad in ((Q_LORA, Q_LORA_PAD), (KV_LORA, KV_LORA), (ROPE_A, 128),
                       (HEADS * HD_B,) * 2, (KVH_B * HD_B,) * 2, (KVH_B * HD_B,) * 2,
                       (CONV_W,) * 2, (CONV_W,) * 2, (CONV_W,) * 2, (FG * FGD,) * 2):
        segs.append(lax.slice_in_dim(w_in, off, off + wd, axis=2))
        if wd_pad > wd:
            segs.append(jnp.zeros((DEPTH, D, wd_pad - wd), BF16))
        off += wd

    w_uq = p["w_uq"].reshape(DEPTH, Q_LORA, HEADS, QK_A)
    w_uq = _pad_to(_pad_to(w_uq, 3, QK_A_PAD), 1, Q_LORA_PAD).reshape(DEPTH, Q_LORA_PAD, HEADS * QK_A_PAD)
    w_ukv = p["w_ukv"].reshape(DEPTH, KV_LORA, HEADS, NOPE + V_A)
    w_ukv = jnp.concatenate([w_ukv[..., :NOPE].reshape(DEPTH, KV_LORA, HEADS * NOPE),
                             w_ukv[..., NOPE:].reshape(DEPTH, KV_LORA, HEADS * V_A)], axis=2)
    g_ka = p["g_ka"]

    return {
        "w_in": cat_bf16(segs, 2),
        "g_cq": _pad_to(p["g_cq"], 1, Q_LORA_PAD)[:, None, :],
        "w_uq": w_uq.astype(BF16),
        "g_ckv": p["g_ckv"][:, None, :],
        "w_ukv": w_ukv.astype(BF16),
        "g_qa": _pad_to(p["g_qa"], 1, QK_A_PAD)[:, None, :],
        "g_ka_nope": g_ka[:, None, :NOPE],
        "g_ka_pe": _pad_to(g_ka[:, NOPE:], 1, 128)[:, None, :],
        "g_qb": p["g_qb"][:, None, :],
        "g_kb": p["g_kb"][:, None, :],
        "conv_w": p["conv_w"],
        "conv_b": p["conv_b"][:, None, :],
        "w_gate": p["w_gate"].astype(BF16),
        "b_gate": p["b_gate"][:, :, None, :],
        "w_branch": p["w_branch"].astype(BF16),
        "w_o": p["w_o"].astype(BF16),
        "norm_ffn1": p["norm_ffn1"][:, None, :],
        "norm_mix": p["norm_mix"][:, None, :],
        "norm_ffn2": p["norm_ffn2"][:, None, :],
        "ffn1_wi": p["ffn1_wi"].astype(BF16), "ffn1_wo": p["ffn1_wo"].astype(BF16),
        "ffn2_wi": p["ffn2_wi"].astype(BF16), "ffn2_wo": p["ffn2_wo"].astype(BF16),
    }


def kernel(x, c, ctx, c_ctx, w_ada, b_ada, norm_ffn1, ffn1_wi, ffn1_wo, norm_mix, w_in, g_cq, w_uq, g_ckv, w_ukv, g_qa, g_ka, g_qb, g_kb, conv_w, conv_b, w_branch, w_gate, b_gate, w_o, norm_ffn2, ffn2_wi, ffn2_wo):
    p = dict(norm_ffn1=norm_ffn1, ffn1_wi=ffn1_wi, ffn1_wo=ffn1_wo, norm_mix=norm_mix, w_in=w_in,
             g_cq=g_cq, w_uq=w_uq, g_ckv=g_ckv, w_ukv=w_ukv, g_qa=g_qa, g_ka=g_ka, g_qb=g_qb,
             g_kb=g_kb, conv_w=conv_w, conv_b=conv_b, w_branch=w_branch, w_gate=w_gate,
             b_gate=b_gate, w_o=w_o, norm_ffn2=norm_ffn2, ffn2_wi=ffn2_wi, ffn2_wo=ffn2_wo)

    cond8 = jnp.concatenate([c, c_ctx[None, :], jnp.zeros((8 - BATCH - 1, D), F32)], axis=0)
    mod = _ada(cond8, w_ada, b_ada).reshape(DEPTH, 8, N_MOD, 1, D)

    tabs = _rope_tables()
    cc, sc = _dft_mats(FGD)
    ccs = jnp.concatenate([cc, sc], axis=1).astype(BF16)
    dft_lat = tuple(m.astype(BF16) for m in _dft_mats_seq())
    dft_ctx = tuple(m.astype(BF16) for m in _dft_mats(CTX))

    lw = _prep_weights(p)
    h = jnp.concatenate([x.reshape(T_LAT, D), ctx.reshape(T_CTX, D)], axis=0)
    for l in range(DEPTH):
        last = l == DEPTH - 1
        h = _ffn(h, mod, l, 0, lw["norm_ffn1"], lw["ffn1_wi"], lw["ffn1_wo"], T_ALL)
        qa, ka, va, qb, kb, vb, cbo, u, pc, ps = _proj(h, mod, l, lw, tabs, ccs)
        oa, oa_c = _attention(qa, ka, va, QK_A_PAD, V_A, 1, not last)
        ob, ob_c = _attention(qb, kb, vb, HD_B, HD_B, GROUP_B, not last)
        of, of_c = _fourier(pc, ps, dft_lat, dft_ctx, not last)
        n_rows = T_LAT if last else T_ALL
        h = _merge(h, mod, l, (oa, ob, of), None if last else (oa_c, ob_c, of_c), cbo, u, lw, n_rows)
        h = _ffn(h, mod, l, 6, lw["norm_ffn2"], lw["ffn2_wi"], lw["ffn2_wo"], n_rows)
    return h.reshape(BATCH, SEQ, D)
```

```python
import functools
import math

import jax
import jax.numpy as jnp
from jax import lax
from jax.experimental import pallas as pl
from jax.experimental.pallas import tpu as pltpu

F32 = jnp.float32
BF16 = jnp.bfloat16

D = 2048
BATCH = 4
SEQ = 4096
DEPTH = 4
GRID_W = 64
CTX = 256
N_MOD = 9
F_FF = 5504
HEADS = 4
Q_LORA = 448
Q_LORA_PAD = 512
KV_LORA = 128
NOPE = 128
ROPE_A = 64
QK_A = NOPE + ROPE_A
QK_A_PAD = 256
V_A = 128
HD_B = 128
KVH_B = 2
GROUP_B = 2
CONV_W = 512
FG = 4
FGD = 128
BR_W = 512
N_BR = 4
THETA = 10000.0
EPS = 1e-6
LOG2_E = math.log2(math.e)

T_LAT = BATCH * SEQ
T_CTX = BATCH * CTX
T_ALL = T_LAT + T_CTX

W_IN = 3712
OFF_A = 0
OFF_B = 640
OFF_C = 1664
OFF_F = 3200

LANE = 128
VMEM_PHYS_V7X = 64 * 1024 * 1024
VMEM_LIMIT = 60 * 1024 * 1024

TM_FFN = 1024
TF_FFN = 512
TM_PROJ = 512
TM_MERGE = 512
TN_MERGE = 512
TQ_ATT = 1024
KC_ATT = 512
TK_DFT = 512
TN_ADA = 1024


def _cparams(sem):
    return pltpu.CompilerParams(dimension_semantics=sem, vmem_limit_bytes=VMEM_LIMIT)


def _mod_row(i, tiles_per_batch):
    return jnp.minimum(i // tiles_per_batch, BATCH)


def _row_rsqrt(x):
    return lax.rsqrt(jnp.mean(x * x, axis=-1, keepdims=True) + EPS)


def _scale_mod(x, r, gain, shift, scale):
    return (x * r * gain) * (1.0 + scale) + shift


def _norm_mod(x, gain, shift, scale):
    return _scale_mod(x, _row_rsqrt(x), gain, shift, scale)


def _silu(x):
    return x * jax.nn.sigmoid(x)


def _ada_kernel(cond_ref, w_ref, b_ref, o_ref):
    s = _silu(cond_ref[...]).astype(BF16)
    o_ref[...] = jnp.dot(s, w_ref[...].astype(BF16), preferred_element_type=F32) + b_ref[...]


def _ada(cond8, w_ada, b_ada):
    n = N_MOD * D
    return pl.pallas_call(
        _ada_kernel,
        grid=(DEPTH, n // TN_ADA),
        in_specs=[
            pl.BlockSpec((8, D), lambda l, j: (0, 0)),
            pl.BlockSpec((None, D, TN_ADA), lambda l, j: (l, 0, j)),
            pl.BlockSpec((None, 1, TN_ADA), lambda l, j: (l, 0, j)),
        ],
        out_specs=pl.BlockSpec((None, 8, TN_ADA), lambda l, j: (l, 0, j)),
        out_shape=jax.ShapeDtypeStruct((DEPTH, 8, n), F32),
        compiler_params=_cparams(("arbitrary", "arbitrary")),
        name="ada",
    )(cond8, w_ada, b_ada.reshape(DEPTH, 1, n))


def _mod_spec(layer, j, tiles_per_batch):
    return pl.BlockSpec((None, None, None, 1, D),
                        lambda i, *_: (layer, _mod_row(i, tiles_per_batch), j, 0, 0))


def _ffn_kernel(overlap, x_ref, gain_ref, sh_ref, sc_ref, gt_ref, wg_ref, wu_ref, wo_ref, o_ref, r_ref):
    f = pl.program_id(1)
    last = pl.num_programs(1) - 1

    def chunk(r, skip=0):
        xn = _scale_mod(x_ref[...], r, gain_ref[...], sh_ref[...], sc_ref[...]).astype(BF16)
        g = jnp.dot(xn, wg_ref[:, skip:], preferred_element_type=F32)
        u = jnp.dot(xn, wu_ref[:, skip:], preferred_element_type=F32)
        a = (_silu(g) * u).astype(BF16)
        return jnp.dot(a, wo_ref[skip:, :], preferred_element_type=F32)

    @pl.when(f == 0)
    def _():
        r = _row_rsqrt(x_ref[...])
        r_ref[...] = r
        o_ref[...] = chunk(r)

    @pl.when(jnp.logical_and(f > 0, f < last))
    def _():
        o_ref[...] += chunk(r_ref[...])

    @pl.when(f == last)
    def _():
        o_ref[...] = x_ref[...] + (0.5 * gt_ref[...]) * (o_ref[...] + chunk(r_ref[...], overlap))


def _ffn(h, mod, layer, j0, gain, wi, wo, n_rows):
    tm, tf = TM_FFN, TF_FFN
    tpb = SEQ // tm
    nf = pl.cdiv(F_FF, tf)

    def hid(f, base=0):
        return (base // LANE + jnp.minimum(f * (tf // LANE), (F_FF - tf) // LANE)) * LANE

    return pl.pallas_call(
        functools.partial(_ffn_kernel, nf * tf - F_FF),
        grid=(n_rows // tm, nf),
        in_specs=[
            pl.BlockSpec((tm, D), lambda i, f: (i, 0)),
            pl.BlockSpec((None, 1, D), lambda i, f: (layer, 0, 0)),
            _mod_spec(layer, j0, tpb),
            _mod_spec(layer, j0 + 1, tpb),
            _mod_spec(layer, j0 + 2, tpb),
            pl.BlockSpec((None, pl.Element(D), pl.Element(tf)), lambda i, f: (layer, 0, hid(f))),
            pl.BlockSpec((None, pl.Element(D), pl.Element(tf)), lambda i, f: (layer, 0, hid(f, F_FF))),
            pl.BlockSpec((None, pl.Element(tf), pl.Element(D)), lambda i, f: (layer, hid(f), 0)),
        ],
        out_specs=pl.BlockSpec((tm, D), lambda i, f: (i, 0)),
        out_shape=jax.ShapeDtypeStruct((n_rows, D), F32),
        scratch_shapes=[pltpu.VMEM((tm, 1), F32)],
        compiler_params=_cparams(("arbitrary", "arbitrary")),
        name="ffn",
    )(h, gain, mod, mod, mod, wi, wi, wo)


def _rope_a(v, c, s1, s2):
    return v * c + pltpu.roll(v, 96, 1) * s1 + pltpu.roll(v, 32, 1) * s2


def _rope_b(v, c, s):
    return v * c + pltpu.roll(v, 64, 1) * s


def _proj_kernel(h_ref, gain_ref, sh_ref, sc_ref, w_ref, gcq_ref, wuq_ref, gckv_ref, wukv_ref,
                 gqa_ref, gkan_ref, gkap_ref, gqb_ref, gkb_ref, ca_ref, s1a_ref, s2a_ref,
                 cb_ref, sb_ref, ccs_ref,
                 qa_ref, ka_ref, va_ref, qb_ref, kb_ref, vb_ref, cbo_ref, u_ref, pc_ref, ps_ref):
    xn = _norm_mod(h_ref[...], gain_ref[...], sh_ref[...], sc_ref[...]).astype(BF16)
    ca, s1a, s2a = ca_ref[...], s1a_ref[...], s2a_ref[...]
    cb, sb = cb_ref[...], sb_ref[...]
    scale_a = QK_A ** -0.5 * LOG2_E
    scale_b = HD_B ** -0.5 * LOG2_E

    za = jnp.dot(xn, w_ref[:, OFF_A:OFF_B], preferred_element_type=F32)
    hi = lax.broadcasted_iota(jnp.int32, (1, LANE), 1) >= LANE // 2
    s3, s4 = za[:, 3 * LANE:4 * LANE], za[:, 4 * LANE:]
    aq = jnp.concatenate([za[:, :3 * LANE], jnp.where(hi, 0.0, s3)], axis=1)
    akv = pltpu.roll(jnp.where(hi, s3, s4), LANE // 2, 1)
    ape = jnp.where(hi, 0.0, pltpu.roll(s4, LANE // 2, 1))
    r = lax.rsqrt(jnp.sum(aq * aq, axis=-1, keepdims=True) * (1.0 / Q_LORA) + EPS)
    cq = (aq * r * gcq_ref[...]).astype(BF16)
    q = jnp.dot(cq, wuq_ref[...], preferred_element_type=F32)
    gqa = gqa_ref[...]
    for hd in range(HEADS):
        lo = hd * QK_A_PAD
        qh = q[:, lo:lo + QK_A_PAD]
        r = lax.rsqrt(jnp.sum(qh * qh, axis=-1, keepdims=True) * (1.0 / QK_A) + EPS)
        qh = qh * r * gqa
        qa_ref[:, lo:lo + NOPE] = (qh[:, :NOPE] * scale_a).astype(BF16)
        qa_ref[:, lo + NOPE:lo + QK_A_PAD] = (_rope_a(qh[:, NOPE:], ca, s1a, s2a) * scale_a).astype(BF16)
    r = lax.rsqrt(jnp.mean(akv * akv, axis=-1, keepdims=True) + EPS)
    ckv = (akv * r * gckv_ref[...]).astype(BF16)
    kv = jnp.dot(ckv, wukv_ref[...], preferred_element_type=F32)
    ss_pe = jnp.sum(ape * ape, axis=-1, keepdims=True)
    gkan, gkap = gkan_ref[...], gkap_ref[...]
    for hd in range(HEADS):
        kn = kv[:, hd * NOPE:(hd + 1) * NOPE]
        r = lax.rsqrt((jnp.sum(kn * kn, axis=-1, keepdims=True) + ss_pe) * (1.0 / QK_A) + EPS)
        lo = hd * QK_A_PAD
        ka_ref[:, lo:lo + NOPE] = (kn * r * gkan).astype(BF16)
        ka_ref[:, lo + NOPE:lo + QK_A_PAD] = _rope_a(ape * r * gkap, ca, s1a, s2a).astype(BF16)
        va_ref[:, hd * V_A:(hd + 1) * V_A] = kv[:, HEADS * NOPE + hd * V_A:HEADS * NOPE + (hd + 1) * V_A].astype(BF16)

    zb = jnp.dot(xn, w_ref[:, OFF_B:OFF_C], preferred_element_type=F32)
    gqb, gkb = gqb_ref[...], gkb_ref[...]
    for hd in range(HEADS):
        qh = zb[:, hd * HD_B:(hd + 1) * HD_B]
        r = lax.rsqrt(jnp.mean(qh * qh, axis=-1, keepdims=True) + EPS)
        qb_ref[:, hd * HD_B:(hd + 1) * HD_B] = (_rope_b(qh * r * gqb, cb, sb) * scale_b).astype(BF16)
    for hd in range(KVH_B):
        lo = HEADS * HD_B + hd * HD_B
        kh = zb[:, lo:lo + HD_B]
        r = lax.rsqrt(jnp.mean(kh * kh, axis=-1, keepdims=True) + EPS)
        kb_ref[:, hd * HD_B:(hd + 1) * HD_B] = _rope_b(kh * r * gkb, cb, sb).astype(BF16)
    vb_ref[...] = zb[:, (HEADS + KVH_B) * HD_B:].astype(BF16)

    zc = jnp.dot(xn, w_ref[:, OFF_C:OFF_F], preferred_element_type=F32)
    cbo_ref[...] = zc[:, :CONV_W]
    u_ref[...] = zc[:, CONV_W:2 * CONV_W] * zc[:, 2 * CONV_W:]

    zf = jnp.dot(xn, w_ref[:, OFF_F:], preferred_element_type=F32).astype(BF16)
    ccs = ccs_ref[...]
    for g in range(FG):
        pp = jnp.dot(zf[:, g * FGD:(g + 1) * FGD], ccs, preferred_element_type=F32)
        pc_ref[:, g * FGD:(g + 1) * FGD] = pp[:, :FGD].astype(BF16)
        ps_ref[:, g * FGD:(g + 1) * FGD] = pp[:, FGD:].astype(BF16)


def _const_spec(shape, layer=None):
    nd = len(shape)
    if layer is None:
        return pl.BlockSpec(shape, lambda i, *_: (0,) * nd, pipeline_mode=pl.Buffered(1))
    return pl.BlockSpec((None,) + shape, lambda i, *_: (layer,) + (0,) * nd, pipeline_mode=pl.Buffered(1))


def _proj(h, mod, layer, lw, tabs, ccs):
    tm = TM_PROJ
    tpb = SEQ // tm
    n_lat = T_LAT // tm

    def tab_spec():
        return pl.BlockSpec((tm, 128), lambda i: (jnp.where(i < n_lat, i % tpb, tpb), 0))

    def row_spec(w):
        return pl.BlockSpec((tm, w), lambda i: (i, 0))

    widths = (HEADS * QK_A_PAD, HEADS * QK_A_PAD, HEADS * V_A, HEADS * HD_B, KVH_B * HD_B,
              KVH_B * HD_B, CONV_W, CONV_W, FG * FGD, FG * FGD)
    dtypes = (BF16, BF16, BF16, BF16, BF16, BF16, F32, F32, BF16, BF16)
    return pl.pallas_call(
        _proj_kernel,
        grid=(T_ALL // tm,),
        in_specs=[
            row_spec(D),
            _const_spec((1, D), layer),
            _mod_spec(layer, 3, tpb),
            _mod_spec(layer, 4, tpb),
            _const_spec((D, W_IN), layer),
            _const_spec((1, Q_LORA_PAD), layer),
            _const_spec((Q_LORA_PAD, HEADS * QK_A_PAD), layer),
            _const_spec((1, KV_LORA), layer),
            _const_spec((KV_LORA, HEADS * (NOPE + V_A)), layer),
            _const_spec((1, QK_A_PAD), layer),
            _const_spec((1, NOPE), layer),
            _const_spec((1, 128), layer),
            _const_spec((1, HD_B), layer),
            _const_spec((1, HD_B), layer),
            tab_spec(), tab_spec(), tab_spec(), tab_spec(), tab_spec(),
            _const_spec((FGD, 2 * FGD)),
        ],
        out_specs=[row_spec(w) for w in widths],
        out_shape=[jax.ShapeDtypeStruct((T_ALL, w), dt) for w, dt in zip(widths, dtypes)],
        compiler_params=_cparams(("arbitrary",)),
        name="proj",
    )(h, lw["norm_mix"], mod, mod, lw["w_in"], lw["g_cq"], lw["w_uq"], lw["g_ckv"], lw["w_ukv"],
      lw["g_qa"], lw["g_ka_nope"], lw["g_ka_pe"], lw["g_qb"], lw["g_kb"], *tabs, ccs)


_QK_DIMS = (((1,), (1,)), ((), ()))


def _att_lat_kernel(q_ref, k_ref, kc_ref, v_ref, vc_ref, o_ref):
    q = q_ref[...]
    chunks = [(k_ref, v_ref, j * KC_ATT, KC_ATT) for j in range(SEQ // KC_ATT)] + [(kc_ref, vc_ref, 0, CTX)]

    def scores(chunk):
        kr, _, lo, n = chunk
        return lax.dot_general(q, kr[lo:lo + n, :], _QK_DIMS, preferred_element_type=F32)

    s = scores(chunks[0])
    m = l = acc = None
    for idx, (_, vr, lo, n) in enumerate(chunks):
        s_next = scores(chunks[idx + 1]) if idx + 1 < len(chunks) else None
        m_chunk = jnp.max(s, axis=-1, keepdims=True)
        m_new = m_chunk if m is None else jnp.maximum(m, m_chunk)
        p = jnp.exp2(s - m_new)
        pv = jnp.dot(p.astype(BF16), vr[lo:lo + n, :], preferred_element_type=F32)
        if m is None:
            l = jnp.sum(p, axis=-1, keepdims=True)
            acc = pv
        else:
            alpha = jnp.exp2(m - m_new)
            l = alpha * l + jnp.sum(p, axis=-1, keepdims=True)
            acc = alpha * acc + pv
        m, s = m_new, s_next
    o_ref[...] = (acc / l).astype(o_ref.dtype)


def _att_ctx_kernel(q_ref, k_ref, v_ref, o_ref):
    s = lax.dot_general(q_ref[...], k_ref[...], _QK_DIMS, preferred_element_type=F32)
    m = jnp.max(s, axis=-1, keepdims=True)
    p = jnp.exp2(s - m)
    l = jnp.sum(p, axis=-1, keepdims=True)
    o = jnp.dot(p.astype(BF16), v_ref[...], preferred_element_type=F32)
    o_ref[...] = (o / l).astype(o_ref.dtype)


def _attention(q, k, v, dq, dv, group, with_ctx):
    tq = TQ_ATT
    nq = SEQ // tq
    ctx0 = T_LAT // CTX
    out = pl.pallas_call(
        _att_lat_kernel,
        grid=(BATCH, HEADS, nq),
        in_specs=[
            pl.BlockSpec((tq, dq), lambda b, h, i: (b * nq + i, h)),
            pl.BlockSpec((SEQ, dq), lambda b, h, i: (b, h // group)),
            pl.BlockSpec((CTX, dq), lambda b, h, i: (ctx0 + b, h // group)),
            pl.BlockSpec((SEQ, dv), lambda b, h, i: (b, h // group)),
            pl.BlockSpec((CTX, dv), lambda b, h, i: (ctx0 + b, h // group)),
        ],
        out_specs=pl.BlockSpec((tq, dv), lambda b, h, i: (b * nq + i, h)),
        out_shape=jax.ShapeDtypeStruct((T_LAT, HEADS * dv), BF16),
        compiler_params=_cparams(("arbitrary", "arbitrary", "arbitrary")),
        name="att_lat",
    )(q, k, k, v, v)
    if not with_ctx:
        return out, None
    out_ctx = pl.pallas_call(
        _att_ctx_kernel,
        grid=(BATCH, HEADS),
        in_specs=[
            pl.BlockSpec((CTX, dq), lambda b, h: (ctx0 + b, h)),
            pl.BlockSpec((CTX, dq), lambda b, h: (ctx0 + b, h // group)),
            pl.BlockSpec((CTX, dv), lambda b, h: (ctx0 + b, h // group)),
        ],
        out_specs=pl.BlockSpec((CTX, dv), lambda b, h: (b, h)),
        out_shape=jax.ShapeDtypeStruct((T_CTX, HEADS * dv), BF16),
        compiler_params=_cparams(("arbitrary", "arbitrary")),
        name="att_ctx",
    )(q, k, v)
    return out, out_ctx


def _dft_kernel(scale, c_ref, s_ref, pc_ref, ps_ref, o_ref):
    y = (jnp.dot(c_ref[...], pc_ref[...], preferred_element_type=F32)
         - jnp.dot(s_ref[...], ps_ref[...], preferred_element_type=F32))
    o_ref[...] = (y * scale).astype(o_ref.dtype)


def _dft4_kernel(scale, c_ref, s_ref, tw_ref, pc_ref, ps_ref, o_ref):
    w = o_ref.shape[-1]
    c, s = c_ref[...], s_ref[...]
    pc, ps = pc_ref[...], ps_ref[...]
    ur = jnp.dot(c, pc, preferred_element_type=F32) - jnp.dot(s, ps, preferred_element_type=F32)
    ui = -(jnp.dot(c, ps, preferred_element_type=F32) + jnp.dot(s, pc, preferred_element_type=F32))
    a = [ur[:, :w]]
    b = [None]
    for r in range(1, 4):
        u_r, v_r = ur[:, r * w:(r + 1) * w], ui[:, r * w:(r + 1) * w]
        cr, sr = tw_ref[:, 2 * r - 2:2 * r - 1], tw_ref[:, 2 * r - 1:2 * r]
        a.append(u_r * cr + v_r * sr)
        b.append(None if r == 2 else v_r * cr - u_r * sr)
    o_ref[0] = ((a[0] + a[1] + a[2] + a[3]) * scale).astype(o_ref.dtype)
    o_ref[1] = ((a[0] + b[1] - a[2] - b[3]) * scale).astype(o_ref.dtype)
    o_ref[2] = ((a[0] - a[1] + a[2] - a[3]) * scale).astype(o_ref.dtype)
    o_ref[3] = ((a[0] - b[1] - a[2] + b[3]) * scale).astype(o_ref.dtype)


def _fourier(pc, ps, dft_lat, dft_ctx, with_ctx):
    w = FG * FGD
    nq = SEQ // 4
    tk = TK_DFT
    c_q, s_q, tw = dft_lat
    pc4, ps4 = pc.reshape(T_ALL // 4, 4 * w), ps.reshape(T_ALL // 4, 4 * w)
    out = pl.pallas_call(
        functools.partial(_dft4_kernel, (SEQ * FGD) ** -0.5),
        grid=(BATCH, nq // tk),
        in_specs=[
            pl.BlockSpec((tk, nq), lambda b, k: (k, 0)),
            pl.BlockSpec((tk, nq), lambda b, k: (k, 0)),
            pl.BlockSpec((tk, 8), lambda b, k: (k, 0)),
            pl.BlockSpec((nq, 4 * w), lambda b, k: (b, 0)),
            pl.BlockSpec((nq, 4 * w), lambda b, k: (b, 0)),
        ],
        out_specs=pl.BlockSpec((None, 4, tk, w), lambda b, k: (b, 0, k, 0)),
        out_shape=jax.ShapeDtypeStruct((BATCH, 4, nq, w), BF16),
        compiler_params=_cparams(("arbitrary", "arbitrary")),
        name="dft_lat",
    )(c_q, s_q, tw, pc4, ps4).reshape(T_LAT, w)
    if not with_ctx:
        return out, None
    ctx0 = T_LAT // CTX
    out_ctx = pl.pallas_call(
        functools.partial(_dft_kernel, (CTX * FGD) ** -0.5),
        grid=(BATCH,),
        in_specs=[
            pl.BlockSpec((CTX, CTX), lambda b: (0, 0)),
            pl.BlockSpec((CTX, CTX), lambda b: (0, 0)),
            pl.BlockSpec((CTX, w), lambda b: (ctx0 + b, 0)),
            pl.BlockSpec((CTX, w), lambda b: (ctx0 + b, 0)),
        ],
        out_specs=pl.BlockSpec((CTX, w), lambda b: (b, 0)),
        out_shape=jax.ShapeDtypeStruct((T_CTX, w), BF16),
        compiler_params=_cparams(("arbitrary",)),
        name="dft_ctx",
    )(dft_ctx[0], dft_ctx[1], pc, ps)
    return out, out_ctx


def _merge_kernel(n_lat_tiles, with_ctx, h_ref, gain_ref, sh_ref, sc_ref, gt_ref, *refs):
    lat_refs, refs = refs[:3], refs[3:]
    if with_ctx:
        ctx_refs, refs = refs[:3], refs[3:]
    (cbo_ref, u_ref, up_ref, un_ref, cw_ref, cbias_ref, wg_ref, bg_ref, wb_ref, wo_ref,
     o_ref, r_ref) = refs
    i = pl.program_id(0)
    n = pl.program_id(1)
    last = pl.num_programs(1) - 1
    tm = h_ref.shape[0]

    def chunk(r):
        xn = _scale_mod(h_ref[...], r, gain_ref[...], sh_ref[...], sc_ref[...]).astype(BF16)

        u = u_ref[...]
        is_lat = i < n_lat_tiles
        seq_len = jnp.where(is_lat, SEQ, CTX)
        ridx = lax.broadcasted_iota(jnp.int32, (tm, 1), 0)
        pos = (i * tm + ridx) & (seq_len - 1)
        prev = jnp.where(ridx == 0, up_ref[7:8, :], pltpu.roll(u, 1, 0))
        prev = jnp.where(pos == 0, 0.0, prev)
        nxt = jnp.where(ridx == tm - 1, un_ref[0:1, :], pltpu.roll(u, tm - 1, 0))
        nxt = jnp.where(pos == seq_len - 1, 0.0, nxt)
        y = prev * cw_ref[0:1, :] + u * cw_ref[1:2, :] + nxt * cw_ref[2:3, :] + cbias_ref[...]
        conv = (cbo_ref[...] * y).astype(BF16)

        if with_ctx:
            oa, ob, of = (jnp.where(is_lat, lr[...], cr[...]) for lr, cr in zip(lat_refs, ctx_refs))
        else:
            oa, ob, of = (lr[...] for lr in lat_refs)

        merged = None
        for k, br in enumerate((oa, ob, conv, of)):
            gate = jax.nn.sigmoid(jnp.dot(xn, wg_ref[k], preferred_element_type=F32) + bg_ref[k])
            term = gate * jnp.dot(br, wb_ref[k], preferred_element_type=F32)
            merged = term if merged is None else merged + term
        return jnp.dot(merged.astype(BF16), wo_ref[...], preferred_element_type=F32)

    @pl.when(n == 0)
    def _():
        r = _row_rsqrt(h_ref[...])
        r_ref[...] = r
        o_ref[...] = chunk(r)

    @pl.when(jnp.logical_and(n > 0, n < last))
    def _():
        o_ref[...] += chunk(r_ref[...])

    @pl.when(n == last)
    def _():
        o_ref[...] = h_ref[...] + gt_ref[...] * (o_ref[...] + chunk(r_ref[...]))


def _merge(h, mod, layer, br_lat, br_ctx, cbo, u, lw, n_rows):
    tm, tn = TM_MERGE, TN_MERGE
    tpb = SEQ // tm
    n_lat = T_LAT // tm
    with_ctx = br_ctx is not None
    last_blk = T_ALL // 8 - 1
    br_specs = [pl.BlockSpec((tm, BR_W), lambda i, n: (jnp.minimum(i, n_lat - 1), 0))] * 3
    if with_ctx:
        br_specs += [pl.BlockSpec((tm, BR_W), lambda i, n: (jnp.maximum(i - n_lat, 0), 0))] * 3

    def row_spec(w):
        return pl.BlockSpec((tm, w), lambda i, n: (i, 0))

    return pl.pallas_call(
        functools.partial(_merge_kernel, n_lat, with_ctx),
        grid=(n_rows // tm, D // tn),
        in_specs=[
            row_spec(D),
            pl.BlockSpec((None, 1, D), lambda i, n: (layer, 0, 0)),
            _mod_spec(layer, 3, tpb),
            _mod_spec(layer, 4, tpb),
            _mod_spec(layer, 5, tpb),
            *br_specs,
            row_spec(CONV_W), row_spec(CONV_W),
            pl.BlockSpec((8, CONV_W), lambda i, n: (jnp.maximum(i * (tm // 8) - 1, 0), 0)),
            pl.BlockSpec((8, CONV_W), lambda i, n: (jnp.minimum((i + 1) * (tm // 8), last_blk), 0)),
            pl.BlockSpec((None, 3, CONV_W), lambda i, n: (layer, 0, 0)),
            pl.BlockSpec((None, 1, CONV_W), lambda i, n: (layer, 0, 0)),
            pl.BlockSpec((None, N_BR, D, tn), lambda i, n: (layer, 0, 0, n)),
            pl.BlockSpec((None, N_BR, 1, tn), lambda i, n: (layer, 0, 0, n)),
            pl.BlockSpec((None, N_BR, BR_W, tn), lambda i, n: (layer, 0, 0, n)),
            pl.BlockSpec((None, tn, D), lambda i, n: (layer, n, 0)),
        ],
        out_specs=row_spec(D),
        out_shape=jax.ShapeDtypeStruct((n_rows, D), F32),
        scratch_shapes=[pltpu.VMEM((tm, 1), F32)],
        compiler_params=_cparams(("arbitrary", "arbitrary")),
        name="merge",
    )(h, lw["norm_mix"], mod, mod, mod, *br_lat, *(br_ctx or ()), cbo, u, u, u, lw["conv_w"],
      lw["conv_b"], lw["w_gate"], lw["b_gate"], lw["w_branch"], lw["w_o"])


def _rope_tables():
    t = jnp.arange(SEQ, dtype=jnp.int32)
    pos_row = (t // GRID_W).astype(F32)
    pos_col = (t % GRID_W).astype(F32)

    def ang(rot_dim):
        n = rot_dim // 4
        inv_freq = THETA ** (-jnp.arange(n, dtype=F32) / n)
        a = jnp.concatenate([pos_row[:, None] * inv_freq, pos_col[:, None] * inv_freq], axis=-1)
        return jnp.cos(a), jnp.sin(a)

    ca, sa = ang(ROPE_A)
    z32 = jnp.zeros_like(ca)
    z64 = jnp.zeros((SEQ, 64), F32)
    tab_c = jnp.concatenate([ca, ca, z64], axis=-1)
    tab_s1 = jnp.concatenate([-sa, z32, z64], axis=-1)
    tab_s2 = jnp.concatenate([z32, sa, z64], axis=-1)
    cb, sb = ang(HD_B)
    tab_cb = jnp.concatenate([cb, cb], axis=-1)
    tab_sb = jnp.concatenate([-sb, sb], axis=-1)
    ones = jnp.ones((TM_PROJ, 128), F32)
    zeros = jnp.zeros((TM_PROJ, 128), F32)
    return tuple(jnp.concatenate([tab, ident], axis=0)
                 for tab, ident in ((tab_c, ones), (tab_s1, zeros), (tab_s2, zeros),
                                    (tab_cb, ones), (tab_sb, zeros)))


def _dft_mats(n):
    j = jnp.arange(n, dtype=jnp.int32)
    a = ((j[:, None] * j[None, :]) % n).astype(F32) * (2.0 * math.pi / n)
    return jnp.cos(a), jnp.sin(a)


def _dft4_tables():
    nq = SEQ // 4
    c_q, s_q = _dft_mats(nq)
    k = jnp.arange(nq, dtype=jnp.int32)[:, None]
    ang = (k * jnp.arange(1, 4, dtype=jnp.int32)[None, :]).astype(F32) * (2.0 * math.pi / SEQ)
    tw = jnp.stack([jnp.cos(ang), jnp.sin(ang)], axis=-1).reshape(nq, 6)
    return c_q.astype(BF16), s_q.astype(BF16), _pad_to(tw, 1, 8)


def _pad_to(a, axis, size):
    pad = [(0, 0)] * a.ndim
    pad[axis] = (0, size - a.shape[axis])
    return jnp.pad(a, pad)


def _prep_weights(p):
    w_uq = p["w_uq"].reshape(DEPTH, Q_LORA, HEADS, QK_A)
    w_uq = _pad_to(_pad_to(w_uq, 3, QK_A_PAD), 1, Q_LORA_PAD).reshape(DEPTH, Q_LORA_PAD, HEADS * QK_A_PAD)
    w_ukv = p["w_ukv"].reshape(DEPTH, KV_LORA, HEADS, NOPE + V_A)
    w_ukv = jnp.concatenate([w_ukv[..., :NOPE].reshape(DEPTH, KV_LORA, HEADS * NOPE),
                             w_ukv[..., NOPE:].reshape(DEPTH, KV_LORA, HEADS * V_A)], axis=2)
    g_ka = p["g_ka"]

    return {
        "w_in": p["w_in"].astype(BF16),
        "g_cq": _pad_to(p["g_cq"], 1, Q_LORA_PAD)[:, None, :],
        "w_uq": w_uq.astype(BF16),
        "g_ckv": p["g_ckv"][:, None, :],
        "w_ukv": w_ukv.astype(BF16),
        "g_qa": _pad_to(p["g_qa"], 1, QK_A_PAD)[:, None, :],
        "g_ka_nope": g_ka[:, None, :NOPE],
        "g_ka_pe": _pad_to(g_ka[:, NOPE:], 1, 128)[:, None, :],
        "g_qb": p["g_qb"][:, None, :],
        "g_kb": p["g_kb"][:, None, :],
        "conv_w": p["conv_w"],
        "conv_b": p["conv_b"][:, None, :],
        "w_gate": p["w_gate"].astype(BF16),
        "b_gate": p["b_gate"][:, :, None, :],
        "w_branch": p["w_branch"].astype(BF16),
        "w_o": p["w_o"].astype(BF16),
        "norm_ffn1": p["norm_ffn1"][:, None, :],
        "norm_mix": p["norm_mix"][:, None, :],
        "norm_ffn2": p["norm_ffn2"][:, None, :],
        "ffn1_wi": p["ffn1_wi"].astype(BF16), "ffn1_wo": p["ffn1_wo"].astype(BF16),
        "ffn2_wi": p["ffn2_wi"].astype(BF16), "ffn2_wo": p["ffn2_wo"].astype(BF16),
    }


def kernel(x, c, ctx, c_ctx, w_ada, b_ada, norm_ffn1, ffn1_wi, ffn1_wo, norm_mix, w_in, g_cq, w_uq, g_ckv, w_ukv, g_qa, g_ka, g_qb, g_kb, conv_w, conv_b, w_branch, w_gate, b_gate, w_o, norm_ffn2, ffn2_wi, ffn2_wo):
    p = dict(norm_ffn1=norm_ffn1, ffn1_wi=ffn1_wi, ffn1_wo=ffn1_wo, norm_mix=norm_mix, w_in=w_in,
             g_cq=g_cq, w_uq=w_uq, g_ckv=g_ckv, w_ukv=w_ukv, g_qa=g_qa, g_ka=g_ka, g_qb=g_qb,
             g_kb=g_kb, conv_w=conv_w, conv_b=conv_b, w_branch=w_branch, w_gate=w_gate,
             b_gate=b_gate, w_o=w_o, norm_ffn2=norm_ffn2, ffn2_wi=ffn2_wi, ffn2_wo=ffn2_wo)

    cond8 = jnp.concatenate([c, c_ctx[None, :], jnp.zeros((8 - BATCH - 1, D), F32)], axis=0)
    mod = _ada(cond8, w_ada, b_ada).reshape(DEPTH, 8, N_MOD, 1, D)

    tabs = _rope_tables()
    cc, sc = _dft_mats(FGD)
    ccs = jnp.concatenate([cc, sc], axis=1).astype(BF16)
    dft_lat = _dft4_tables()
    dft_ctx = tuple(m.astype(BF16) for m in _dft_mats(CTX))

    lw = _prep_weights(p)
    h = jnp.concatenate([x.reshape(T_LAT, D), ctx.reshape(T_CTX, D)], axis=0)
    for l in range(DEPTH):
        last = l == DEPTH - 1
        h = _ffn(h, mod, l, 0, lw["norm_ffn1"], lw["ffn1_wi"], lw["ffn1_wo"], T_ALL)
        qa, ka, va, qb, kb, vb, cbo, u, pc, ps = _proj(h, mod, l, lw, tabs, ccs)
        oa, oa_c = _attention(qa, ka, va, QK_A_PAD, V_A, 1, not last)
        ob, ob_c = _attention(qb, kb, vb, HD_B, HD_B, GROUP_B, not last)
        of, of_c = _fourier(pc, ps, dft_lat, dft_ctx, not last)
        n_rows = T_LAT if last else T_ALL
        h = _merge(h, mod, l, (oa, ob, of), None if last else (oa_c, ob_c, of_c), cbo, u, lw, n_rows)
        h = _ffn(h, mod, l, 6, lw["norm_ffn2"], lw["ffn2_wi"], lw["ffn2_wo"], n_rows)
    return h.reshape(BATCH, SEQ, D)
```

```python
import functools
import math

import jax
import jax.numpy as jnp
from jax import lax
from jax.experimental import pallas as pl
from jax.experimental.pallas import tpu as pltpu

F32 = jnp.float32
BF16 = jnp.bfloat16

D = 2048
BATCH = 4
SEQ = 4096
DEPTH = 4
GRID_W = 64
CTX = 256
N_MOD = 9
F_FF = 5504
HEADS = 4
Q_LORA = 448
Q_LORA_PAD = 512
KV_LORA = 128
NOPE = 128
ROPE_A = 64
QK_A = NOPE + ROPE_A
QK_A_PAD = 256
V_A = 128
HD_B = 128
KVH_B = 2
GROUP_B = 2
CONV_W = 512
FG = 4
FGD = 128
BR_W = 512
N_BR = 4
THETA = 10000.0
EPS = 1e-6
LOG2_E = math.log2(math.e)

T_LAT = BATCH * SEQ
T_CTX = BATCH * CTX
T_ALL = T_LAT + T_CTX

W_IN = 3712
OFF_A = 0
OFF_B = 640
OFF_C = 1664
OFF_F = 3200

LANE = 128
VMEM_PHYS_V7X = 64 * 1024 * 1024
VMEM_LIMIT = 60 * 1024 * 1024

TM_FFN = 1024
TF_FFN = 512
TM_PROJ = 512
TM_MERGE = 512
TN_MERGE = 512
TQ_ATT = 1024
KC_ATT = 512
TK_DFT = 512
TN_ADA = 1024


def _cparams(sem):
    return pltpu.CompilerParams(dimension_semantics=sem, vmem_limit_bytes=VMEM_LIMIT)


def _mod_row(i, tiles_per_batch):
    return jnp.minimum(i // tiles_per_batch, BATCH)


def _row_rsqrt(x):
    return lax.rsqrt(jnp.mean(x * x, axis=-1, keepdims=True) + EPS)


def _scale_mod(x, r, gain, shift, scale):
    return (x * r * gain) * (1.0 + scale) + shift


def _norm_mod(x, gain, shift, scale):
    return _scale_mod(x, _row_rsqrt(x), gain, shift, scale)


def _silu(x):
    return x * jax.nn.sigmoid(x)


def _ada_kernel(cond_ref, w_ref, b_ref, o_ref):
    s = _silu(cond_ref[...]).astype(BF16)
    o_ref[...] = jnp.dot(s, w_ref[...].astype(BF16), preferred_element_type=F32) + b_ref[...]


def _ada(cond8, w_ada, b_ada):
    n = N_MOD * D
    return pl.pallas_call(
        _ada_kernel,
        grid=(DEPTH, n // TN_ADA),
        in_specs=[
            pl.BlockSpec((8, D), lambda l, j: (0, 0)),
            pl.BlockSpec((None, D, TN_ADA), lambda l, j: (l, 0, j)),
            pl.BlockSpec((None, 1, TN_ADA), lambda l, j: (l, 0, j)),
        ],
        out_specs=pl.BlockSpec((None, 8, TN_ADA), lambda l, j: (l, 0, j)),
        out_shape=jax.ShapeDtypeStruct((DEPTH, 8, n), F32),
        compiler_params=_cparams(("arbitrary", "arbitrary")),
        name="ada",
    )(cond8, w_ada, b_ada.reshape(DEPTH, 1, n))


def _mod_spec(layer, j, tiles_per_batch):
    return pl.BlockSpec((None, None, None, 1, D),
                        lambda i, *_: (layer, _mod_row(i, tiles_per_batch), j, 0, 0))


def _ffn_kernel(overlap, x_ref, gain_ref, sh_ref, sc_ref, gt_ref, wg_ref, wu_ref, wo_ref, o_ref, r_ref):
    f = pl.program_id(1)
    last = pl.num_programs(1) - 1

    def chunk(r, skip=0):
        xn = _scale_mod(x_ref[...], r, gain_ref[...], sh_ref[...], sc_ref[...]).astype(BF16)
        g = jnp.dot(xn, wg_ref[:, skip:], preferred_element_type=F32)
        u = jnp.dot(xn, wu_ref[:, skip:], preferred_element_type=F32)
        a = (_silu(g) * u).astype(BF16)
        return jnp.dot(a, wo_ref[skip:, :], preferred_element_type=F32)

    @pl.when(f == 0)
    def _():
        r = _row_rsqrt(x_ref[...])
        r_ref[...] = r
        o_ref[...] = chunk(r)

    @pl.when(jnp.logical_and(f > 0, f < last))
    def _():
        o_ref[...] += chunk(r_ref[...])

    @pl.when(f == last)
    def _():
        o_ref[...] = x_ref[...] + (0.5 * gt_ref[...]) * (o_ref[...] + chunk(r_ref[...], overlap))


def _ffn(h, mod, layer, j0, gain, wi, wo, n_rows):
    tm, tf = TM_FFN, TF_FFN
    tpb = SEQ // tm
    nf = pl.cdiv(F_FF, tf)

    def hid(f, base=0):
        return (base // LANE + jnp.minimum(f * (tf // LANE), (F_FF - tf) // LANE)) * LANE

    return pl.pallas_call(
        functools.partial(_ffn_kernel, nf * tf - F_FF),
        grid=(n_rows // tm, nf),
        in_specs=[
            pl.BlockSpec((tm, D), lambda i, f: (i, 0)),
            pl.BlockSpec((None, 1, D), lambda i, f: (layer, 0, 0)),
            _mod_spec(layer, j0, tpb),
            _mod_spec(layer, j0 + 1, tpb),
            _mod_spec(layer, j0 + 2, tpb),
            pl.BlockSpec((None, pl.Element(D), pl.Element(tf)), lambda i, f: (layer, 0, hid(f))),
            pl.BlockSpec((None, pl.Element(D), pl.Element(tf)), lambda i, f: (layer, 0, hid(f, F_FF))),
            pl.BlockSpec((None, pl.Element(tf), pl.Element(D)), lambda i, f: (layer, hid(f), 0)),
        ],
        out_specs=pl.BlockSpec((tm, D), lambda i, f: (i, 0)),
        out_shape=jax.ShapeDtypeStruct((n_rows, D), F32),
        scratch_shapes=[pltpu.VMEM((tm, 1), F32)],
        compiler_params=_cparams(("arbitrary", "arbitrary")),
        name="ffn",
    )(h, gain, mod, mod, mod, wi, wi, wo)


def _rope_a(v, c, s1, s2):
    return v * c + pltpu.roll(v, 96, 1) * s1 + pltpu.roll(v, 32, 1) * s2


def _rope_b(v, c, s):
    return v * c + pltpu.roll(v, 64, 1) * s


def _proj_kernel(h_ref, gain_ref, sh_ref, sc_ref, w_ref, gcq_ref, wuq_ref, gckv_ref, wukv_ref,
                 gqa_ref, gkan_ref, gkap_ref, gqb_ref, gkb_ref, ca_ref, s1a_ref, s2a_ref,
                 cb_ref, sb_ref, ccs_ref,
                 qa_ref, ka_ref, va_ref, qb_ref, kb_ref, vb_ref, cbo_ref, u_ref, pc_ref, ps_ref,
                 pp_ref):
    xn = _norm_mod(h_ref[...], gain_ref[...], sh_ref[...], sc_ref[...]).astype(BF16)
    ca, s1a, s2a = ca_ref[...], s1a_ref[...], s2a_ref[...]
    cb, sb = cb_ref[...], sb_ref[...]
    scale_a = QK_A ** -0.5 * LOG2_E
    scale_b = HD_B ** -0.5 * LOG2_E

    za = jnp.dot(xn, w_ref[:, OFF_A:OFF_B], preferred_element_type=F32)
    hi = lax.broadcasted_iota(jnp.int32, (1, LANE), 1) >= LANE // 2
    s3, s4 = za[:, 3 * LANE:4 * LANE], za[:, 4 * LANE:]
    aq = jnp.concatenate([za[:, :3 * LANE], jnp.where(hi, 0.0, s3)], axis=1)
    akv = pltpu.roll(jnp.where(hi, s3, s4), LANE // 2, 1)
    ape = jnp.where(hi, 0.0, pltpu.roll(s4, LANE // 2, 1))
    r = lax.rsqrt(jnp.sum(aq * aq, axis=-1, keepdims=True) * (1.0 / Q_LORA) + EPS)
    cq = (aq * r * gcq_ref[...]).astype(BF16)
    q = jnp.dot(cq, wuq_ref[...], preferred_element_type=F32)
    gqa = gqa_ref[...]
    for hd in range(HEADS):
        lo = hd * QK_A_PAD
        qh = q[:, lo:lo + QK_A_PAD]
        r = lax.rsqrt(jnp.sum(qh * qh, axis=-1, keepdims=True) * (1.0 / QK_A) + EPS)
        qh = qh * r * gqa
        qa_ref[:, lo:lo + NOPE] = (qh[:, :NOPE] * scale_a).astype(BF16)
        qa_ref[:, lo + NOPE:lo + QK_A_PAD] = (_rope_a(qh[:, NOPE:], ca, s1a, s2a) * scale_a).astype(BF16)
    r = lax.rsqrt(jnp.mean(akv * akv, axis=-1, keepdims=True) + EPS)
    ckv = (akv * r * gckv_ref[...]).astype(BF16)
    kv = jnp.dot(ckv, wukv_ref[...], preferred_element_type=F32)
    ss_pe = jnp.sum(ape * ape, axis=-1, keepdims=True)
    gkan, gkap = gkan_ref[...], gkap_ref[...]
    for hd in range(HEADS):
        kn = kv[:, hd * NOPE:(hd + 1) * NOPE]
        r = lax.rsqrt((jnp.sum(kn * kn, axis=-1, keepdims=True) + ss_pe) * (1.0 / QK_A) + EPS)
        lo = hd * QK_A_PAD
        ka_ref[:, lo:lo + NOPE] = (kn * r * gkan).astype(BF16)
        ka_ref[:, lo + NOPE:lo + QK_A_PAD] = _rope_a(ape * r * gkap, ca, s1a, s2a).astype(BF16)
        va_ref[:, hd * V_A:(hd + 1) * V_A] = kv[:, HEADS * NOPE + hd * V_A:HEADS * NOPE + (hd + 1) * V_A].astype(BF16)

    zb = jnp.dot(xn, w_ref[:, OFF_B:OFF_C], preferred_element_type=F32)
    gqb, gkb = gqb_ref[...], gkb_ref[...]
    for hd in range(HEADS):
        qh = zb[:, hd * HD_B:(hd + 1) * HD_B]
        r = lax.rsqrt(jnp.mean(qh * qh, axis=-1, keepdims=True) + EPS)
        qb_ref[:, hd * HD_B:(hd + 1) * HD_B] = (_rope_b(qh * r * gqb, cb, sb) * scale_b).astype(BF16)
    for hd in range(KVH_B):
        lo = HEADS * HD_B + hd * HD_B
        kh = zb[:, lo:lo + HD_B]
        r = lax.rsqrt(jnp.mean(kh * kh, axis=-1, keepdims=True) + EPS)
        kb_ref[:, hd * HD_B:(hd + 1) * HD_B] = _rope_b(kh * r * gkb, cb, sb).astype(BF16)
    vb_ref[...] = zb[:, (HEADS + KVH_B) * HD_B:].astype(BF16)

    zc = jnp.dot(xn, w_ref[:, OFF_C:OFF_F], preferred_element_type=F32)
    cbo_ref[...] = zc[:, :CONV_W]
    u_ref[...] = zc[:, CONV_W:2 * CONV_W] * zc[:, 2 * CONV_W:]

    zf = jnp.dot(xn, w_ref[:, OFF_F:], preferred_element_type=F32).astype(BF16)
    ccs = ccs_ref[...]
    tm = h_ref.shape[0]
    w = FG * FGD
    for g in range(FG):
        pp = jnp.dot(zf[:, g * FGD:(g + 1) * FGD], ccs, preferred_element_type=F32)
        pp_ref[g] = pp[:, :FGD]
        pp_ref[FG + g] = pp[:, FGD:]
    for r in range(4):
        for g in range(FG):
            lo = r * w + g * FGD
            pc_ref[:, lo:lo + FGD] = pp_ref[g, pl.ds(r, tm // 4, stride=4), :].astype(BF16)
            ps_ref[:, lo:lo + FGD] = pp_ref[FG + g, pl.ds(r, tm // 4, stride=4), :].astype(BF16)


def _const_spec(shape, layer=None):
    nd = len(shape)
    if layer is None:
        return pl.BlockSpec(shape, lambda i, *_: (0,) * nd, pipeline_mode=pl.Buffered(1))
    return pl.BlockSpec((None,) + shape, lambda i, *_: (layer,) + (0,) * nd, pipeline_mode=pl.Buffered(1))


def _proj(h, mod, layer, lw, tabs, ccs):
    tm = TM_PROJ
    tpb = SEQ // tm
    n_lat = T_LAT // tm

    def tab_spec():
        return pl.BlockSpec((tm, 128), lambda i: (jnp.where(i < n_lat, i % tpb, tpb), 0))

    def row_spec(w):
        return pl.BlockSpec((tm, w), lambda i: (i, 0))

    widths = (HEADS * QK_A_PAD, HEADS * QK_A_PAD, HEADS * V_A, HEADS * HD_B, KVH_B * HD_B,
              KVH_B * HD_B, CONV_W, CONV_W)
    dtypes = (BF16, BF16, BF16, BF16, BF16, BF16, F32, F32)
    dft_spec = pl.BlockSpec((tm // 4, 4 * FG * FGD), lambda i: (i, 0))
    dft_shape = jax.ShapeDtypeStruct((T_ALL // 4, 4 * FG * FGD), BF16)
    return pl.pallas_call(
        _proj_kernel,
        grid=(T_ALL // tm,),
        in_specs=[
            row_spec(D),
            _const_spec((1, D), layer),
            _mod_spec(layer, 3, tpb),
            _mod_spec(layer, 4, tpb),
            _const_spec((D, W_IN), layer),
            _const_spec((1, Q_LORA_PAD), layer),
            _const_spec((Q_LORA_PAD, HEADS * QK_A_PAD), layer),
            _const_spec((1, KV_LORA), layer),
            _const_spec((KV_LORA, HEADS * (NOPE + V_A)), layer),
            _const_spec((1, QK_A_PAD), layer),
            _const_spec((1, NOPE), layer),
            _const_spec((1, 128), layer),
            _const_spec((1, HD_B), layer),
            _const_spec((1, HD_B), layer),
            tab_spec(), tab_spec(), tab_spec(), tab_spec(), tab_spec(),
            _const_spec((FGD, 2 * FGD)),
        ],
        out_specs=[row_spec(w) for w in widths] + [dft_spec, dft_spec],
        out_shape=[jax.ShapeDtypeStruct((T_ALL, w), dt) for w, dt in zip(widths, dtypes)]
        + [dft_shape, dft_shape],
        scratch_shapes=[pltpu.VMEM((2 * FG, tm, FGD), F32)],
        compiler_params=_cparams(("arbitrary",)),
        name="proj",
    )(h, lw["norm_mix"], mod, mod, lw["w_in"], lw["g_cq"], lw["w_uq"], lw["g_ckv"], lw["w_ukv"],
      lw["g_qa"], lw["g_ka_nope"], lw["g_ka_pe"], lw["g_qb"], lw["g_kb"], *tabs, ccs)


_QK_DIMS = (((1,), (1,)), ((), ()))


def _att_lat_kernel(q_ref, k_ref, kc_ref, v_ref, vc_ref, o_ref):
    q = q_ref[...]
    chunks = [(k_ref, v_ref, j * KC_ATT, KC_ATT) for j in range(SEQ // KC_ATT)] + [(kc_ref, vc_ref, 0, CTX)]

    def scores(chunk):
        kr, _, lo, n = chunk
        return lax.dot_general(q, kr[lo:lo + n, :], _QK_DIMS, preferred_element_type=F32)

    s = scores(chunks[0])
    m = l = acc = None
    for idx, (_, vr, lo, n) in enumerate(chunks):
        s_next = scores(chunks[idx + 1]) if idx + 1 < len(chunks) else None
        m_chunk = jnp.max(s, axis=-1, keepdims=True)
        m_new = m_chunk if m is None else jnp.maximum(m, m_chunk)
        p = jnp.exp2(s - m_new)
        pv = jnp.dot(p.astype(BF16), vr[lo:lo + n, :], preferred_element_type=F32)
        if m is None:
            l = jnp.sum(p, axis=-1, keepdims=True)
            acc = pv
        else:
            alpha = jnp.exp2(m - m_new)
            l = alpha * l + jnp.sum(p, axis=-1, keepdims=True)
            acc = alpha * acc + pv
        m, s = m_new, s_next
    o_ref[...] = (acc / l).astype(o_ref.dtype)


def _att_ctx_kernel(q_ref, k_ref, v_ref, o_ref):
    s = lax.dot_general(q_ref[...], k_ref[...], _QK_DIMS, preferred_element_type=F32)
    m = jnp.max(s, axis=-1, keepdims=True)
    p = jnp.exp2(s - m)
    l = jnp.sum(p, axis=-1, keepdims=True)
    o = jnp.dot(p.astype(BF16), v_ref[...], preferred_element_type=F32)
    o_ref[...] = (o / l).astype(o_ref.dtype)


def _attention(q, k, v, dq, dv, group, with_ctx):
    tq = TQ_ATT
    nq = SEQ // tq
    ctx0 = T_LAT // CTX
    out = pl.pallas_call(
        _att_lat_kernel,
        grid=(BATCH, HEADS, nq),
        in_specs=[
            pl.BlockSpec((tq, dq), lambda b, h, i: (b * nq + i, h)),
            pl.BlockSpec((SEQ, dq), lambda b, h, i: (b, h // group)),
            pl.BlockSpec((CTX, dq), lambda b, h, i: (ctx0 + b, h // group)),
            pl.BlockSpec((SEQ, dv), lambda b, h, i: (b, h // group)),
            pl.BlockSpec((CTX, dv), lambda b, h, i: (ctx0 + b, h // group)),
        ],
        out_specs=pl.BlockSpec((tq, dv), lambda b, h, i: (b * nq + i, h)),
        out_shape=jax.ShapeDtypeStruct((T_LAT, HEADS * dv), BF16),
        compiler_params=_cparams(("arbitrary", "arbitrary", "arbitrary")),
        name="att_lat",
    )(q, k, k, v, v)
    if not with_ctx:
        return out, None
    out_ctx = pl.pallas_call(
        _att_ctx_kernel,
        grid=(BATCH, HEADS),
        in_specs=[
            pl.BlockSpec((CTX, dq), lambda b, h: (ctx0 + b, h)),
            pl.BlockSpec((CTX, dq), lambda b, h: (ctx0 + b, h // group)),
            pl.BlockSpec((CTX, dv), lambda b, h: (ctx0 + b, h // group)),
        ],
        out_specs=pl.BlockSpec((CTX, dv), lambda b, h: (b, h)),
        out_shape=jax.ShapeDtypeStruct((T_CTX, HEADS * dv), BF16),
        compiler_params=_cparams(("arbitrary", "arbitrary")),
        name="att_ctx",
    )(q, k, v)
    return out, out_ctx


def _dft4_kernel(scale, c_ref, s_ref, tw_ref, pc_ref, ps_ref, o_ref):
    w = o_ref.shape[-1]
    c, s = c_ref[...], s_ref[...]
    pc, ps = pc_ref[...], ps_ref[...]
    ur = jnp.dot(c, pc, preferred_element_type=F32) - jnp.dot(s, ps, preferred_element_type=F32)
    ui = -(jnp.dot(c, ps, preferred_element_type=F32) + jnp.dot(s, pc, preferred_element_type=F32))
    a = [ur[:, :w]]
    b = [None]
    for r in range(1, 4):
        u_r, v_r = ur[:, r * w:(r + 1) * w], ui[:, r * w:(r + 1) * w]
        cr, sr = tw_ref[:, 2 * r - 2:2 * r - 1], tw_ref[:, 2 * r - 1:2 * r]
        a.append(u_r * cr + v_r * sr)
        b.append(None if r == 2 else v_r * cr - u_r * sr)
    o_ref[0] = ((a[0] + a[1] + a[2] + a[3]) * scale).astype(o_ref.dtype)
    o_ref[1] = ((a[0] + b[1] - a[2] - b[3]) * scale).astype(o_ref.dtype)
    o_ref[2] = ((a[0] - a[1] + a[2] - a[3]) * scale).astype(o_ref.dtype)
    o_ref[3] = ((a[0] - b[1] - a[2] + b[3]) * scale).astype(o_ref.dtype)


def _dft4(pc4, ps4, tables, n, tk, first_block, name):
    w = FG * FGD
    nq = n // 4
    c_q, s_q, tw = tables
    return pl.pallas_call(
        functools.partial(_dft4_kernel, (n * FGD) ** -0.5),
        grid=(BATCH, nq // tk),
        in_specs=[
            pl.BlockSpec((tk, nq), lambda b, k: (k, 0)),
            pl.BlockSpec((tk, nq), lambda b, k: (k, 0)),
            pl.BlockSpec((tk, 8), lambda b, k: (k, 0)),
            pl.BlockSpec((nq, 4 * w), lambda b, k: (first_block + b, 0)),
            pl.BlockSpec((nq, 4 * w), lambda b, k: (first_block + b, 0)),
        ],
        out_specs=pl.BlockSpec((None, 4, tk, w), lambda b, k: (b, 0, k, 0)),
        out_shape=jax.ShapeDtypeStruct((BATCH, 4, nq, w), BF16),
        compiler_params=_cparams(("arbitrary", "arbitrary")),
        name=name,
    )(c_q, s_q, tw, pc4, ps4).reshape(BATCH * n, w)


def _fourier(pc4, ps4, dft_lat, dft_ctx, with_ctx):
    out = _dft4(pc4, ps4, dft_lat, SEQ, TK_DFT, 0, "dft_lat")
    if not with_ctx:
        return out, None
    return out, _dft4(pc4, ps4, dft_ctx, CTX, CTX // 4, T_LAT // CTX, "dft_ctx")


def _merge_kernel(n_lat_tiles, with_ctx, h_ref, gain_ref, sh_ref, sc_ref, gt_ref, *refs):
    lat_refs, refs = refs[:3], refs[3:]
    if with_ctx:
        ctx_refs, refs = refs[:3], refs[3:]
    (cbo_ref, u_ref, up_ref, un_ref, cw_ref, cbias_ref, wg_ref, bg_ref, wb_ref, wo_ref,
     o_ref, r_ref) = refs
    i = pl.program_id(0)
    n = pl.program_id(1)
    last = pl.num_programs(1) - 1
    tm = h_ref.shape[0]

    def chunk(r):
        xn = _scale_mod(h_ref[...], r, gain_ref[...], sh_ref[...], sc_ref[...]).astype(BF16)

        u = u_ref[...]
        is_lat = i < n_lat_tiles
        seq_len = jnp.where(is_lat, SEQ, CTX)
        ridx = lax.broadcasted_iota(jnp.int32, (tm, 1), 0)
        pos = (i * tm + ridx) & (seq_len - 1)
        prev = jnp.where(ridx == 0, up_ref[7:8, :], pltpu.roll(u, 1, 0))
        prev = jnp.where(pos == 0, 0.0, prev)
        nxt = jnp.where(ridx == tm - 1, un_ref[0:1, :], pltpu.roll(u, tm - 1, 0))
        nxt = jnp.where(pos == seq_len - 1, 0.0, nxt)
        y = prev * cw_ref[0:1, :] + u * cw_ref[1:2, :] + nxt * cw_ref[2:3, :] + cbias_ref[...]
        conv = (cbo_ref[...] * y).astype(BF16)

        if with_ctx:
            oa, ob, of = (jnp.where(is_lat, lr[...], cr[...]) for lr, cr in zip(lat_refs, ctx_refs))
        else:
            oa, ob, of = (lr[...] for lr in lat_refs)

        merged = None
        for k, br in enumerate((oa, ob, conv, of)):
            gate = jax.nn.sigmoid(jnp.dot(xn, wg_ref[k], preferred_element_type=F32) + bg_ref[k])
            term = gate * jnp.dot(br, wb_ref[k], preferred_element_type=F32)
            merged = term if merged is None else merged + term
        return jnp.dot(merged.astype(BF16), wo_ref[...], preferred_element_type=F32)

    @pl.when(n == 0)
    def _():
        r = _row_rsqrt(h_ref[...])
        r_ref[...] = r
        o_ref[...] = chunk(r)

    @pl.when(jnp.logical_and(n > 0, n < last))
    def _():
        o_ref[...] += chunk(r_ref[...])

    @pl.when(n == last)
    def _():
        o_ref[...] = h_ref[...] + gt_ref[...] * (o_ref[...] + chunk(r_ref[...]))


def _merge(h, mod, layer, br_lat, br_ctx, cbo, u, lw, n_rows):
    tm, tn = TM_MERGE, TN_MERGE
    tpb = SEQ // tm
    n_lat = T_LAT // tm
    with_ctx = br_ctx is not None
    last_blk = T_ALL // 8 - 1
    br_specs = [pl.BlockSpec((tm, BR_W), lambda i, n: (jnp.minimum(i, n_lat - 1), 0))] * 3
    if with_ctx:
        br_specs += [pl.BlockSpec((tm, BR_W), lambda i, n: (jnp.maximum(i - n_lat, 0), 0))] * 3

    def row_spec(w):
        return pl.BlockSpec((tm, w), lambda i, n: (i, 0))

    return pl.pallas_call(
        functools.partial(_merge_kernel, n_lat, with_ctx),
        grid=(n_rows // tm, D // tn),
        in_specs=[
            row_spec(D),
            pl.BlockSpec((None, 1, D), lambda i, n: (layer, 0, 0)),
            _mod_spec(layer, 3, tpb),
            _mod_spec(layer, 4, tpb),
            _mod_spec(layer, 5, tpb),
            *br_specs,
            row_spec(CONV_W), row_spec(CONV_W),
            pl.BlockSpec((8, CONV_W), lambda i, n: (jnp.maximum(i * (tm // 8) - 1, 0), 0)),
            pl.BlockSpec((8, CONV_W), lambda i, n: (jnp.minimum((i + 1) * (tm // 8), last_blk), 0)),
            pl.BlockSpec((None, 3, CONV_W), lambda i, n: (layer, 0, 0)),
            pl.BlockSpec((None, 1, CONV_W), lambda i, n: (layer, 0, 0)),
            pl.BlockSpec((None, N_BR, D, tn), lambda i, n: (layer, 0, 0, n)),
            pl.BlockSpec((None, N_BR, 1, tn), lambda i, n: (layer, 0, 0, n)),
            pl.BlockSpec((None, N_BR, BR_W, tn), lambda i, n: (layer, 0, 0, n)),
            pl.BlockSpec((None, tn, D), lambda i, n: (layer, n, 0)),
        ],
        out_specs=row_spec(D),
        out_shape=jax.ShapeDtypeStruct((n_rows, D), F32),
        scratch_shapes=[pltpu.VMEM((tm, 1), F32)],
        compiler_params=_cparams(("arbitrary", "arbitrary")),
        name="merge",
    )(h, lw["norm_mix"], mod, mod, mod, *br_lat, *(br_ctx or ()), cbo, u, u, u, lw["conv_w"],
      lw["conv_b"], lw["w_gate"], lw["b_gate"], lw["w_branch"], lw["w_o"])


def _rope_tables():
    t = jnp.arange(SEQ, dtype=jnp.int32)
    pos_row = (t // GRID_W).astype(F32)
    pos_col = (t % GRID_W).astype(F32)

    def ang(rot_dim):
        n = rot_dim // 4
        inv_freq = THETA ** (-jnp.arange(n, dtype=F32) / n)
        a = jnp.concatenate([pos_row[:, None] * inv_freq, pos_col[:, None] * inv_freq], axis=-1)
        return jnp.cos(a), jnp.sin(a)

    ca, sa = ang(ROPE_A)
    z32 = jnp.zeros_like(ca)
    z64 = jnp.zeros((SEQ, 64), F32)
    tab_c = jnp.concatenate([ca, ca, z64], axis=-1)
    tab_s1 = jnp.concatenate([-sa, z32, z64], axis=-1)
    tab_s2 = jnp.concatenate([z32, sa, z64], axis=-1)
    cb, sb = ang(HD_B)
    tab_cb = jnp.concatenate([cb, cb], axis=-1)
    tab_sb = jnp.concatenate([-sb, sb], axis=-1)
    ones = jnp.ones((TM_PROJ, 128), F32)
    zeros = jnp.zeros((TM_PROJ, 128), F32)
    return tuple(jnp.concatenate([tab, ident], axis=0)
                 for tab, ident in ((tab_c, ones), (tab_s1, zeros), (tab_s2, zeros),
                                    (tab_cb, ones), (tab_sb, zeros)))


def _dft_mats(n):
    j = jnp.arange(n, dtype=jnp.int32)
    a = ((j[:, None] * j[None, :]) % n).astype(F32) * (2.0 * math.pi / n)
    return jnp.cos(a), jnp.sin(a)


def _dft4_tables(n):
    nq = n // 4
    c_q, s_q = _dft_mats(nq)
    k = jnp.arange(nq, dtype=jnp.int32)[:, None]
    ang = (k * jnp.arange(1, 4, dtype=jnp.int32)[None, :]).astype(F32) * (2.0 * math.pi / n)
    tw = jnp.stack([jnp.cos(ang), jnp.sin(ang)], axis=-1).reshape(nq, 6)
    return c_q.astype(BF16), s_q.astype(BF16), _pad_to(tw, 1, 8)


def _pad_to(a, axis, size):
    pad = [(0, 0)] * a.ndim
    pad[axis] = (0, size - a.shape[axis])
    return jnp.pad(a, pad)


def _prep_weights(p):
    w_uq = p["w_uq"].reshape(DEPTH, Q_LORA, HEADS, QK_A)
    w_uq = _pad_to(_pad_to(w_uq, 3, QK_A_PAD), 1, Q_LORA_PAD).reshape(DEPTH, Q_LORA_PAD, HEADS * QK_A_PAD)
    w_ukv = p["w_ukv"].reshape(DEPTH, KV_LORA, HEADS, NOPE + V_A)
    w_ukv = jnp.concatenate([w_ukv[..., :NOPE].reshape(DEPTH, KV_LORA, HEADS * NOPE),
                             w_ukv[..., NOPE:].reshape(DEPTH, KV_LORA, HEADS * V_A)], axis=2)
    g_ka = p["g_ka"]

    return {
        "w_in": p["w_in"].astype(BF16),
        "g_cq": _pad_to(p["g_cq"], 1, Q_LORA_PAD)[:, None, :],
        "w_uq": w_uq.astype(BF16),
        "g_ckv": p["g_ckv"][:, None, :],
        "w_ukv": w_ukv.astype(BF16),
        "g_qa": _pad_to(p["g_qa"], 1, QK_A_PAD)[:, None, :],
        "g_ka_nope": g_ka[:, None, :NOPE],
        "g_ka_pe": _pad_to(g_ka[:, NOPE:], 1, 128)[:, None, :],
        "g_qb": p["g_qb"][:, None, :],
        "g_kb": p["g_kb"][:, None, :],
        "conv_w": p["conv_w"],
        "conv_b": p["conv_b"][:, None, :],
        "w_gate": p["w_gate"].astype(BF16),
        "b_gate": p["b_gate"][:, :, None, :],
        "w_branch": p["w_branch"].astype(BF16),
        "w_o": p["w_o"].astype(BF16),
        "norm_ffn1": p["norm_ffn1"][:, None, :],
        "norm_mix": p["norm_mix"][:, None, :],
        "norm_ffn2": p["norm_ffn2"][:, None, :],
        "ffn1_wi": p["ffn1_wi"].astype(BF16), "ffn1_wo": p["ffn1_wo"].astype(BF16),
        "ffn2_wi": p["ffn2_wi"].astype(BF16), "ffn2_wo": p["ffn2_wo"].astype(BF16),
    }


def kernel(x, c, ctx, c_ctx, w_ada, b_ada, norm_ffn1, ffn1_wi, ffn1_wo, norm_mix, w_in, g_cq, w_uq, g_ckv, w_ukv, g_qa, g_ka, g_qb, g_kb, conv_w, conv_b, w_branch, w_gate, b_gate, w_o, norm_ffn2, ffn2_wi, ffn2_wo):
    p = dict(norm_ffn1=norm_ffn1, ffn1_wi=ffn1_wi, ffn1_wo=ffn1_wo, norm_mix=norm_mix, w_in=w_in,
             g_cq=g_cq, w_uq=w_uq, g_ckv=g_ckv, w_ukv=w_ukv, g_qa=g_qa, g_ka=g_ka, g_qb=g_qb,
             g_kb=g_kb, conv_w=conv_w, conv_b=conv_b, w_branch=w_branch, w_gate=w_gate,
             b_gate=b_gate, w_o=w_o, norm_ffn2=norm_ffn2, ffn2_wi=ffn2_wi, ffn2_wo=ffn2_wo)

    cond8 = jnp.concatenate([c, c_ctx[None, :], jnp.zeros((8 - BATCH - 1, D), F32)], axis=0)
    mod = _ada(cond8, w_ada, b_ada).reshape(DEPTH, 8, N_MOD, 1, D)

    tabs = _rope_tables()
    cc, sc = _dft_mats(FGD)
    ccs = jnp.concatenate([cc, sc], axis=1).astype(BF16)
    dft_lat = _dft4_tables(SEQ)
    dft_ctx = _dft4_tables(CTX)

    lw = _prep_weights(p)
    h = jnp.concatenate([x.reshape(T_LAT, D), ctx.reshape(T_CTX, D)], axis=0)
    for l in range(DEPTH):
        last = l == DEPTH - 1
        h = _ffn(h, mod, l, 0, lw["norm_ffn1"], lw["ffn1_wi"], lw["ffn1_wo"], T_ALL)
        qa, ka, va, qb, kb, vb, cbo, u, pc, ps = _proj(h, mod, l, lw, tabs, ccs)
        oa, oa_c = _attention(qa, ka, va, QK_A_PAD, V_A, 1, not last)
        ob, ob_c = _attention(qb, kb, vb, HD_B, HD_B, GROUP_B, not last)
        of, of_c = _fourier(pc, ps, dft_lat, dft_ctx, not last)
        n_rows = T_LAT if last else T_ALL
        h = _merge(h, mod, l, (oa, ob, of), None if last else (oa_c, ob_c, of_c), cbo, u, lw, n_rows)
        h = _ffn(h, mod, l, 6, lw["norm_ffn2"], lw["ffn2_wi"], lw["ffn2_wo"], n_rows)
    return h.reshape(BATCH, SEQ, D)
```

```python
import functools
import math

import jax
import jax.numpy as jnp
from jax import lax
from jax.experimental import pallas as pl
from jax.experimental.pallas import tpu as pltpu

F32 = jnp.float32
BF16 = jnp.bfloat16

D = 2048
BATCH = 4
SEQ = 4096
DEPTH = 4
GRID_W = 64
CTX = 256
N_MOD = 9
F_FF = 5504
HEADS = 4
Q_LORA = 448
Q_LORA_PAD = 512
KV_LORA = 128
NOPE = 128
ROPE_A = 64
QK_A = NOPE + ROPE_A
QK_A_PAD = 256
V_A = 128
HD_B = 128
KVH_B = 2
GROUP_B = 2
CONV_W = 512
FG = 4
FGD = 128
BR_W = 512
N_BR = 4
THETA = 10000.0
EPS = 1e-6
LOG2_E = math.log2(math.e)

T_LAT = BATCH * SEQ
T_CTX = BATCH * CTX
T_ALL = T_LAT + T_CTX

W_IN = 3712
OFF_A = 0
OFF_B = 640
OFF_C = 1664
OFF_F = 3200

LANE = 128
VMEM_PHYS_V7X = 64 * 1024 * 1024
VMEM_LIMIT = 60 * 1024 * 1024

TM_FFN = 1024
TF_FFN = 768
TM_PROJ = 512
TM_MERGE = 512
TN_MERGE = 512
TQ_ATT = 1024
KC_ATT = 512
TK_DFT = 512
TN_ADA = 1024


def _cparams(sem):
    return pltpu.CompilerParams(dimension_semantics=sem, vmem_limit_bytes=VMEM_LIMIT)


def _mod_row(i, tiles_per_batch):
    return jnp.minimum(i // tiles_per_batch, BATCH)


def _row_rsqrt(x):
    return lax.rsqrt(jnp.mean(x * x, axis=-1, keepdims=True) + EPS)


def _scale_mod(x, r, gain, shift, scale):
    return (x * r * gain) * (1.0 + scale) + shift


def _norm_mod(x, gain, shift, scale):
    return _scale_mod(x, _row_rsqrt(x), gain, shift, scale)


def _silu(x):
    return x * jax.nn.sigmoid(x)


def _ada_kernel(cond_ref, w_ref, b_ref, o_ref):
    s = _silu(cond_ref[...]).astype(BF16)
    o_ref[...] = jnp.dot(s, w_ref[...].astype(BF16), preferred_element_type=F32) + b_ref[...]


def _ada(cond8, w_ada, b_ada):
    n = N_MOD * D
    return pl.pallas_call(
        _ada_kernel,
        grid=(DEPTH, n // TN_ADA),
        in_specs=[
            pl.BlockSpec((8, D), lambda l, j: (0, 0)),
            pl.BlockSpec((None, D, TN_ADA), lambda l, j: (l, 0, j)),
            pl.BlockSpec((None, 1, TN_ADA), lambda l, j: (l, 0, j)),
        ],
        out_specs=pl.BlockSpec((None, 8, TN_ADA), lambda l, j: (l, 0, j)),
        out_shape=jax.ShapeDtypeStruct((DEPTH, 8, n), F32),
        compiler_params=_cparams(("arbitrary", "arbitrary")),
        name="ada",
    )(cond8, w_ada, b_ada.reshape(DEPTH, 1, n))


def _mod_spec(layer, j, tiles_per_batch):
    return pl.BlockSpec((None, None, None, 1, D),
                        lambda i, *_: (layer, _mod_row(i, tiles_per_batch), j, 0, 0))


def _ffn_kernel(overlap, x_ref, gain_ref, sh_ref, sc_ref, gt_ref, wg_ref, wu_ref, wo_ref, o_ref, r_ref):
    f = pl.program_id(1)
    last = pl.num_programs(1) - 1

    def chunk(r, skip=0):
        xn = _scale_mod(x_ref[...], r, gain_ref[...], sh_ref[...], sc_ref[...]).astype(BF16)
        g = jnp.dot(xn, wg_ref[:, skip:], preferred_element_type=F32)
        u = jnp.dot(xn, wu_ref[:, skip:], preferred_element_type=F32)
        a = (_silu(g) * u).astype(BF16)
        return jnp.dot(a, wo_ref[skip:, :], preferred_element_type=F32)

    @pl.when(f == 0)
    def _():
        r = _row_rsqrt(x_ref[...])
        r_ref[...] = r
        o_ref[...] = chunk(r)

    @pl.when(jnp.logical_and(f > 0, f < last))
    def _():
        o_ref[...] += chunk(r_ref[...])

    @pl.when(f == last)
    def _():
        o_ref[...] = x_ref[...] + (0.5 * gt_ref[...]) * (o_ref[...] + chunk(r_ref[...], overlap))


def _ffn(h, mod, layer, j0, gain, wi, wo, n_rows):
    tm, tf = TM_FFN, TF_FFN
    tpb = SEQ // tm
    nf = pl.cdiv(F_FF, tf)

    def hid(f, base=0):
        return (base // LANE + jnp.minimum(f * (tf // LANE), (F_FF - tf) // LANE)) * LANE

    return pl.pallas_call(
        functools.partial(_ffn_kernel, nf * tf - F_FF),
        grid=(n_rows // tm, nf),
        in_specs=[
            pl.BlockSpec((tm, D), lambda i, f: (i, 0)),
            pl.BlockSpec((None, 1, D), lambda i, f: (layer, 0, 0)),
            _mod_spec(layer, j0, tpb),
            _mod_spec(layer, j0 + 1, tpb),
            _mod_spec(layer, j0 + 2, tpb),
            pl.BlockSpec((None, pl.Element(D), pl.Element(tf)), lambda i, f: (layer, 0, hid(f))),
            pl.BlockSpec((None, pl.Element(D), pl.Element(tf)), lambda i, f: (layer, 0, hid(f, F_FF))),
            pl.BlockSpec((None, pl.Element(tf), pl.Element(D)), lambda i, f: (layer, hid(f), 0)),
        ],
        out_specs=pl.BlockSpec((tm, D), lambda i, f: (i, 0)),
        out_shape=jax.ShapeDtypeStruct((n_rows, D), F32),
        scratch_shapes=[pltpu.VMEM((tm, 1), F32)],
        compiler_params=_cparams(("arbitrary", "arbitrary")),
        name="ffn",
    )(h, gain, mod, mod, mod, wi, wi, wo)


def _rope_a(v, c, s1, s2):
    return v * c + pltpu.roll(v, 96, 1) * s1 + pltpu.roll(v, 32, 1) * s2


def _rope_b(v, c, s):
    return v * c + pltpu.roll(v, 64, 1) * s


def _proj_kernel(h_ref, gain_ref, sh_ref, sc_ref, w_ref, gcq_ref, wuq_ref, gckv_ref, wukv_ref,
                 gqa_ref, gkan_ref, gkap_ref, gqb_ref, gkb_ref, ca_ref, s1a_ref, s2a_ref,
                 cb_ref, sb_ref, ccs_ref,
                 qa_ref, ka_ref, va_ref, qb_ref, kb_ref, vb_ref, cbo_ref, u_ref, pc_ref, ps_ref,
                 pp_ref):
    xn = _norm_mod(h_ref[...], gain_ref[...], sh_ref[...], sc_ref[...]).astype(BF16)
    ca, s1a, s2a = ca_ref[...], s1a_ref[...], s2a_ref[...]
    cb, sb = cb_ref[...], sb_ref[...]
    scale_a = QK_A ** -0.5 * LOG2_E
    scale_b = HD_B ** -0.5 * LOG2_E

    za = jnp.dot(xn, w_ref[:, OFF_A:OFF_B], preferred_element_type=F32)
    hi = lax.broadcasted_iota(jnp.int32, (1, LANE), 1) >= LANE // 2
    s3, s4 = za[:, 3 * LANE:4 * LANE], za[:, 4 * LANE:]
    aq = jnp.concatenate([za[:, :3 * LANE], jnp.where(hi, 0.0, s3)], axis=1)
    akv = pltpu.roll(jnp.where(hi, s3, s4), LANE // 2, 1)
    ape = jnp.where(hi, 0.0, pltpu.roll(s4, LANE // 2, 1))
    r = lax.rsqrt(jnp.sum(aq * aq, axis=-1, keepdims=True) * (1.0 / Q_LORA) + EPS)
    cq = (aq * r * gcq_ref[...]).astype(BF16)
    q = jnp.dot(cq, wuq_ref[...], preferred_element_type=F32)
    gqa = gqa_ref[...]
    for hd in range(HEADS):
        lo = hd * QK_A_PAD
        qh = q[:, lo:lo + QK_A_PAD]
        r = lax.rsqrt(jnp.sum(qh * qh, axis=-1, keepdims=True) * (1.0 / QK_A) + EPS)
        qh = qh * r * gqa
        qa_ref[:, lo:lo + NOPE] = (qh[:, :NOPE] * scale_a).astype(BF16)
        qa_ref[:, lo + NOPE:lo + QK_A_PAD] = (_rope_a(qh[:, NOPE:], ca, s1a, s2a) * scale_a).astype(BF16)
    r = lax.rsqrt(jnp.mean(akv * akv, axis=-1, keepdims=True) + EPS)
    ckv = (akv * r * gckv_ref[...]).astype(BF16)
    kv = jnp.dot(ckv, wukv_ref[...], preferred_element_type=F32)
    ss_pe = jnp.sum(ape * ape, axis=-1, keepdims=True)
    ones_blk = jnp.ones((h_ref.shape[0], LANE), BF16)
    gkan, gkap = gkan_ref[...], gkap_ref[...]
    for hd in range(HEADS):
        kn = kv[:, hd * NOPE:(hd + 1) * NOPE]
        r = lax.rsqrt((jnp.sum(kn * kn, axis=-1, keepdims=True) + ss_pe) * (1.0 / QK_A) + EPS)
        lo = hd * QK_A_PAD
        ka_ref[:, lo:lo + NOPE] = (kn * r * gkan).astype(BF16)
        ka_ref[:, lo + NOPE:lo + QK_A_PAD] = _rope_a(ape * r * gkap, ca, s1a, s2a).astype(BF16)
        va_ref[:, 2 * hd * V_A:(2 * hd + 1) * V_A] = kv[:, HEADS * NOPE + hd * V_A:HEADS * NOPE + (hd + 1) * V_A].astype(BF16)
        va_ref[:, (2 * hd + 1) * V_A:(2 * hd + 2) * V_A] = ones_blk

    zb = jnp.dot(xn, w_ref[:, OFF_B:OFF_C], preferred_element_type=F32)
    gqb, gkb = gqb_ref[...], gkb_ref[...]
    for hd in range(HEADS):
        qh = zb[:, hd * HD_B:(hd + 1) * HD_B]
        r = lax.rsqrt(jnp.mean(qh * qh, axis=-1, keepdims=True) + EPS)
        qb_ref[:, hd * HD_B:(hd + 1) * HD_B] = (_rope_b(qh * r * gqb, cb, sb) * scale_b).astype(BF16)
    for hd in range(KVH_B):
        lo = HEADS * HD_B + hd * HD_B
        kh = zb[:, lo:lo + HD_B]
        r = lax.rsqrt(jnp.mean(kh * kh, axis=-1, keepdims=True) + EPS)
        kb_ref[:, hd * HD_B:(hd + 1) * HD_B] = _rope_b(kh * r * gkb, cb, sb).astype(BF16)
    for hd in range(KVH_B):
        lo = (HEADS + KVH_B + hd) * HD_B
        vb_ref[:, 2 * hd * HD_B:(2 * hd + 1) * HD_B] = zb[:, lo:lo + HD_B].astype(BF16)
        vb_ref[:, (2 * hd + 1) * HD_B:(2 * hd + 2) * HD_B] = ones_blk

    zc = jnp.dot(xn, w_ref[:, OFF_C:OFF_F], preferred_element_type=F32)
    cbo_ref[...] = zc[:, :CONV_W]
    u_ref[...] = zc[:, CONV_W:2 * CONV_W] * zc[:, 2 * CONV_W:]

    zf = jnp.dot(xn, w_ref[:, OFF_F:], preferred_element_type=F32).astype(BF16)
    ccs = ccs_ref[...]
    tm = h_ref.shape[0]
    w = FG * FGD
    for g in range(FG):
        pp = jnp.dot(zf[:, g * FGD:(g + 1) * FGD], ccs, preferred_element_type=F32)
        pp_ref[g] = pp[:, :FGD]
        pp_ref[FG + g] = pp[:, FGD:]
    for r in range(4):
        for g in range(FG):
            lo = r * w + g * FGD
            pc_ref[:, lo:lo + FGD] = pp_ref[g, pl.ds(r, tm // 4, stride=4), :].astype(BF16)
            ps_ref[:, lo:lo + FGD] = pp_ref[FG + g, pl.ds(r, tm // 4, stride=4), :].astype(BF16)


def _const_spec(shape, layer=None):
    nd = len(shape)
    if layer is None:
        return pl.BlockSpec(shape, lambda i, *_: (0,) * nd, pipeline_mode=pl.Buffered(1))
    return pl.BlockSpec((None,) + shape, lambda i, *_: (layer,) + (0,) * nd, pipeline_mode=pl.Buffered(1))


def _proj(h, mod, layer, lw, tabs, ccs):
    tm = TM_PROJ
    tpb = SEQ // tm
    n_lat = T_LAT // tm

    def tab_spec():
        return pl.BlockSpec((tm, 128), lambda i: (jnp.where(i < n_lat, i % tpb, tpb), 0))

    def row_spec(w):
        return pl.BlockSpec((tm, w), lambda i: (i, 0))

    widths = (HEADS * QK_A_PAD, HEADS * QK_A_PAD, HEADS * 2 * V_A, HEADS * HD_B, KVH_B * HD_B,
              KVH_B * 2 * HD_B, CONV_W, CONV_W)
    dtypes = (BF16, BF16, BF16, BF16, BF16, BF16, F32, F32)
    dft_spec = pl.BlockSpec((tm // 4, 4 * FG * FGD), lambda i: (i, 0))
    dft_shape = jax.ShapeDtypeStruct((T_ALL // 4, 4 * FG * FGD), BF16)
    return pl.pallas_call(
        _proj_kernel,
        grid=(T_ALL // tm,),
        in_specs=[
            row_spec(D),
            _const_spec((1, D), layer),
            _mod_spec(layer, 3, tpb),
            _mod_spec(layer, 4, tpb),
            _const_spec((D, W_IN), layer),
            _const_spec((1, Q_LORA_PAD), layer),
            _const_spec((Q_LORA_PAD, HEADS * QK_A_PAD), layer),
            _const_spec((1, KV_LORA), layer),
            _const_spec((KV_LORA, HEADS * (NOPE + V_A)), layer),
            _const_spec((1, QK_A_PAD), layer),
            _const_spec((1, NOPE), layer),
            _const_spec((1, 128), layer),
            _const_spec((1, HD_B), layer),
            _const_spec((1, HD_B), layer),
            tab_spec(), tab_spec(), tab_spec(), tab_spec(), tab_spec(),
            _const_spec((FGD, 2 * FGD)),
        ],
        out_specs=[row_spec(w) for w in widths] + [dft_spec, dft_spec],
        out_shape=[jax.ShapeDtypeStruct((T_ALL, w), dt) for w, dt in zip(widths, dtypes)]
        + [dft_shape, dft_shape],
        scratch_shapes=[pltpu.VMEM((2 * FG, tm, FGD), F32)],
        compiler_params=_cparams(("arbitrary",)),
        name="proj",
    )(h, lw["norm_mix"], mod, mod, lw["w_in"], lw["g_cq"], lw["w_uq"], lw["g_ckv"], lw["w_ukv"],
      lw["g_qa"], lw["g_ka_nope"], lw["g_ka_pe"], lw["g_qb"], lw["g_kb"], *tabs, ccs)


_QK_DIMS = (((1,), (1,)), ((), ()))


def _att_lat_kernel(q_ref, k_ref, kc_ref, v_ref, vc_ref, o_ref):
    q = q_ref[...]
    chunks = [(k_ref, v_ref, j * KC_ATT, KC_ATT) for j in range(SEQ // KC_ATT)] + [(kc_ref, vc_ref, 0, CTX)]

    def scores(chunk):
        kr, _, lo, n = chunk
        return lax.dot_general(q, kr[lo:lo + n, :], _QK_DIMS, preferred_element_type=F32)

    dv = o_ref.shape[-1]
    s = scores(chunks[0])
    m = acc = None
    for idx, (_, vr, lo, n) in enumerate(chunks):
        s_next = scores(chunks[idx + 1]) if idx + 1 < len(chunks) else None
        m_chunk = jnp.max(s, axis=-1, keepdims=True)
        m_new = m_chunk if m is None else jnp.maximum(m, m_chunk)
        p = _exp2_bf16(s - m_new)
        pv = jnp.dot(p, vr[lo:lo + n, :], preferred_element_type=F32)
        acc = pv if m is None else jnp.exp2(m - m_new) * acc + pv
        m, s = m_new, s_next
    o_ref[...] = (acc[:, :dv] / acc[:, dv:]).astype(o_ref.dtype)


def _exp2_bf16(x):
    return jnp.exp2(x.astype(BF16))


def _att_ctx_kernel(q_ref, k_ref, v_ref, o_ref):
    dv = o_ref.shape[-1]
    s = lax.dot_general(q_ref[...], k_ref[...], _QK_DIMS, preferred_element_type=F32)
    p = _exp2_bf16(s - jnp.max(s, axis=-1, keepdims=True))
    o = jnp.dot(p, v_ref[...], preferred_element_type=F32)
    o_ref[...] = (o[:, :dv] / o[:, dv:]).astype(o_ref.dtype)


def _attention(q, k, v, dq, dv, group, with_ctx):
    tq = TQ_ATT
    nq = SEQ // tq
    ctx0 = T_LAT // CTX
    out = pl.pallas_call(
        _att_lat_kernel,
        grid=(BATCH, HEADS, nq),
        in_specs=[
            pl.BlockSpec((tq, dq), lambda b, h, i: (b * nq + i, h)),
            pl.BlockSpec((SEQ, dq), lambda b, h, i: (b, h // group)),
            pl.BlockSpec((CTX, dq), lambda b, h, i: (ctx0 + b, h // group)),
            pl.BlockSpec((SEQ, 2 * dv), lambda b, h, i: (b, h // group)),
            pl.BlockSpec((CTX, 2 * dv), lambda b, h, i: (ctx0 + b, h // group)),
        ],
        out_specs=pl.BlockSpec((tq, dv), lambda b, h, i: (b * nq + i, h)),
        out_shape=jax.ShapeDtypeStruct((T_LAT, HEADS * dv), BF16),
        compiler_params=_cparams(("arbitrary", "arbitrary", "arbitrary")),
        name="att_lat",
    )(q, k, k, v, v)
    if not with_ctx:
        return out, None
    out_ctx = pl.pallas_call(
        _att_ctx_kernel,
        grid=(BATCH, HEADS),
        in_specs=[
            pl.BlockSpec((CTX, dq), lambda b, h: (ctx0 + b, h)),
            pl.BlockSpec((CTX, dq), lambda b, h: (ctx0 + b, h // group)),
            pl.BlockSpec((CTX, 2 * dv), lambda b, h: (ctx0 + b, h // group)),
        ],
        out_specs=pl.BlockSpec((CTX, dv), lambda b, h: (b, h)),
        out_shape=jax.ShapeDtypeStruct((T_CTX, HEADS * dv), BF16),
        compiler_params=_cparams(("arbitrary", "arbitrary")),
        name="att_ctx",
    )(q, k, v)
    return out, out_ctx


def _dft4_kernel(scale, c_ref, s_ref, tw_ref, pc_ref, ps_ref, o_ref):
    w = o_ref.shape[-1]
    c, s = c_ref[...], s_ref[...]
    pc, ps = pc_ref[...], ps_ref[...]
    ur = jnp.dot(c, pc, preferred_element_type=F32) - jnp.dot(s, ps, preferred_element_type=F32)
    ui = -(jnp.dot(c, ps, preferred_element_type=F32) + jnp.dot(s, pc, preferred_element_type=F32))
    a = [ur[:, :w]]
    b = [None]
    for r in range(1, 4):
        u_r, v_r = ur[:, r * w:(r + 1) * w], ui[:, r * w:(r + 1) * w]
        cr, sr = tw_ref[:, 2 * r - 2:2 * r - 1], tw_ref[:, 2 * r - 1:2 * r]
        a.append(u_r * cr + v_r * sr)
        b.append(None if r == 2 else v_r * cr - u_r * sr)
    o_ref[0] = ((a[0] + a[1] + a[2] + a[3]) * scale).astype(o_ref.dtype)
    o_ref[1] = ((a[0] + b[1] - a[2] - b[3]) * scale).astype(o_ref.dtype)
    o_ref[2] = ((a[0] - a[1] + a[2] - a[3]) * scale).astype(o_ref.dtype)
    o_ref[3] = ((a[0] - b[1] - a[2] + b[3]) * scale).astype(o_ref.dtype)


def _dft4(pc4, ps4, tables, n, tk, first_block, name):
    w = FG * FGD
    nq = n // 4
    c_q, s_q, tw = tables
    return pl.pallas_call(
        functools.partial(_dft4_kernel, (n * FGD) ** -0.5),
        grid=(BATCH, nq // tk),
        in_specs=[
            pl.BlockSpec((tk, nq), lambda b, k: (k, 0)),
            pl.BlockSpec((tk, nq), lambda b, k: (k, 0)),
            pl.BlockSpec((tk, 8), lambda b, k: (k, 0)),
            pl.BlockSpec((nq, 4 * w), lambda b, k: (first_block + b, 0)),
            pl.BlockSpec((nq, 4 * w), lambda b, k: (first_block + b, 0)),
        ],
        out_specs=pl.BlockSpec((None, 4, tk, w), lambda b, k: (b, 0, k, 0)),
        out_shape=jax.ShapeDtypeStruct((BATCH, 4, nq, w), BF16),
        compiler_params=_cparams(("arbitrary", "arbitrary")),
        name=name,
    )(c_q, s_q, tw, pc4, ps4).reshape(BATCH * n, w)


def _fourier(pc4, ps4, dft_lat, dft_ctx, with_ctx):
    out = _dft4(pc4, ps4, dft_lat, SEQ, TK_DFT, 0, "dft_lat")
    if not with_ctx:
        return out, None
    return out, _dft4(pc4, ps4, dft_ctx, CTX, CTX // 4, T_LAT // CTX, "dft_ctx")


def _merge_kernel(n_lat_tiles, with_ctx, h_ref, gain_ref, sh_ref, sc_ref, gt_ref, *refs):
    lat_refs, refs = refs[:3], refs[3:]
    if with_ctx:
        ctx_refs, refs = refs[:3], refs[3:]
    (cbo_ref, u_ref, up_ref, un_ref, cw_ref, cbias_ref, wg_ref, bg_ref, wb_ref, wo_ref,
     o_ref, r_ref) = refs
    i = pl.program_id(0)
    n = pl.program_id(1)
    last = pl.num_programs(1) - 1
    tm = h_ref.shape[0]

    def chunk(r):
        xn = _scale_mod(h_ref[...], r, gain_ref[...], sh_ref[...], sc_ref[...]).astype(BF16)

        u = u_ref[...]
        is_lat = i < n_lat_tiles
        seq_len = jnp.where(is_lat, SEQ, CTX)
        ridx = lax.broadcasted_iota(jnp.int32, (tm, 1), 0)
        pos = (i * tm + ridx) & (seq_len - 1)
        prev = jnp.where(ridx == 0, up_ref[7:8, :], pltpu.roll(u, 1, 0))
        prev = jnp.where(pos == 0, 0.0, prev)
        nxt = jnp.where(ridx == tm - 1, un_ref[0:1, :], pltpu.roll(u, tm - 1, 0))
        nxt = jnp.where(pos == seq_len - 1, 0.0, nxt)
        y = prev * cw_ref[0:1, :] + u * cw_ref[1:2, :] + nxt * cw_ref[2:3, :] + cbias_ref[...]
        conv = (cbo_ref[...] * y).astype(BF16)

        if with_ctx:
            oa, ob, of = (jnp.where(is_lat, lr[...], cr[...]) for lr, cr in zip(lat_refs, ctx_refs))
        else:
            oa, ob, of = (lr[...] for lr in lat_refs)

        merged = None
        for k, br in enumerate((oa, ob, conv, of)):
            gate = jax.nn.sigmoid(jnp.dot(xn, wg_ref[k], preferred_element_type=F32) + bg_ref[k])
            term = gate * jnp.dot(br, wb_ref[k], preferred_element_type=F32)
            merged = term if merged is None else merged + term
        return jnp.dot(merged.astype(BF16), wo_ref[...], preferred_element_type=F32)

    @pl.when(n == 0)
    def _():
        r = _row_rsqrt(h_ref[...])
        r_ref[...] = r
        o_ref[...] = chunk(r)

    @pl.when(jnp.logical_and(n > 0, n < last))
    def _():
        o_ref[...] += chunk(r_ref[...])

    @pl.when(n == last)
    def _():
        o_ref[...] = h_ref[...] + gt_ref[...] * (o_ref[...] + chunk(r_ref[...]))


def _merge(h, mod, layer, br_lat, br_ctx, cbo, u, lw, n_rows):
    tm, tn = TM_MERGE, TN_MERGE
    tpb = SEQ // tm
    n_lat = T_LAT // tm
    with_ctx = br_ctx is not None
    last_blk = T_ALL // 8 - 1
    br_specs = [pl.BlockSpec((tm, BR_W), lambda i, n: (jnp.minimum(i, n_lat - 1), 0))] * 3
    if with_ctx:
        br_specs += [pl.BlockSpec((tm, BR_W), lambda i, n: (jnp.maximum(i - n_lat, 0), 0))] * 3

    def row_spec(w):
        return pl.BlockSpec((tm, w), lambda i, n: (i, 0))

    return pl.pallas_call(
        functools.partial(_merge_kernel, n_lat, with_ctx),
        grid=(n_rows // tm, D // tn),
        in_specs=[
            row_spec(D),
            pl.BlockSpec((None, 1, D), lambda i, n: (layer, 0, 0)),
            _mod_spec(layer, 3, tpb),
            _mod_spec(layer, 4, tpb),
            _mod_spec(layer, 5, tpb),
            *br_specs,
            row_spec(CONV_W), row_spec(CONV_W),
            pl.BlockSpec((8, CONV_W), lambda i, n: (jnp.maximum(i * (tm // 8) - 1, 0), 0)),
            pl.BlockSpec((8, CONV_W), lambda i, n: (jnp.minimum((i + 1) * (tm // 8), last_blk), 0)),
            pl.BlockSpec((None, 3, CONV_W), lambda i, n: (layer, 0, 0)),
            pl.BlockSpec((None, 1, CONV_W), lambda i, n: (layer, 0, 0)),
            pl.BlockSpec((None, N_BR, D, tn), lambda i, n: (layer, 0, 0, n)),
            pl.BlockSpec((None, N_BR, 1, tn), lambda i, n: (layer, 0, 0, n)),
            pl.BlockSpec((None, N_BR, BR_W, tn), lambda i, n: (layer, 0, 0, n)),
            pl.BlockSpec((None, tn, D), lambda i, n: (layer, n, 0)),
        ],
        out_specs=row_spec(D),
        out_shape=jax.ShapeDtypeStruct((n_rows, D), F32),
        scratch_shapes=[pltpu.VMEM((tm, 1), F32)],
        compiler_params=_cparams(("arbitrary", "arbitrary")),
        name="merge",
    )(h, lw["norm_mix"], mod, mod, mod, *br_lat, *(br_ctx or ()), cbo, u, u, u, lw["conv_w"],
      lw["conv_b"], lw["w_gate"], lw["b_gate"], lw["w_branch"], lw["w_o"])


def _rope_tables():
    t = jnp.arange(SEQ, dtype=jnp.int32)
    pos_row = (t // GRID_W).astype(F32)
    pos_col = (t % GRID_W).astype(F32)

    def ang(rot_dim):
        n = rot_dim // 4
        inv_freq = THETA ** (-jnp.arange(n, dtype=F32) / n)
        a = jnp.concatenate([pos_row[:, None] * inv_freq, pos_col[:, None] * inv_freq], axis=-1)
        return jnp.cos(a), jnp.sin(a)

    ca, sa = ang(ROPE_A)
    z32 = jnp.zeros_like(ca)
    z64 = jnp.zeros((SEQ, 64), F32)
    tab_c = jnp.concatenate([ca, ca, z64], axis=-1)
    tab_s1 = jnp.concatenate([-sa, z32, z64], axis=-1)
    tab_s2 = jnp.concatenate([z32, sa, z64], axis=-1)
    cb, sb = ang(HD_B)
    tab_cb = jnp.concatenate([cb, cb], axis=-1)
    tab_sb = jnp.concatenate([-sb, sb], axis=-1)
    ones = jnp.ones((TM_PROJ, 128), F32)
    zeros = jnp.zeros((TM_PROJ, 128), F32)
    return tuple(jnp.concatenate([tab, ident], axis=0)
                 for tab, ident in ((tab_c, ones), (tab_s1, zeros), (tab_s2, zeros),
                                    (tab_cb, ones), (tab_sb, zeros)))


def _dft_mats(n):
    j = jnp.arange(n, dtype=jnp.int32)
    a = ((j[:, None] * j[None, :]) % n).astype(F32) * (2.0 * math.pi / n)
    return jnp.cos(a), jnp.sin(a)


def _dft4_tables(n):
    nq = n // 4
    c_q, s_q = _dft_mats(nq)
    k = jnp.arange(nq, dtype=jnp.int32)[:, None]
    ang = (k * jnp.arange(1, 4, dtype=jnp.int32)[None, :]).astype(F32) * (2.0 * math.pi / n)
    tw = jnp.stack([jnp.cos(ang), jnp.sin(ang)], axis=-1).reshape(nq, 6)
    return c_q.astype(BF16), s_q.astype(BF16), _pad_to(tw, 1, 8)


def _pad_to(a, axis, size):
    pad = [(0, 0)] * a.ndim
    pad[axis] = (0, size - a.shape[axis])
    return jnp.pad(a, pad)


def _prep_weights(p):
    w_uq = p["w_uq"].reshape(DEPTH, Q_LORA, HEADS, QK_A)
    w_uq = _pad_to(_pad_to(w_uq, 3, QK_A_PAD), 1, Q_LORA_PAD).reshape(DEPTH, Q_LORA_PAD, HEADS * QK_A_PAD)
    w_ukv = p["w_ukv"].reshape(DEPTH, KV_LORA, HEADS, NOPE + V_A)
    w_ukv = jnp.concatenate([w_ukv[..., :NOPE].reshape(DEPTH, KV_LORA, HEADS * NOPE),
                             w_ukv[..., NOPE:].reshape(DEPTH, KV_LORA, HEADS * V_A)], axis=2)
    g_ka = p["g_ka"]

    return {
        "w_in": p["w_in"].astype(BF16),
        "g_cq": _pad_to(p["g_cq"], 1, Q_LORA_PAD)[:, None, :],
        "w_uq": w_uq.astype(BF16),
        "g_ckv": p["g_ckv"][:, None, :],
        "w_ukv": w_ukv.astype(BF16),
        "g_qa": _pad_to(p["g_qa"], 1, QK_A_PAD)[:, None, :],
        "g_ka_nope": g_ka[:, None, :NOPE],
        "g_ka_pe": _pad_to(g_ka[:, NOPE:], 1, 128)[:, None, :],
        "g_qb": p["g_qb"][:, None, :],
        "g_kb": p["g_kb"][:, None, :],
        "conv_w": p["conv_w"],
        "conv_b": p["conv_b"][:, None, :],
        "w_gate": p["w_gate"].astype(BF16),
        "b_gate": p["b_gate"][:, :, None, :],
        "w_branch": p["w_branch"].astype(BF16),
        "w_o": p["w_o"].astype(BF16),
        "norm_ffn1": p["norm_ffn1"][:, None, :],
        "norm_mix": p["norm_mix"][:, None, :],
        "norm_ffn2": p["norm_ffn2"][:, None, :],
        "ffn1_wi": p["ffn1_wi"].astype(BF16), "ffn1_wo": p["ffn1_wo"].astype(BF16),
        "ffn2_wi": p["ffn2_wi"].astype(BF16), "ffn2_wo": p["ffn2_wo"].astype(BF16),
    }


def kernel(x, c, ctx, c_ctx, w_ada, b_ada, norm_ffn1, ffn1_wi, ffn1_wo, norm_mix, w_in, g_cq, w_uq, g_ckv, w_ukv, g_qa, g_ka, g_qb, g_kb, conv_w, conv_b, w_branch, w_gate, b_gate, w_o, norm_ffn2, ffn2_wi, ffn2_wo):
    p = dict(norm_ffn1=norm_ffn1, ffn1_wi=ffn1_wi, ffn1_wo=ffn1_wo, norm_mix=norm_mix, w_in=w_in,
             g_cq=g_cq, w_uq=w_uq, g_ckv=g_ckv, w_ukv=w_ukv, g_qa=g_qa, g_ka=g_ka, g_qb=g_qb,
             g_kb=g_kb, conv_w=conv_w, conv_b=conv_b, w_branch=w_branch, w_gate=w_gate,
             b_gate=b_gate, w_o=w_o, norm_ffn2=norm_ffn2, ffn2_wi=ffn2_wi, ffn2_wo=ffn2_wo)

    cond8 = jnp.concatenate([c, c_ctx[None, :], jnp.zeros((8 - BATCH - 1, D), F32)], axis=0)
    mod = _ada(cond8, w_ada, b_ada).reshape(DEPTH, 8, N_MOD, 1, D)

    tabs = _rope_tables()
    cc, sc = _dft_mats(FGD)
    ccs = jnp.concatenate([cc, sc], axis=1).astype(BF16)
    dft_lat = _dft4_tables(SEQ)
    dft_ctx = _dft4_tables(CTX)

    lw = _prep_weights(p)
    h = jnp.concatenate([x.reshape(T_LAT, D), ctx.reshape(T_CTX, D)], axis=0)
    for l in range(DEPTH):
        last = l == DEPTH - 1
        h = _ffn(h, mod, l, 0, lw["norm_ffn1"], lw["ffn1_wi"], lw["ffn1_wo"], T_ALL)
        qa, ka, va, qb, kb, vb, cbo, u, pc, ps = _proj(h, mod, l, lw, tabs, ccs)
        oa, oa_c = _attention(qa, ka, va, QK_A_PAD, V_A, 1, not last)
        ob, ob_c = _attention(qb, kb, vb, HD_B, HD_B, GROUP_B, not last)
        of, of_c = _fourier(pc, ps, dft_lat, dft_ctx, not last)
        n_rows = T_LAT if last else T_ALL
        h = _merge(h, mod, l, (oa, ob, of), None if last else (oa_c, ob_c, of_c), cbo, u, lw, n_rows)
        h = _ffn(h, mod, l, 6, lw["norm_ffn2"], lw["ffn2_wi"], lw["ffn2_wo"], n_rows)
    return h.reshape(BATCH, SEQ, D)
```

```python
import functools
import math

import jax
import jax.numpy as jnp
from jax import lax
from jax.experimental import pallas as pl
from jax.experimental.pallas import tpu as pltpu

F32 = jnp.float32
BF16 = jnp.bfloat16

D = 2048
BATCH = 4
SEQ = 4096
DEPTH = 4
GRID_W = 64
CTX = 256
N_MOD = 9
F_FF = 5504
HEADS = 4
Q_LORA = 448
Q_LORA_PAD = 512
KV_LORA = 128
NOPE = 128
ROPE_A = 64
QK_A = NOPE + ROPE_A
QK_A_PAD = 256
V_A = 128
HD_B = 128
KVH_B = 2
GROUP_B = 2
CONV_W = 512
FG = 4
FGD = 128
BR_W = 512
N_BR = 4
THETA = 10000.0
EPS = 1e-6
LOG2_E = math.log2(math.e)

T_LAT = BATCH * SEQ
T_CTX = BATCH * CTX
T_ALL = T_LAT + T_CTX

W_IN = 3712
OFF_A = 0
OFF_B = 640
OFF_C = 1664
OFF_F = 3200

LANE = 128
VMEM_PHYS_V7X = 64 * 1024 * 1024
VMEM_LIMIT = 60 * 1024 * 1024

TM_FFN = 1024
TF_FFN = 512
TM_PROJ = 512
TM_MERGE = 512
TN_MERGE = 512
TQ_ATT = 1024
KC_ATT = 512
TK_DFT = 512
TN_ADA = 1024


def _cparams(sem):
    return pltpu.CompilerParams(dimension_semantics=sem, vmem_limit_bytes=VMEM_LIMIT)


def _mod_row(i, tiles_per_batch):
    return jnp.minimum(i // tiles_per_batch, BATCH)


def _row_rsqrt(x):
    return lax.rsqrt(jnp.mean(x * x, axis=-1, keepdims=True) + EPS)


def _scale_mod(x, r, gain, shift, scale):
    return (x * r * gain) * (1.0 + scale) + shift


def _norm_mod(x, gain, shift, scale):
    return _scale_mod(x, _row_rsqrt(x), gain, shift, scale)


def _silu(x):
    return x * jax.nn.sigmoid(x)


def _ada_kernel(cond_ref, w_ref, b_ref, o_ref):
    s = _silu(cond_ref[...]).astype(BF16)
    o_ref[...] = jnp.dot(s, w_ref[...].astype(BF16), preferred_element_type=F32) + b_ref[...]


def _ada(cond8, w_ada, b_ada):
    n = N_MOD * D
    return pl.pallas_call(
        _ada_kernel,
        grid=(DEPTH, n // TN_ADA),
        in_specs=[
            pl.BlockSpec((8, D), lambda l, j: (0, 0)),
            pl.BlockSpec((None, D, TN_ADA), lambda l, j: (l, 0, j)),
            pl.BlockSpec((None, 1, TN_ADA), lambda l, j: (l, 0, j)),
        ],
        out_specs=pl.BlockSpec((None, 8, TN_ADA), lambda l, j: (l, 0, j)),
        out_shape=jax.ShapeDtypeStruct((DEPTH, 8, n), F32),
        compiler_params=_cparams(("arbitrary", "arbitrary")),
        name="ada",
    )(cond8, w_ada, b_ada.reshape(DEPTH, 1, n))


def _mod_spec(layer, j, tiles_per_batch):
    return pl.BlockSpec((None, None, None, 1, D),
                        lambda i, *_: (layer, _mod_row(i, tiles_per_batch), j, 0, 0))


def _ffn_kernel(overlap, x_ref, gain_ref, sh_ref, sc_ref, gt_ref, wg_ref, wu_ref, wo_ref, o_ref, r_ref):
    f = pl.program_id(1)
    last = pl.num_programs(1) - 1

    def chunk(r, skip=0):
        xn = _scale_mod(x_ref[...], r, gain_ref[...], sh_ref[...], sc_ref[...]).astype(BF16)
        g = jnp.dot(xn, wg_ref[:, skip:], preferred_element_type=F32)
        u = jnp.dot(xn, wu_ref[:, skip:], preferred_element_type=F32)
        a = (_silu(g) * u).astype(BF16)
        return jnp.dot(a, wo_ref[skip:, :], preferred_element_type=F32)

    @pl.when(f == 0)
    def _():
        r = _row_rsqrt(x_ref[...])
        r_ref[...] = r
        o_ref[...] = chunk(r)

    @pl.when(jnp.logical_and(f > 0, f < last))
    def _():
        o_ref[...] += chunk(r_ref[...])

    @pl.when(f == last)
    def _():
        o_ref[...] = x_ref[...] + (0.5 * gt_ref[...]) * (o_ref[...] + chunk(r_ref[...], overlap))


def _ffn(h, mod, layer, j0, gain, wi, wo, n_rows, tf=TF_FFN):
    tm = TM_FFN
    tpb = SEQ // tm
    nf = pl.cdiv(F_FF, tf)

    def hid(f, base=0):
        return (base // LANE + jnp.minimum(f * (tf // LANE), (F_FF - tf) // LANE)) * LANE

    return pl.pallas_call(
        functools.partial(_ffn_kernel, nf * tf - F_FF),
        grid=(n_rows // tm, nf),
        in_specs=[
            pl.BlockSpec((tm, D), lambda i, f: (i, 0)),
            pl.BlockSpec((None, 1, D), lambda i, f: (layer, 0, 0)),
            _mod_spec(layer, j0, tpb),
            _mod_spec(layer, j0 + 1, tpb),
            _mod_spec(layer, j0 + 2, tpb),
            pl.BlockSpec((None, pl.Element(D), pl.Element(tf)), lambda i, f: (layer, 0, hid(f))),
            pl.BlockSpec((None, pl.Element(D), pl.Element(tf)), lambda i, f: (layer, 0, hid(f, F_FF))),
            pl.BlockSpec((None, pl.Element(tf), pl.Element(D)), lambda i, f: (layer, hid(f), 0)),
        ],
        out_specs=pl.BlockSpec((tm, D), lambda i, f: (i, 0)),
        out_shape=jax.ShapeDtypeStruct((n_rows, D), F32),
        scratch_shapes=[pltpu.VMEM((tm, 1), F32)],
        compiler_params=_cparams(("arbitrary", "arbitrary")),
        name="ffn",
    )(h, gain, mod, mod, mod, wi, wi, wo)


def _rope_a(v, c, s1, s2):
    return v * c + pltpu.roll(v, 96, 1) * s1 + pltpu.roll(v, 32, 1) * s2


def _rope_b(v, c, s):
    return v * c + pltpu.roll(v, 64, 1) * s


def _proj_kernel(h_ref, gain_ref, sh_ref, sc_ref, w_ref, gcq_ref, wuq_ref, gckv_ref, wukv_ref,
                 gqa_ref, gkan_ref, gkap_ref, gqb_ref, gkb_ref, ca_ref, s1a_ref, s2a_ref,
                 cb_ref, sb_ref, ccs_ref,
                 qa_ref, ka_ref, va_ref, qb_ref, kb_ref, vb_ref, cbo_ref, u_ref, pc_ref, ps_ref,
                 pp_ref):
    xn = _norm_mod(h_ref[...], gain_ref[...], sh_ref[...], sc_ref[...]).astype(BF16)
    ca, s1a, s2a = ca_ref[...], s1a_ref[...], s2a_ref[...]
    cb, sb = cb_ref[...], sb_ref[...]
    scale_a = QK_A ** -0.5 * LOG2_E
    scale_b = HD_B ** -0.5 * LOG2_E

    za = jnp.dot(xn, w_ref[:, OFF_A:OFF_B], preferred_element_type=F32)
    hi = lax.broadcasted_iota(jnp.int32, (1, LANE), 1) >= LANE // 2
    s3, s4 = za[:, 3 * LANE:4 * LANE], za[:, 4 * LANE:]
    aq = jnp.concatenate([za[:, :3 * LANE], jnp.where(hi, 0.0, s3)], axis=1)
    akv = pltpu.roll(jnp.where(hi, s3, s4), LANE // 2, 1)
    ape = jnp.where(hi, 0.0, pltpu.roll(s4, LANE // 2, 1))
    r = lax.rsqrt(jnp.sum(aq * aq, axis=-1, keepdims=True) * (1.0 / Q_LORA) + EPS)
    cq = (aq * r * gcq_ref[...]).astype(BF16)
    q = jnp.dot(cq, wuq_ref[...], preferred_element_type=F32)
    gqa = gqa_ref[...]
    for hd in range(HEADS):
        lo = hd * QK_A_PAD
        qh = q[:, lo:lo + QK_A_PAD]
        r = lax.rsqrt(jnp.sum(qh * qh, axis=-1, keepdims=True) * (1.0 / QK_A) + EPS)
        qh = qh * r * gqa
        qa_ref[:, lo:lo + NOPE] = (qh[:, :NOPE] * scale_a).astype(BF16)
        qa_ref[:, lo + NOPE:lo + QK_A_PAD] = (_rope_a(qh[:, NOPE:], ca, s1a, s2a) * scale_a).astype(BF16)
    r = lax.rsqrt(jnp.mean(akv * akv, axis=-1, keepdims=True) + EPS)
    ckv = (akv * r * gckv_ref[...]).astype(BF16)
    kv = jnp.dot(ckv, wukv_ref[...], preferred_element_type=F32)
    ss_pe = jnp.sum(ape * ape, axis=-1, keepdims=True)
    ones_blk = jnp.ones((h_ref.shape[0], LANE), BF16)
    gkan, gkap = gkan_ref[...], gkap_ref[...]
    for hd in range(HEADS):
        kn = kv[:, hd * NOPE:(hd + 1) * NOPE]
        r = lax.rsqrt((jnp.sum(kn * kn, axis=-1, keepdims=True) + ss_pe) * (1.0 / QK_A) + EPS)
        lo = hd * QK_A_PAD
        ka_ref[:, lo:lo + NOPE] = (kn * r * gkan).astype(BF16)
        ka_ref[:, lo + NOPE:lo + QK_A_PAD] = _rope_a(ape * r * gkap, ca, s1a, s2a).astype(BF16)
        va_ref[:, 2 * hd * V_A:(2 * hd + 1) * V_A] = kv[:, HEADS * NOPE + hd * V_A:HEADS * NOPE + (hd + 1) * V_A].astype(BF16)
        va_ref[:, (2 * hd + 1) * V_A:(2 * hd + 2) * V_A] = ones_blk

    zb = jnp.dot(xn, w_ref[:, OFF_B:OFF_C], preferred_element_type=F32)
    gqb, gkb = gqb_ref[...], gkb_ref[...]
    for hd in range(HEADS):
        qh = zb[:, hd * HD_B:(hd + 1) * HD_B]
        r = lax.rsqrt(jnp.mean(qh * qh, axis=-1, keepdims=True) + EPS)
        qb_ref[:, hd * HD_B:(hd + 1) * HD_B] = (_rope_b(qh * r * gqb, cb, sb) * scale_b).astype(BF16)
    for hd in range(KVH_B):
        lo = HEADS * HD_B + hd * HD_B
        kh = zb[:, lo:lo + HD_B]
        r = lax.rsqrt(jnp.mean(kh * kh, axis=-1, keepdims=True) + EPS)
        kb_ref[:, hd * HD_B:(hd + 1) * HD_B] = _rope_b(kh * r * gkb, cb, sb).astype(BF16)
    for hd in range(KVH_B):
        lo = (HEADS + KVH_B + hd) * HD_B
        vb_ref[:, 2 * hd * HD_B:(2 * hd + 1) * HD_B] = zb[:, lo:lo + HD_B].astype(BF16)
        vb_ref[:, (2 * hd + 1) * HD_B:(2 * hd + 2) * HD_B] = ones_blk

    zc = jnp.dot(xn, w_ref[:, OFF_C:OFF_F], preferred_element_type=F32)
    cbo_ref[...] = zc[:, :CONV_W]
    u_ref[...] = zc[:, CONV_W:2 * CONV_W] * zc[:, 2 * CONV_W:]

    zf = jnp.dot(xn, w_ref[:, OFF_F:], preferred_element_type=F32).astype(BF16)
    ccs = ccs_ref[...]
    tm = h_ref.shape[0]
    w = FG * FGD
    for g in range(FG):
        pp = jnp.dot(zf[:, g * FGD:(g + 1) * FGD], ccs, preferred_element_type=F32)
        pp_ref[g] = pp[:, :FGD]
        pp_ref[FG + g] = pp[:, FGD:]
    for r in range(4):
        for g in range(FG):
            lo = r * w + g * FGD
            pc_ref[:, lo:lo + FGD] = pp_ref[g, pl.ds(r, tm // 4, stride=4), :].astype(BF16)
            ps_ref[:, lo:lo + FGD] = pp_ref[FG + g, pl.ds(r, tm // 4, stride=4), :].astype(BF16)


def _const_spec(shape, layer=None):
    nd = len(shape)
    if layer is None:
        return pl.BlockSpec(shape, lambda i, *_: (0,) * nd, pipeline_mode=pl.Buffered(1))
    return pl.BlockSpec((None,) + shape, lambda i, *_: (layer,) + (0,) * nd, pipeline_mode=pl.Buffered(1))


def _proj(h, mod, layer, lw, tabs, ccs):
    tm = TM_PROJ
    tpb = SEQ // tm
    n_lat = T_LAT // tm

    def tab_spec():
        return pl.BlockSpec((tm, 128), lambda i: (jnp.where(i < n_lat, i % tpb, tpb), 0))

    def row_spec(w):
        return pl.BlockSpec((tm, w), lambda i: (i, 0))

    widths = (HEADS * QK_A_PAD, HEADS * QK_A_PAD, HEADS * 2 * V_A, HEADS * HD_B, KVH_B * HD_B,
              KVH_B * 2 * HD_B, CONV_W, CONV_W)
    dtypes = (BF16, BF16, BF16, BF16, BF16, BF16, F32, F32)
    dft_spec = pl.BlockSpec((tm // 4, 4 * FG * FGD), lambda i: (i, 0))
    dft_shape = jax.ShapeDtypeStruct((T_ALL // 4, 4 * FG * FGD), BF16)
    return pl.pallas_call(
        _proj_kernel,
        grid=(T_ALL // tm,),
        in_specs=[
            row_spec(D),
            _const_spec((1, D), layer),
            _mod_spec(layer, 3, tpb),
            _mod_spec(layer, 4, tpb),
            _const_spec((D, W_IN), layer),
            _const_spec((1, Q_LORA_PAD), layer),
            _const_spec((Q_LORA_PAD, HEADS * QK_A_PAD), layer),
            _const_spec((1, KV_LORA), layer),
            _const_spec((KV_LORA, HEADS * (NOPE + V_A)), layer),
            _const_spec((1, QK_A_PAD), layer),
            _const_spec((1, NOPE), layer),
            _const_spec((1, 128), layer),
            _const_spec((1, HD_B), layer),
            _const_spec((1, HD_B), layer),
            tab_spec(), tab_spec(), tab_spec(), tab_spec(), tab_spec(),
            _const_spec((FGD, 2 * FGD)),
        ],
        out_specs=[row_spec(w) for w in widths] + [dft_spec, dft_spec],
        out_shape=[jax.ShapeDtypeStruct((T_ALL, w), dt) for w, dt in zip(widths, dtypes)]
        + [dft_shape, dft_shape],
        scratch_shapes=[pltpu.VMEM((2 * FG, tm, FGD), F32)],
        compiler_params=_cparams(("arbitrary",)),
        name="proj",
    )(h, lw["norm_mix"], mod, mod, lw["w_in"], lw["g_cq"], lw["w_uq"], lw["g_ckv"], lw["w_ukv"],
      lw["g_qa"], lw["g_ka_nope"], lw["g_ka_pe"], lw["g_qb"], lw["g_kb"], *tabs, ccs)


_QK_DIMS = (((1,), (1,)), ((), ()))


def _att_lat_kernel(q_ref, k_ref, kc_ref, v_ref, vc_ref, o_ref):
    q = q_ref[...]
    chunks = [(k_ref, v_ref, j * KC_ATT, KC_ATT) for j in range(SEQ // KC_ATT)] + [(kc_ref, vc_ref, 0, CTX)]

    def scores(chunk):
        kr, _, lo, n = chunk
        return lax.dot_general(q, kr[lo:lo + n, :], _QK_DIMS, preferred_element_type=F32)

    dv = o_ref.shape[-1]
    s = scores(chunks[0])
    m = acc = None
    for idx, (_, vr, lo, n) in enumerate(chunks):
        s_next = scores(chunks[idx + 1]) if idx + 1 < len(chunks) else None
        m_chunk = jnp.max(s, axis=-1, keepdims=True)
        m_new = m_chunk if m is None else jnp.maximum(m, m_chunk)
        p = _exp2_bf16(s - m_new)
        pv = jnp.dot(p, vr[lo:lo + n, :], preferred_element_type=F32)
        acc = pv if m is None else jnp.exp2(m - m_new) * acc + pv
        m, s = m_new, s_next
    o_ref[...] = (acc[:, :dv] / acc[:, dv:]).astype(o_ref.dtype)


def _exp2_bf16(x):
    return jnp.exp2(x.astype(BF16))


def _att_ctx_kernel(q_ref, k_ref, v_ref, o_ref):
    dv = o_ref.shape[-1]
    s = lax.dot_general(q_ref[...], k_ref[...], _QK_DIMS, preferred_element_type=F32)
    p = _exp2_bf16(s - jnp.max(s, axis=-1, keepdims=True))
    o = jnp.dot(p, v_ref[...], preferred_element_type=F32)
    o_ref[...] = (o[:, :dv] / o[:, dv:]).astype(o_ref.dtype)


def _attention(q, k, v, dq, dv, group, with_ctx):
    tq = TQ_ATT
    nq = SEQ // tq
    ctx0 = T_LAT // CTX
    out = pl.pallas_call(
        _att_lat_kernel,
        grid=(BATCH, HEADS, nq),
        in_specs=[
            pl.BlockSpec((tq, dq), lambda b, h, i: (b * nq + i, h)),
            pl.BlockSpec((SEQ, dq), lambda b, h, i: (b, h // group)),
            pl.BlockSpec((CTX, dq), lambda b, h, i: (ctx0 + b, h // group)),
            pl.BlockSpec((SEQ, 2 * dv), lambda b, h, i: (b, h // group)),
            pl.BlockSpec((CTX, 2 * dv), lambda b, h, i: (ctx0 + b, h // group)),
        ],
        out_specs=pl.BlockSpec((tq, dv), lambda b, h, i: (b * nq + i, h)),
        out_shape=jax.ShapeDtypeStruct((T_LAT, HEADS * dv), BF16),
        compiler_params=_cparams(("arbitrary", "arbitrary", "arbitrary")),
        name="att_lat",
    )(q, k, k, v, v)
    if not with_ctx:
        return out, None
    out_ctx = pl.pallas_call(
        _att_ctx_kernel,
        grid=(BATCH, HEADS),
        in_specs=[
            pl.BlockSpec((CTX, dq), lambda b, h: (ctx0 + b, h)),
            pl.BlockSpec((CTX, dq), lambda b, h: (ctx0 + b, h // group)),
            pl.BlockSpec((CTX, 2 * dv), lambda b, h: (ctx0 + b, h // group)),
        ],
        out_specs=pl.BlockSpec((CTX, dv), lambda b, h: (b, h)),
        out_shape=jax.ShapeDtypeStruct((T_CTX, HEADS * dv), BF16),
        compiler_params=_cparams(("arbitrary", "arbitrary")),
        name="att_ctx",
    )(q, k, v)
    return out, out_ctx


def _dft4_kernel(scale, c_ref, s_ref, tw_ref, pc_ref, ps_ref, o_ref):
    w = o_ref.shape[-1]
    c, s = c_ref[...], s_ref[...]
    pc, ps = pc_ref[...], ps_ref[...]
    ur = jnp.dot(c, pc, preferred_element_type=F32) - jnp.dot(s, ps, preferred_element_type=F32)
    ui = -(jnp.dot(c, ps, preferred_element_type=F32) + jnp.dot(s, pc, preferred_element_type=F32))
    a = [ur[:, :w]]
    b = [None]
    for r in range(1, 4):
        u_r, v_r = ur[:, r * w:(r + 1) * w], ui[:, r * w:(r + 1) * w]
        cr, sr = tw_ref[:, 2 * r - 2:2 * r - 1], tw_ref[:, 2 * r - 1:2 * r]
        a.append(u_r * cr + v_r * sr)
        b.append(None if r == 2 else v_r * cr - u_r * sr)
    o_ref[0] = ((a[0] + a[1] + a[2] + a[3]) * scale).astype(o_ref.dtype)
    o_ref[1] = ((a[0] + b[1] - a[2] - b[3]) * scale).astype(o_ref.dtype)
    o_ref[2] = ((a[0] - a[1] + a[2] - a[3]) * scale).astype(o_ref.dtype)
    o_ref[3] = ((a[0] - b[1] - a[2] + b[3]) * scale).astype(o_ref.dtype)


def _dft4(pc4, ps4, tables, n, tk, first_block, name):
    w = FG * FGD
    nq = n // 4
    c_q, s_q, tw = tables
    return pl.pallas_call(
        functools.partial(_dft4_kernel, (n * FGD) ** -0.5),
        grid=(BATCH, nq // tk),
        in_specs=[
            pl.BlockSpec((tk, nq), lambda b, k: (k, 0)),
            pl.BlockSpec((tk, nq), lambda b, k: (k, 0)),
            pl.BlockSpec((tk, 8), lambda b, k: (k, 0)),
            pl.BlockSpec((nq, 4 * w), lambda b, k: (first_block + b, 0)),
            pl.BlockSpec((nq, 4 * w), lambda b, k: (first_block + b, 0)),
        ],
        out_specs=pl.BlockSpec((None, 4, tk, w), lambda b, k: (b, 0, k, 0)),
        out_shape=jax.ShapeDtypeStruct((BATCH, 4, nq, w), BF16),
        compiler_params=_cparams(("arbitrary", "arbitrary")),
        name=name,
    )(c_q, s_q, tw, pc4, ps4).reshape(BATCH * n, w)


def _fourier(pc4, ps4, dft_lat, dft_ctx, with_ctx):
    out = _dft4(pc4, ps4, dft_lat, SEQ, TK_DFT, 0, "dft_lat")
    if not with_ctx:
        return out, None
    return out, _dft4(pc4, ps4, dft_ctx, CTX, CTX // 4, T_LAT // CTX, "dft_ctx")


def _merge_kernel(n_lat_tiles, with_ctx, h_ref, gain_ref, sh_ref, sc_ref, gt_ref, *refs):
    lat_refs, refs = refs[:3], refs[3:]
    if with_ctx:
        ctx_refs, refs = refs[:3], refs[3:]
    (cbo_ref, u_ref, up_ref, un_ref, cw_ref, cbias_ref, wg_ref, bg_ref, wb_ref, wo_ref,
     o_ref, r_ref) = refs
    i = pl.program_id(0)
    n = pl.program_id(1)
    last = pl.num_programs(1) - 1
    tm = h_ref.shape[0]

    def chunk(r):
        xn = _scale_mod(h_ref[...], r, gain_ref[...], sh_ref[...], sc_ref[...]).astype(BF16)

        u = u_ref[...]
        is_lat = i < n_lat_tiles
        seq_len = jnp.where(is_lat, SEQ, CTX)
        ridx = lax.broadcasted_iota(jnp.int32, (tm, 1), 0)
        pos = (i * tm + ridx) & (seq_len - 1)
        prev = jnp.where(ridx == 0, up_ref[7:8, :], pltpu.roll(u, 1, 0))
        prev = jnp.where(pos == 0, 0.0, prev)
        nxt = jnp.where(ridx == tm - 1, un_ref[0:1, :], pltpu.roll(u, tm - 1, 0))
        nxt = jnp.where(pos == seq_len - 1, 0.0, nxt)
        y = prev * cw_ref[0:1, :] + u * cw_ref[1:2, :] + nxt * cw_ref[2:3, :] + cbias_ref[...]
        conv = (cbo_ref[...] * y).astype(BF16)

        if with_ctx:
            oa, ob, of = (jnp.where(is_lat, lr[...], cr[...]) for lr, cr in zip(lat_refs, ctx_refs))
        else:
            oa, ob, of = (lr[...] for lr in lat_refs)

        merged = None
        for k, br in enumerate((oa, ob, conv, of)):
            gate = jax.nn.sigmoid(jnp.dot(xn, wg_ref[k], preferred_element_type=F32) + bg_ref[k])
            term = gate * jnp.dot(br, wb_ref[k], preferred_element_type=F32)
            merged = term if merged is None else merged + term
        return jnp.dot(merged.astype(BF16), wo_ref[...], preferred_element_type=F32)

    @pl.when(n == 0)
    def _():
        r = _row_rsqrt(h_ref[...])
        r_ref[...] = r
        o_ref[...] = chunk(r)

    @pl.when(jnp.logical_and(n > 0, n < last))
    def _():
        o_ref[...] += chunk(r_ref[...])

    @pl.when(n == last)
    def _():
        o_ref[...] = h_ref[...] + gt_ref[...] * (o_ref[...] + chunk(r_ref[...]))


def _merge(h, mod, layer, br_lat, br_ctx, cbo, u, lw, n_rows, tn=TN_MERGE):
    tm = TM_MERGE
    tpb = SEQ // tm
    n_lat = T_LAT // tm
    with_ctx = br_ctx is not None
    last_blk = T_ALL // 8 - 1
    br_specs = [pl.BlockSpec((tm, BR_W), lambda i, n: (jnp.minimum(i, n_lat - 1), 0))] * 3
    if with_ctx:
        br_specs += [pl.BlockSpec((tm, BR_W), lambda i, n: (jnp.maximum(i - n_lat, 0), 0))] * 3

    def row_spec(w):
        return pl.BlockSpec((tm, w), lambda i, n: (i, 0))

    return pl.pallas_call(
        functools.partial(_merge_kernel, n_lat, with_ctx),
        grid=(n_rows // tm, D // tn),
        in_specs=[
            row_spec(D),
            pl.BlockSpec((None, 1, D), lambda i, n: (layer, 0, 0)),
            _mod_spec(layer, 3, tpb),
            _mod_spec(layer, 4, tpb),
            _mod_spec(layer, 5, tpb),
            *br_specs,
            row_spec(CONV_W), row_spec(CONV_W),
            pl.BlockSpec((8, CONV_W), lambda i, n: (jnp.maximum(i * (tm // 8) - 1, 0), 0)),
            pl.BlockSpec((8, CONV_W), lambda i, n: (jnp.minimum((i + 1) * (tm // 8), last_blk), 0)),
            pl.BlockSpec((None, 3, CONV_W), lambda i, n: (layer, 0, 0)),
            pl.BlockSpec((None, 1, CONV_W), lambda i, n: (layer, 0, 0)),
            pl.BlockSpec((None, N_BR, D, tn), lambda i, n: (layer, 0, 0, n)),
            pl.BlockSpec((None, N_BR, 1, tn), lambda i, n: (layer, 0, 0, n)),
            pl.BlockSpec((None, N_BR, BR_W, tn), lambda i, n: (layer, 0, 0, n)),
            pl.BlockSpec((None, tn, D), lambda i, n: (layer, n, 0)),
        ],
        out_specs=row_spec(D),
        out_shape=jax.ShapeDtypeStruct((n_rows, D), F32),
        scratch_shapes=[pltpu.VMEM((tm, 1), F32)],
        compiler_params=_cparams(("arbitrary", "arbitrary")),
        name="merge",
    )(h, lw["norm_mix"], mod, mod, mod, *br_lat, *(br_ctx or ()), cbo, u, u, u, lw["conv_w"],
      lw["conv_b"], lw["w_gate"], lw["b_gate"], lw["w_branch"], lw["w_o"])


def _rope_tables():
    t = jnp.arange(SEQ, dtype=jnp.int32)
    pos_row = (t // GRID_W).astype(F32)
    pos_col = (t % GRID_W).astype(F32)

    def ang(rot_dim):
        n = rot_dim // 4
        inv_freq = THETA ** (-jnp.arange(n, dtype=F32) / n)
        a = jnp.concatenate([pos_row[:, None] * inv_freq, pos_col[:, None] * inv_freq], axis=-1)
        return jnp.cos(a), jnp.sin(a)

    ca, sa = ang(ROPE_A)
    z32 = jnp.zeros_like(ca)
    z64 = jnp.zeros((SEQ, 64), F32)
    tab_c = jnp.concatenate([ca, ca, z64], axis=-1)
    tab_s1 = jnp.concatenate([-sa, z32, z64], axis=-1)
    tab_s2 = jnp.concatenate([z32, sa, z64], axis=-1)
    cb, sb = ang(HD_B)
    tab_cb = jnp.concatenate([cb, cb], axis=-1)
    tab_sb = jnp.concatenate([-sb, sb], axis=-1)
    ones = jnp.ones((TM_PROJ, 128), F32)
    zeros = jnp.zeros((TM_PROJ, 128), F32)
    return tuple(jnp.concatenate([tab, ident], axis=0)
                 for tab, ident in ((tab_c, ones), (tab_s1, zeros), (tab_s2, zeros),
                                    (tab_cb, ones), (tab_sb, zeros)))


def _dft_mats(n):
    j = jnp.arange(n, dtype=jnp.int32)
    a = ((j[:, None] * j[None, :]) % n).astype(F32) * (2.0 * math.pi / n)
    return jnp.cos(a), jnp.sin(a)


def _dft4_tables(n):
    nq = n // 4
    c_q, s_q = _dft_mats(nq)
    k = jnp.arange(nq, dtype=jnp.int32)[:, None]
    ang = (k * jnp.arange(1, 4, dtype=jnp.int32)[None, :]).astype(F32) * (2.0 * math.pi / n)
    tw = jnp.stack([jnp.cos(ang), jnp.sin(ang)], axis=-1).reshape(nq, 6)
    return c_q.astype(BF16), s_q.astype(BF16), _pad_to(tw, 1, 8)


def _pad_to(a, axis, size):
    pad = [(0, 0)] * a.ndim
    pad[axis] = (0, size - a.shape[axis])
    return jnp.pad(a, pad)


def _prep_weights(p):
    w_uq = p["w_uq"].reshape(DEPTH, Q_LORA, HEADS, QK_A)
    w_uq = _pad_to(_pad_to(w_uq, 3, QK_A_PAD), 1, Q_LORA_PAD).reshape(DEPTH, Q_LORA_PAD, HEADS * QK_A_PAD)
    w_ukv = p["w_ukv"].reshape(DEPTH, KV_LORA, HEADS, NOPE + V_A)
    w_ukv = jnp.concatenate([w_ukv[..., :NOPE].reshape(DEPTH, KV_LORA, HEADS * NOPE),
                             w_ukv[..., NOPE:].reshape(DEPTH, KV_LORA, HEADS * V_A)], axis=2)
    g_ka = p["g_ka"]

    return {
        "w_in": p["w_in"].astype(BF16),
        "g_cq": _pad_to(p["g_cq"], 1, Q_LORA_PAD)[:, None, :],
        "w_uq": w_uq.astype(BF16),
        "g_ckv": p["g_ckv"][:, None, :],
        "w_ukv": w_ukv.astype(BF16),
        "g_qa": _pad_to(p["g_qa"], 1, QK_A_PAD)[:, None, :],
        "g_ka_nope": g_ka[:, None, :NOPE],
        "g_ka_pe": _pad_to(g_ka[:, NOPE:], 1, 128)[:, None, :],
        "g_qb": p["g_qb"][:, None, :],
        "g_kb": p["g_kb"][:, None, :],
        "conv_w": p["conv_w"],
        "conv_b": p["conv_b"][:, None, :],
        "w_gate": p["w_gate"].astype(BF16),
        "b_gate": p["b_gate"][:, :, None, :],
        "w_branch": p["w_branch"].astype(BF16),
        "w_o": p["w_o"].astype(BF16),
        "norm_ffn1": p["norm_ffn1"][:, None, :],
        "norm_mix": p["norm_mix"][:, None, :],
        "norm_ffn2": p["norm_ffn2"][:, None, :],
        "ffn1_wi": p["ffn1_wi"].astype(BF16), "ffn1_wo": p["ffn1_wo"].astype(BF16),
        "ffn2_wi": p["ffn2_wi"].astype(BF16), "ffn2_wo": p["ffn2_wo"].astype(BF16),
    }


def kernel(x, c, ctx, c_ctx, w_ada, b_ada, norm_ffn1, ffn1_wi, ffn1_wo, norm_mix, w_in, g_cq, w_uq, g_ckv, w_ukv, g_qa, g_ka, g_qb, g_kb, conv_w, conv_b, w_branch, w_gate, b_gate, w_o, norm_ffn2, ffn2_wi, ffn2_wo):
    p = dict(norm_ffn1=norm_ffn1, ffn1_wi=ffn1_wi, ffn1_wo=ffn1_wo, norm_mix=norm_mix, w_in=w_in,
             g_cq=g_cq, w_uq=w_uq, g_ckv=g_ckv, w_ukv=w_ukv, g_qa=g_qa, g_ka=g_ka, g_qb=g_qb,
             g_kb=g_kb, conv_w=conv_w, conv_b=conv_b, w_branch=w_branch, w_gate=w_gate,
             b_gate=b_gate, w_o=w_o, norm_ffn2=norm_ffn2, ffn2_wi=ffn2_wi, ffn2_wo=ffn2_wo)

    cond8 = jnp.concatenate([c, c_ctx[None, :], jnp.zeros((8 - BATCH - 1, D), F32)], axis=0)
    mod = _ada(cond8, w_ada, b_ada).reshape(DEPTH, 8, N_MOD, 1, D)

    tabs = _rope_tables()
    cc, sc = _dft_mats(FGD)
    ccs = jnp.concatenate([cc, sc], axis=1).astype(BF16)
    dft_lat = _dft4_tables(SEQ)
    dft_ctx = _dft4_tables(CTX)

    lw = _prep_weights(p)
    h = jnp.concatenate([x.reshape(T_LAT, D), ctx.reshape(T_CTX, D)], axis=0)
    for l in range(DEPTH):
        last = l == DEPTH - 1
        h = _ffn(h, mod, l, 0, lw["norm_ffn1"], lw["ffn1_wi"], lw["ffn1_wo"], T_ALL)
        qa, ka, va, qb, kb, vb, cbo, u, pc, ps = _proj(h, mod, l, lw, tabs, ccs)
        oa, oa_c = _attention(qa, ka, va, QK_A_PAD, V_A, 1, not last)
        ob, ob_c = _attention(qb, kb, vb, HD_B, HD_B, GROUP_B, not last)
        of, of_c = _fourier(pc, ps, dft_lat, dft_ctx, not last)
        n_rows = T_LAT if last else T_ALL
        h = _merge(h, mod, l, (oa, ob, of), None if last else (oa_c, ob_c, of_c), cbo, u, lw, n_rows,
                   256 if l == 1 else TN_MERGE)
        h = _ffn(h, mod, l, 6, lw["norm_ffn2"], lw["ffn2_wi"], lw["ffn2_wo"], n_rows, 256)
    return h.reshape(BATCH, SEQ, D)
```

```python
import functools
import math

import jax
import jax.numpy as jnp
from jax import lax
from jax.experimental import pallas as pl
from jax.experimental.pallas import tpu as pltpu

F32 = jnp.float32
BF16 = jnp.bfloat16

D = 2048
BATCH = 4
SEQ = 4096
DEPTH = 4
GRID_W = 64
CTX = 256
N_MOD = 9
F_FF = 5504
HEADS = 4
Q_LORA = 448
Q_LORA_PAD = 512
KV_LORA = 128
NOPE = 128
ROPE_A = 64
QK_A = NOPE + ROPE_A
QK_A_PAD = 256
V_A = 128
HD_B = 128
KVH_B = 2
GROUP_B = 2
CONV_W = 512
FG = 4
FGD = 128
BR_W = 512
N_BR = 4
THETA = 10000.0
EPS = 1e-6
LOG2_E = math.log2(math.e)

T_LAT = BATCH * SEQ
T_CTX = BATCH * CTX
T_ALL = T_LAT + T_CTX

W_IN = 3712
OFF_A = 0
OFF_B = 640
OFF_C = 1664
OFF_F = 3200

LANE = 128
VMEM_PHYS_V7X = 64 * 1024 * 1024
VMEM_LIMIT = 60 * 1024 * 1024

TM_FFN = 1024
TF_FFN = 512
TM_PROJ = 512
TM_MERGE = 512
TN_MERGE = 512
TQ_ATT = 1024
KC_ATT = 512
TK_DFT = 512
TN_ADA = 1024


def _cparams(sem):
    return pltpu.CompilerParams(dimension_semantics=sem, vmem_limit_bytes=VMEM_LIMIT)


def _mod_row(i, tiles_per_batch):
    return jnp.minimum(i // tiles_per_batch, BATCH)


def _row_rsqrt(x):
    return lax.rsqrt(jnp.mean(x * x, axis=-1, keepdims=True) + EPS)


def _scale_mod(x, r, gain, shift, scale):
    return (x * r * gain) * (1.0 + scale) + shift


def _norm_mod(x, gain, shift, scale):
    return _scale_mod(x, _row_rsqrt(x), gain, shift, scale)


def _silu(x):
    return x * jax.nn.sigmoid(x)


def _ada_kernel(cond_ref, w_ref, b_ref, o_ref):
    s = _silu(cond_ref[...]).astype(BF16)
    o_ref[...] = jnp.dot(s, w_ref[...].astype(BF16), preferred_element_type=F32) + b_ref[...]


def _ada(cond8, w_ada, b_ada):
    n = N_MOD * D
    return pl.pallas_call(
        _ada_kernel,
        grid=(DEPTH, n // TN_ADA),
        in_specs=[
            pl.BlockSpec((8, D), lambda l, j: (0, 0)),
            pl.BlockSpec((None, D, TN_ADA), lambda l, j: (l, 0, j)),
            pl.BlockSpec((None, 1, TN_ADA), lambda l, j: (l, 0, j)),
        ],
        out_specs=pl.BlockSpec((None, 8, TN_ADA), lambda l, j: (l, 0, j)),
        out_shape=jax.ShapeDtypeStruct((DEPTH, 8, n), F32),
        compiler_params=_cparams(("arbitrary", "arbitrary")),
        name="ada",
    )(cond8, w_ada, b_ada.reshape(DEPTH, 1, n))


def _mod_spec(layer, j, tiles_per_batch):
    return pl.BlockSpec((None, None, None, 1, D),
                        lambda i, *_: (layer, _mod_row(i, tiles_per_batch), j, 0, 0))


def _ffn_kernel(overlap, x_ref, gain_ref, sh_ref, sc_ref, gt_ref, wg_ref, wu_ref, wo_ref, o_ref, r_ref):
    f = pl.program_id(1)
    last = pl.num_programs(1) - 1

    def chunk(r, skip=0):
        xn = _scale_mod(x_ref[...], r, gain_ref[...], sh_ref[...], sc_ref[...]).astype(BF16)
        g = jnp.dot(xn, wg_ref[:, skip:], preferred_element_type=F32)
        u = jnp.dot(xn, wu_ref[:, skip:], preferred_element_type=F32)
        a = (_silu(g) * u).astype(BF16)
        return jnp.dot(a, wo_ref[skip:, :], preferred_element_type=F32)

    @pl.when(f == 0)
    def _():
        r = _row_rsqrt(x_ref[...])
        r_ref[...] = r
        o_ref[...] = chunk(r)

    @pl.when(jnp.logical_and(f > 0, f < last))
    def _():
        o_ref[...] += chunk(r_ref[...])

    @pl.when(f == last)
    def _():
        o_ref[...] = x_ref[...] + (0.5 * gt_ref[...]) * (o_ref[...] + chunk(r_ref[...], overlap))


def _ffn(h, mod, layer, j0, gain, wi, wo, n_rows):
    tm, tf = TM_FFN, TF_FFN
    tpb = SEQ // tm
    nf = pl.cdiv(F_FF, tf)

    def hid(f, base=0):
        return (base // LANE + jnp.minimum(f * (tf // LANE), (F_FF - tf) // LANE)) * LANE

    return pl.pallas_call(
        functools.partial(_ffn_kernel, nf * tf - F_FF),
        grid=(n_rows // tm, nf),
        in_specs=[
            pl.BlockSpec((tm, D), lambda i, f: (i, 0)),
            pl.BlockSpec((None, 1, D), lambda i, f: (layer, 0, 0)),
            _mod_spec(layer, j0, tpb),
            _mod_spec(layer, j0 + 1, tpb),
            _mod_spec(layer, j0 + 2, tpb),
            pl.BlockSpec((None, pl.Element(D), pl.Element(tf)), lambda i, f: (layer, 0, hid(f))),
            pl.BlockSpec((None, pl.Element(D), pl.Element(tf)), lambda i, f: (layer, 0, hid(f, F_FF))),
            pl.BlockSpec((None, pl.Element(tf), pl.Element(D)), lambda i, f: (layer, hid(f), 0)),
        ],
        out_specs=pl.BlockSpec((tm, D), lambda i, f: (i, 0)),
        out_shape=jax.ShapeDtypeStruct((n_rows, D), F32),
        scratch_shapes=[pltpu.VMEM((tm, 1), F32)],
        compiler_params=_cparams(("arbitrary", "arbitrary")),
        name="ffn",
    )(h, gain, mod, mod, mod, wi, wi, wo)


def _rope_a(v, c, s1, s2):
    return v * c + pltpu.roll(v, 96, 1) * s1 + pltpu.roll(v, 32, 1) * s2


def _rope_b(v, c, s):
    return v * c + pltpu.roll(v, 64, 1) * s


def _proj_kernel(h_ref, gain_ref, sh_ref, sc_ref, w_ref, gcq_ref, wuq_ref, gckv_ref, wukv_ref,
                 gqa_ref, gkan_ref, gkap_ref, gqb_ref, gkb_ref, ca_ref, s1a_ref, s2a_ref,
                 cb_ref, sb_ref, ccs_ref,
                 qa_ref, ka_ref, va_ref, qb_ref, kb_ref, vb_ref, cbo_ref, u_ref, pc_ref, ps_ref,
                 pp_ref):
    xn = _norm_mod(h_ref[...], gain_ref[...], sh_ref[...], sc_ref[...]).astype(BF16)
    ca, s1a, s2a = ca_ref[...], s1a_ref[...], s2a_ref[...]
    cb, sb = cb_ref[...], sb_ref[...]
    scale_a = QK_A ** -0.5 * LOG2_E
    scale_b = HD_B ** -0.5 * LOG2_E

    za = jnp.dot(xn, w_ref[:, OFF_A:OFF_B], preferred_element_type=F32)
    hi = lax.broadcasted_iota(jnp.int32, (1, LANE), 1) >= LANE // 2
    s3, s4 = za[:, 3 * LANE:4 * LANE], za[:, 4 * LANE:]
    aq = jnp.concatenate([za[:, :3 * LANE], jnp.where(hi, 0.0, s3)], axis=1)
    akv = pltpu.roll(jnp.where(hi, s3, s4), LANE // 2, 1)
    ape = jnp.where(hi, 0.0, pltpu.roll(s4, LANE // 2, 1))
    r = lax.rsqrt(jnp.sum(aq * aq, axis=-1, keepdims=True) * (1.0 / Q_LORA) + EPS)
    cq = (aq * r * gcq_ref[...]).astype(BF16)
    q = jnp.dot(cq, wuq_ref[...], preferred_element_type=F32)
    gqa = gqa_ref[...]
    for hd in range(HEADS):
        lo = hd * QK_A_PAD
        qh = q[:, lo:lo + QK_A_PAD]
        r = lax.rsqrt(jnp.sum(qh * qh, axis=-1, keepdims=True) * (1.0 / QK_A) + EPS)
        qh = qh * r * gqa
        qa_ref[:, lo:lo + NOPE] = (qh[:, :NOPE] * scale_a).astype(BF16)
        qa_ref[:, lo + NOPE:lo + QK_A_PAD] = (_rope_a(qh[:, NOPE:], ca, s1a, s2a) * scale_a).astype(BF16)
    r = lax.rsqrt(jnp.mean(akv * akv, axis=-1, keepdims=True) + EPS)
    ckv = (akv * r * gckv_ref[...]).astype(BF16)
    kv = jnp.dot(ckv, wukv_ref[...], preferred_element_type=F32)
    ss_pe = jnp.sum(ape * ape, axis=-1, keepdims=True)
    ones_blk = jnp.ones((h_ref.shape[0], LANE), BF16)
    gkan, gkap = gkan_ref[...], gkap_ref[...]
    for hd in range(HEADS):
        kn = kv[:, hd * NOPE:(hd + 1) * NOPE]
        r = lax.rsqrt((jnp.sum(kn * kn, axis=-1, keepdims=True) + ss_pe) * (1.0 / QK_A) + EPS)
        lo = hd * QK_A_PAD
        ka_ref[:, lo:lo + NOPE] = (kn * r * gkan).astype(BF16)
        ka_ref[:, lo + NOPE:lo + QK_A_PAD] = _rope_a(ape * r * gkap, ca, s1a, s2a).astype(BF16)
        va_ref[:, 2 * hd * V_A:(2 * hd + 1) * V_A] = kv[:, HEADS * NOPE + hd * V_A:HEADS * NOPE + (hd + 1) * V_A].astype(BF16)
        va_ref[:, (2 * hd + 1) * V_A:(2 * hd + 2) * V_A] = ones_blk

    zb = jnp.dot(xn, w_ref[:, OFF_B:OFF_C], preferred_element_type=F32)
    gqb, gkb = gqb_ref[...], gkb_ref[...]
    for hd in range(HEADS):
        qh = zb[:, hd * HD_B:(hd + 1) * HD_B]
        r = lax.rsqrt(jnp.mean(qh * qh, axis=-1, keepdims=True) + EPS)
        qb_ref[:, hd * HD_B:(hd + 1) * HD_B] = (_rope_b(qh * r * gqb, cb, sb) * scale_b).astype(BF16)
    for hd in range(KVH_B):
        lo = HEADS * HD_B + hd * HD_B
        kh = zb[:, lo:lo + HD_B]
        r = lax.rsqrt(jnp.mean(kh * kh, axis=-1, keepdims=True) + EPS)
        kb_ref[:, hd * HD_B:(hd + 1) * HD_B] = _rope_b(kh * r * gkb, cb, sb).astype(BF16)
    for hd in range(KVH_B):
        lo = (HEADS + KVH_B + hd) * HD_B
        vb_ref[:, 2 * hd * HD_B:(2 * hd + 1) * HD_B] = zb[:, lo:lo + HD_B].astype(BF16)
        vb_ref[:, (2 * hd + 1) * HD_B:(2 * hd + 2) * HD_B] = ones_blk

    zc = jnp.dot(xn, w_ref[:, OFF_C:OFF_F], preferred_element_type=F32)
    cbo_ref[...] = zc[:, :CONV_W]
    u_ref[...] = zc[:, CONV_W:2 * CONV_W] * zc[:, 2 * CONV_W:]

    zf = jnp.dot(xn, w_ref[:, OFF_F:], preferred_element_type=F32).astype(BF16)
    ccs = ccs_ref[...]
    tm = h_ref.shape[0]
    w = FG * FGD
    for g in range(FG):
        pp = jnp.dot(zf[:, g * FGD:(g + 1) * FGD], ccs, preferred_element_type=F32)
        pp_ref[g] = pp[:, :FGD]
        pp_ref[FG + g] = pp[:, FGD:]
    for r in range(4):
        for g in range(FG):
            lo = r * w + g * FGD
            pc_ref[:, lo:lo + FGD] = pp_ref[g, pl.ds(r, tm // 4, stride=4), :].astype(BF16)
            ps_ref[:, lo:lo + FGD] = pp_ref[FG + g, pl.ds(r, tm // 4, stride=4), :].astype(BF16)


def _const_spec(shape, layer=None):
    nd = len(shape)
    if layer is None:
        return pl.BlockSpec(shape, lambda i, *_: (0,) * nd, pipeline_mode=pl.Buffered(1))
    return pl.BlockSpec((None,) + shape, lambda i, *_: (layer,) + (0,) * nd, pipeline_mode=pl.Buffered(1))


def _proj(h, mod, layer, lw, tabs, ccs):
    tm = TM_PROJ
    tpb = SEQ // tm
    n_lat = T_LAT // tm

    def tab_spec():
        return pl.BlockSpec((tm, 128), lambda i: (jnp.where(i < n_lat, i % tpb, tpb), 0))

    def row_spec(w):
        return pl.BlockSpec((tm, w), lambda i: (i, 0))

    widths = (HEADS * QK_A_PAD, HEADS * QK_A_PAD, HEADS * 2 * V_A, HEADS * HD_B, KVH_B * HD_B,
              KVH_B * 2 * HD_B, CONV_W, CONV_W)
    dtypes = (BF16, BF16, BF16, BF16, BF16, BF16, F32, F32)
    dft_spec = pl.BlockSpec((tm // 4, 4 * FG * FGD), lambda i: (i, 0))
    dft_shape = jax.ShapeDtypeStruct((T_ALL // 4, 4 * FG * FGD), BF16)
    return pl.pallas_call(
        _proj_kernel,
        grid=(T_ALL // tm,),
        in_specs=[
            row_spec(D),
            _const_spec((1, D), layer),
            _mod_spec(layer, 3, tpb),
            _mod_spec(layer, 4, tpb),
            _const_spec((D, W_IN), layer),
            _const_spec((1, Q_LORA_PAD), layer),
            _const_spec((Q_LORA_PAD, HEADS * QK_A_PAD), layer),
            _const_spec((1, KV_LORA), layer),
            _const_spec((KV_LORA, HEADS * (NOPE + V_A)), layer),
            _const_spec((1, QK_A_PAD), layer),
            _const_spec((1, NOPE), layer),
            _const_spec((1, 128), layer),
            _const_spec((1, HD_B), layer),
            _const_spec((1, HD_B), layer),
            tab_spec(), tab_spec(), tab_spec(), tab_spec(), tab_spec(),
            _const_spec((FGD, 2 * FGD)),
        ],
        out_specs=[row_spec(w) for w in widths] + [dft_spec, dft_spec],
        out_shape=[jax.ShapeDtypeStruct((T_ALL, w), dt) for w, dt in zip(widths, dtypes)]
        + [dft_shape, dft_shape],
        scratch_shapes=[pltpu.VMEM((2 * FG, tm, FGD), F32)],
        compiler_params=_cparams(("arbitrary",)),
        name="proj",
    )(h, lw["norm_mix"], mod, mod, lw["w_in"], lw["g_cq"], lw["w_uq"], lw["g_ckv"], lw["w_ukv"],
      lw["g_qa"], lw["g_ka_nope"], lw["g_ka_pe"], lw["g_qb"], lw["g_kb"], *tabs, ccs)


_QK_DIMS = (((1,), (1,)), ((), ()))


def _att_lat_kernel(q_ref, k_ref, kc_ref, v_ref, vc_ref, o_ref):
    q = q_ref[...]
    chunks = [(k_ref, v_ref, j * KC_ATT, KC_ATT) for j in range(SEQ // KC_ATT)] + [(kc_ref, vc_ref, 0, CTX)]

    def scores(chunk):
        kr, _, lo, n = chunk
        return lax.dot_general(q, kr[lo:lo + n, :], _QK_DIMS, preferred_element_type=F32)

    dv = o_ref.shape[-1]
    s = scores(chunks[0])
    m = acc = None
    for idx, (_, vr, lo, n) in enumerate(chunks):
        s_next = scores(chunks[idx + 1]) if idx + 1 < len(chunks) else None
        m_chunk = jnp.max(s, axis=-1, keepdims=True)
        m_new = m_chunk if m is None else jnp.maximum(m, m_chunk)
        p = _exp2_bf16(s - m_new)
        pv = jnp.dot(p, vr[lo:lo + n, :], preferred_element_type=F32)
        acc = pv if m is None else jnp.exp2(m - m_new) * acc + pv
        m, s = m_new, s_next
    o_ref[...] = (acc[:, :dv] / acc[:, dv:]).astype(o_ref.dtype)


def _exp2_bf16(x):
    return jnp.exp2(x.astype(BF16))


def _att_ctx_kernel(q_ref, k_ref, v_ref, o_ref):
    dv = o_ref.shape[-1]
    s = lax.dot_general(q_ref[...], k_ref[...], _QK_DIMS, preferred_element_type=F32)
    p = _exp2_bf16(s - jnp.max(s, axis=-1, keepdims=True))
    o = jnp.dot(p, v_ref[...], preferred_element_type=F32)
    o_ref[...] = (o[:, :dv] / o[:, dv:]).astype(o_ref.dtype)


def _attention(q, k, v, dq, dv, group, with_ctx):
    tq = TQ_ATT
    nq = SEQ // tq
    ctx0 = T_LAT // CTX
    out = pl.pallas_call(
        _att_lat_kernel,
        grid=(BATCH, HEADS, nq),
        in_specs=[
            pl.BlockSpec((tq, dq), lambda b, h, i: (b * nq + i, h)),
            pl.BlockSpec((SEQ, dq), lambda b, h, i: (b, h // group)),
            pl.BlockSpec((CTX, dq), lambda b, h, i: (ctx0 + b, h // group)),
            pl.BlockSpec((SEQ, 2 * dv), lambda b, h, i: (b, h // group)),
            pl.BlockSpec((CTX, 2 * dv), lambda b, h, i: (ctx0 + b, h // group)),
        ],
        out_specs=pl.BlockSpec((tq, dv), lambda b, h, i: (b * nq + i, h)),
        out_shape=jax.ShapeDtypeStruct((T_LAT, HEADS * dv), BF16),
        compiler_params=_cparams(("arbitrary", "arbitrary", "arbitrary")),
        name="att_lat",
    )(q, k, k, v, v)
    if not with_ctx:
        return out, None
    out_ctx = pl.pallas_call(
        _att_ctx_kernel,
        grid=(BATCH, HEADS),
        in_specs=[
            pl.BlockSpec((CTX, dq), lambda b, h: (ctx0 + b, h)),
            pl.BlockSpec((CTX, dq), lambda b, h: (ctx0 + b, h // group)),
            pl.BlockSpec((CTX, 2 * dv), lambda b, h: (ctx0 + b, h // group)),
        ],
        out_specs=pl.BlockSpec((CTX, dv), lambda b, h: (b, h)),
        out_shape=jax.ShapeDtypeStruct((T_CTX, HEADS * dv), BF16),
        compiler_params=_cparams(("arbitrary", "arbitrary")),
        name="att_ctx",
    )(q, k, v)
    return out, out_ctx


def _dft4_kernel(scale, c_ref, s_ref, tw_ref, pc_ref, ps_ref, o_ref):
    w = o_ref.shape[-1]
    c, s = c_ref[...], s_ref[...]
    pc, ps = pc_ref[...], ps_ref[...]
    ur = jnp.dot(c, pc, preferred_element_type=F32) - jnp.dot(s, ps, preferred_element_type=F32)
    ui = -(jnp.dot(c, ps, preferred_element_type=F32) + jnp.dot(s, pc, preferred_element_type=F32))
    a = [ur[:, :w]]
    b = [None]
    for r in range(1, 4):
        u_r, v_r = ur[:, r * w:(r + 1) * w], ui[:, r * w:(r + 1) * w]
        cr, sr = tw_ref[:, 2 * r - 2:2 * r - 1], tw_ref[:, 2 * r - 1:2 * r]
        a.append(u_r * cr + v_r * sr)
        b.append(None if r == 2 else v_r * cr - u_r * sr)
    o_ref[0] = ((a[0] + a[1] + a[2] + a[3]) * scale).astype(o_ref.dtype)
    o_ref[1] = ((a[0] + b[1] - a[2] - b[3]) * scale).astype(o_ref.dtype)
    o_ref[2] = ((a[0] - a[1] + a[2] - a[3]) * scale).astype(o_ref.dtype)
    o_ref[3] = ((a[0] - b[1] - a[2] + b[3]) * scale).astype(o_ref.dtype)


def _dft4(pc4, ps4, tables, n, tk, first_block, name):
    w = FG * FGD
    nq = n // 4
    c_q, s_q, tw = tables
    return pl.pallas_call(
        functools.partial(_dft4_kernel, (n * FGD) ** -0.5),
        grid=(BATCH, nq // tk),
        in_specs=[
            pl.BlockSpec((tk, nq), lambda b, k: (k, 0)),
            pl.BlockSpec((tk, nq), lambda b, k: (k, 0)),
            pl.BlockSpec((tk, 8), lambda b, k: (k, 0)),
            pl.BlockSpec((nq, 4 * w), lambda b, k: (first_block + b, 0)),
            pl.BlockSpec((nq, 4 * w), lambda b, k: (first_block + b, 0)),
        ],
        out_specs=pl.BlockSpec((None, 4, tk, w), lambda b, k: (b, 0, k, 0)),
        out_shape=jax.ShapeDtypeStruct((BATCH, 4, nq, w), BF16),
        compiler_params=_cparams(("arbitrary", "arbitrary")),
        name=name,
    )(c_q, s_q, tw, pc4, ps4).reshape(BATCH * n, w)


def _fourier(pc4, ps4, dft_lat, dft_ctx, with_ctx):
    out = _dft4(pc4, ps4, dft_lat, SEQ, TK_DFT, 0, "dft_lat")
    if not with_ctx:
        return out, None
    return out, _dft4(pc4, ps4, dft_ctx, CTX, CTX // 4, T_LAT // CTX, "dft_ctx")


def _merge_kernel(n_lat_tiles, with_ctx, h_ref, gain_ref, sh_ref, sc_ref, gt_ref, *refs):
    lat_refs, refs = refs[:3], refs[3:]
    if with_ctx:
        ctx_refs, refs = refs[:3], refs[3:]
    (cbo_ref, u_ref, up_ref, un_ref, cw_ref, cbias_ref, wg_ref, bg_ref, wb_ref, wo_ref,
     o_ref, r_ref) = refs
    i = pl.program_id(0)
    n = pl.program_id(1)
    last = pl.num_programs(1) - 1
    tm = h_ref.shape[0]

    def chunk(r):
        xn = _scale_mod(h_ref[...], r, gain_ref[...], sh_ref[...], sc_ref[...]).astype(BF16)

        u = u_ref[...]
        is_lat = i < n_lat_tiles
        seq_len = jnp.where(is_lat, SEQ, CTX)
        ridx = lax.broadcasted_iota(jnp.int32, (tm, 1), 0)
        pos = (i * tm + ridx) & (seq_len - 1)
        prev = jnp.where(ridx == 0, up_ref[7:8, :], pltpu.roll(u, 1, 0))
        prev = jnp.where(pos == 0, 0.0, prev)
        nxt = jnp.where(ridx == tm - 1, un_ref[0:1, :], pltpu.roll(u, tm - 1, 0))
        nxt = jnp.where(pos == seq_len - 1, 0.0, nxt)
        y = prev * cw_ref[0:1, :] + u * cw_ref[1:2, :] + nxt * cw_ref[2:3, :] + cbias_ref[...]
        conv = (cbo_ref[...] * y).astype(BF16)

        if with_ctx:
            oa, ob, of = (jnp.where(is_lat, lr[...], cr[...]) for lr, cr in zip(lat_refs, ctx_refs))
        else:
            oa, ob, of = (lr[...] for lr in lat_refs)

        merged = None
        for k, br in enumerate((oa, ob, conv, of)):
            gate = jax.nn.sigmoid(jnp.dot(xn, wg_ref[k], preferred_element_type=F32) + bg_ref[k])
            term = gate * jnp.dot(br, wb_ref[k], preferred_element_type=F32)
            merged = term if merged is None else merged + term
        return jnp.dot(merged.astype(BF16), wo_ref[...], preferred_element_type=F32)

    @pl.when(n == 0)
    def _():
        r = _row_rsqrt(h_ref[...])
        r_ref[...] = r
        o_ref[...] = chunk(r)

    @pl.when(jnp.logical_and(n > 0, n < last))
    def _():
        o_ref[...] += chunk(r_ref[...])

    @pl.when(n == last)
    def _():
        o_ref[...] = h_ref[...] + gt_ref[...] * (o_ref[...] + chunk(r_ref[...]))


def _merge(h, mod, layer, br_lat, br_ctx, cbo, u, lw, n_rows):
    tm, tn = TM_MERGE, TN_MERGE
    tpb = SEQ // tm
    n_lat = T_LAT // tm
    with_ctx = br_ctx is not None
    last_blk = T_ALL // 8 - 1
    br_specs = [pl.BlockSpec((tm, BR_W), lambda i, n: (jnp.minimum(i, n_lat - 1), 0))] * 3
    if with_ctx:
        br_specs += [pl.BlockSpec((tm, BR_W), lambda i, n: (jnp.maximum(i - n_lat, 0), 0))] * 3

    def row_spec(w):
        return pl.BlockSpec((tm, w), lambda i, n: (i, 0))

    return pl.pallas_call(
        functools.partial(_merge_kernel, n_lat, with_ctx),
        grid=(n_rows // tm, D // tn),
        in_specs=[
            row_spec(D),
            pl.BlockSpec((None, 1, D), lambda i, n: (layer, 0, 0)),
            _mod_spec(layer, 3, tpb),
            _mod_spec(layer, 4, tpb),
            _mod_spec(layer, 5, tpb),
            *br_specs,
            row_spec(CONV_W), row_spec(CONV_W),
            pl.BlockSpec((8, CONV_W), lambda i, n: (jnp.maximum(i * (tm // 8) - 1, 0), 0)),
            pl.BlockSpec((8, CONV_W), lambda i, n: (jnp.minimum((i + 1) * (tm // 8), last_blk), 0)),
            pl.BlockSpec((None, 3, CONV_W), lambda i, n: (layer, 0, 0)),
            pl.BlockSpec((None, 1, CONV_W), lambda i, n: (layer, 0, 0)),
            pl.BlockSpec((None, N_BR, D, tn), lambda i, n: (layer, 0, 0, n)),
            pl.BlockSpec((None, N_BR, 1, tn), lambda i, n: (layer, 0, 0, n)),
            pl.BlockSpec((None, N_BR, BR_W, tn), lambda i, n: (layer, 0, 0, n)),
            pl.BlockSpec((None, tn, D), lambda i, n: (layer, n, 0)),
        ],
        out_specs=row_spec(D),
        out_shape=jax.ShapeDtypeStruct((n_rows, D), F32),
        scratch_shapes=[pltpu.VMEM((tm, 1), F32)],
        compiler_params=_cparams(("arbitrary", "arbitrary")),
        name="merge",
    )(h, lw["norm_mix"], mod, mod, mod, *br_lat, *(br_ctx or ()), cbo, u, u, u, lw["conv_w"],
      lw["conv_b"], lw["w_gate"], lw["b_gate"], lw["w_branch"], lw["w_o"])


def _rope_tables():
    t = jnp.arange(SEQ, dtype=jnp.int32)
    pos_row = (t // GRID_W).astype(F32)
    pos_col = (t % GRID_W).astype(F32)

    def ang(rot_dim):
        n = rot_dim // 4
        inv_freq = THETA ** (-jnp.arange(n, dtype=F32) / n)
        a = jnp.concatenate([pos_row[:, None] * inv_freq, pos_col[:, None] * inv_freq], axis=-1)
        return jnp.cos(a), jnp.sin(a)

    ca, sa = ang(ROPE_A)
    z32 = jnp.zeros_like(ca)
    z64 = jnp.zeros((SEQ, 64), F32)
    tab_c = jnp.concatenate([ca, ca, z64], axis=-1)
    tab_s1 = jnp.concatenate([-sa, z32, z64], axis=-1)
    tab_s2 = jnp.concatenate([z32, sa, z64], axis=-1)
    cb, sb = ang(HD_B)
    tab_cb = jnp.concatenate([cb, cb], axis=-1)
    tab_sb = jnp.concatenate([-sb, sb], axis=-1)
    ones = jnp.ones((TM_PROJ, 128), F32)
    zeros = jnp.zeros((TM_PROJ, 128), F32)
    return tuple(jnp.concatenate([tab, ident], axis=0)
                 for tab, ident in ((tab_c, ones), (tab_s1, zeros), (tab_s2, zeros),
                                    (tab_cb, ones), (tab_sb, zeros)))


def _dft_mats(n):
    j = jnp.arange(n, dtype=jnp.int32)
    a = ((j[:, None] * j[None, :]) % n).astype(F32) * (2.0 * math.pi / n)
    return jnp.cos(a), jnp.sin(a)


def _dft4_tables(n):
    nq = n // 4
    c_q, s_q = _dft_mats(nq)
    k = jnp.arange(nq, dtype=jnp.int32)[:, None]
    ang = (k * jnp.arange(1, 4, dtype=jnp.int32)[None, :]).astype(F32) * (2.0 * math.pi / n)
    tw = jnp.stack([jnp.cos(ang), jnp.sin(ang)], axis=-1).reshape(nq, 6)
    return c_q.astype(BF16), s_q.astype(BF16), _pad_to(tw, 1, 8)


def _pad_to(a, axis, size):
    pad = [(0, 0)] * a.ndim
    pad[axis] = (0, size - a.shape[axis])
    return jnp.pad(a, pad)


def _prep_weights(p):
    w_uq = p["w_uq"].reshape(DEPTH, Q_LORA, HEADS, QK_A)
    w_uq = _pad_to(_pad_to(w_uq, 3, QK_A_PAD), 1, Q_LORA_PAD).reshape(DEPTH, Q_LORA_PAD, HEADS * QK_A_PAD)
    w_ukv = p["w_ukv"].reshape(DEPTH, KV_LORA, HEADS, NOPE + V_A)
    w_ukv = jnp.concatenate([w_ukv[..., :NOPE].reshape(DEPTH, KV_LORA, HEADS * NOPE),
                             w_ukv[..., NOPE:].reshape(DEPTH, KV_LORA, HEADS * V_A)], axis=2)
    g_ka = p["g_ka"]

    return {
        "w_in": p["w_in"].astype(BF16),
        "g_cq": _pad_to(p["g_cq"], 1, Q_LORA_PAD)[:, None, :],
        "w_uq": w_uq.astype(BF16),
        "g_ckv": p["g_ckv"][:, None, :],
        "w_ukv": w_ukv.astype(BF16),
        "g_qa": _pad_to(p["g_qa"], 1, QK_A_PAD)[:, None, :],
        "g_ka_nope": g_ka[:, None, :NOPE],
        "g_ka_pe": _pad_to(g_ka[:, NOPE:], 1, 128)[:, None, :],
        "g_qb": p["g_qb"][:, None, :],
        "g_kb": p["g_kb"][:, None, :],
        "conv_w": p["conv_w"],
        "conv_b": p["conv_b"][:, None, :],
        "w_gate": p["w_gate"].astype(BF16),
        "b_gate": p["b_gate"][:, :, None, :],
        "w_branch": p["w_branch"].astype(BF16),
        "w_o": p["w_o"].astype(BF16),
        "norm_ffn1": p["norm_ffn1"][:, None, :],
        "norm_mix": p["norm_mix"][:, None, :],
        "norm_ffn2": p["norm_ffn2"][:, None, :],
        "ffn1_wi": p["ffn1_wi"].astype(BF16), "ffn1_wo": p["ffn1_wo"].astype(BF16),
        "ffn2_wi": p["ffn2_wi"].astype(BF16), "ffn2_wo": p["ffn2_wo"].astype(BF16),
    }


def kernel(x, c, ctx, c_ctx, w_ada, b_ada, norm_ffn1, ffn1_wi, ffn1_wo, norm_mix, w_in, g_cq, w_uq, g_ckv, w_ukv, g_qa, g_ka, g_qb, g_kb, conv_w, conv_b, w_branch, w_gate, b_gate, w_o, norm_ffn2, ffn2_wi, ffn2_wo):
    p = dict(norm_ffn1=norm_ffn1, ffn1_wi=ffn1_wi, ffn1_wo=ffn1_wo, norm_mix=norm_mix, w_in=w_in,
             g_cq=g_cq, w_uq=w_uq, g_ckv=g_ckv, w_ukv=w_ukv, g_qa=g_qa, g_ka=g_ka, g_qb=g_qb,
             g_kb=g_kb, conv_w=conv_w, conv_b=conv_b, w_branch=w_branch, w_gate=w_gate,
             b_gate=b_gate, w_o=w_o, norm_ffn2=norm_ffn2, ffn2_wi=ffn2_wi, ffn2_wo=ffn2_wo)

    cond8 = jnp.concatenate([c, c_ctx[None, :], jnp.zeros((8 - BATCH - 1, D), F32)], axis=0)
    mod = _ada(cond8, w_ada, b_ada).reshape(DEPTH, 8, N_MOD, 1, D)

    tabs = _rope_tables()
    cc, sc = _dft_mats(FGD)
    ccs = jnp.concatenate([cc, sc], axis=1).astype(BF16)
    dft_lat = _dft4_tables(SEQ)
    dft_ctx = _dft4_tables(CTX)

    lw = _prep_weights(p)
    h = jnp.concatenate([x.reshape(T_LAT, D), ctx.reshape(T_CTX, D)], axis=0)
    for l in range(DEPTH):
        last = l == DEPTH - 1
        h = _ffn(h, mod, l, 0, lw["norm_ffn1"], lw["ffn1_wi"], lw["ffn1_wo"], T_ALL)
        qa, ka, va, qb, kb, vb, cbo, u, pc, ps = _proj(h, mod, l, lw, tabs, ccs)
        oa, oa_c = _attention(qa, ka, va, QK_A_PAD, V_A, 1, not last)
        ob, ob_c = _attention(qb, kb, vb, HD_B, HD_B, GROUP_B, not last)
        of, of_c = _fourier(pc, ps, dft_lat, dft_ctx, not last)
        n_rows = T_LAT if last else T_ALL
        h = _merge(h, mod, l, (oa, ob, of), None if last else (oa_c, ob_c, of_c), cbo, u, lw, n_rows)
        h = _ffn(h, mod, l, 6, lw["norm_ffn2"], lw["ffn2_wi"], lw["ffn2_wo"], n_rows)
    return h.reshape(BATCH, SEQ, D)
```

```python
import functools
import math

import jax
import jax.numpy as jnp
from jax import lax
from jax.experimental import pallas as pl
from jax.experimental.pallas import tpu as pltpu

F32 = jnp.float32
BF16 = jnp.bfloat16

D = 2048
BATCH = 4
SEQ = 4096
DEPTH = 4
GRID_W = 64
CTX = 256
N_MOD = 9
F_FF = 5504
HEADS = 4
Q_LORA = 448
Q_LORA_PAD = 512
KV_LORA = 128
NOPE = 128
ROPE_A = 64
QK_A = NOPE + ROPE_A
QK_A_PAD = 256
V_A = 128
HD_B = 128
KVH_B = 2
GROUP_B = 2
CONV_W = 512
FG = 4
FGD = 128
BR_W = 512
N_BR = 4
THETA = 10000.0
EPS = 1e-6
LOG2_E = math.log2(math.e)

T_LAT = BATCH * SEQ
T_CTX = BATCH * CTX
T_ALL = T_LAT + T_CTX

W_IN = 3712
OFF_A = 0
OFF_B = 640
OFF_C = 1664
OFF_F = 3200

LANE = 128
VMEM_PHYS_V7X = 64 * 1024 * 1024
VMEM_LIMIT = 60 * 1024 * 1024

TM_FFN = 1024
TF_FFN = 512
TM_PROJ = 512
TM_MERGE = 512
TN_MERGE = 512
TQ_ATT = 1024
KC_ATT = 512
TK_DFT = 512
TN_ADA = 1024


def _cparams(sem):
    return pltpu.CompilerParams(dimension_semantics=sem, vmem_limit_bytes=VMEM_LIMIT)


def _mod_row(i, tiles_per_batch):
    return jnp.minimum(i // tiles_per_batch, BATCH)


def _row_rsqrt(x):
    return lax.rsqrt(jnp.mean(x * x, axis=-1, keepdims=True) + EPS)


def _scale_mod(x, r, gain, shift, scale):
    return (x * r * gain) * (1.0 + scale) + shift


def _norm_mod(x, gain, shift, scale):
    return _scale_mod(x, _row_rsqrt(x), gain, shift, scale)


def _silu(x):
    return x * jax.nn.sigmoid(x)


def _ada_kernel(cond_ref, w_ref, b_ref, o_ref):
    s = _silu(cond_ref[...]).astype(BF16)
    o_ref[...] = jnp.dot(s, w_ref[...].astype(BF16), preferred_element_type=F32) + b_ref[...]


def _ada(cond8, w_ada, b_ada):
    n = N_MOD * D
    return pl.pallas_call(
        _ada_kernel,
        grid=(DEPTH, n // TN_ADA),
        in_specs=[
            pl.BlockSpec((8, D), lambda l, j: (0, 0)),
            pl.BlockSpec((None, D, TN_ADA), lambda l, j: (l, 0, j)),
            pl.BlockSpec((None, 1, TN_ADA), lambda l, j: (l, 0, j)),
        ],
        out_specs=pl.BlockSpec((None, 8, TN_ADA), lambda l, j: (l, 0, j)),
        out_shape=jax.ShapeDtypeStruct((DEPTH, 8, n), F32),
        compiler_params=_cparams(("arbitrary", "arbitrary")),
        name="ada",
    )(cond8, w_ada, b_ada.reshape(DEPTH, 1, n))


def _mod_spec(layer, j, tiles_per_batch):
    return pl.BlockSpec((None, None, None, 1, D),
                        lambda i, *_: (layer, _mod_row(i, tiles_per_batch), j, 0, 0))


def _ffn_kernel(overlap, x_ref, gain_ref, sh_ref, sc_ref, gt_ref, wg_ref, wu_ref, wo_ref, o_ref, r_ref):
    f = pl.program_id(1)
    last = pl.num_programs(1) - 1

    def chunk(r, skip=0):
        xn = _scale_mod(x_ref[...], r, gain_ref[...], sh_ref[...], sc_ref[...]).astype(BF16)
        g = jnp.dot(xn, wg_ref[:, skip:], preferred_element_type=F32)
        u = jnp.dot(xn, wu_ref[:, skip:], preferred_element_type=F32)
        a = (_silu(g) * u).astype(BF16)
        return jnp.dot(a, wo_ref[skip:, :], preferred_element_type=F32)

    @pl.when(f == 0)
    def _():
        r = _row_rsqrt(x_ref[...])
        r_ref[...] = r
        o_ref[...] = chunk(r)

    @pl.when(jnp.logical_and(f > 0, f < last))
    def _():
        o_ref[...] += chunk(r_ref[...])

    @pl.when(f == last)
    def _():
        o_ref[...] = x_ref[...] + (0.5 * gt_ref[...]) * (o_ref[...] + chunk(r_ref[...], overlap))


def _ffn(h, mod, layer, j0, gain, wi, wo, n_rows):
    tm, tf = TM_FFN, TF_FFN
    tpb = SEQ // tm
    nf = pl.cdiv(F_FF, tf)

    def hid(f, base=0):
        return (base // LANE + jnp.minimum(f * (tf // LANE), (F_FF - tf) // LANE)) * LANE

    return pl.pallas_call(
        functools.partial(_ffn_kernel, nf * tf - F_FF),
        grid=(n_rows // tm, nf),
        in_specs=[
            pl.BlockSpec((tm, D), lambda i, f: (i, 0)),
            pl.BlockSpec((None, 1, D), lambda i, f: (layer, 0, 0)),
            _mod_spec(layer, j0, tpb),
            _mod_spec(layer, j0 + 1, tpb),
            _mod_spec(layer, j0 + 2, tpb),
            pl.BlockSpec((None, pl.Element(D), pl.Element(tf)), lambda i, f: (layer, 0, hid(f))),
            pl.BlockSpec((None, pl.Element(D), pl.Element(tf)), lambda i, f: (layer, 0, hid(f, F_FF))),
            pl.BlockSpec((None, pl.Element(tf), pl.Element(D)), lambda i, f: (layer, hid(f), 0)),
        ],
        out_specs=pl.BlockSpec((tm, D), lambda i, f: (i, 0)),
        out_shape=jax.ShapeDtypeStruct((n_rows, D), F32),
        scratch_shapes=[pltpu.VMEM((tm, 1), F32)],
        compiler_params=_cparams(("arbitrary", "arbitrary")),
        name="ffn",
    )(h, gain, mod, mod, mod, wi, wi, wo)


def _rope_a(v, c, s1, s2):
    return v * c + pltpu.roll(v, 96, 1) * s1 + pltpu.roll(v, 32, 1) * s2


def _rope_b(v, c, s):
    return v * c + pltpu.roll(v, 64, 1) * s


def _proj_kernel(h_ref, gain_ref, sh_ref, sc_ref, w_ref, gcq_ref, wuq_ref, gckv_ref, wukv_ref,
                 gqa_ref, gkan_ref, gkap_ref, gqb_ref, gkb_ref, ca_ref, s1a_ref, s2a_ref,
                 cb_ref, sb_ref, ccs_ref,
                 qa_ref, ka_ref, va_ref, qb_ref, kb_ref, vb_ref, cbo_ref, u_ref, pc_ref, ps_ref,
                 pp_ref):
    xn = _norm_mod(h_ref[...], gain_ref[...], sh_ref[...], sc_ref[...]).astype(BF16)
    ca, s1a, s2a = ca_ref[...], s1a_ref[...], s2a_ref[...]
    cb, sb = cb_ref[...], sb_ref[...]
    scale_a = QK_A ** -0.5 * LOG2_E
    scale_b = HD_B ** -0.5 * LOG2_E

    za = jnp.dot(xn, w_ref[:, OFF_A:OFF_B], preferred_element_type=F32)
    zb = jnp.dot(xn, w_ref[:, OFF_B:OFF_C], preferred_element_type=F32)

    hi = lax.broadcasted_iota(jnp.int32, (1, LANE), 1) >= LANE // 2
    s3, s4 = za[:, 3 * LANE:4 * LANE], za[:, 4 * LANE:]
    aq = jnp.concatenate([za[:, :3 * LANE], jnp.where(hi, 0.0, s3)], axis=1)
    akv = pltpu.roll(jnp.where(hi, s3, s4), LANE // 2, 1)
    ape = jnp.where(hi, 0.0, pltpu.roll(s4, LANE // 2, 1))
    r = lax.rsqrt(jnp.sum(aq * aq, axis=-1, keepdims=True) * (1.0 / Q_LORA) + EPS)
    cq = (aq * r * gcq_ref[...]).astype(BF16)
    q = jnp.dot(cq, wuq_ref[...], preferred_element_type=F32)
    gqa = gqa_ref[...]
    for hd in range(HEADS):
        lo = hd * QK_A_PAD
        qh = q[:, lo:lo + QK_A_PAD]
        r = lax.rsqrt(jnp.sum(qh * qh, axis=-1, keepdims=True) * (1.0 / QK_A) + EPS)
        qh = qh * r * gqa
        qa_ref[:, lo:lo + NOPE] = (qh[:, :NOPE] * scale_a).astype(BF16)
        qa_ref[:, lo + NOPE:lo + QK_A_PAD] = (_rope_a(qh[:, NOPE:], ca, s1a, s2a) * scale_a).astype(BF16)
    r = lax.rsqrt(jnp.mean(akv * akv, axis=-1, keepdims=True) + EPS)
    ckv = (akv * r * gckv_ref[...]).astype(BF16)
    kv = jnp.dot(ckv, wukv_ref[...], preferred_element_type=F32)
    ss_pe = jnp.sum(ape * ape, axis=-1, keepdims=True)
    ones_blk = jnp.ones((h_ref.shape[0], LANE), BF16)
    gkan, gkap = gkan_ref[...], gkap_ref[...]
    for hd in range(HEADS):
        kn = kv[:, hd * NOPE:(hd + 1) * NOPE]
        r = lax.rsqrt((jnp.sum(kn * kn, axis=-1, keepdims=True) + ss_pe) * (1.0 / QK_A) + EPS)
        lo = hd * QK_A_PAD
        ka_ref[:, lo:lo + NOPE] = (kn * r * gkan).astype(BF16)
        ka_ref[:, lo + NOPE:lo + QK_A_PAD] = _rope_a(ape * r * gkap, ca, s1a, s2a).astype(BF16)
        va_ref[:, 2 * hd * V_A:(2 * hd + 1) * V_A] = kv[:, HEADS * NOPE + hd * V_A:HEADS * NOPE + (hd + 1) * V_A].astype(BF16)
        va_ref[:, (2 * hd + 1) * V_A:(2 * hd + 2) * V_A] = ones_blk

    zc = jnp.dot(xn, w_ref[:, OFF_C:OFF_F], preferred_element_type=F32)
    gqb, gkb = gqb_ref[...], gkb_ref[...]
    for hd in range(HEADS):
        qh = zb[:, hd * HD_B:(hd + 1) * HD_B]
        r = lax.rsqrt(jnp.mean(qh * qh, axis=-1, keepdims=True) + EPS)
        qb_ref[:, hd * HD_B:(hd + 1) * HD_B] = (_rope_b(qh * r * gqb, cb, sb) * scale_b).astype(BF16)
    for hd in range(KVH_B):
        lo = HEADS * HD_B + hd * HD_B
        kh = zb[:, lo:lo + HD_B]
        r = lax.rsqrt(jnp.mean(kh * kh, axis=-1, keepdims=True) + EPS)
        kb_ref[:, hd * HD_B:(hd + 1) * HD_B] = _rope_b(kh * r * gkb, cb, sb).astype(BF16)
    for hd in range(KVH_B):
        lo = (HEADS + KVH_B + hd) * HD_B
        vb_ref[:, 2 * hd * HD_B:(2 * hd + 1) * HD_B] = zb[:, lo:lo + HD_B].astype(BF16)
        vb_ref[:, (2 * hd + 1) * HD_B:(2 * hd + 2) * HD_B] = ones_blk

    zf = jnp.dot(xn, w_ref[:, OFF_F:], preferred_element_type=F32).astype(BF16)
    cbo_ref[...] = zc[:, :CONV_W]
    u_ref[...] = zc[:, CONV_W:2 * CONV_W] * zc[:, 2 * CONV_W:]

    ccs = ccs_ref[...]
    tm = h_ref.shape[0]
    w = FG * FGD
    for g in range(FG):
        pp = jnp.dot(zf[:, g * FGD:(g + 1) * FGD], ccs, preferred_element_type=F32)
        pp_ref[g] = pp[:, :FGD]
        pp_ref[FG + g] = pp[:, FGD:]
    for r in range(4):
        for g in range(FG):
            lo = r * w + g * FGD
            pc_ref[:, lo:lo + FGD] = pp_ref[g, pl.ds(r, tm // 4, stride=4), :].astype(BF16)
            ps_ref[:, lo:lo + FGD] = pp_ref[FG + g, pl.ds(r, tm // 4, stride=4), :].astype(BF16)


def _const_spec(shape, layer=None):
    nd = len(shape)
    if layer is None:
        return pl.BlockSpec(shape, lambda i, *_: (0,) * nd, pipeline_mode=pl.Buffered(1))
    return pl.BlockSpec((None,) + shape, lambda i, *_: (layer,) + (0,) * nd, pipeline_mode=pl.Buffered(1))


def _proj(h, mod, layer, lw, tabs, ccs):
    tm = TM_PROJ
    tpb = SEQ // tm
    n_lat = T_LAT // tm

    def tab_spec():
        return pl.BlockSpec((tm, 128), lambda i: (jnp.where(i < n_lat, i % tpb, tpb), 0))

    def row_spec(w):
        return pl.BlockSpec((tm, w), lambda i: (i, 0))

    widths = (HEADS * QK_A_PAD, HEADS * QK_A_PAD, HEADS * 2 * V_A, HEADS * HD_B, KVH_B * HD_B,
              KVH_B * 2 * HD_B, CONV_W, CONV_W)
    dtypes = (BF16, BF16, BF16, BF16, BF16, BF16, F32, F32)
    dft_spec = pl.BlockSpec((tm // 4, 4 * FG * FGD), lambda i: (i, 0))
    dft_shape = jax.ShapeDtypeStruct((T_ALL // 4, 4 * FG * FGD), BF16)
    return pl.pallas_call(
        _proj_kernel,
        grid=(T_ALL // tm,),
        in_specs=[
            row_spec(D),
            _const_spec((1, D), layer),
            _mod_spec(layer, 3, tpb),
            _mod_spec(layer, 4, tpb),
            _const_spec((D, W_IN), layer),
            _const_spec((1, Q_LORA_PAD), layer),
            _const_spec((Q_LORA_PAD, HEADS * QK_A_PAD), layer),
            _const_spec((1, KV_LORA), layer),
            _const_spec((KV_LORA, HEADS * (NOPE + V_A)), layer),
            _const_spec((1, QK_A_PAD), layer),
            _const_spec((1, NOPE), layer),
            _const_spec((1, 128), layer),
            _const_spec((1, HD_B), layer),
            _const_spec((1, HD_B), layer),
            tab_spec(), tab_spec(), tab_spec(), tab_spec(), tab_spec(),
            _const_spec((FGD, 2 * FGD)),
        ],
        out_specs=[row_spec(w) for w in widths] + [dft_spec, dft_spec],
        out_shape=[jax.ShapeDtypeStruct((T_ALL, w), dt) for w, dt in zip(widths, dtypes)]
        + [dft_shape, dft_shape],
        scratch_shapes=[pltpu.VMEM((2 * FG, tm, FGD), F32)],
        compiler_params=_cparams(("arbitrary",)),
        name="proj",
    )(h, lw["norm_mix"], mod, mod, lw["w_in"], lw["g_cq"], lw["w_uq"], lw["g_ckv"], lw["w_ukv"],
      lw["g_qa"], lw["g_ka_nope"], lw["g_ka_pe"], lw["g_qb"], lw["g_kb"], *tabs, ccs)


_QK_DIMS = (((1,), (1,)), ((), ()))


def _att_lat_kernel(q_ref, k_ref, kc_ref, v_ref, vc_ref, o_ref):
    q = q_ref[...]
    chunks = [(k_ref, v_ref, j * KC_ATT, KC_ATT) for j in range(SEQ // KC_ATT)] + [(kc_ref, vc_ref, 0, CTX)]

    def scores(chunk):
        kr, _, lo, n = chunk
        return lax.dot_general(q, kr[lo:lo + n, :], _QK_DIMS, preferred_element_type=F32)

    dv = o_ref.shape[-1]
    s = scores(chunks[0])
    m = acc = None
    for idx, (_, vr, lo, n) in enumerate(chunks):
        s_next = scores(chunks[idx + 1]) if idx + 1 < len(chunks) else None
        m_chunk = jnp.max(s, axis=-1, keepdims=True)
        m_new = m_chunk if m is None else jnp.maximum(m, m_chunk)
        p = _exp2_bf16(s - m_new)
        pv = jnp.dot(p, vr[lo:lo + n, :], preferred_element_type=F32)
        acc = pv if m is None else jnp.exp2(m - m_new) * acc + pv
        m, s = m_new, s_next
    o_ref[...] = (acc[:, :dv] / acc[:, dv:]).astype(o_ref.dtype)


def _exp2_bf16(x):
    return jnp.exp2(x.astype(BF16))


def _att_ctx_kernel(q_ref, k_ref, v_ref, o_ref):
    dv = o_ref.shape[-1]
    s = lax.dot_general(q_ref[...], k_ref[...], _QK_DIMS, preferred_element_type=F32)
    p = _exp2_bf16(s - jnp.max(s, axis=-1, keepdims=True))
    o = jnp.dot(p, v_ref[...], preferred_element_type=F32)
    o_ref[...] = (o[:, :dv] / o[:, dv:]).astype(o_ref.dtype)


def _attention(q, k, v, dq, dv, group, with_ctx):
    tq = TQ_ATT
    nq = SEQ // tq
    ctx0 = T_LAT // CTX
    out = pl.pallas_call(
        _att_lat_kernel,
        grid=(BATCH, HEADS, nq),
        in_specs=[
            pl.BlockSpec((tq, dq), lambda b, h, i: (b * nq + i, h)),
            pl.BlockSpec((SEQ, dq), lambda b, h, i: (b, h // group)),
            pl.BlockSpec((CTX, dq), lambda b, h, i: (ctx0 + b, h // group)),
            pl.BlockSpec((SEQ, 2 * dv), lambda b, h, i: (b, h // group)),
            pl.BlockSpec((CTX, 2 * dv), lambda b, h, i: (ctx0 + b, h // group)),
        ],
        out_specs=pl.BlockSpec((tq, dv), lambda b, h, i: (b * nq + i, h)),
        out_shape=jax.ShapeDtypeStruct((T_LAT, HEADS * dv), BF16),
        compiler_params=_cparams(("arbitrary", "arbitrary", "arbitrary")),
        name="att_lat",
    )(q, k, k, v, v)
    if not with_ctx:
        return out, None
    out_ctx = pl.pallas_call(
        _att_ctx_kernel,
        grid=(BATCH, HEADS),
        in_specs=[
            pl.BlockSpec((CTX, dq), lambda b, h: (ctx0 + b, h)),
            pl.BlockSpec((CTX, dq), lambda b, h: (ctx0 + b, h // group)),
            pl.BlockSpec((CTX, 2 * dv), lambda b, h: (ctx0 + b, h // group)),
        ],
        out_specs=pl.BlockSpec((CTX, dv), lambda b, h: (b, h)),
        out_shape=jax.ShapeDtypeStruct((T_CTX, HEADS * dv), BF16),
        compiler_params=_cparams(("arbitrary", "arbitrary")),
        name="att_ctx",
    )(q, k, v)
    return out, out_ctx


def _dft4_kernel(scale, c_ref, s_ref, tw_ref, pc_ref, ps_ref, o_ref):
    w = o_ref.shape[-1]
    c, s = c_ref[...], s_ref[...]
    pc, ps = pc_ref[...], ps_ref[...]
    ur = jnp.dot(c, pc, preferred_element_type=F32) - jnp.dot(s, ps, preferred_element_type=F32)
    ui = -(jnp.dot(c, ps, preferred_element_type=F32) + jnp.dot(s, pc, preferred_element_type=F32))
    a = [ur[:, :w]]
    b = [None]
    for r in range(1, 4):
        u_r, v_r = ur[:, r * w:(r + 1) * w], ui[:, r * w:(r + 1) * w]
        cr, sr = tw_ref[:, 2 * r - 2:2 * r - 1], tw_ref[:, 2 * r - 1:2 * r]
        a.append(u_r * cr + v_r * sr)
        b.append(None if r == 2 else v_r * cr - u_r * sr)
    o_ref[0] = ((a[0] + a[1] + a[2] + a[3]) * scale).astype(o_ref.dtype)
    o_ref[1] = ((a[0] + b[1] - a[2] - b[3]) * scale).astype(o_ref.dtype)
    o_ref[2] = ((a[0] - a[1] + a[2] - a[3]) * scale).astype(o_ref.dtype)
    o_ref[3] = ((a[0] - b[1] - a[2] + b[3]) * scale).astype(o_ref.dtype)


def _dft4(pc4, ps4, tables, n, tk, first_block, name):
    w = FG * FGD
    nq = n // 4
    c_q, s_q, tw = tables
    return pl.pallas_call(
        functools.partial(_dft4_kernel, (n * FGD) ** -0.5),
        grid=(BATCH, nq // tk),
        in_specs=[
            pl.BlockSpec((tk, nq), lambda b, k: (k, 0)),
            pl.BlockSpec((tk, nq), lambda b, k: (k, 0)),
            pl.BlockSpec((tk, 8), lambda b, k: (k, 0)),
            pl.BlockSpec((nq, 4 * w), lambda b, k: (first_block + b, 0)),
            pl.BlockSpec((nq, 4 * w), lambda b, k: (first_block + b, 0)),
        ],
        out_specs=pl.BlockSpec((None, 4, tk, w), lambda b, k: (b, 0, k, 0)),
        out_shape=jax.ShapeDtypeStruct((BATCH, 4, nq, w), BF16),
        compiler_params=_cparams(("arbitrary", "arbitrary")),
        name=name,
    )(c_q, s_q, tw, pc4, ps4).reshape(BATCH * n, w)


def _fourier(pc4, ps4, dft_lat, dft_ctx, with_ctx):
    out = _dft4(pc4, ps4, dft_lat, SEQ, TK_DFT, 0, "dft_lat")
    if not with_ctx:
        return out, None
    return out, _dft4(pc4, ps4, dft_ctx, CTX, CTX // 4, T_LAT // CTX, "dft_ctx")


def _merge_kernel(n_lat_tiles, with_ctx, h_ref, gain_ref, sh_ref, sc_ref, gt_ref, *refs):
    lat_refs, refs = refs[:3], refs[3:]
    if with_ctx:
        ctx_refs, refs = refs[:3], refs[3:]
    (cbo_ref, u_ref, up_ref, un_ref, cw_ref, cbias_ref, wg_ref, bg_ref, wb_ref, wo_ref,
     o_ref, r_ref, conv_ref) = refs
    i = pl.program_id(0)
    n = pl.program_id(1)
    last = pl.num_programs(1) - 1
    tm = h_ref.shape[0]
    is_lat = i < n_lat_tiles

    def conv_branch():
        u = u_ref[...]
        seq_len = jnp.where(is_lat, SEQ, CTX)
        ridx = lax.broadcasted_iota(jnp.int32, (tm, 1), 0)
        pos = (i * tm + ridx) & (seq_len - 1)
        prev = jnp.where(ridx == 0, up_ref[7:8, :], pltpu.roll(u, 1, 0))
        prev = jnp.where(pos == 0, 0.0, prev)
        nxt = jnp.where(ridx == tm - 1, un_ref[0:1, :], pltpu.roll(u, tm - 1, 0))
        nxt = jnp.where(pos == seq_len - 1, 0.0, nxt)
        y = prev * cw_ref[0:1, :] + u * cw_ref[1:2, :] + nxt * cw_ref[2:3, :] + cbias_ref[...]
        return (cbo_ref[...] * y).astype(BF16)

    def chunk(r, first=False):
        xn = _scale_mod(h_ref[...], r, gain_ref[...], sh_ref[...], sc_ref[...]).astype(BF16)
        if first:
            conv = conv_branch()
            conv_ref[...] = conv
        else:
            conv = conv_ref[...]

        if with_ctx:
            oa, ob, of = (jnp.where(is_lat, lr[...], cr[...]) for lr, cr in zip(lat_refs, ctx_refs))
        else:
            oa, ob, of = (lr[...] for lr in lat_refs)

        merged = None
        for k, br in enumerate((oa, ob, conv, of)):
            gate = jax.nn.sigmoid(jnp.dot(xn, wg_ref[k], preferred_element_type=F32) + bg_ref[k])
            term = gate * jnp.dot(br, wb_ref[k], preferred_element_type=F32)
            merged = term if merged is None else merged + term
        return jnp.dot(merged.astype(BF16), wo_ref[...], preferred_element_type=F32)

    @pl.when(n == 0)
    def _():
        r = _row_rsqrt(h_ref[...])
        r_ref[...] = r
        o_ref[...] = chunk(r, first=True)

    @pl.when(jnp.logical_and(n > 0, n < last))
    def _():
        o_ref[...] += chunk(r_ref[...])

    @pl.when(n == last)
    def _():
        o_ref[...] = h_ref[...] + gt_ref[...] * (o_ref[...] + chunk(r_ref[...]))


def _merge(h, mod, layer, br_lat, br_ctx, cbo, u, lw, n_rows):
    tm, tn = TM_MERGE, TN_MERGE
    tpb = SEQ // tm
    n_lat = T_LAT // tm
    with_ctx = br_ctx is not None
    last_blk = T_ALL // 8 - 1
    br_specs = [pl.BlockSpec((tm, BR_W), lambda i, n: (jnp.minimum(i, n_lat - 1), 0))] * 3
    if with_ctx:
        br_specs += [pl.BlockSpec((tm, BR_W), lambda i, n: (jnp.maximum(i - n_lat, 0), 0))] * 3

    def row_spec(w):
        return pl.BlockSpec((tm, w), lambda i, n: (i, 0))

    return pl.pallas_call(
        functools.partial(_merge_kernel, n_lat, with_ctx),
        grid=(n_rows // tm, D // tn),
        in_specs=[
            row_spec(D),
            pl.BlockSpec((None, 1, D), lambda i, n: (layer, 0, 0)),
            _mod_spec(layer, 3, tpb),
            _mod_spec(layer, 4, tpb),
            _mod_spec(layer, 5, tpb),
            *br_specs,
            row_spec(CONV_W), row_spec(CONV_W),
            pl.BlockSpec((8, CONV_W), lambda i, n: (jnp.maximum(i * (tm // 8) - 1, 0), 0)),
            pl.BlockSpec((8, CONV_W), lambda i, n: (jnp.minimum((i + 1) * (tm // 8), last_blk), 0)),
            pl.BlockSpec((None, 3, CONV_W), lambda i, n: (layer, 0, 0)),
            pl.BlockSpec((None, 1, CONV_W), lambda i, n: (layer, 0, 0)),
            pl.BlockSpec((None, N_BR, D, tn), lambda i, n: (layer, 0, 0, n)),
            pl.BlockSpec((None, N_BR, 1, tn), lambda i, n: (layer, 0, 0, n)),
            pl.BlockSpec((None, N_BR, BR_W, tn), lambda i, n: (layer, 0, 0, n)),
            pl.BlockSpec((None, tn, D), lambda i, n: (layer, n, 0)),
        ],
        out_specs=row_spec(D),
        out_shape=jax.ShapeDtypeStruct((n_rows, D), F32),
        scratch_shapes=[pltpu.VMEM((tm, 1), F32), pltpu.VMEM((tm, CONV_W), BF16)],
        compiler_params=_cparams(("arbitrary", "arbitrary")),
        name="merge",
    )(h, lw["norm_mix"], mod, mod, mod, *br_lat, *(br_ctx or ()), cbo, u, u, u, lw["conv_w"],
      lw["conv_b"], lw["w_gate"], lw["b_gate"], lw["w_branch"], lw["w_o"])


def _rope_tables():
    t = jnp.arange(SEQ, dtype=jnp.int32)
    pos_row = (t // GRID_W).astype(F32)
    pos_col = (t % GRID_W).astype(F32)

    def ang(rot_dim):
        n = rot_dim // 4
        inv_freq = THETA ** (-jnp.arange(n, dtype=F32) / n)
        a = jnp.concatenate([pos_row[:, None] * inv_freq, pos_col[:, None] * inv_freq], axis=-1)
        return jnp.cos(a), jnp.sin(a)

    ca, sa = ang(ROPE_A)
    z32 = jnp.zeros_like(ca)
    z64 = jnp.zeros((SEQ, 64), F32)
    tab_c = jnp.concatenate([ca, ca, z64], axis=-1)
    tab_s1 = jnp.concatenate([-sa, z32, z64], axis=-1)
    tab_s2 = jnp.concatenate([z32, sa, z64], axis=-1)
    cb, sb = ang(HD_B)
    tab_cb = jnp.concatenate([cb, cb], axis=-1)
    tab_sb = jnp.concatenate([-sb, sb], axis=-1)
    ones = jnp.ones((TM_PROJ, 128), F32)
    zeros = jnp.zeros((TM_PROJ, 128), F32)
    return tuple(jnp.concatenate([tab, ident], axis=0)
                 for tab, ident in ((tab_c, ones), (tab_s1, zeros), (tab_s2, zeros),
                                    (tab_cb, ones), (tab_sb, zeros)))


def _dft_mats(n):
    j = jnp.arange(n, dtype=jnp.int32)
    a = ((j[:, None] * j[None, :]) % n).astype(F32) * (2.0 * math.pi / n)
    return jnp.cos(a), jnp.sin(a)


def _dft4_tables(n):
    nq = n // 4
    c_q, s_q = _dft_mats(nq)
    k = jnp.arange(nq, dtype=jnp.int32)[:, None]
    ang = (k * jnp.arange(1, 4, dtype=jnp.int32)[None, :]).astype(F32) * (2.0 * math.pi / n)
    tw = jnp.stack([jnp.cos(ang), jnp.sin(ang)], axis=-1).reshape(nq, 6)
    return c_q.astype(BF16), s_q.astype(BF16), _pad_to(tw, 1, 8)


def _pad_to(a, axis, size):
    pad = [(0, 0)] * a.ndim
    pad[axis] = (0, size - a.shape[axis])
    return jnp.pad(a, pad)


def _prep_weights(p):
    w_uq = p["w_uq"].reshape(DEPTH, Q_LORA, HEADS, QK_A)
    w_uq = _pad_to(_pad_to(w_uq, 3, QK_A_PAD), 1, Q_LORA_PAD).reshape(DEPTH, Q_LORA_PAD, HEADS * QK_A_PAD)
    w_ukv = p["w_ukv"].reshape(DEPTH, KV_LORA, HEADS, NOPE + V_A)
    w_ukv = jnp.concatenate([w_ukv[..., :NOPE].reshape(DEPTH, KV_LORA, HEADS * NOPE),
                             w_ukv[..., NOPE:].reshape(DEPTH, KV_LORA, HEADS * V_A)], axis=2)
    g_ka = p["g_ka"]

    return {
        "w_in": p["w_in"].astype(BF16),
        "g_cq": _pad_to(p["g_cq"], 1, Q_LORA_PAD)[:, None, :],
        "w_uq": w_uq.astype(BF16),
        "g_ckv": p["g_ckv"][:, None, :],
        "w_ukv": w_ukv.astype(BF16),
        "g_qa": _pad_to(p["g_qa"], 1, QK_A_PAD)[:, None, :],
        "g_ka_nope": g_ka[:, None, :NOPE],
        "g_ka_pe": _pad_to(g_ka[:, NOPE:], 1, 128)[:, None, :],
        "g_qb": p["g_qb"][:, None, :],
        "g_kb": p["g_kb"][:, None, :],
        "conv_w": p["conv_w"],
        "conv_b": p["conv_b"][:, None, :],
        "w_gate": p["w_gate"].astype(BF16),
        "b_gate": p["b_gate"][:, :, None, :],
        "w_branch": p["w_branch"].astype(BF16),
        "w_o": p["w_o"].astype(BF16),
        "norm_ffn1": p["norm_ffn1"][:, None, :],
        "norm_mix": p["norm_mix"][:, None, :],
        "norm_ffn2": p["norm_ffn2"][:, None, :],
        "ffn1_wi": p["ffn1_wi"].astype(BF16), "ffn1_wo": p["ffn1_wo"].astype(BF16),
        "ffn2_wi": p["ffn2_wi"].astype(BF16), "ffn2_wo": p["ffn2_wo"].astype(BF16),
    }


def kernel(x, c, ctx, c_ctx, w_ada, b_ada, norm_ffn1, ffn1_wi, ffn1_wo, norm_mix, w_in, g_cq, w_uq, g_ckv, w_ukv, g_qa, g_ka, g_qb, g_kb, conv_w, conv_b, w_branch, w_gate, b_gate, w_o, norm_ffn2, ffn2_wi, ffn2_wo):
    p = dict(norm_ffn1=norm_ffn1, ffn1_wi=ffn1_wi, ffn1_wo=ffn1_wo, norm_mix=norm_mix, w_in=w_in,
             g_cq=g_cq, w_uq=w_uq, g_ckv=g_ckv, w_ukv=w_ukv, g_qa=g_qa, g_ka=g_ka, g_qb=g_qb,
             g_kb=g_kb, conv_w=conv_w, conv_b=conv_b, w_branch=w_branch, w_gate=w_gate,
             b_gate=b_gate, w_o=w_o, norm_ffn2=norm_ffn2, ffn2_wi=ffn2_wi, ffn2_wo=ffn2_wo)

    cond8 = jnp.concatenate([c, c_ctx[None, :], jnp.zeros((8 - BATCH - 1, D), F32)], axis=0)
    mod = _ada(cond8, w_ada, b_ada).reshape(DEPTH, 8, N_MOD, 1, D)

    tabs = _rope_tables()
    cc, sc = _dft_mats(FGD)
    ccs = jnp.concatenate([cc, sc], axis=1).astype(BF16)
    dft_lat = _dft4_tables(SEQ)
    dft_ctx = _dft4_tables(CTX)

    lw = _prep_weights(p)
    h = jnp.concatenate([x.reshape(T_LAT, D), ctx.reshape(T_CTX, D)], axis=0)
    for l in range(DEPTH):
        last = l == DEPTH - 1
        h = _ffn(h, mod, l, 0, lw["norm_ffn1"], lw["ffn1_wi"], lw["ffn1_wo"], T_ALL)
        qa, ka, va, qb, kb, vb, cbo, u, pc, ps = _proj(h, mod, l, lw, tabs, ccs)
        oa, oa_c = _attention(qa, ka, va, QK_A_PAD, V_A, 1, not last)
        ob, ob_c = _attention(qb, kb, vb, HD_B, HD_B, GROUP_B, not last)
        of, of_c = _fourier(pc, ps, dft_lat, dft_ctx, not last)
        n_rows = T_LAT if last else T_ALL
        h = _merge(h, mod, l, (oa, ob, of), None if last else (oa_c, ob_c, of_c), cbo, u, lw, n_rows)
        h = _ffn(h, mod, l, 6, lw["norm_ffn2"], lw["ffn2_wi"], lw["ffn2_wo"], n_rows)
    return h.reshape(BATCH, SEQ, D)
```

```python
import functools
import math

import jax
import jax.numpy as jnp
from jax import lax
from jax.experimental import pallas as pl
from jax.experimental.pallas import tpu as pltpu

F32 = jnp.float32
BF16 = jnp.bfloat16

D = 2048
BATCH = 4
SEQ = 4096
DEPTH = 4
GRID_W = 64
CTX = 256
N_MOD = 9
F_FF = 5504
HEADS = 4
Q_LORA = 448
Q_LORA_PAD = 512
KV_LORA = 128
NOPE = 128
ROPE_A = 64
QK_A = NOPE + ROPE_A
QK_A_PAD = 256
V_A = 128
HD_B = 128
KVH_B = 2
GROUP_B = 2
CONV_W = 512
FG = 4
FGD = 128
BR_W = 512
N_BR = 4
THETA = 10000.0
EPS = 1e-6
LOG2_E = math.log2(math.e)

T_LAT = BATCH * SEQ
T_CTX = BATCH * CTX
T_ALL = T_LAT + T_CTX

W_IN = 3712
OFF_A = 0
OFF_B = 640
OFF_C = 1664
OFF_F = 3200

LANE = 128
VMEM_PHYS_V7X = 64 * 1024 * 1024
VMEM_LIMIT = 60 * 1024 * 1024

TM_FFN = 1024
TF_FFN = 512
TM_PROJ = 512
TM_MERGE = 512
TN_MERGE = 512
TQ_ATT = 1024
KC_ATT = 512
TK_DFT = 512
TN_ADA = 1024


def _cparams(sem):
    return pltpu.CompilerParams(dimension_semantics=sem, vmem_limit_bytes=VMEM_LIMIT)


def _mod_row(i, tiles_per_batch):
    return jnp.minimum(i // tiles_per_batch, BATCH)


def _row_rsqrt(x):
    return lax.rsqrt(jnp.mean(x * x, axis=-1, keepdims=True) + EPS)


def _scale_mod(x, r, gain, shift, scale):
    return (x * r * gain) * (1.0 + scale) + shift


def _norm_mod(x, gain, shift, scale):
    return _scale_mod(x, _row_rsqrt(x), gain, shift, scale)


def _silu(x):
    return x * jax.nn.sigmoid(x)


def _ada_kernel(cond_ref, w_ref, b_ref, o_ref):
    s = _silu(cond_ref[...]).astype(BF16)
    o_ref[...] = jnp.dot(s, w_ref[...].astype(BF16), preferred_element_type=F32) + b_ref[...]


def _ada(cond8, w_ada, b_ada):
    n = N_MOD * D
    return pl.pallas_call(
        _ada_kernel,
        grid=(DEPTH, n // TN_ADA),
        in_specs=[
            pl.BlockSpec((8, D), lambda l, j: (0, 0)),
            pl.BlockSpec((None, D, TN_ADA), lambda l, j: (l, 0, j)),
            pl.BlockSpec((None, 1, TN_ADA), lambda l, j: (l, 0, j)),
        ],
        out_specs=pl.BlockSpec((None, 8, TN_ADA), lambda l, j: (l, 0, j)),
        out_shape=jax.ShapeDtypeStruct((DEPTH, 8, n), F32),
        compiler_params=_cparams(("arbitrary", "arbitrary")),
        name="ada",
    )(cond8, w_ada, b_ada.reshape(DEPTH, 1, n))


def _mod_spec(layer, j, tiles_per_batch):
    return pl.BlockSpec((None, None, None, 1, D),
                        lambda i, *_: (layer, _mod_row(i, tiles_per_batch), j, 0, 0))


def _ffn_kernel(overlap, cache_xn, x_ref, gain_ref, sh_ref, sc_ref, gt_ref, wg_ref, wu_ref, wo_ref, o_ref,
                r_ref, *xn_refs):
    f = pl.program_id(1)
    last = pl.num_programs(1) - 1

    def chunk(r, skip=0, first=False):
        if cache_xn and not first:
            xn = xn_refs[0][...]
        else:
            xn = _scale_mod(x_ref[...], r, gain_ref[...], sh_ref[...], sc_ref[...]).astype(BF16)
            if cache_xn:
                xn_refs[0][...] = xn
        g = jnp.dot(xn, wg_ref[:, skip:], preferred_element_type=F32)
        u = jnp.dot(xn, wu_ref[:, skip:], preferred_element_type=F32)
        a = (_silu(g) * u).astype(BF16)
        return jnp.dot(a, wo_ref[skip:, :], preferred_element_type=F32)

    @pl.when(f == 0)
    def _():
        r = _row_rsqrt(x_ref[...])
        r_ref[...] = r
        o_ref[...] = chunk(r, first=True)

    @pl.when(jnp.logical_and(f > 0, f < last))
    def _():
        o_ref[...] += chunk(r_ref[...])

    @pl.when(f == last)
    def _():
        o_ref[...] = x_ref[...] + (0.5 * gt_ref[...]) * (o_ref[...] + chunk(r_ref[...], overlap))


def _ffn(h, mod, layer, j0, gain, wi, wo, n_rows, cache_xn=False):
    tm, tf = TM_FFN, TF_FFN
    tpb = SEQ // tm
    nf = pl.cdiv(F_FF, tf)

    def hid(f, base=0):
        return (base // LANE + jnp.minimum(f * (tf // LANE), (F_FF - tf) // LANE)) * LANE

    return pl.pallas_call(
        functools.partial(_ffn_kernel, nf * tf - F_FF, cache_xn),
        grid=(n_rows // tm, nf),
        in_specs=[
            pl.BlockSpec((tm, D), lambda i, f: (i, 0)),
            pl.BlockSpec((None, 1, D), lambda i, f: (layer, 0, 0)),
            _mod_spec(layer, j0, tpb),
            _mod_spec(layer, j0 + 1, tpb),
            _mod_spec(layer, j0 + 2, tpb),
            pl.BlockSpec((None, pl.Element(D), pl.Element(tf)), lambda i, f: (layer, 0, hid(f))),
            pl.BlockSpec((None, pl.Element(D), pl.Element(tf)), lambda i, f: (layer, 0, hid(f, F_FF))),
            pl.BlockSpec((None, pl.Element(tf), pl.Element(D)), lambda i, f: (layer, hid(f), 0)),
        ],
        out_specs=pl.BlockSpec((tm, D), lambda i, f: (i, 0)),
        out_shape=jax.ShapeDtypeStruct((n_rows, D), F32),
        scratch_shapes=[pltpu.VMEM((tm, 1), F32)] + ([pltpu.VMEM((tm, D), BF16)] if cache_xn else []),
        compiler_params=_cparams(("arbitrary", "arbitrary")),
        name="ffn",
    )(h, gain, mod, mod, mod, wi, wi, wo)


def _rope_a(v, c, s1, s2):
    return v * c + pltpu.roll(v, 96, 1) * s1 + pltpu.roll(v, 32, 1) * s2


def _rope_b(v, c, s):
    return v * c + pltpu.roll(v, 64, 1) * s


def _proj_kernel(h_ref, gain_ref, sh_ref, sc_ref, w_ref, gcq_ref, wuq_ref, gckv_ref, wukv_ref,
                 gqa_ref, gkan_ref, gkap_ref, gqb_ref, gkb_ref, ca_ref, s1a_ref, s2a_ref,
                 cb_ref, sb_ref, ccs_ref,
                 qa_ref, ka_ref, va_ref, qb_ref, kb_ref, vb_ref, cbo_ref, u_ref, pc_ref, ps_ref,
                 pp_ref):
    xn = _norm_mod(h_ref[...], gain_ref[...], sh_ref[...], sc_ref[...]).astype(BF16)
    ca, s1a, s2a = ca_ref[...], s1a_ref[...], s2a_ref[...]
    cb, sb = cb_ref[...], sb_ref[...]
    scale_a = QK_A ** -0.5 * LOG2_E
    scale_b = HD_B ** -0.5 * LOG2_E

    za = jnp.dot(xn, w_ref[:, OFF_A:OFF_B], preferred_element_type=F32)
    zb = jnp.dot(xn, w_ref[:, OFF_B:OFF_C], preferred_element_type=F32)

    hi = lax.broadcasted_iota(jnp.int32, (1, LANE), 1) >= LANE // 2
    s3, s4 = za[:, 3 * LANE:4 * LANE], za[:, 4 * LANE:]
    aq = jnp.concatenate([za[:, :3 * LANE], jnp.where(hi, 0.0, s3)], axis=1)
    akv = pltpu.roll(jnp.where(hi, s3, s4), LANE // 2, 1)
    ape = jnp.where(hi, 0.0, pltpu.roll(s4, LANE // 2, 1))
    r = lax.rsqrt(jnp.sum(aq * aq, axis=-1, keepdims=True) * (1.0 / Q_LORA) + EPS)
    cq = (aq * r * gcq_ref[...]).astype(BF16)
    q = jnp.dot(cq, wuq_ref[...], preferred_element_type=F32)
    gqa = gqa_ref[...]
    for hd in range(HEADS):
        lo = hd * QK_A_PAD
        qh = q[:, lo:lo + QK_A_PAD]
        r = lax.rsqrt(jnp.sum(qh * qh, axis=-1, keepdims=True) * (1.0 / QK_A) + EPS)
        qh = qh * r * gqa
        qa_ref[:, lo:lo + NOPE] = (qh[:, :NOPE] * scale_a).astype(BF16)
        qa_ref[:, lo + NOPE:lo + QK_A_PAD] = (_rope_a(qh[:, NOPE:], ca, s1a, s2a) * scale_a).astype(BF16)
    r = lax.rsqrt(jnp.mean(akv * akv, axis=-1, keepdims=True) + EPS)
    ckv = (akv * r * gckv_ref[...]).astype(BF16)
    kv = jnp.dot(ckv, wukv_ref[...], preferred_element_type=F32)
    ss_pe = jnp.sum(ape * ape, axis=-1, keepdims=True)
    ones_blk = jnp.ones((h_ref.shape[0], LANE), BF16)
    gkan, gkap = gkan_ref[...], gkap_ref[...]
    for hd in range(HEADS):
        kn = kv[:, hd * NOPE:(hd + 1) * NOPE]
        r = lax.rsqrt((jnp.sum(kn * kn, axis=-1, keepdims=True) + ss_pe) * (1.0 / QK_A) + EPS)
        lo = hd * QK_A_PAD
        ka_ref[:, lo:lo + NOPE] = (kn * r * gkan).astype(BF16)
        ka_ref[:, lo + NOPE:lo + QK_A_PAD] = _rope_a(ape * r * gkap, ca, s1a, s2a).astype(BF16)
        va_ref[:, 2 * hd * V_A:(2 * hd + 1) * V_A] = kv[:, HEADS * NOPE + hd * V_A:HEADS * NOPE + (hd + 1) * V_A].astype(BF16)
        va_ref[:, (2 * hd + 1) * V_A:(2 * hd + 2) * V_A] = ones_blk

    zc = jnp.dot(xn, w_ref[:, OFF_C:OFF_F], preferred_element_type=F32)
    gqb, gkb = gqb_ref[...], gkb_ref[...]
    for hd in range(HEADS):
        qh = zb[:, hd * HD_B:(hd + 1) * HD_B]
        r = lax.rsqrt(jnp.mean(qh * qh, axis=-1, keepdims=True) + EPS)
        qb_ref[:, hd * HD_B:(hd + 1) * HD_B] = (_rope_b(qh * r * gqb, cb, sb) * scale_b).astype(BF16)
    for hd in range(KVH_B):
        lo = HEADS * HD_B + hd * HD_B
        kh = zb[:, lo:lo + HD_B]
        r = lax.rsqrt(jnp.mean(kh * kh, axis=-1, keepdims=True) + EPS)
        kb_ref[:, hd * HD_B:(hd + 1) * HD_B] = _rope_b(kh * r * gkb, cb, sb).astype(BF16)
    for hd in range(KVH_B):
        lo = (HEADS + KVH_B + hd) * HD_B
        vb_ref[:, 2 * hd * HD_B:(2 * hd + 1) * HD_B] = zb[:, lo:lo + HD_B].astype(BF16)
        vb_ref[:, (2 * hd + 1) * HD_B:(2 * hd + 2) * HD_B] = ones_blk

    zf = jnp.dot(xn, w_ref[:, OFF_F:], preferred_element_type=F32).astype(BF16)
    cbo_ref[...] = zc[:, :CONV_W]
    u_ref[...] = zc[:, CONV_W:2 * CONV_W] * zc[:, 2 * CONV_W:]

    ccs = ccs_ref[...]
    tm = h_ref.shape[0]
    w = FG * FGD
    for g in range(FG):
        pp = jnp.dot(zf[:, g * FGD:(g + 1) * FGD], ccs, preferred_element_type=F32)
        pp_ref[g] = pp[:, :FGD]
        pp_ref[FG + g] = pp[:, FGD:]
    for r in range(4):
        for g in range(FG):
            lo = r * w + g * FGD
            pc_ref[:, lo:lo + FGD] = pp_ref[g, pl.ds(r, tm // 4, stride=4), :].astype(BF16)
            ps_ref[:, lo:lo + FGD] = pp_ref[FG + g, pl.ds(r, tm // 4, stride=4), :].astype(BF16)


def _const_spec(shape, layer=None):
    nd = len(shape)
    if layer is None:
        return pl.BlockSpec(shape, lambda i, *_: (0,) * nd, pipeline_mode=pl.Buffered(1))
    return pl.BlockSpec((None,) + shape, lambda i, *_: (layer,) + (0,) * nd, pipeline_mode=pl.Buffered(1))


def _proj(h, mod, layer, lw, tabs, ccs):
    tm = TM_PROJ
    tpb = SEQ // tm
    n_lat = T_LAT // tm

    def tab_spec():
        return pl.BlockSpec((tm, 128), lambda i: (jnp.where(i < n_lat, i % tpb, tpb), 0))

    def row_spec(w):
        return pl.BlockSpec((tm, w), lambda i: (i, 0))

    widths = (HEADS * QK_A_PAD, HEADS * QK_A_PAD, HEADS * 2 * V_A, HEADS * HD_B, KVH_B * HD_B,
              KVH_B * 2 * HD_B, CONV_W, CONV_W)
    dtypes = (BF16, BF16, BF16, BF16, BF16, BF16, F32, F32)
    dft_spec = pl.BlockSpec((tm // 4, 4 * FG * FGD), lambda i: (i, 0))
    dft_shape = jax.ShapeDtypeStruct((T_ALL // 4, 4 * FG * FGD), BF16)
    return pl.pallas_call(
        _proj_kernel,
        grid=(T_ALL // tm,),
        in_specs=[
            row_spec(D),
            _const_spec((1, D), layer),
            _mod_spec(layer, 3, tpb),
            _mod_spec(layer, 4, tpb),
            _const_spec((D, W_IN), layer),
            _const_spec((1, Q_LORA_PAD), layer),
            _const_spec((Q_LORA_PAD, HEADS * QK_A_PAD), layer),
            _const_spec((1, KV_LORA), layer),
            _const_spec((KV_LORA, HEADS * (NOPE + V_A)), layer),
            _const_spec((1, QK_A_PAD), layer),
            _const_spec((1, NOPE), layer),
            _const_spec((1, 128), layer),
            _const_spec((1, HD_B), layer),
            _const_spec((1, HD_B), layer),
            tab_spec(), tab_spec(), tab_spec(), tab_spec(), tab_spec(),
            _const_spec((FGD, 2 * FGD)),
        ],
        out_specs=[row_spec(w) for w in widths] + [dft_spec, dft_spec],
        out_shape=[jax.ShapeDtypeStruct((T_ALL, w), dt) for w, dt in zip(widths, dtypes)]
        + [dft_shape, dft_shape],
        scratch_shapes=[pltpu.VMEM((2 * FG, tm, FGD), F32)],
        compiler_params=_cparams(("arbitrary",)),
        name="proj",
    )(h, lw["norm_mix"], mod, mod, lw["w_in"], lw["g_cq"], lw["w_uq"], lw["g_ckv"], lw["w_ukv"],
      lw["g_qa"], lw["g_ka_nope"], lw["g_ka_pe"], lw["g_qb"], lw["g_kb"], *tabs, ccs)


_QK_DIMS = (((1,), (1,)), ((), ()))


def _att_lat_kernel(q_ref, k_ref, kc_ref, v_ref, vc_ref, o_ref):
    q = q_ref[...]
    chunks = [(k_ref, v_ref, j * KC_ATT, KC_ATT) for j in range(SEQ // KC_ATT)] + [(kc_ref, vc_ref, 0, CTX)]

    def scores(chunk):
        kr, _, lo, n = chunk
        return lax.dot_general(q, kr[lo:lo + n, :], _QK_DIMS, preferred_element_type=F32)

    dv = o_ref.shape[-1]
    s = scores(chunks[0])
    m = acc = None
    for idx, (_, vr, lo, n) in enumerate(chunks):
        s_next = scores(chunks[idx + 1]) if idx + 1 < len(chunks) else None
        m_chunk = jnp.max(s, axis=-1, keepdims=True)
        m_new = m_chunk if m is None else jnp.maximum(m, m_chunk)
        p = _exp2_bf16(s - m_new)
        pv = jnp.dot(p, vr[lo:lo + n, :], preferred_element_type=F32)
        acc = pv if m is None else jnp.exp2(m - m_new) * acc + pv
        m, s = m_new, s_next
    o_ref[...] = (acc[:, :dv] / acc[:, dv:]).astype(o_ref.dtype)


def _exp2_bf16(x):
    return jnp.exp2(x.astype(BF16))


def _att_ctx_kernel(group, q_ref, k_ref, v_ref, o_ref):
    dv = o_ref.shape[-1] // HEADS
    dq = q_ref.shape[-1] // HEADS
    for hd in range(HEADS):
        kv = hd // group
        s = lax.dot_general(q_ref[:, hd * dq:(hd + 1) * dq], k_ref[:, kv * dq:(kv + 1) * dq], _QK_DIMS,
                            preferred_element_type=F32)
        p = _exp2_bf16(s - jnp.max(s, axis=-1, keepdims=True))
        o = jnp.dot(p, v_ref[:, 2 * kv * dv:2 * (kv + 1) * dv], preferred_element_type=F32)
        o_ref[:, hd * dv:(hd + 1) * dv] = (o[:, :dv] / o[:, dv:]).astype(o_ref.dtype)


def _attention(q, k, v, dq, dv, group, with_ctx):
    tq = TQ_ATT
    nq = SEQ // tq
    ctx0 = T_LAT // CTX
    out = pl.pallas_call(
        _att_lat_kernel,
        grid=(BATCH, HEADS, nq),
        in_specs=[
            pl.BlockSpec((tq, dq), lambda b, h, i: (b * nq + i, h)),
            pl.BlockSpec((SEQ, dq), lambda b, h, i: (b, h // group)),
            pl.BlockSpec((CTX, dq), lambda b, h, i: (ctx0 + b, h // group)),
            pl.BlockSpec((SEQ, 2 * dv), lambda b, h, i: (b, h // group)),
            pl.BlockSpec((CTX, 2 * dv), lambda b, h, i: (ctx0 + b, h // group)),
        ],
        out_specs=pl.BlockSpec((tq, dv), lambda b, h, i: (b * nq + i, h)),
        out_shape=jax.ShapeDtypeStruct((T_LAT, HEADS * dv), BF16),
        compiler_params=_cparams(("arbitrary", "arbitrary", "arbitrary")),
        name="att_lat",
    )(q, k, k, v, v)
    if not with_ctx:
        return out, None
    kvh = HEADS // group
    out_ctx = pl.pallas_call(
        functools.partial(_att_ctx_kernel, group),
        grid=(BATCH,),
        in_specs=[
            pl.BlockSpec((CTX, HEADS * dq), lambda b: (ctx0 + b, 0)),
            pl.BlockSpec((CTX, kvh * dq), lambda b: (ctx0 + b, 0)),
            pl.BlockSpec((CTX, kvh * 2 * dv), lambda b: (ctx0 + b, 0)),
        ],
        out_specs=pl.BlockSpec((CTX, HEADS * dv), lambda b: (b, 0)),
        out_shape=jax.ShapeDtypeStruct((T_CTX, HEADS * dv), BF16),
        compiler_params=_cparams(("arbitrary",)),
        name="att_ctx",
    )(q, k, v)
    return out, out_ctx


def _dft4_kernel(scale, c_ref, s_ref, tw_ref, pc_ref, ps_ref, o_ref):
    w = o_ref.shape[-1]
    c, s = c_ref[...], s_ref[...]
    pc, ps = pc_ref[...], ps_ref[...]
    ur = jnp.dot(c, pc, preferred_element_type=F32) - jnp.dot(s, ps, preferred_element_type=F32)
    ui = -(jnp.dot(c, ps, preferred_element_type=F32) + jnp.dot(s, pc, preferred_element_type=F32))
    a = [ur[:, :w]]
    b = [None]
    for r in range(1, 4):
        u_r, v_r = ur[:, r * w:(r + 1) * w], ui[:, r * w:(r + 1) * w]
        cr, sr = tw_ref[:, 2 * r - 2:2 * r - 1], tw_ref[:, 2 * r - 1:2 * r]
        a.append(u_r * cr + v_r * sr)
        b.append(None if r == 2 else v_r * cr - u_r * sr)
    o_ref[0] = ((a[0] + a[1] + a[2] + a[3]) * scale).astype(o_ref.dtype)
    o_ref[1] = ((a[0] + b[1] - a[2] - b[3]) * scale).astype(o_ref.dtype)
    o_ref[2] = ((a[0] - a[1] + a[2] - a[3]) * scale).astype(o_ref.dtype)
    o_ref[3] = ((a[0] - b[1] - a[2] + b[3]) * scale).astype(o_ref.dtype)


def _dft4(pc4, ps4, tables, n, tk, first_block, name):
    w = FG * FGD
    nq = n // 4
    c_q, s_q, tw = tables
    return pl.pallas_call(
        functools.partial(_dft4_kernel, (n * FGD) ** -0.5),
        grid=(BATCH, nq // tk),
        in_specs=[
            pl.BlockSpec((tk, nq), lambda b, k: (k, 0)),
            pl.BlockSpec((tk, nq), lambda b, k: (k, 0)),
            pl.BlockSpec((tk, 8), lambda b, k: (k, 0)),
            pl.BlockSpec((nq, 4 * w), lambda b, k: (first_block + b, 0)),
            pl.BlockSpec((nq, 4 * w), lambda b, k: (first_block + b, 0)),
        ],
        out_specs=pl.BlockSpec((None, 4, tk, w), lambda b, k: (b, 0, k, 0)),
        out_shape=jax.ShapeDtypeStruct((BATCH, 4, nq, w), BF16),
        compiler_params=_cparams(("arbitrary", "arbitrary")),
        name=name,
    )(c_q, s_q, tw, pc4, ps4).reshape(BATCH * n, w)


def _fourier(pc4, ps4, dft_lat, dft_ctx, with_ctx):
    out = _dft4(pc4, ps4, dft_lat, SEQ, TK_DFT, 0, "dft_lat")
    if not with_ctx:
        return out, None
    return out, _dft4(pc4, ps4, dft_ctx, CTX, CTX // 4, T_LAT // CTX, "dft_ctx")


def _merge_kernel(n_lat_tiles, with_ctx, h_ref, gain_ref, sh_ref, sc_ref, gt_ref, *refs):
    lat_refs, refs = refs[:3], refs[3:]
    if with_ctx:
        ctx_refs, refs = refs[:3], refs[3:]
    (cbo_ref, u_ref, up_ref, un_ref, cw_ref, cbias_ref, wg_ref, bg_ref, wb_ref, wo_ref,
     o_ref, xn_ref, conv_ref) = refs
    i = pl.program_id(0)
    n = pl.program_id(1)
    last = pl.num_programs(1) - 1
    tm = h_ref.shape[0]
    is_lat = i < n_lat_tiles

    def conv_branch():
        u = u_ref[...]
        seq_len = jnp.where(is_lat, SEQ, CTX)
        ridx = lax.broadcasted_iota(jnp.int32, (tm, 1), 0)
        pos = (i * tm + ridx) & (seq_len - 1)
        prev = jnp.where(ridx == 0, up_ref[7:8, :], pltpu.roll(u, 1, 0))
        prev = jnp.where(pos == 0, 0.0, prev)
        nxt = jnp.where(ridx == tm - 1, un_ref[0:1, :], pltpu.roll(u, tm - 1, 0))
        nxt = jnp.where(pos == seq_len - 1, 0.0, nxt)
        y = prev * cw_ref[0:1, :] + u * cw_ref[1:2, :] + nxt * cw_ref[2:3, :] + cbias_ref[...]
        return (cbo_ref[...] * y).astype(BF16)

    def chunk(first=False):
        if first:
            xn = _norm_mod(h_ref[...], gain_ref[...], sh_ref[...], sc_ref[...]).astype(BF16)
            xn_ref[...] = xn
            conv = conv_branch()
            conv_ref[...] = conv
        else:
            xn = xn_ref[...]
            conv = conv_ref[...]

        if with_ctx:
            oa, ob, of = (jnp.where(is_lat, lr[...], cr[...]) for lr, cr in zip(lat_refs, ctx_refs))
        else:
            oa, ob, of = (lr[...] for lr in lat_refs)

        merged = None
        for k, br in enumerate((oa, ob, conv, of)):
            gate = jax.nn.sigmoid(jnp.dot(xn, wg_ref[k], preferred_element_type=F32) + bg_ref[k])
            term = gate * jnp.dot(br, wb_ref[k], preferred_element_type=F32)
            merged = term if merged is None else merged + term
        return jnp.dot(merged.astype(BF16), wo_ref[...], preferred_element_type=F32)

    @pl.when(n == 0)
    def _():
        o_ref[...] = chunk(first=True)

    @pl.when(jnp.logical_and(n > 0, n < last))
    def _():
        o_ref[...] += chunk()

    @pl.when(n == last)
    def _():
        o_ref[...] = h_ref[...] + gt_ref[...] * (o_ref[...] + chunk())


def _merge(h, mod, layer, br_lat, br_ctx, cbo, u, lw, n_rows):
    tm, tn = TM_MERGE, TN_MERGE
    tpb = SEQ // tm
    n_lat = T_LAT // tm
    with_ctx = br_ctx is not None
    last_blk = T_ALL // 8 - 1
    br_specs = [pl.BlockSpec((tm, BR_W), lambda i, n: (jnp.minimum(i, n_lat - 1), 0))] * 3
    if with_ctx:
        br_specs += [pl.BlockSpec((tm, BR_W), lambda i, n: (jnp.maximum(i - n_lat, 0), 0))] * 3

    def row_spec(w):
        return pl.BlockSpec((tm, w), lambda i, n: (i, 0))

    return pl.pallas_call(
        functools.partial(_merge_kernel, n_lat, with_ctx),
        grid=(n_rows // tm, D // tn),
        in_specs=[
            row_spec(D),
            pl.BlockSpec((None, 1, D), lambda i, n: (layer, 0, 0)),
            _mod_spec(layer, 3, tpb),
            _mod_spec(layer, 4, tpb),
            _mod_spec(layer, 5, tpb),
            *br_specs,
            row_spec(CONV_W), row_spec(CONV_W),
            pl.BlockSpec((8, CONV_W), lambda i, n: (jnp.maximum(i * (tm // 8) - 1, 0), 0)),
            pl.BlockSpec((8, CONV_W), lambda i, n: (jnp.minimum((i + 1) * (tm // 8), last_blk), 0)),
            pl.BlockSpec((None, 3, CONV_W), lambda i, n: (layer, 0, 0)),
            pl.BlockSpec((None, 1, CONV_W), lambda i, n: (layer, 0, 0)),
            pl.BlockSpec((None, N_BR, D, tn), lambda i, n: (layer, 0, 0, n)),
            pl.BlockSpec((None, N_BR, 1, tn), lambda i, n: (layer, 0, 0, n)),
            pl.BlockSpec((None, N_BR, BR_W, tn), lambda i, n: (layer, 0, 0, n)),
            pl.BlockSpec((None, tn, D), lambda i, n: (layer, n, 0)),
        ],
        out_specs=row_spec(D),
        out_shape=jax.ShapeDtypeStruct((n_rows, D), F32),
        scratch_shapes=[pltpu.VMEM((tm, D), BF16), pltpu.VMEM((tm, CONV_W), BF16)],
        compiler_params=_cparams(("arbitrary", "arbitrary")),
        name="merge",
    )(h, lw["norm_mix"], mod, mod, mod, *br_lat, *(br_ctx or ()), cbo, u, u, u, lw["conv_w"],
      lw["conv_b"], lw["w_gate"], lw["b_gate"], lw["w_branch"], lw["w_o"])


def _rope_tables():
    t = jnp.arange(SEQ, dtype=jnp.int32)
    pos_row = (t // GRID_W).astype(F32)
    pos_col = (t % GRID_W).astype(F32)

    def ang(rot_dim):
        n = rot_dim // 4
        inv_freq = THETA ** (-jnp.arange(n, dtype=F32) / n)
        a = jnp.concatenate([pos_row[:, None] * inv_freq, pos_col[:, None] * inv_freq], axis=-1)
        return jnp.cos(a), jnp.sin(a)

    ca, sa = ang(ROPE_A)
    z32 = jnp.zeros_like(ca)
    z64 = jnp.zeros((SEQ, 64), F32)
    tab_c = jnp.concatenate([ca, ca, z64], axis=-1)
    tab_s1 = jnp.concatenate([-sa, z32, z64], axis=-1)
    tab_s2 = jnp.concatenate([z32, sa, z64], axis=-1)
    cb, sb = ang(HD_B)
    tab_cb = jnp.concatenate([cb, cb], axis=-1)
    tab_sb = jnp.concatenate([-sb, sb], axis=-1)
    ones = jnp.ones((TM_PROJ, 128), F32)
    zeros = jnp.zeros((TM_PROJ, 128), F32)
    return tuple(jnp.concatenate([tab, ident], axis=0)
                 for tab, ident in ((tab_c, ones), (tab_s1, zeros), (tab_s2, zeros),
                                    (tab_cb, ones), (tab_sb, zeros)))


def _dft_mats(n):
    j = jnp.arange(n, dtype=jnp.int32)
    a = ((j[:, None] * j[None, :]) % n).astype(F32) * (2.0 * math.pi / n)
    return jnp.cos(a), jnp.sin(a)


def _dft4_tables(n):
    nq = n // 4
    c_q, s_q = _dft_mats(nq)
    k = jnp.arange(nq, dtype=jnp.int32)[:, None]
    ang = (k * jnp.arange(1, 4, dtype=jnp.int32)[None, :]).astype(F32) * (2.0 * math.pi / n)
    tw = jnp.stack([jnp.cos(ang), jnp.sin(ang)], axis=-1).reshape(nq, 6)
    return c_q.astype(BF16), s_q.astype(BF16), _pad_to(tw, 1, 8)


def _pad_to(a, axis, size):
    pad = [(0, 0)] * a.ndim
    pad[axis] = (0, size - a.shape[axis])
    return jnp.pad(a, pad)


def _prep_weights(p):
    w_uq = p["w_uq"].reshape(DEPTH, Q_LORA, HEADS, QK_A)
    w_uq = _pad_to(_pad_to(w_uq, 3, QK_A_PAD), 1, Q_LORA_PAD).reshape(DEPTH, Q_LORA_PAD, HEADS * QK_A_PAD)
    w_ukv = p["w_ukv"].reshape(DEPTH, KV_LORA, HEADS, NOPE + V_A)
    w_ukv = jnp.concatenate([w_ukv[..., :NOPE].reshape(DEPTH, KV_LORA, HEADS * NOPE),
                             w_ukv[..., NOPE:].reshape(DEPTH, KV_LORA, HEADS * V_A)], axis=2)
    g_ka = p["g_ka"]

    return {
        "w_in": p["w_in"].astype(BF16),
        "g_cq": _pad_to(p["g_cq"], 1, Q_LORA_PAD)[:, None, :],
        "w_uq": w_uq.astype(BF16),
        "g_ckv": p["g_ckv"][:, None, :],
        "w_ukv": w_ukv.astype(BF16),
        "g_qa": _pad_to(p["g_qa"], 1, QK_A_PAD)[:, None, :],
        "g_ka_nope": g_ka[:, None, :NOPE],
        "g_ka_pe": _pad_to(g_ka[:, NOPE:], 1, 128)[:, None, :],
        "g_qb": p["g_qb"][:, None, :],
        "g_kb": p["g_kb"][:, None, :],
        "conv_w": p["conv_w"],
        "conv_b": p["conv_b"][:, None, :],
        "w_gate": p["w_gate"].astype(BF16),
        "b_gate": p["b_gate"][:, :, None, :],
        "w_branch": p["w_branch"].astype(BF16),
        "w_o": p["w_o"].astype(BF16),
        "norm_ffn1": p["norm_ffn1"][:, None, :],
        "norm_mix": p["norm_mix"][:, None, :],
        "norm_ffn2": p["norm_ffn2"][:, None, :],
        "ffn1_wi": p["ffn1_wi"].astype(BF16), "ffn1_wo": p["ffn1_wo"].astype(BF16),
        "ffn2_wi": p["ffn2_wi"].astype(BF16), "ffn2_wo": p["ffn2_wo"].astype(BF16),
    }


def kernel(x, c, ctx, c_ctx, w_ada, b_ada, norm_ffn1, ffn1_wi, ffn1_wo, norm_mix, w_in, g_cq, w_uq, g_ckv, w_ukv, g_qa, g_ka, g_qb, g_kb, conv_w, conv_b, w_branch, w_gate, b_gate, w_o, norm_ffn2, ffn2_wi, ffn2_wo):
    p = dict(norm_ffn1=norm_ffn1, ffn1_wi=ffn1_wi, ffn1_wo=ffn1_wo, norm_mix=norm_mix, w_in=w_in,
             g_cq=g_cq, w_uq=w_uq, g_ckv=g_ckv, w_ukv=w_ukv, g_qa=g_qa, g_ka=g_ka, g_qb=g_qb,
             g_kb=g_kb, conv_w=conv_w, conv_b=conv_b, w_branch=w_branch, w_gate=w_gate,
             b_gate=b_gate, w_o=w_o, norm_ffn2=norm_ffn2, ffn2_wi=ffn2_wi, ffn2_wo=ffn2_wo)

    cond8 = jnp.concatenate([c, c_ctx[None, :], jnp.zeros((8 - BATCH - 1, D), F32)], axis=0)
    mod = _ada(cond8, w_ada, b_ada).reshape(DEPTH, 8, N_MOD, 1, D)

    tabs = _rope_tables()
    cc, sc = _dft_mats(FGD)
    ccs = jnp.concatenate([cc, sc], axis=1).astype(BF16)
    dft_lat = _dft4_tables(SEQ)
    dft_ctx = _dft4_tables(CTX)

    lw = _prep_weights(p)
    h = jnp.concatenate([x.reshape(T_LAT, D), ctx.reshape(T_CTX, D)], axis=0)
    for l in range(DEPTH):
        last = l == DEPTH - 1
        h = _ffn(h, mod, l, 0, lw["norm_ffn1"], lw["ffn1_wi"], lw["ffn1_wo"], T_ALL)
        qa, ka, va, qb, kb, vb, cbo, u, pc, ps = _proj(h, mod, l, lw, tabs, ccs)
        oa, oa_c = _attention(qa, ka, va, QK_A_PAD, V_A, 1, not last)
        ob, ob_c = _attention(qb, kb, vb, HD_B, HD_B, GROUP_B, not last)
        of, of_c = _fourier(pc, ps, dft_lat, dft_ctx, not last)
        n_rows = T_LAT if last else T_ALL
        h = _merge(h, mod, l, (oa, ob, of), None if last else (oa_c, ob_c, of_c), cbo, u, lw, n_rows)
        h = _ffn(h, mod, l, 6, lw["norm_ffn2"], lw["ffn2_wi"], lw["ffn2_wo"], n_rows, cache_xn=True)
    return h.reshape(BATCH, SEQ, D)
```

```python
import functools
import math

import jax
import jax.numpy as jnp
from jax import lax
from jax.experimental import pallas as pl
from jax.experimental.pallas import tpu as pltpu

F32 = jnp.float32
BF16 = jnp.bfloat16

D = 2048
BATCH = 4
SEQ = 4096
DEPTH = 4
GRID_W = 64
CTX = 256
N_MOD = 9
F_FF = 5504
HEADS = 4
Q_LORA = 448
Q_LORA_PAD = 512
KV_LORA = 128
NOPE = 128
ROPE_A = 64
QK_A = NOPE + ROPE_A
QK_A_PAD = 256
V_A = 128
HD_B = 128
KVH_B = 2
GROUP_B = 2
CONV_W = 512
FG = 4
FGD = 128
BR_W = 512
N_BR = 4
THETA = 10000.0
EPS = 1e-6
LOG2_E = math.log2(math.e)

T_LAT = BATCH * SEQ
T_CTX = BATCH * CTX
T_ALL = T_LAT + T_CTX

W_IN = 3712
OFF_A = 0
OFF_B = 640
OFF_C = 1664
OFF_F = 3200

LANE = 128
VMEM_PHYS_V7X = 64 * 1024 * 1024
VMEM_LIMIT = 60 * 1024 * 1024

TM_FFN = 1024
TF_FFN = 512
TM_PROJ = 512
TM_MERGE = 512
TN_MERGE = 512
TQ_ATT = 1024
KC_ATT = 512
TK_DFT = 512
TN_ADA = 1024


def _cparams(sem):
    return pltpu.CompilerParams(dimension_semantics=sem, vmem_limit_bytes=VMEM_LIMIT)


def _mod_row(i, tiles_per_batch):
    return jnp.minimum(i // tiles_per_batch, BATCH)


def _row_rsqrt(x):
    return lax.rsqrt(jnp.mean(x * x, axis=-1, keepdims=True) + EPS)


def _scale_mod(x, r, gain, shift, scale):
    return (x * r * gain) * (1.0 + scale) + shift


def _norm_mod(x, gain, shift, scale):
    return _scale_mod(x, _row_rsqrt(x), gain, shift, scale)


def _silu(x):
    return x * jax.nn.sigmoid(x)


def _ada_kernel(cond_ref, w_ref, b_ref, o_ref):
    s = _silu(cond_ref[...]).astype(BF16)
    o_ref[...] = jnp.dot(s, w_ref[...].astype(BF16), preferred_element_type=F32) + b_ref[...]


def _ada(cond8, w_ada, b_ada):
    n = N_MOD * D
    return pl.pallas_call(
        _ada_kernel,
        grid=(DEPTH, n // TN_ADA),
        in_specs=[
            pl.BlockSpec((8, D), lambda l, j: (0, 0)),
            pl.BlockSpec((None, D, TN_ADA), lambda l, j: (l, 0, j)),
            pl.BlockSpec((None, 1, TN_ADA), lambda l, j: (l, 0, j)),
        ],
        out_specs=pl.BlockSpec((None, 8, TN_ADA), lambda l, j: (l, 0, j)),
        out_shape=jax.ShapeDtypeStruct((DEPTH, 8, n), F32),
        compiler_params=_cparams(("arbitrary", "arbitrary")),
        name="ada",
    )(cond8, w_ada, b_ada.reshape(DEPTH, 1, n))


def _mod_spec(layer, j, tiles_per_batch):
    return pl.BlockSpec((None, None, None, 1, D),
                        lambda i, *_: (layer, _mod_row(i, tiles_per_batch), j, 0, 0))


def _ffn_kernel(overlap, x_ref, gain_ref, sh_ref, sc_ref, gt_ref, wg_ref, wu_ref, wo_ref, o_ref, xn_ref):
    f = pl.program_id(1)
    last = pl.num_programs(1) - 1

    def chunk(skip=0, first=False):
        if first:
            xn = _norm_mod(x_ref[...], gain_ref[...], sh_ref[...], sc_ref[...]).astype(BF16)
            xn_ref[...] = xn
        else:
            xn = xn_ref[...]
        g = jnp.dot(xn, wg_ref[:, skip:], preferred_element_type=F32)
        u = jnp.dot(xn, wu_ref[:, skip:], preferred_element_type=F32)
        a = (_silu(g) * u).astype(BF16)
        return jnp.dot(a, wo_ref[skip:, :], preferred_element_type=F32)

    @pl.when(f == 0)
    def _():
        o_ref[...] = chunk(first=True)

    @pl.when(jnp.logical_and(f > 0, f < last))
    def _():
        o_ref[...] += chunk()

    @pl.when(f == last)
    def _():
        o_ref[...] = x_ref[...] + (0.5 * gt_ref[...]) * (o_ref[...] + chunk(overlap))


def _ffn(h, mod, layer, j0, gain, wi, wo, n_rows):
    tm, tf = TM_FFN, TF_FFN
    tpb = SEQ // tm
    nf = pl.cdiv(F_FF, tf)

    def hid(f, base=0):
        return (base // LANE + jnp.minimum(f * (tf // LANE), (F_FF - tf) // LANE)) * LANE

    return pl.pallas_call(
        functools.partial(_ffn_kernel, nf * tf - F_FF),
        grid=(n_rows // tm, nf),
        in_specs=[
            pl.BlockSpec((tm, D), lambda i, f: (i, 0)),
            pl.BlockSpec((None, 1, D), lambda i, f: (layer, 0, 0)),
            _mod_spec(layer, j0, tpb),
            _mod_spec(layer, j0 + 1, tpb),
            _mod_spec(layer, j0 + 2, tpb),
            pl.BlockSpec((None, pl.Element(D), pl.Element(tf)), lambda i, f: (layer, 0, hid(f))),
            pl.BlockSpec((None, pl.Element(D), pl.Element(tf)), lambda i, f: (layer, 0, hid(f, F_FF))),
            pl.BlockSpec((None, pl.Element(tf), pl.Element(D)), lambda i, f: (layer, hid(f), 0)),
        ],
        out_specs=pl.BlockSpec((tm, D), lambda i, f: (i, 0)),
        out_shape=jax.ShapeDtypeStruct((n_rows, D), F32),
        scratch_shapes=[pltpu.VMEM((tm, D), BF16)],
        compiler_params=_cparams(("arbitrary", "arbitrary")),
        name="ffn",
    )(h, gain, mod, mod, mod, wi, wi, wo)


def _rope_a(v, c, s1, s2):
    return v * c + pltpu.roll(v, 96, 1) * s1 + pltpu.roll(v, 32, 1) * s2


def _rope_b(v, c, s):
    return v * c + pltpu.roll(v, 64, 1) * s


def _proj_kernel(h_ref, gain_ref, sh_ref, sc_ref, w_ref, gcq_ref, wuq_ref, gckv_ref, wukv_ref,
                 gqa_ref, gkan_ref, gkap_ref, gqb_ref, gkb_ref, ca_ref, s1a_ref, s2a_ref,
                 cb_ref, sb_ref, ccs_ref,
                 qa_ref, ka_ref, va_ref, qb_ref, kb_ref, vb_ref, cbo_ref, u_ref, pc_ref, ps_ref,
                 pp_ref):
    xn = _norm_mod(h_ref[...], gain_ref[...], sh_ref[...], sc_ref[...]).astype(BF16)
    ca, s1a, s2a = ca_ref[...], s1a_ref[...], s2a_ref[...]
    cb, sb = cb_ref[...], sb_ref[...]
    scale_a = QK_A ** -0.5 * LOG2_E
    scale_b = HD_B ** -0.5 * LOG2_E

    za = jnp.dot(xn, w_ref[:, OFF_A:OFF_B], preferred_element_type=F32)
    zb = jnp.dot(xn, w_ref[:, OFF_B:OFF_C], preferred_element_type=F32)

    hi = lax.broadcasted_iota(jnp.int32, (1, LANE), 1) >= LANE // 2
    s3, s4 = za[:, 3 * LANE:4 * LANE], za[:, 4 * LANE:]
    aq = jnp.concatenate([za[:, :3 * LANE], jnp.where(hi, 0.0, s3)], axis=1)
    akv = pltpu.roll(jnp.where(hi, s3, s4), LANE // 2, 1)
    ape = jnp.where(hi, 0.0, pltpu.roll(s4, LANE // 2, 1))
    r = lax.rsqrt(jnp.sum(aq * aq, axis=-1, keepdims=True) * (1.0 / Q_LORA) + EPS)
    cq = (aq * r * gcq_ref[...]).astype(BF16)
    q = jnp.dot(cq, wuq_ref[...], preferred_element_type=F32)
    gqa = gqa_ref[...]
    for hd in range(HEADS):
        lo = hd * QK_A_PAD
        qh = q[:, lo:lo + QK_A_PAD]
        r = lax.rsqrt(jnp.sum(qh * qh, axis=-1, keepdims=True) * (1.0 / QK_A) + EPS)
        qh = qh * r * gqa
        qa_ref[:, lo:lo + NOPE] = (qh[:, :NOPE] * scale_a).astype(BF16)
        qa_ref[:, lo + NOPE:lo + QK_A_PAD] = (_rope_a(qh[:, NOPE:], ca, s1a, s2a) * scale_a).astype(BF16)
    r = lax.rsqrt(jnp.mean(akv * akv, axis=-1, keepdims=True) + EPS)
    ckv = (akv * r * gckv_ref[...]).astype(BF16)
    kv = jnp.dot(ckv, wukv_ref[...], preferred_element_type=F32)
    ss_pe = jnp.sum(ape * ape, axis=-1, keepdims=True)
    ones_blk = jnp.ones((h_ref.shape[0], LANE), BF16)
    gkan, gkap = gkan_ref[...], gkap_ref[...]
    for hd in range(HEADS):
        kn = kv[:, hd * NOPE:(hd + 1) * NOPE]
        r = lax.rsqrt((jnp.sum(kn * kn, axis=-1, keepdims=True) + ss_pe) * (1.0 / QK_A) + EPS)
        lo = hd * QK_A_PAD
        ka_ref[:, lo:lo + NOPE] = (kn * r * gkan).astype(BF16)
        ka_ref[:, lo + NOPE:lo + QK_A_PAD] = _rope_a(ape * r * gkap, ca, s1a, s2a).astype(BF16)
        va_ref[:, 2 * hd * V_A:(2 * hd + 1) * V_A] = kv[:, HEADS * NOPE + hd * V_A:HEADS * NOPE + (hd + 1) * V_A].astype(BF16)
        va_ref[:, (2 * hd + 1) * V_A:(2 * hd + 2) * V_A] = ones_blk

    zc = jnp.dot(xn, w_ref[:, OFF_C:OFF_F], preferred_element_type=F32)
    gqb, gkb = gqb_ref[...], gkb_ref[...]
    for hd in range(HEADS):
        qh = zb[:, hd * HD_B:(hd + 1) * HD_B]
        r = lax.rsqrt(jnp.mean(qh * qh, axis=-1, keepdims=True) + EPS)
        qb_ref[:, hd * HD_B:(hd + 1) * HD_B] = (_rope_b(qh * r * gqb, cb, sb) * scale_b).astype(BF16)
    for hd in range(KVH_B):
        lo = HEADS * HD_B + hd * HD_B
        kh = zb[:, lo:lo + HD_B]
        r = lax.rsqrt(jnp.mean(kh * kh, axis=-1, keepdims=True) + EPS)
        kb_ref[:, hd * HD_B:(hd + 1) * HD_B] = _rope_b(kh * r * gkb, cb, sb).astype(BF16)
    for hd in range(KVH_B):
        lo = (HEADS + KVH_B + hd) * HD_B
        vb_ref[:, 2 * hd * HD_B:(2 * hd + 1) * HD_B] = zb[:, lo:lo + HD_B].astype(BF16)
        vb_ref[:, (2 * hd + 1) * HD_B:(2 * hd + 2) * HD_B] = ones_blk

    zf = jnp.dot(xn, w_ref[:, OFF_F:], preferred_element_type=F32).astype(BF16)
    cbo_ref[...] = zc[:, :CONV_W]
    u_ref[...] = zc[:, CONV_W:2 * CONV_W] * zc[:, 2 * CONV_W:]

    ccs = ccs_ref[...]
    tm = h_ref.shape[0]
    w = FG * FGD
    for g in range(FG):
        pp = jnp.dot(zf[:, g * FGD:(g + 1) * FGD], ccs, preferred_element_type=F32)
        pp_ref[g] = pp[:, :FGD]
        pp_ref[FG + g] = pp[:, FGD:]
    for r in range(4):
        for g in range(FG):
            lo = r * w + g * FGD
            pc_ref[:, lo:lo + FGD] = pp_ref[g, pl.ds(r, tm // 4, stride=4), :].astype(BF16)
            ps_ref[:, lo:lo + FGD] = pp_ref[FG + g, pl.ds(r, tm // 4, stride=4), :].astype(BF16)


def _const_spec(shape, layer=None):
    nd = len(shape)
    if layer is None:
        return pl.BlockSpec(shape, lambda i, *_: (0,) * nd, pipeline_mode=pl.Buffered(1))
    return pl.BlockSpec((None,) + shape, lambda i, *_: (layer,) + (0,) * nd, pipeline_mode=pl.Buffered(1))


def _proj(h, mod, layer, lw, tabs, ccs):
    tm = TM_PROJ
    tpb = SEQ // tm
    n_lat = T_LAT // tm

    def tab_spec():
        return pl.BlockSpec((tm, 128), lambda i: (jnp.where(i < n_lat, i % tpb, tpb), 0))

    def row_spec(w):
        return pl.BlockSpec((tm, w), lambda i: (i, 0))

    widths = (HEADS * QK_A_PAD, HEADS * QK_A_PAD, HEADS * 2 * V_A, HEADS * HD_B, KVH_B * HD_B,
              KVH_B * 2 * HD_B, CONV_W, CONV_W)
    dtypes = (BF16, BF16, BF16, BF16, BF16, BF16, F32, F32)
    dft_spec = pl.BlockSpec((tm // 4, 4 * FG * FGD), lambda i: (i, 0))
    dft_shape = jax.ShapeDtypeStruct((T_ALL // 4, 4 * FG * FGD), BF16)
    return pl.pallas_call(
        _proj_kernel,
        grid=(T_ALL // tm,),
        in_specs=[
            row_spec(D),
            _const_spec((1, D), layer),
            _mod_spec(layer, 3, tpb),
            _mod_spec(layer, 4, tpb),
            _const_spec((D, W_IN), layer),
            _const_spec((1, Q_LORA_PAD), layer),
            _const_spec((Q_LORA_PAD, HEADS * QK_A_PAD), layer),
            _const_spec((1, KV_LORA), layer),
            _const_spec((KV_LORA, HEADS * (NOPE + V_A)), layer),
            _const_spec((1, QK_A_PAD), layer),
            _const_spec((1, NOPE), layer),
            _const_spec((1, 128), layer),
            _const_spec((1, HD_B), layer),
            _const_spec((1, HD_B), layer),
            tab_spec(), tab_spec(), tab_spec(), tab_spec(), tab_spec(),
            _const_spec((FGD, 2 * FGD)),
        ],
        out_specs=[row_spec(w) for w in widths] + [dft_spec, dft_spec],
        out_shape=[jax.ShapeDtypeStruct((T_ALL, w), dt) for w, dt in zip(widths, dtypes)]
        + [dft_shape, dft_shape],
        scratch_shapes=[pltpu.VMEM((2 * FG, tm, FGD), F32)],
        compiler_params=_cparams(("arbitrary",)),
        name="proj",
    )(h, lw["norm_mix"], mod, mod, lw["w_in"], lw["g_cq"], lw["w_uq"], lw["g_ckv"], lw["w_ukv"],
      lw["g_qa"], lw["g_ka_nope"], lw["g_ka_pe"], lw["g_qb"], lw["g_kb"], *tabs, ccs)


_QK_DIMS = (((1,), (1,)), ((), ()))


def _att_lat_kernel(q_ref, k_ref, kc_ref, v_ref, vc_ref, o_ref):
    q = q_ref[...]
    chunks = [(k_ref, v_ref, j * KC_ATT, KC_ATT) for j in range(SEQ // KC_ATT)] + [(kc_ref, vc_ref, 0, CTX)]

    def scores(chunk):
        kr, _, lo, n = chunk
        return lax.dot_general(q, kr[lo:lo + n, :], _QK_DIMS, preferred_element_type=F32)

    dv = o_ref.shape[-1]
    s = scores(chunks[0])
    m = acc = None
    for idx, (_, vr, lo, n) in enumerate(chunks):
        s_next = scores(chunks[idx + 1]) if idx + 1 < len(chunks) else None
        m_chunk = jnp.max(s, axis=-1, keepdims=True)
        m_new = m_chunk if m is None else jnp.maximum(m, m_chunk)
        p = _exp2_bf16(s - m_new)
        pv = jnp.dot(p, vr[lo:lo + n, :], preferred_element_type=F32)
        acc = pv if m is None else jnp.exp2(m - m_new) * acc + pv
        m, s = m_new, s_next
    o_ref[...] = (acc[:, :dv] / acc[:, dv:]).astype(o_ref.dtype)


def _exp2_bf16(x):
    return jnp.exp2(x.astype(BF16))


def _att_ctx_kernel(group, q_ref, k_ref, v_ref, o_ref):
    dv = o_ref.shape[-1] // HEADS
    dq = q_ref.shape[-1] // HEADS
    for hd in range(HEADS):
        kv = hd // group
        s = lax.dot_general(q_ref[:, hd * dq:(hd + 1) * dq], k_ref[:, kv * dq:(kv + 1) * dq], _QK_DIMS,
                            preferred_element_type=F32)
        p = _exp2_bf16(s - jnp.max(s, axis=-1, keepdims=True))
        o = jnp.dot(p, v_ref[:, 2 * kv * dv:2 * (kv + 1) * dv], preferred_element_type=F32)
        o_ref[:, hd * dv:(hd + 1) * dv] = (o[:, :dv] / o[:, dv:]).astype(o_ref.dtype)


def _attention(q, k, v, dq, dv, group, with_ctx):
    tq = TQ_ATT
    nq = SEQ // tq
    ctx0 = T_LAT // CTX
    out = pl.pallas_call(
        _att_lat_kernel,
        grid=(BATCH, HEADS, nq),
        in_specs=[
            pl.BlockSpec((tq, dq), lambda b, h, i: (b * nq + i, h)),
            pl.BlockSpec((SEQ, dq), lambda b, h, i: (b, h // group)),
            pl.BlockSpec((CTX, dq), lambda b, h, i: (ctx0 + b, h // group)),
            pl.BlockSpec((SEQ, 2 * dv), lambda b, h, i: (b, h // group)),
            pl.BlockSpec((CTX, 2 * dv), lambda b, h, i: (ctx0 + b, h // group)),
        ],
        out_specs=pl.BlockSpec((tq, dv), lambda b, h, i: (b * nq + i, h)),
        out_shape=jax.ShapeDtypeStruct((T_LAT, HEADS * dv), BF16),
        compiler_params=_cparams(("arbitrary", "arbitrary", "arbitrary")),
        name="att_lat",
    )(q, k, k, v, v)
    if not with_ctx:
        return out, None
    kvh = HEADS // group
    out_ctx = pl.pallas_call(
        functools.partial(_att_ctx_kernel, group),
        grid=(BATCH,),
        in_specs=[
            pl.BlockSpec((CTX, HEADS * dq), lambda b: (ctx0 + b, 0)),
            pl.BlockSpec((CTX, kvh * dq), lambda b: (ctx0 + b, 0)),
            pl.BlockSpec((CTX, kvh * 2 * dv), lambda b: (ctx0 + b, 0)),
        ],
        out_specs=pl.BlockSpec((CTX, HEADS * dv), lambda b: (b, 0)),
        out_shape=jax.ShapeDtypeStruct((T_CTX, HEADS * dv), BF16),
        compiler_params=_cparams(("arbitrary",)),
        name="att_ctx",
    )(q, k, v)
    return out, out_ctx


def _dft4_kernel(scale, c_ref, s_ref, tw_ref, pc_ref, ps_ref, o_ref):
    w = o_ref.shape[-1]
    c, s = c_ref[...], s_ref[...]
    pc, ps = pc_ref[...], ps_ref[...]
    ur = jnp.dot(c, pc, preferred_element_type=F32) - jnp.dot(s, ps, preferred_element_type=F32)
    ui = -(jnp.dot(c, ps, preferred_element_type=F32) + jnp.dot(s, pc, preferred_element_type=F32))
    a = [ur[:, :w]]
    b = [None]
    for r in range(1, 4):
        u_r, v_r = ur[:, r * w:(r + 1) * w], ui[:, r * w:(r + 1) * w]
        cr, sr = tw_ref[:, 2 * r - 2:2 * r - 1], tw_ref[:, 2 * r - 1:2 * r]
        a.append(u_r * cr + v_r * sr)
        b.append(None if r == 2 else v_r * cr - u_r * sr)
    o_ref[0] = ((a[0] + a[1] + a[2] + a[3]) * scale).astype(o_ref.dtype)
    o_ref[1] = ((a[0] + b[1] - a[2] - b[3]) * scale).astype(o_ref.dtype)
    o_ref[2] = ((a[0] - a[1] + a[2] - a[3]) * scale).astype(o_ref.dtype)
    o_ref[3] = ((a[0] - b[1] - a[2] + b[3]) * scale).astype(o_ref.dtype)


def _dft4(pc4, ps4, tables, n, tk, first_block, name):
    w = FG * FGD
    nq = n // 4
    c_q, s_q, tw = tables
    return pl.pallas_call(
        functools.partial(_dft4_kernel, (n * FGD) ** -0.5),
        grid=(BATCH, nq // tk),
        in_specs=[
            pl.BlockSpec((tk, nq), lambda b, k: (k, 0)),
            pl.BlockSpec((tk, nq), lambda b, k: (k, 0)),
            pl.BlockSpec((tk, 8), lambda b, k: (k, 0)),
            pl.BlockSpec((nq, 4 * w), lambda b, k: (first_block + b, 0)),
            pl.BlockSpec((nq, 4 * w), lambda b, k: (first_block + b, 0)),
        ],
        out_specs=pl.BlockSpec((None, 4, tk, w), lambda b, k: (b, 0, k, 0)),
        out_shape=jax.ShapeDtypeStruct((BATCH, 4, nq, w), BF16),
        compiler_params=_cparams(("arbitrary", "arbitrary")),
        name=name,
    )(c_q, s_q, tw, pc4, ps4).reshape(BATCH * n, w)


def _fourier(pc4, ps4, dft_lat, dft_ctx, with_ctx):
    out = _dft4(pc4, ps4, dft_lat, SEQ, TK_DFT, 0, "dft_lat")
    if not with_ctx:
        return out, None
    return out, _dft4(pc4, ps4, dft_ctx, CTX, CTX // 4, T_LAT // CTX, "dft_ctx")


def _merge_kernel(n_lat_tiles, with_ctx, h_ref, gain_ref, sh_ref, sc_ref, gt_ref, *refs):
    lat_refs, refs = refs[:3], refs[3:]
    if with_ctx:
        ctx_refs, refs = refs[:3], refs[3:]
    (cbo_ref, u_ref, up_ref, un_ref, cw_ref, cbias_ref, wg_ref, bg_ref, wb_ref, wo_ref,
     o_ref, xn_ref, conv_ref) = refs
    i = pl.program_id(0)
    n = pl.program_id(1)
    last = pl.num_programs(1) - 1
    tm = h_ref.shape[0]
    is_lat = i < n_lat_tiles

    def conv_branch():
        u = u_ref[...]
        seq_len = jnp.where(is_lat, SEQ, CTX)
        ridx = lax.broadcasted_iota(jnp.int32, (tm, 1), 0)
        pos = (i * tm + ridx) & (seq_len - 1)
        prev = jnp.where(ridx == 0, up_ref[7:8, :], pltpu.roll(u, 1, 0))
        prev = jnp.where(pos == 0, 0.0, prev)
        nxt = jnp.where(ridx == tm - 1, un_ref[0:1, :], pltpu.roll(u, tm - 1, 0))
        nxt = jnp.where(pos == seq_len - 1, 0.0, nxt)
        y = prev * cw_ref[0:1, :] + u * cw_ref[1:2, :] + nxt * cw_ref[2:3, :] + cbias_ref[...]
        return (cbo_ref[...] * y).astype(BF16)

    def chunk(first=False):
        if first:
            xn = _norm_mod(h_ref[...], gain_ref[...], sh_ref[...], sc_ref[...]).astype(BF16)
            xn_ref[...] = xn
            conv = conv_branch()
            conv_ref[...] = conv
        else:
            xn = xn_ref[...]
            conv = conv_ref[...]

        if with_ctx:
            oa, ob, of = (jnp.where(is_lat, lr[...], cr[...]) for lr, cr in zip(lat_refs, ctx_refs))
        else:
            oa, ob, of = (lr[...] for lr in lat_refs)

        merged = None
        for k, br in enumerate((oa, ob, conv, of)):
            gate = jax.nn.sigmoid(jnp.dot(xn, wg_ref[k], preferred_element_type=F32) + bg_ref[k])
            term = gate * jnp.dot(br, wb_ref[k], preferred_element_type=F32)
            merged = term if merged is None else merged + term
        return jnp.dot(merged.astype(BF16), wo_ref[...], preferred_element_type=F32)

    @pl.when(n == 0)
    def _():
        o_ref[...] = chunk(first=True)

    @pl.when(jnp.logical_and(n > 0, n < last))
    def _():
        o_ref[...] += chunk()

    @pl.when(n == last)
    def _():
        o_ref[...] = h_ref[...] + gt_ref[...] * (o_ref[...] + chunk())


def _merge(h, mod, layer, br_lat, br_ctx, cbo, u, lw, n_rows):
    tm, tn = TM_MERGE, TN_MERGE
    tpb = SEQ // tm
    n_lat = T_LAT // tm
    with_ctx = br_ctx is not None
    last_blk = T_ALL // 8 - 1
    br_specs = [pl.BlockSpec((tm, BR_W), lambda i, n: (jnp.minimum(i, n_lat - 1), 0))] * 3
    if with_ctx:
        br_specs += [pl.BlockSpec((tm, BR_W), lambda i, n: (jnp.maximum(i - n_lat, 0), 0))] * 3

    def row_spec(w):
        return pl.BlockSpec((tm, w), lambda i, n: (i, 0))

    return pl.pallas_call(
        functools.partial(_merge_kernel, n_lat, with_ctx),
        grid=(n_rows // tm, D // tn),
        in_specs=[
            row_spec(D),
            pl.BlockSpec((None, 1, D), lambda i, n: (layer, 0, 0)),
            _mod_spec(layer, 3, tpb),
            _mod_spec(layer, 4, tpb),
            _mod_spec(layer, 5, tpb),
            *br_specs,
            row_spec(CONV_W), row_spec(CONV_W),
            pl.BlockSpec((8, CONV_W), lambda i, n: (jnp.maximum(i * (tm // 8) - 1, 0), 0)),
            pl.BlockSpec((8, CONV_W), lambda i, n: (jnp.minimum((i + 1) * (tm // 8), last_blk), 0)),
            pl.BlockSpec((None, 3, CONV_W), lambda i, n: (layer, 0, 0)),
            pl.BlockSpec((None, 1, CONV_W), lambda i, n: (layer, 0, 0)),
            pl.BlockSpec((None, N_BR, D, tn), lambda i, n: (layer, 0, 0, n)),
            pl.BlockSpec((None, N_BR, 1, tn), lambda i, n: (layer, 0, 0, n)),
            pl.BlockSpec((None, N_BR, BR_W, tn), lambda i, n: (layer, 0, 0, n)),
            pl.BlockSpec((None, tn, D), lambda i, n: (layer, n, 0)),
        ],
        out_specs=row_spec(D),
        out_shape=jax.ShapeDtypeStruct((n_rows, D), F32),
        scratch_shapes=[pltpu.VMEM((tm, D), BF16), pltpu.VMEM((tm, CONV_W), BF16)],
        compiler_params=_cparams(("arbitrary", "arbitrary")),
        name="merge",
    )(h, lw["norm_mix"], mod, mod, mod, *br_lat, *(br_ctx or ()), cbo, u, u, u, lw["conv_w"],
      lw["conv_b"], lw["w_gate"], lw["b_gate"], lw["w_branch"], lw["w_o"])


def _rope_tables():
    t = jnp.arange(SEQ, dtype=jnp.int32)
    pos_row = (t // GRID_W).astype(F32)
    pos_col = (t % GRID_W).astype(F32)

    def ang(rot_dim):
        n = rot_dim // 4
        inv_freq = THETA ** (-jnp.arange(n, dtype=F32) / n)
        a = jnp.concatenate([pos_row[:, None] * inv_freq, pos_col[:, None] * inv_freq], axis=-1)
        return jnp.cos(a), jnp.sin(a)

    ca, sa = ang(ROPE_A)
    z32 = jnp.zeros_like(ca)
    z64 = jnp.zeros((SEQ, 64), F32)
    tab_c = jnp.concatenate([ca, ca, z64], axis=-1)
    tab_s1 = jnp.concatenate([-sa, z32, z64], axis=-1)
    tab_s2 = jnp.concatenate([z32, sa, z64], axis=-1)
    cb, sb = ang(HD_B)
    tab_cb = jnp.concatenate([cb, cb], axis=-1)
    tab_sb = jnp.concatenate([-sb, sb], axis=-1)
    ones = jnp.ones((TM_PROJ, 128), F32)
    zeros = jnp.zeros((TM_PROJ, 128), F32)
    return tuple(jnp.concatenate([tab, ident], axis=0)
                 for tab, ident in ((tab_c, ones), (tab_s1, zeros), (tab_s2, zeros),
                                    (tab_cb, ones), (tab_sb, zeros)))


def _dft_mats(n):
    j = jnp.arange(n, dtype=jnp.int32)
    a = ((j[:, None] * j[None, :]) % n).astype(F32) * (2.0 * math.pi / n)
    return jnp.cos(a), jnp.sin(a)


def _dft4_tables(n):
    nq = n // 4
    c_q, s_q = _dft_mats(nq)
    k = jnp.arange(nq, dtype=jnp.int32)[:, None]
    ang = (k * jnp.arange(1, 4, dtype=jnp.int32)[None, :]).astype(F32) * (2.0 * math.pi / n)
    tw = jnp.stack([jnp.cos(ang), jnp.sin(ang)], axis=-1).reshape(nq, 6)
    return c_q.astype(BF16), s_q.astype(BF16), _pad_to(tw, 1, 8)


def _pad_to(a, axis, size):
    pad = [(0, 0)] * a.ndim
    pad[axis] = (0, size - a.shape[axis])
    return jnp.pad(a, pad)


def _prep_weights(p):
    w_uq = p["w_uq"].reshape(DEPTH, Q_LORA, HEADS, QK_A)
    w_uq = _pad_to(_pad_to(w_uq, 3, QK_A_PAD), 1, Q_LORA_PAD).reshape(DEPTH, Q_LORA_PAD, HEADS * QK_A_PAD)
    w_ukv = p["w_ukv"].reshape(DEPTH, KV_LORA, HEADS, NOPE + V_A)
    w_ukv = jnp.concatenate([w_ukv[..., :NOPE].reshape(DEPTH, KV_LORA, HEADS * NOPE),
                             w_ukv[..., NOPE:].reshape(DEPTH, KV_LORA, HEADS * V_A)], axis=2)
    g_ka = p["g_ka"]

    return {
        "w_in": p["w_in"].astype(BF16),
        "g_cq": _pad_to(p["g_cq"], 1, Q_LORA_PAD)[:, None, :],
        "w_uq": w_uq.astype(BF16),
        "g_ckv": p["g_ckv"][:, None, :],
        "w_ukv": w_ukv.astype(BF16),
        "g_qa": _pad_to(p["g_qa"], 1, QK_A_PAD)[:, None, :],
        "g_ka_nope": g_ka[:, None, :NOPE],
        "g_ka_pe": _pad_to(g_ka[:, NOPE:], 1, 128)[:, None, :],
        "g_qb": p["g_qb"][:, None, :],
        "g_kb": p["g_kb"][:, None, :],
        "conv_w": p["conv_w"],
        "conv_b": p["conv_b"][:, None, :],
        "w_gate": p["w_gate"].astype(BF16),
        "b_gate": p["b_gate"][:, :, None, :],
        "w_branch": p["w_branch"].astype(BF16),
        "w_o": p["w_o"].astype(BF16),
        "norm_ffn1": p["norm_ffn1"][:, None, :],
        "norm_mix": p["norm_mix"][:, None, :],
        "norm_ffn2": p["norm_ffn2"][:, None, :],
        "ffn1_wi": p["ffn1_wi"].astype(BF16), "ffn1_wo": p["ffn1_wo"].astype(BF16),
        "ffn2_wi": p["ffn2_wi"].astype(BF16), "ffn2_wo": p["ffn2_wo"].astype(BF16),
    }


def kernel(x, c, ctx, c_ctx, w_ada, b_ada, norm_ffn1, ffn1_wi, ffn1_wo, norm_mix, w_in, g_cq, w_uq, g_ckv, w_ukv, g_qa, g_ka, g_qb, g_kb, conv_w, conv_b, w_branch, w_gate, b_gate, w_o, norm_ffn2, ffn2_wi, ffn2_wo):
    p = dict(norm_ffn1=norm_ffn1, ffn1_wi=ffn1_wi, ffn1_wo=ffn1_wo, norm_mix=norm_mix, w_in=w_in,
             g_cq=g_cq, w_uq=w_uq, g_ckv=g_ckv, w_ukv=w_ukv, g_qa=g_qa, g_ka=g_ka, g_qb=g_qb,
             g_kb=g_kb, conv_w=conv_w, conv_b=conv_b, w_branch=w_branch, w_gate=w_gate,
             b_gate=b_gate, w_o=w_o, norm_ffn2=norm_ffn2, ffn2_wi=ffn2_wi, ffn2_wo=ffn2_wo)

    cond8 = jnp.concatenate([c, c_ctx[None, :], jnp.zeros((8 - BATCH - 1, D), F32)], axis=0)
    mod = _ada(cond8, w_ada, b_ada).reshape(DEPTH, 8, N_MOD, 1, D)

    tabs = _rope_tables()
    cc, sc = _dft_mats(FGD)
    ccs = jnp.concatenate([cc, sc], axis=1).astype(BF16)
    dft_lat = _dft4_tables(SEQ)
    dft_ctx = _dft4_tables(CTX)

    lw = _prep_weights(p)
    h = jnp.concatenate([x.reshape(T_LAT, D), ctx.reshape(T_CTX, D)], axis=0)
    for l in range(DEPTH):
        last = l == DEPTH - 1
        h = _ffn(h, mod, l, 0, lw["norm_ffn1"], lw["ffn1_wi"], lw["ffn1_wo"], T_ALL)
        qa, ka, va, qb, kb, vb, cbo, u, pc, ps = _proj(h, mod, l, lw, tabs, ccs)
        oa, oa_c = _attention(qa, ka, va, QK_A_PAD, V_A, 1, not last)
        ob, ob_c = _attention(qb, kb, vb, HD_B, HD_B, GROUP_B, not last)
        of, of_c = _fourier(pc, ps, dft_lat, dft_ctx, not last)
        n_rows = T_LAT if last else T_ALL
        h = _merge(h, mod, l, (oa, ob, of), None if last else (oa_c, ob_c, of_c), cbo, u, lw, n_rows)
        h = _ffn(h, mod, l, 6, lw["norm_ffn2"], lw["ffn2_wi"], lw["ffn2_wo"], n_rows)
    return h.reshape(BATCH, SEQ, D)
```

```python
import functools
import math

import jax
import jax.numpy as jnp
from jax import lax
from jax.experimental import pallas as pl
from jax.experimental.pallas import tpu as pltpu

F32 = jnp.float32
BF16 = jnp.bfloat16

D = 2048
BATCH = 4
SEQ = 4096
DEPTH = 4
GRID_W = 64
CTX = 256
N_MOD = 9
F_FF = 5504
HEADS = 4
Q_LORA = 448
Q_LORA_PAD = 512
KV_LORA = 128
NOPE = 128
ROPE_A = 64
QK_A = NOPE + ROPE_A
QK_A_PAD = 256
V_A = 128
HD_B = 128
KVH_B = 2
GROUP_B = 2
CONV_W = 512
FG = 4
FGD = 128
BR_W = 512
N_BR = 4
THETA = 10000.0
EPS = 1e-6
LOG2_E = math.log2(math.e)

T_LAT = BATCH * SEQ
T_CTX = BATCH * CTX
T_ALL = T_LAT + T_CTX

W_IN = 3712
OFF_A = 0
OFF_B = 640
OFF_C = 1664
OFF_F = 3200

LANE = 128
VMEM_PHYS_V7X = 64 * 1024 * 1024
VMEM_LIMIT = 60 * 1024 * 1024

TM_FFN = 1024
TF_FFN = 512
TM_PROJ = 512
TM_MERGE = 512
TN_MERGE = 512
TQ_ATT = 1024
KC_ATT = 1024
TK_DFT = 512
TN_ADA = 2048


def _cparams(sem):
    return pltpu.CompilerParams(dimension_semantics=sem, vmem_limit_bytes=VMEM_LIMIT)


def _mod_row(i, tiles_per_batch):
    return jnp.minimum(i // tiles_per_batch, BATCH)


def _row_rsqrt(x):
    return lax.rsqrt(jnp.mean(x * x, axis=-1, keepdims=True) + EPS)


def _scale_mod(x, r, gain, shift, scale):
    return (x * r * gain) * (1.0 + scale) + shift


def _norm_mod(x, gain, shift, scale):
    return _scale_mod(x, _row_rsqrt(x), gain, shift, scale)


def _silu(x):
    return x * jax.nn.sigmoid(x)


def _ada_kernel(cond_ref, w_ref, b_ref, o_ref):
    s = _silu(cond_ref[...]).astype(BF16)
    o_ref[...] = jnp.dot(s, w_ref[...].astype(BF16), preferred_element_type=F32) + b_ref[...]


def _ada(cond8, w_ada, b_ada):
    n = N_MOD * D
    return pl.pallas_call(
        _ada_kernel,
        grid=(DEPTH, n // TN_ADA),
        in_specs=[
            pl.BlockSpec((8, D), lambda l, j: (0, 0)),
            pl.BlockSpec((None, D, TN_ADA), lambda l, j: (l, 0, j)),
            pl.BlockSpec((None, 1, TN_ADA), lambda l, j: (l, 0, j)),
        ],
        out_specs=pl.BlockSpec((None, 8, TN_ADA), lambda l, j: (l, 0, j)),
        out_shape=jax.ShapeDtypeStruct((DEPTH, 8, n), F32),
        compiler_params=_cparams(("arbitrary", "arbitrary")),
        name="ada",
    )(cond8, w_ada, b_ada.reshape(DEPTH, 1, n))


def _mod_spec(layer, j, tiles_per_batch):
    return pl.BlockSpec((None, None, None, 1, D),
                        lambda i, *_: (layer, _mod_row(i, tiles_per_batch), j, 0, 0))


def _ffn_kernel(overlap, x_ref, gain_ref, sh_ref, sc_ref, gt_ref, wg_ref, wu_ref, wo_ref, o_ref, xn_ref):
    f = pl.program_id(1)
    last = pl.num_programs(1) - 1

    def chunk(skip=0, first=False):
        if first:
            xn = _norm_mod(x_ref[...], gain_ref[...], sh_ref[...], sc_ref[...]).astype(BF16)
            xn_ref[...] = xn
        else:
            xn = xn_ref[...]
        g = jnp.dot(xn, wg_ref[:, skip:], preferred_element_type=F32)
        u = jnp.dot(xn, wu_ref[:, skip:], preferred_element_type=F32)
        a = (_silu(g) * u).astype(BF16)
        return jnp.dot(a, wo_ref[skip:, :], preferred_element_type=F32)

    @pl.when(f == 0)
    def _():
        o_ref[...] = chunk(first=True)

    @pl.when(jnp.logical_and(f > 0, f < last))
    def _():
        o_ref[...] += chunk()

    @pl.when(f == last)
    def _():
        o_ref[...] = x_ref[...] + (0.5 * gt_ref[...]) * (o_ref[...] + chunk(overlap))


def _ffn(h, mod, layer, j0, gain, wi, wo, n_rows):
    tm, tf = TM_FFN, TF_FFN
    tpb = SEQ // tm
    nf = pl.cdiv(F_FF, tf)

    def hid(f, base=0):
        return (base // LANE + jnp.minimum(f * (tf // LANE), (F_FF - tf) // LANE)) * LANE

    return pl.pallas_call(
        functools.partial(_ffn_kernel, nf * tf - F_FF),
        grid=(n_rows // tm, nf),
        in_specs=[
            pl.BlockSpec((tm, D), lambda i, f: (i, 0)),
            pl.BlockSpec((None, 1, D), lambda i, f: (layer, 0, 0)),
            _mod_spec(layer, j0, tpb),
            _mod_spec(layer, j0 + 1, tpb),
            _mod_spec(layer, j0 + 2, tpb),
            pl.BlockSpec((None, pl.Element(D), pl.Element(tf)), lambda i, f: (layer, 0, hid(f))),
            pl.BlockSpec((None, pl.Element(D), pl.Element(tf)), lambda i, f: (layer, 0, hid(f, F_FF))),
            pl.BlockSpec((None, pl.Element(tf), pl.Element(D)), lambda i, f: (layer, hid(f), 0)),
        ],
        out_specs=pl.BlockSpec((tm, D), lambda i, f: (i, 0)),
        out_shape=jax.ShapeDtypeStruct((n_rows, D), F32),
        scratch_shapes=[pltpu.VMEM((tm, D), BF16)],
        compiler_params=_cparams(("arbitrary", "arbitrary")),
        name="ffn",
    )(h, gain, mod, mod, mod, wi, wi, wo)


def _rope_a(v, c, s1, s2):
    return v * c + pltpu.roll(v, 96, 1) * s1 + pltpu.roll(v, 32, 1) * s2


def _rope_b(v, c, s):
    return v * c + pltpu.roll(v, 64, 1) * s


def _proj_kernel(h_ref, gain_ref, sh_ref, sc_ref, w_ref, gcq_ref, wuq_ref, gckv_ref, wukv_ref,
                 gqa_ref, gkan_ref, gkap_ref, gqb_ref, gkb_ref, ca_ref, s1a_ref, s2a_ref,
                 cb_ref, sb_ref, ccs_ref,
                 qa_ref, ka_ref, va_ref, qb_ref, kb_ref, vb_ref, cbo_ref, u_ref, pc_ref, ps_ref,
                 pp_ref):
    xn = _norm_mod(h_ref[...], gain_ref[...], sh_ref[...], sc_ref[...]).astype(BF16)
    ca, s1a, s2a = ca_ref[...], s1a_ref[...], s2a_ref[...]
    cb, sb = cb_ref[...], sb_ref[...]
    scale_a = QK_A ** -0.5 * LOG2_E
    scale_b = HD_B ** -0.5 * LOG2_E

    za = jnp.dot(xn, w_ref[:, OFF_A:OFF_B], preferred_element_type=F32)
    zb = jnp.dot(xn, w_ref[:, OFF_B:OFF_C], preferred_element_type=F32)

    hi = lax.broadcasted_iota(jnp.int32, (1, LANE), 1) >= LANE // 2
    s3, s4 = za[:, 3 * LANE:4 * LANE], za[:, 4 * LANE:]
    aq = jnp.concatenate([za[:, :3 * LANE], jnp.where(hi, 0.0, s3)], axis=1)
    akv = pltpu.roll(jnp.where(hi, s3, s4), LANE // 2, 1)
    ape = jnp.where(hi, 0.0, pltpu.roll(s4, LANE // 2, 1))
    r = lax.rsqrt(jnp.sum(aq * aq, axis=-1, keepdims=True) * (1.0 / Q_LORA) + EPS)
    cq = (aq * r * gcq_ref[...]).astype(BF16)
    q = jnp.dot(cq, wuq_ref[...], preferred_element_type=F32)
    gqa = gqa_ref[...]
    for hd in range(HEADS):
        lo = hd * QK_A_PAD
        qh = q[:, lo:lo + QK_A_PAD]
        r = lax.rsqrt(jnp.sum(qh * qh, axis=-1, keepdims=True) * (1.0 / QK_A) + EPS)
        qh = qh * r * gqa
        qa_ref[:, lo:lo + NOPE] = (qh[:, :NOPE] * scale_a).astype(BF16)
        qa_ref[:, lo + NOPE:lo + QK_A_PAD] = (_rope_a(qh[:, NOPE:], ca, s1a, s2a) * scale_a).astype(BF16)
    r = lax.rsqrt(jnp.mean(akv * akv, axis=-1, keepdims=True) + EPS)
    ckv = (akv * r * gckv_ref[...]).astype(BF16)
    kv = jnp.dot(ckv, wukv_ref[...], preferred_element_type=F32)
    ss_pe = jnp.sum(ape * ape, axis=-1, keepdims=True)
    ones_blk = jnp.ones((h_ref.shape[0], LANE), BF16)
    gkan, gkap = gkan_ref[...], gkap_ref[...]
    for hd in range(HEADS):
        kn = kv[:, hd * NOPE:(hd + 1) * NOPE]
        r = lax.rsqrt((jnp.sum(kn * kn, axis=-1, keepdims=True) + ss_pe) * (1.0 / QK_A) + EPS)
        lo = hd * QK_A_PAD
        ka_ref[:, lo:lo + NOPE] = (kn * r * gkan).astype(BF16)
        ka_ref[:, lo + NOPE:lo + QK_A_PAD] = _rope_a(ape * r * gkap, ca, s1a, s2a).astype(BF16)
        va_ref[:, 2 * hd * V_A:(2 * hd + 1) * V_A] = kv[:, HEADS * NOPE + hd * V_A:HEADS * NOPE + (hd + 1) * V_A].astype(BF16)
        va_ref[:, (2 * hd + 1) * V_A:(2 * hd + 2) * V_A] = ones_blk

    zc = jnp.dot(xn, w_ref[:, OFF_C:OFF_F], preferred_element_type=F32)
    gqb, gkb = gqb_ref[...], gkb_ref[...]
    for hd in range(HEADS):
        qh = zb[:, hd * HD_B:(hd + 1) * HD_B]
        r = lax.rsqrt(jnp.mean(qh * qh, axis=-1, keepdims=True) + EPS)
        qb_ref[:, hd * HD_B:(hd + 1) * HD_B] = (_rope_b(qh * r * gqb, cb, sb) * scale_b).astype(BF16)
    for hd in range(KVH_B):
        lo = HEADS * HD_B + hd * HD_B
        kh = zb[:, lo:lo + HD_B]
        r = lax.rsqrt(jnp.mean(kh * kh, axis=-1, keepdims=True) + EPS)
        kb_ref[:, hd * HD_B:(hd + 1) * HD_B] = _rope_b(kh * r * gkb, cb, sb).astype(BF16)
    for hd in range(KVH_B):
        lo = (HEADS + KVH_B + hd) * HD_B
        vb_ref[:, 2 * hd * HD_B:(2 * hd + 1) * HD_B] = zb[:, lo:lo + HD_B].astype(BF16)
        vb_ref[:, (2 * hd + 1) * HD_B:(2 * hd + 2) * HD_B] = ones_blk

    zf = jnp.dot(xn, w_ref[:, OFF_F:], preferred_element_type=F32).astype(BF16)
    cbo_ref[...] = zc[:, :CONV_W]
    u_ref[...] = zc[:, CONV_W:2 * CONV_W] * zc[:, 2 * CONV_W:]

    ccs = ccs_ref[...]
    tm = h_ref.shape[0]
    w = FG * FGD
    for g in range(FG):
        pp = jnp.dot(zf[:, g * FGD:(g + 1) * FGD], ccs, preferred_element_type=F32)
        pp_ref[g] = pp[:, :FGD]
        pp_ref[FG + g] = pp[:, FGD:]
    for r in range(4):
        for g in range(FG):
            lo = r * w + g * FGD
            pc_ref[:, lo:lo + FGD] = pp_ref[g, pl.ds(r, tm // 4, stride=4), :].astype(BF16)
            ps_ref[:, lo:lo + FGD] = pp_ref[FG + g, pl.ds(r, tm // 4, stride=4), :].astype(BF16)


def _const_spec(shape, layer=None):
    nd = len(shape)
    if layer is None:
        return pl.BlockSpec(shape, lambda i, *_: (0,) * nd, pipeline_mode=pl.Buffered(1))
    return pl.BlockSpec((None,) + shape, lambda i, *_: (layer,) + (0,) * nd, pipeline_mode=pl.Buffered(1))


def _proj(h, mod, layer, lw, tabs, ccs):
    tm = TM_PROJ
    tpb = SEQ // tm
    n_lat = T_LAT // tm

    def tab_spec():
        return pl.BlockSpec((tm, 128), lambda i: (jnp.where(i < n_lat, i % tpb, tpb), 0))

    def row_spec(w):
        return pl.BlockSpec((tm, w), lambda i: (i, 0))

    widths = (HEADS * QK_A_PAD, HEADS * QK_A_PAD, HEADS * 2 * V_A, HEADS * HD_B, KVH_B * HD_B,
              KVH_B * 2 * HD_B, CONV_W, CONV_W)
    dtypes = (BF16, BF16, BF16, BF16, BF16, BF16, F32, F32)
    dft_spec = pl.BlockSpec((tm // 4, 4 * FG * FGD), lambda i: (i, 0))
    dft_shape = jax.ShapeDtypeStruct((T_ALL // 4, 4 * FG * FGD), BF16)
    return pl.pallas_call(
        _proj_kernel,
        grid=(T_ALL // tm,),
        in_specs=[
            row_spec(D),
            _const_spec((1, D), layer),
            _mod_spec(layer, 3, tpb),
            _mod_spec(layer, 4, tpb),
            _const_spec((D, W_IN), layer),
            _const_spec((1, Q_LORA_PAD), layer),
            _const_spec((Q_LORA_PAD, HEADS * QK_A_PAD), layer),
            _const_spec((1, KV_LORA), layer),
            _const_spec((KV_LORA, HEADS * (NOPE + V_A)), layer),
            _const_spec((1, QK_A_PAD), layer),
            _const_spec((1, NOPE), layer),
            _const_spec((1, 128), layer),
            _const_spec((1, HD_B), layer),
            _const_spec((1, HD_B), layer),
            tab_spec(), tab_spec(), tab_spec(), tab_spec(), tab_spec(),
            _const_spec((FGD, 2 * FGD)),
        ],
        out_specs=[row_spec(w) for w in widths] + [dft_spec, dft_spec],
        out_shape=[jax.ShapeDtypeStruct((T_ALL, w), dt) for w, dt in zip(widths, dtypes)]
        + [dft_shape, dft_shape],
        scratch_shapes=[pltpu.VMEM((2 * FG, tm, FGD), F32)],
        compiler_params=_cparams(("arbitrary",)),
        name="proj",
    )(h, lw["norm_mix"], mod, mod, lw["w_in"], lw["g_cq"], lw["w_uq"], lw["g_ckv"], lw["w_ukv"],
      lw["g_qa"], lw["g_ka_nope"], lw["g_ka_pe"], lw["g_qb"], lw["g_kb"], *tabs, ccs)


_QK_DIMS = (((1,), (1,)), ((), ()))


def _att_lat_kernel(kc, q_ref, k_ref, kc_ref, v_ref, vc_ref, o_ref):
    q = q_ref[...]
    chunks = [(k_ref, v_ref, j * kc, kc) for j in range(SEQ // kc)] + [(kc_ref, vc_ref, 0, CTX)]

    def scores(chunk):
        kr, _, lo, n = chunk
        return lax.dot_general(q, kr[lo:lo + n, :], _QK_DIMS, preferred_element_type=F32)

    dv = o_ref.shape[-1]
    s = scores(chunks[0])
    m = acc = None
    for idx, (_, vr, lo, n) in enumerate(chunks):
        s_next = scores(chunks[idx + 1]) if idx + 1 < len(chunks) else None
        m_chunk = jnp.max(s, axis=-1, keepdims=True)
        m_new = m_chunk if m is None else jnp.maximum(m, m_chunk)
        p = _exp2_bf16(s - m_new)
        pv = jnp.dot(p, vr[lo:lo + n, :], preferred_element_type=F32)
        acc = pv if m is None else jnp.exp2(m - m_new) * acc + pv
        m, s = m_new, s_next
    o_ref[...] = (acc[:, :dv] / acc[:, dv:]).astype(o_ref.dtype)


def _exp2_bf16(x):
    return jnp.exp2(x.astype(BF16))


def _att_ctx_kernel(group, q_ref, k_ref, v_ref, o_ref):
    dv = o_ref.shape[-1] // HEADS
    dq = q_ref.shape[-1] // HEADS
    for hd in range(HEADS):
        kv = hd // group
        s = lax.dot_general(q_ref[:, hd * dq:(hd + 1) * dq], k_ref[:, kv * dq:(kv + 1) * dq], _QK_DIMS,
                            preferred_element_type=F32)
        p = _exp2_bf16(s - jnp.max(s, axis=-1, keepdims=True))
        o = jnp.dot(p, v_ref[:, 2 * kv * dv:2 * (kv + 1) * dv], preferred_element_type=F32)
        o_ref[:, hd * dv:(hd + 1) * dv] = (o[:, :dv] / o[:, dv:]).astype(o_ref.dtype)


def _attention(q, k, v, dq, dv, group, with_ctx, kc=KC_ATT):
    tq = TQ_ATT
    nq = SEQ // tq
    ctx0 = T_LAT // CTX
    out = pl.pallas_call(
        functools.partial(_att_lat_kernel, kc),
        grid=(BATCH, HEADS, nq),
        in_specs=[
            pl.BlockSpec((tq, dq), lambda b, h, i: (b * nq + i, h)),
            pl.BlockSpec((SEQ, dq), lambda b, h, i: (b, h // group)),
            pl.BlockSpec((CTX, dq), lambda b, h, i: (ctx0 + b, h // group)),
            pl.BlockSpec((SEQ, 2 * dv), lambda b, h, i: (b, h // group)),
            pl.BlockSpec((CTX, 2 * dv), lambda b, h, i: (ctx0 + b, h // group)),
        ],
        out_specs=pl.BlockSpec((tq, dv), lambda b, h, i: (b * nq + i, h)),
        out_shape=jax.ShapeDtypeStruct((T_LAT, HEADS * dv), BF16),
        compiler_params=_cparams(("arbitrary", "arbitrary", "arbitrary")),
        name="att_lat",
    )(q, k, k, v, v)
    if not with_ctx:
        return out, None
    kvh = HEADS // group
    out_ctx = pl.pallas_call(
        functools.partial(_att_ctx_kernel, group),
        grid=(BATCH,),
        in_specs=[
            pl.BlockSpec((CTX, HEADS * dq), lambda b: (ctx0 + b, 0)),
            pl.BlockSpec((CTX, kvh * dq), lambda b: (ctx0 + b, 0)),
            pl.BlockSpec((CTX, kvh * 2 * dv), lambda b: (ctx0 + b, 0)),
        ],
        out_specs=pl.BlockSpec((CTX, HEADS * dv), lambda b: (b, 0)),
        out_shape=jax.ShapeDtypeStruct((T_CTX, HEADS * dv), BF16),
        compiler_params=_cparams(("arbitrary",)),
        name="att_ctx",
    )(q, k, v)
    return out, out_ctx


def _dft4_kernel(scale, c_ref, s_ref, tw_ref, pc_ref, ps_ref, o_ref):
    w = o_ref.shape[-1]
    c, s = c_ref[...], s_ref[...]
    pc, ps = pc_ref[...], ps_ref[...]
    ur = jnp.dot(c, pc, preferred_element_type=F32) - jnp.dot(s, ps, preferred_element_type=F32)
    ui = -(jnp.dot(c, ps, preferred_element_type=F32) + jnp.dot(s, pc, preferred_element_type=F32))
    a = [ur[:, :w]]
    b = [None]
    for r in range(1, 4):
        u_r, v_r = ur[:, r * w:(r + 1) * w], ui[:, r * w:(r + 1) * w]
        cr, sr = tw_ref[:, 2 * r - 2:2 * r - 1], tw_ref[:, 2 * r - 1:2 * r]
        a.append(u_r * cr + v_r * sr)
        b.append(None if r == 2 else v_r * cr - u_r * sr)
    o_ref[0] = ((a[0] + a[1] + a[2] + a[3]) * scale).astype(o_ref.dtype)
    o_ref[1] = ((a[0] + b[1] - a[2] - b[3]) * scale).astype(o_ref.dtype)
    o_ref[2] = ((a[0] - a[1] + a[2] - a[3]) * scale).astype(o_ref.dtype)
    o_ref[3] = ((a[0] - b[1] - a[2] + b[3]) * scale).astype(o_ref.dtype)


def _dft4(pc4, ps4, tables, n, tk, first_block, name):
    w = FG * FGD
    nq = n // 4
    c_q, s_q, tw = tables
    return pl.pallas_call(
        functools.partial(_dft4_kernel, (n * FGD) ** -0.5),
        grid=(BATCH, nq // tk),
        in_specs=[
            pl.BlockSpec((tk, nq), lambda b, k: (k, 0)),
            pl.BlockSpec((tk, nq), lambda b, k: (k, 0)),
            pl.BlockSpec((tk, 8), lambda b, k: (k, 0)),
            pl.BlockSpec((nq, 4 * w), lambda b, k: (first_block + b, 0)),
            pl.BlockSpec((nq, 4 * w), lambda b, k: (first_block + b, 0)),
        ],
        out_specs=pl.BlockSpec((None, 4, tk, w), lambda b, k: (b, 0, k, 0)),
        out_shape=jax.ShapeDtypeStruct((BATCH, 4, nq, w), BF16),
        compiler_params=_cparams(("arbitrary", "arbitrary")),
        name=name,
    )(c_q, s_q, tw, pc4, ps4).reshape(BATCH * n, w)


def _fourier(pc4, ps4, dft_lat, dft_ctx, with_ctx):
    out = _dft4(pc4, ps4, dft_lat, SEQ, TK_DFT, 0, "dft_lat")
    if not with_ctx:
        return out, None
    return out, _dft4(pc4, ps4, dft_ctx, CTX, CTX // 4, T_LAT // CTX, "dft_ctx")


def _merge_kernel(n_lat_tiles, with_ctx, h_ref, gain_ref, sh_ref, sc_ref, gt_ref, *refs):
    lat_refs, refs = refs[:3], refs[3:]
    if with_ctx:
        ctx_refs, refs = refs[:3], refs[3:]
    (cbo_ref, u_ref, up_ref, un_ref, cw_ref, cbias_ref, wg_ref, bg_ref, wb_ref, wo_ref,
     o_ref, xn_ref, conv_ref) = refs
    i = pl.program_id(0)
    n = pl.program_id(1)
    last = pl.num_programs(1) - 1
    tm = h_ref.shape[0]
    is_lat = i < n_lat_tiles

    def conv_branch():
        u = u_ref[...]
        seq_len = jnp.where(is_lat, SEQ, CTX)
        ridx = lax.broadcasted_iota(jnp.int32, (tm, 1), 0)
        pos = (i * tm + ridx) & (seq_len - 1)
        prev = jnp.where(ridx == 0, up_ref[7:8, :], pltpu.roll(u, 1, 0))
        prev = jnp.where(pos == 0, 0.0, prev)
        nxt = jnp.where(ridx == tm - 1, un_ref[0:1, :], pltpu.roll(u, tm - 1, 0))
        nxt = jnp.where(pos == seq_len - 1, 0.0, nxt)
        y = prev * cw_ref[0:1, :] + u * cw_ref[1:2, :] + nxt * cw_ref[2:3, :] + cbias_ref[...]
        return (cbo_ref[...] * y).astype(BF16)

    def chunk(first=False):
        if first:
            xn = _norm_mod(h_ref[...], gain_ref[...], sh_ref[...], sc_ref[...]).astype(BF16)
            xn_ref[...] = xn
            conv = conv_branch()
            conv_ref[...] = conv
        else:
            xn = xn_ref[...]
            conv = conv_ref[...]

        if with_ctx:
            oa, ob, of = (jnp.where(is_lat, lr[...], cr[...]) for lr, cr in zip(lat_refs, ctx_refs))
        else:
            oa, ob, of = (lr[...] for lr in lat_refs)

        merged = None
        for k, br in enumerate((oa, ob, conv, of)):
            gate = jax.nn.sigmoid(jnp.dot(xn, wg_ref[k], preferred_element_type=F32) + bg_ref[k])
            term = gate * jnp.dot(br, wb_ref[k], preferred_element_type=F32)
            merged = term if merged is None else merged + term
        return jnp.dot(merged.astype(BF16), wo_ref[...], preferred_element_type=F32)

    @pl.when(n == 0)
    def _():
        o_ref[...] = chunk(first=True)

    @pl.when(jnp.logical_and(n > 0, n < last))
    def _():
        o_ref[...] += chunk()

    @pl.when(n == last)
    def _():
        o_ref[...] = h_ref[...] + gt_ref[...] * (o_ref[...] + chunk())


def _merge(h, mod, layer, br_lat, br_ctx, cbo, u, lw, n_rows):
    tm, tn = TM_MERGE, TN_MERGE
    tpb = SEQ // tm
    n_lat = T_LAT // tm
    with_ctx = br_ctx is not None
    last_blk = T_ALL // 8 - 1
    br_specs = [pl.BlockSpec((tm, BR_W), lambda i, n: (jnp.minimum(i, n_lat - 1), 0))] * 3
    if with_ctx:
        br_specs += [pl.BlockSpec((tm, BR_W), lambda i, n: (jnp.maximum(i - n_lat, 0), 0))] * 3

    def row_spec(w):
        return pl.BlockSpec((tm, w), lambda i, n: (i, 0))

    return pl.pallas_call(
        functools.partial(_merge_kernel, n_lat, with_ctx),
        grid=(n_rows // tm, D // tn),
        in_specs=[
            row_spec(D),
            pl.BlockSpec((None, 1, D), lambda i, n: (layer, 0, 0)),
            _mod_spec(layer, 3, tpb),
            _mod_spec(layer, 4, tpb),
            _mod_spec(layer, 5, tpb),
            *br_specs,
            row_spec(CONV_W), row_spec(CONV_W),
            pl.BlockSpec((8, CONV_W), lambda i, n: (jnp.maximum(i * (tm // 8) - 1, 0), 0)),
            pl.BlockSpec((8, CONV_W), lambda i, n: (jnp.minimum((i + 1) * (tm // 8), last_blk), 0)),
            pl.BlockSpec((None, 3, CONV_W), lambda i, n: (layer, 0, 0)),
            pl.BlockSpec((None, 1, CONV_W), lambda i, n: (layer, 0, 0)),
            pl.BlockSpec((None, N_BR, D, tn), lambda i, n: (layer, 0, 0, n)),
            pl.BlockSpec((None, N_BR, 1, tn), lambda i, n: (layer, 0, 0, n)),
            pl.BlockSpec((None, N_BR, BR_W, tn), lambda i, n: (layer, 0, 0, n)),
            pl.BlockSpec((None, tn, D), lambda i, n: (layer, n, 0)),
        ],
        out_specs=row_spec(D),
        out_shape=jax.ShapeDtypeStruct((n_rows, D), F32),
        scratch_shapes=[pltpu.VMEM((tm, D), BF16), pltpu.VMEM((tm, CONV_W), BF16)],
        compiler_params=_cparams(("arbitrary", "arbitrary")),
        name="merge",
    )(h, lw["norm_mix"], mod, mod, mod, *br_lat, *(br_ctx or ()), cbo, u, u, u, lw["conv_w"],
      lw["conv_b"], lw["w_gate"], lw["b_gate"], lw["w_branch"], lw["w_o"])


def _rope_tables():
    t = jnp.arange(SEQ, dtype=jnp.int32)
    pos_row = (t // GRID_W).astype(F32)
    pos_col = (t % GRID_W).astype(F32)

    def ang(rot_dim):
        n = rot_dim // 4
        inv_freq = THETA ** (-jnp.arange(n, dtype=F32) / n)
        a = jnp.concatenate([pos_row[:, None] * inv_freq, pos_col[:, None] * inv_freq], axis=-1)
        return jnp.cos(a), jnp.sin(a)

    ca, sa = ang(ROPE_A)
    z32 = jnp.zeros_like(ca)
    z64 = jnp.zeros((SEQ, 64), F32)
    tab_c = jnp.concatenate([ca, ca, z64], axis=-1)
    tab_s1 = jnp.concatenate([-sa, z32, z64], axis=-1)
    tab_s2 = jnp.concatenate([z32, sa, z64], axis=-1)
    cb, sb = ang(HD_B)
    tab_cb = jnp.concatenate([cb, cb], axis=-1)
    tab_sb = jnp.concatenate([-sb, sb], axis=-1)
    ones = jnp.ones((TM_PROJ, 128), F32)
    zeros = jnp.zeros((TM_PROJ, 128), F32)
    return tuple(jnp.concatenate([tab, ident], axis=0)
                 for tab, ident in ((tab_c, ones), (tab_s1, zeros), (tab_s2, zeros),
                                    (tab_cb, ones), (tab_sb, zeros)))


def _dft_mats(n):
    j = jnp.arange(n, dtype=jnp.int32)
    a = ((j[:, None] * j[None, :]) % n).astype(F32) * (2.0 * math.pi / n)
    return jnp.cos(a), jnp.sin(a)


def _dft4_tables(n):
    nq = n // 4
    c_q, s_q = _dft_mats(nq)
    k = jnp.arange(nq, dtype=jnp.int32)[:, None]
    ang = (k * jnp.arange(1, 4, dtype=jnp.int32)[None, :]).astype(F32) * (2.0 * math.pi / n)
    tw = jnp.stack([jnp.cos(ang), jnp.sin(ang)], axis=-1).reshape(nq, 6)
    return c_q.astype(BF16), s_q.astype(BF16), _pad_to(tw, 1, 8)


def _pad_to(a, axis, size):
    pad = [(0, 0)] * a.ndim
    pad[axis] = (0, size - a.shape[axis])
    return jnp.pad(a, pad)


def _prep_weights(p):
    w_uq = p["w_uq"].reshape(DEPTH, Q_LORA, HEADS, QK_A)
    w_uq = _pad_to(_pad_to(w_uq, 3, QK_A_PAD), 1, Q_LORA_PAD).reshape(DEPTH, Q_LORA_PAD, HEADS * QK_A_PAD)
    w_ukv = p["w_ukv"].reshape(DEPTH, KV_LORA, HEADS, NOPE + V_A)
    w_ukv = jnp.concatenate([w_ukv[..., :NOPE].reshape(DEPTH, KV_LORA, HEADS * NOPE),
                             w_ukv[..., NOPE:].reshape(DEPTH, KV_LORA, HEADS * V_A)], axis=2)
    g_ka = p["g_ka"]

    return {
        "w_in": p["w_in"].astype(BF16),
        "g_cq": _pad_to(p["g_cq"], 1, Q_LORA_PAD)[:, None, :],
        "w_uq": w_uq.astype(BF16),
        "g_ckv": p["g_ckv"][:, None, :],
        "w_ukv": w_ukv.astype(BF16),
        "g_qa": _pad_to(p["g_qa"], 1, QK_A_PAD)[:, None, :],
        "g_ka_nope": g_ka[:, None, :NOPE],
        "g_ka_pe": _pad_to(g_ka[:, NOPE:], 1, 128)[:, None, :],
        "g_qb": p["g_qb"][:, None, :],
        "g_kb": p["g_kb"][:, None, :],
        "conv_w": p["conv_w"],
        "conv_b": p["conv_b"][:, None, :],
        "w_gate": p["w_gate"].astype(BF16),
        "b_gate": p["b_gate"][:, :, None, :],
        "w_branch": p["w_branch"].astype(BF16),
        "w_o": p["w_o"].astype(BF16),
        "norm_ffn1": p["norm_ffn1"][:, None, :],
        "norm_mix": p["norm_mix"][:, None, :],
        "norm_ffn2": p["norm_ffn2"][:, None, :],
        "ffn1_wi": p["ffn1_wi"].astype(BF16), "ffn1_wo": p["ffn1_wo"].astype(BF16),
        "ffn2_wi": p["ffn2_wi"].astype(BF16), "ffn2_wo": p["ffn2_wo"].astype(BF16),
    }


def kernel(x, c, ctx, c_ctx, w_ada, b_ada, norm_ffn1, ffn1_wi, ffn1_wo, norm_mix, w_in, g_cq, w_uq, g_ckv, w_ukv, g_qa, g_ka, g_qb, g_kb, conv_w, conv_b, w_branch, w_gate, b_gate, w_o, norm_ffn2, ffn2_wi, ffn2_wo):
    p = dict(norm_ffn1=norm_ffn1, ffn1_wi=ffn1_wi, ffn1_wo=ffn1_wo, norm_mix=norm_mix, w_in=w_in,
             g_cq=g_cq, w_uq=w_uq, g_ckv=g_ckv, w_ukv=w_ukv, g_qa=g_qa, g_ka=g_ka, g_qb=g_qb,
             g_kb=g_kb, conv_w=conv_w, conv_b=conv_b, w_branch=w_branch, w_gate=w_gate,
             b_gate=b_gate, w_o=w_o, norm_ffn2=norm_ffn2, ffn2_wi=ffn2_wi, ffn2_wo=ffn2_wo)

    cond8 = jnp.concatenate([c, c_ctx[None, :], jnp.zeros((8 - BATCH - 1, D), F32)], axis=0)
    mod = _ada(cond8, w_ada, b_ada).reshape(DEPTH, 8, N_MOD, 1, D)

    tabs = _rope_tables()
    cc, sc = _dft_mats(FGD)
    ccs = jnp.concatenate([cc, sc], axis=1).astype(BF16)
    dft_lat = _dft4_tables(SEQ)
    dft_ctx = _dft4_tables(CTX)

    lw = _prep_weights(p)
    h = jnp.concatenate([x.reshape(T_LAT, D), ctx.reshape(T_CTX, D)], axis=0)
    for l in range(DEPTH):
        last = l == DEPTH - 1
        h = _ffn(h, mod, l, 0, lw["norm_ffn1"], lw["ffn1_wi"], lw["ffn1_wo"], T_ALL)
        qa, ka, va, qb, kb, vb, cbo, u, pc, ps = _proj(h, mod, l, lw, tabs, ccs)
        oa, oa_c = _attention(qa, ka, va, QK_A_PAD, V_A, 1, not last)
        ob, ob_c = _attention(qb, kb, vb, HD_B, HD_B, GROUP_B, not last, 512)
        of, of_c = _fourier(pc, ps, dft_lat, dft_ctx, not last)
        n_rows = T_LAT if last else T_ALL
        h = _merge(h, mod, l, (oa, ob, of), None if last else (oa_c, ob_c, of_c), cbo, u, lw, n_rows)
        h = _ffn(h, mod, l, 6, lw["norm_ffn2"], lw["ffn2_wi"], lw["ffn2_wo"], n_rows)
    return h.reshape(BATCH, SEQ, D)
```

```python
import functools
import math

import jax
import jax.numpy as jnp
from jax import lax
from jax.experimental import pallas as pl
from jax.experimental.pallas import tpu as pltpu

F32 = jnp.float32
BF16 = jnp.bfloat16

D = 2048
BATCH = 4
SEQ = 4096
DEPTH = 4
GRID_W = 64
CTX = 256
N_MOD = 9
F_FF = 5504
HEADS = 4
Q_LORA = 448
Q_LORA_PAD = 512
KV_LORA = 128
NOPE = 128
ROPE_A = 64
QK_A = NOPE + ROPE_A
QK_A_PAD = 256
V_A = 128
HD_B = 128
KVH_B = 2
GROUP_B = 2
CONV_W = 512
FG = 4
FGD = 128
BR_W = 512
N_BR = 4
THETA = 10000.0
EPS = 1e-6
LOG2_E = math.log2(math.e)

T_LAT = BATCH * SEQ
T_CTX = BATCH * CTX
T_ALL = T_LAT + T_CTX

W_IN = 3712
OFF_A = 0
OFF_B = 640
OFF_C = 1664
OFF_F = 3200

LANE = 128
VMEM_PHYS_V7X = 64 * 1024 * 1024
VMEM_LIMIT = 60 * 1024 * 1024

TM_FFN = 1024
TF_FFN = 512
TM_PROJ = 512
TM_MERGE = 512
TN_MERGE = 512
TQ_ATT = 1024
KC_ATT = 1024
TK_DFT = 512
TN_ADA = 1024


def _cparams(sem):
    return pltpu.CompilerParams(dimension_semantics=sem, vmem_limit_bytes=VMEM_LIMIT)


def _mod_row(i, tiles_per_batch):
    return jnp.minimum(i // tiles_per_batch, BATCH)


def _row_rsqrt(x):
    return lax.rsqrt(jnp.mean(x * x, axis=-1, keepdims=True) + EPS)


def _scale_mod(x, r, gain, shift, scale):
    return (x * r * gain) * (1.0 + scale) + shift


def _norm_mod(x, gain, shift, scale):
    return _scale_mod(x, _row_rsqrt(x), gain, shift, scale)


def _silu(x):
    return x * jax.nn.sigmoid(x)


def _ada_kernel(cond_ref, w_ref, b_ref, o_ref):
    s = _silu(cond_ref[...]).astype(BF16)
    o_ref[...] = jnp.dot(s, w_ref[...].astype(BF16), preferred_element_type=F32) + b_ref[...]


def _ada(cond8, w_ada, b_ada):
    n = N_MOD * D
    return pl.pallas_call(
        _ada_kernel,
        grid=(DEPTH, n // TN_ADA),
        in_specs=[
            pl.BlockSpec((8, D), lambda l, j: (0, 0)),
            pl.BlockSpec((None, D, TN_ADA), lambda l, j: (l, 0, j)),
            pl.BlockSpec((None, 1, TN_ADA), lambda l, j: (l, 0, j)),
        ],
        out_specs=pl.BlockSpec((None, 8, TN_ADA), lambda l, j: (l, 0, j)),
        out_shape=jax.ShapeDtypeStruct((DEPTH, 8, n), F32),
        compiler_params=_cparams(("arbitrary", "arbitrary")),
        name="ada",
    )(cond8, w_ada, b_ada.reshape(DEPTH, 1, n))


def _mod_spec(layer, j, tiles_per_batch):
    return pl.BlockSpec((None, None, None, 1, D),
                        lambda i, *_: (layer, _mod_row(i, tiles_per_batch), j, 0, 0))


def _ffn_kernel(overlap, x_ref, gain_ref, sh_ref, sc_ref, gt_ref, wg_ref, wu_ref, wo_ref, o_ref, xn_ref):
    f = pl.program_id(1)
    last = pl.num_programs(1) - 1

    def chunk(skip=0, first=False):
        if first:
            xn = _norm_mod(x_ref[...], gain_ref[...], sh_ref[...], sc_ref[...]).astype(BF16)
            xn_ref[...] = xn
        else:
            xn = xn_ref[...]
        g = jnp.dot(xn, wg_ref[:, skip:], preferred_element_type=F32)
        u = jnp.dot(xn, wu_ref[:, skip:], preferred_element_type=F32)
        a = (_silu(g) * u).astype(BF16)
        return jnp.dot(a, wo_ref[skip:, :], preferred_element_type=F32)

    @pl.when(f == 0)
    def _():
        o_ref[...] = chunk(first=True)

    @pl.when(jnp.logical_and(f > 0, f < last))
    def _():
        o_ref[...] += chunk()

    @pl.when(f == last)
    def _():
        o_ref[...] = x_ref[...] + (0.5 * gt_ref[...]) * (o_ref[...] + chunk(overlap))


def _ffn(h, mod, layer, j0, gain, wi, wo, n_rows):
    tm, tf = TM_FFN, TF_FFN
    tpb = SEQ // tm
    nf = pl.cdiv(F_FF, tf)

    def hid(f, base=0):
        return (base // LANE + jnp.minimum(f * (tf // LANE), (F_FF - tf) // LANE)) * LANE

    return pl.pallas_call(
        functools.partial(_ffn_kernel, nf * tf - F_FF),
        grid=(n_rows // tm, nf),
        in_specs=[
            pl.BlockSpec((tm, D), lambda i, f: (i, 0)),
            pl.BlockSpec((None, 1, D), lambda i, f: (layer, 0, 0)),
            _mod_spec(layer, j0, tpb),
            _mod_spec(layer, j0 + 1, tpb),
            _mod_spec(layer, j0 + 2, tpb),
            pl.BlockSpec((None, pl.Element(D), pl.Element(tf)), lambda i, f: (layer, 0, hid(f))),
            pl.BlockSpec((None, pl.Element(D), pl.Element(tf)), lambda i, f: (layer, 0, hid(f, F_FF))),
            pl.BlockSpec((None, pl.Element(tf), pl.Element(D)), lambda i, f: (layer, hid(f), 0)),
        ],
        out_specs=pl.BlockSpec((tm, D), lambda i, f: (i, 0)),
        out_shape=jax.ShapeDtypeStruct((n_rows, D), F32),
        scratch_shapes=[pltpu.VMEM((tm, D), BF16)],
        compiler_params=_cparams(("arbitrary", "arbitrary")),
        name="ffn",
    )(h, gain, mod, mod, mod, wi, wi, wo)


def _rope_a(v, c, s1, s2):
    return v * c + pltpu.roll(v, 96, 1) * s1 + pltpu.roll(v, 32, 1) * s2


def _rope_b(v, c, s):
    return v * c + pltpu.roll(v, 64, 1) * s


def _proj_kernel(h_ref, gain_ref, sh_ref, sc_ref, w_ref, gcq_ref, wuq_ref, gckv_ref, wukv_ref,
                 gqa_ref, gkan_ref, gkap_ref, gqb_ref, gkb_ref, ca_ref, s1a_ref, s2a_ref,
                 cb_ref, sb_ref, ccs_ref,
                 qa_ref, ka_ref, va_ref, qb_ref, kb_ref, vb_ref, cbo_ref, u_ref, pc_ref, ps_ref,
                 pp_ref):
    xn = _norm_mod(h_ref[...], gain_ref[...], sh_ref[...], sc_ref[...]).astype(BF16)
    ca, s1a, s2a = ca_ref[...], s1a_ref[...], s2a_ref[...]
    cb, sb = cb_ref[...], sb_ref[...]
    scale_a = QK_A ** -0.5 * LOG2_E
    scale_b = HD_B ** -0.5 * LOG2_E

    za = jnp.dot(xn, w_ref[:, OFF_A:OFF_B], preferred_element_type=F32)
    zb = jnp.dot(xn, w_ref[:, OFF_B:OFF_C], preferred_element_type=F32)

    hi = lax.broadcasted_iota(jnp.int32, (1, LANE), 1) >= LANE // 2
    s3, s4 = za[:, 3 * LANE:4 * LANE], za[:, 4 * LANE:]
    aq = jnp.concatenate([za[:, :3 * LANE], jnp.where(hi, 0.0, s3)], axis=1)
    akv = pltpu.roll(jnp.where(hi, s3, s4), LANE // 2, 1)
    ape = jnp.where(hi, 0.0, pltpu.roll(s4, LANE // 2, 1))
    r = lax.rsqrt(jnp.sum(aq * aq, axis=-1, keepdims=True) * (1.0 / Q_LORA) + EPS)
    cq = (aq * r * gcq_ref[...]).astype(BF16)
    q = jnp.dot(cq, wuq_ref[...], preferred_element_type=F32)
    gqa = gqa_ref[...]
    for hd in range(HEADS):
        lo = hd * QK_A_PAD
        qh = q[:, lo:lo + QK_A_PAD]
        r = lax.rsqrt(jnp.sum(qh * qh, axis=-1, keepdims=True) * (1.0 / QK_A) + EPS)
        qh = qh * r * gqa
        qa_ref[:, lo:lo + NOPE] = (qh[:, :NOPE] * scale_a).astype(BF16)
        qa_ref[:, lo + NOPE:lo + QK_A_PAD] = (_rope_a(qh[:, NOPE:], ca, s1a, s2a) * scale_a).astype(BF16)
    r = lax.rsqrt(jnp.mean(akv * akv, axis=-1, keepdims=True) + EPS)
    ckv = (akv * r * gckv_ref[...]).astype(BF16)
    kv = jnp.dot(ckv, wukv_ref[...], preferred_element_type=F32)
    ss_pe = jnp.sum(ape * ape, axis=-1, keepdims=True)
    ones_blk = jnp.ones((h_ref.shape[0], LANE), BF16)
    gkan, gkap = gkan_ref[...], gkap_ref[...]
    for hd in range(HEADS):
        kn = kv[:, hd * NOPE:(hd + 1) * NOPE]
        r = lax.rsqrt((jnp.sum(kn * kn, axis=-1, keepdims=True) + ss_pe) * (1.0 / QK_A) + EPS)
        lo = hd * QK_A_PAD
        ka_ref[:, lo:lo + NOPE] = (kn * r * gkan).astype(BF16)
        ka_ref[:, lo + NOPE:lo + QK_A_PAD] = _rope_a(ape * r * gkap, ca, s1a, s2a).astype(BF16)
        va_ref[:, 2 * hd * V_A:(2 * hd + 1) * V_A] = kv[:, HEADS * NOPE + hd * V_A:HEADS * NOPE + (hd + 1) * V_A].astype(BF16)
        va_ref[:, (2 * hd + 1) * V_A:(2 * hd + 2) * V_A] = ones_blk

    zc = jnp.dot(xn, w_ref[:, OFF_C:OFF_F], preferred_element_type=F32)
    gqb, gkb = gqb_ref[...], gkb_ref[...]
    for hd in range(HEADS):
        qh = zb[:, hd * HD_B:(hd + 1) * HD_B]
        r = lax.rsqrt(jnp.mean(qh * qh, axis=-1, keepdims=True) + EPS)
        qb_ref[:, hd * HD_B:(hd + 1) * HD_B] = (_rope_b(qh * r * gqb, cb, sb) * scale_b).astype(BF16)
    for hd in range(KVH_B):
        lo = HEADS * HD_B + hd * HD_B
        kh = zb[:, lo:lo + HD_B]
        r = lax.rsqrt(jnp.mean(kh * kh, axis=-1, keepdims=True) + EPS)
        kb_ref[:, hd * HD_B:(hd + 1) * HD_B] = _rope_b(kh * r * gkb, cb, sb).astype(BF16)
    for hd in range(KVH_B):
        lo = (HEADS + KVH_B + hd) * HD_B
        vb_ref[:, 2 * hd * HD_B:(2 * hd + 1) * HD_B] = zb[:, lo:lo + HD_B].astype(BF16)
        vb_ref[:, (2 * hd + 1) * HD_B:(2 * hd + 2) * HD_B] = ones_blk

    zf = jnp.dot(xn, w_ref[:, OFF_F:], preferred_element_type=F32).astype(BF16)
    cbo_ref[...] = zc[:, :CONV_W]
    u_ref[...] = zc[:, CONV_W:2 * CONV_W] * zc[:, 2 * CONV_W:]

    ccs = ccs_ref[...]
    tm = h_ref.shape[0]
    w = FG * FGD
    for g in range(FG):
        pp = jnp.dot(zf[:, g * FGD:(g + 1) * FGD], ccs, preferred_element_type=F32)
        pp_ref[g] = pp[:, :FGD]
        pp_ref[FG + g] = pp[:, FGD:]
    for r in range(4):
        for g in range(FG):
            lo = r * w + g * FGD
            pc_ref[:, lo:lo + FGD] = pp_ref[g, pl.ds(r, tm // 4, stride=4), :].astype(BF16)
            ps_ref[:, lo:lo + FGD] = pp_ref[FG + g, pl.ds(r, tm // 4, stride=4), :].astype(BF16)


def _const_spec(shape, layer=None):
    nd = len(shape)
    if layer is None:
        return pl.BlockSpec(shape, lambda i, *_: (0,) * nd, pipeline_mode=pl.Buffered(1))
    return pl.BlockSpec((None,) + shape, lambda i, *_: (layer,) + (0,) * nd, pipeline_mode=pl.Buffered(1))


def _proj(h, mod, layer, lw, tabs, ccs):
    tm = TM_PROJ
    tpb = SEQ // tm
    n_lat = T_LAT // tm

    def tab_spec():
        return pl.BlockSpec((tm, 128), lambda i: (jnp.where(i < n_lat, i % tpb, tpb), 0))

    def row_spec(w):
        return pl.BlockSpec((tm, w), lambda i: (i, 0))

    widths = (HEADS * QK_A_PAD, HEADS * QK_A_PAD, HEADS * 2 * V_A, HEADS * HD_B, KVH_B * HD_B,
              KVH_B * 2 * HD_B, CONV_W, CONV_W)
    dtypes = (BF16, BF16, BF16, BF16, BF16, BF16, F32, F32)
    dft_spec = pl.BlockSpec((tm // 4, 4 * FG * FGD), lambda i: (i, 0))
    dft_shape = jax.ShapeDtypeStruct((T_ALL // 4, 4 * FG * FGD), BF16)
    return pl.pallas_call(
        _proj_kernel,
        grid=(T_ALL // tm,),
        in_specs=[
            row_spec(D),
            _const_spec((1, D), layer),
            _mod_spec(layer, 3, tpb),
            _mod_spec(layer, 4, tpb),
            _const_spec((D, W_IN), layer),
            _const_spec((1, Q_LORA_PAD), layer),
            _const_spec((Q_LORA_PAD, HEADS * QK_A_PAD), layer),
            _const_spec((1, KV_LORA), layer),
            _const_spec((KV_LORA, HEADS * (NOPE + V_A)), layer),
            _const_spec((1, QK_A_PAD), layer),
            _const_spec((1, NOPE), layer),
            _const_spec((1, 128), layer),
            _const_spec((1, HD_B), layer),
            _const_spec((1, HD_B), layer),
            tab_spec(), tab_spec(), tab_spec(), tab_spec(), tab_spec(),
            _const_spec((FGD, 2 * FGD)),
        ],
        out_specs=[row_spec(w) for w in widths] + [dft_spec, dft_spec],
        out_shape=[jax.ShapeDtypeStruct((T_ALL, w), dt) for w, dt in zip(widths, dtypes)]
        + [dft_shape, dft_shape],
        scratch_shapes=[pltpu.VMEM((2 * FG, tm, FGD), F32)],
        compiler_params=_cparams(("arbitrary",)),
        name="proj",
    )(h, lw["norm_mix"], mod, mod, lw["w_in"], lw["g_cq"], lw["w_uq"], lw["g_ckv"], lw["w_ukv"],
      lw["g_qa"], lw["g_ka_nope"], lw["g_ka_pe"], lw["g_qb"], lw["g_kb"], *tabs, ccs)


_QK_DIMS = (((1,), (1,)), ((), ()))


def _att_lat_kernel(q_ref, k_ref, kc_ref, v_ref, vc_ref, o_ref):
    q = q_ref[...]
    chunks = [(k_ref, v_ref, j * KC_ATT, KC_ATT) for j in range(SEQ // KC_ATT)] + [(kc_ref, vc_ref, 0, CTX)]

    def scores(chunk):
        kr, _, lo, n = chunk
        return lax.dot_general(q, kr[lo:lo + n, :], _QK_DIMS, preferred_element_type=F32)

    dv = o_ref.shape[-1]
    s = scores(chunks[0])
    m = acc = None
    for idx, (_, vr, lo, n) in enumerate(chunks):
        s_next = scores(chunks[idx + 1]) if idx + 1 < len(chunks) else None
        m_chunk = jnp.max(s, axis=-1, keepdims=True)
        m_new = m_chunk if m is None else jnp.maximum(m, m_chunk)
        p = _exp2_bf16(s - m_new)
        pv = jnp.dot(p, vr[lo:lo + n, :], preferred_element_type=F32)
        acc = pv if m is None else jnp.exp2(m - m_new) * acc + pv
        m, s = m_new, s_next
    o_ref[...] = (acc[:, :dv] / acc[:, dv:]).astype(o_ref.dtype)


def _exp2_bf16(x):
    return jnp.exp2(x.astype(BF16))


def _att_ctx_kernel(group, q_ref, k_ref, v_ref, o_ref):
    dv = o_ref.shape[-1] // HEADS
    dq = q_ref.shape[-1] // HEADS
    for hd in range(HEADS):
        kv = hd // group
        s = lax.dot_general(q_ref[:, hd * dq:(hd + 1) * dq], k_ref[:, kv * dq:(kv + 1) * dq], _QK_DIMS,
                            preferred_element_type=F32)
        p = _exp2_bf16(s - jnp.max(s, axis=-1, keepdims=True))
        o = jnp.dot(p, v_ref[:, 2 * kv * dv:2 * (kv + 1) * dv], preferred_element_type=F32)
        o_ref[:, hd * dv:(hd + 1) * dv] = (o[:, :dv] / o[:, dv:]).astype(o_ref.dtype)


def _attention(q, k, v, dq, dv, group, with_ctx):
    tq = TQ_ATT
    nq = SEQ // tq
    ctx0 = T_LAT // CTX
    out = pl.pallas_call(
        _att_lat_kernel,
        grid=(BATCH, HEADS, nq),
        in_specs=[
            pl.BlockSpec((tq, dq), lambda b, h, i: (b * nq + i, h)),
            pl.BlockSpec((SEQ, dq), lambda b, h, i: (b, h // group)),
            pl.BlockSpec((CTX, dq), lambda b, h, i: (ctx0 + b, h // group)),
            pl.BlockSpec((SEQ, 2 * dv), lambda b, h, i: (b, h // group)),
            pl.BlockSpec((CTX, 2 * dv), lambda b, h, i: (ctx0 + b, h // group)),
        ],
        out_specs=pl.BlockSpec((tq, dv), lambda b, h, i: (b * nq + i, h)),
        out_shape=jax.ShapeDtypeStruct((T_LAT, HEADS * dv), BF16),
        compiler_params=_cparams(("arbitrary", "arbitrary", "arbitrary")),
        name="att_lat",
    )(q, k, k, v, v)
    if not with_ctx:
        return out, None
    kvh = HEADS // group
    out_ctx = pl.pallas_call(
        functools.partial(_att_ctx_kernel, group),
        grid=(BATCH,),
        in_specs=[
            pl.BlockSpec((CTX, HEADS * dq), lambda b: (ctx0 + b, 0)),
            pl.BlockSpec((CTX, kvh * dq), lambda b: (ctx0 + b, 0)),
            pl.BlockSpec((CTX, kvh * 2 * dv), lambda b: (ctx0 + b, 0)),
        ],
        out_specs=pl.BlockSpec((CTX, HEADS * dv), lambda b: (b, 0)),
        out_shape=jax.ShapeDtypeStruct((T_CTX, HEADS * dv), BF16),
        compiler_params=_cparams(("arbitrary",)),
        name="att_ctx",
    )(q, k, v)
    return out, out_ctx


def _dft4_kernel(scale, c_ref, s_ref, tw_ref, pc_ref, ps_ref, o_ref):
    w = o_ref.shape[-1]
    c, s = c_ref[...], s_ref[...]
    pc, ps = pc_ref[...], ps_ref[...]
    ur = jnp.dot(c, pc, preferred_element_type=F32) - jnp.dot(s, ps, preferred_element_type=F32)
    ui = -(jnp.dot(c, ps, preferred_element_type=F32) + jnp.dot(s, pc, preferred_element_type=F32))
    a = [ur[:, :w]]
    b = [None]
    for r in range(1, 4):
        u_r, v_r = ur[:, r * w:(r + 1) * w], ui[:, r * w:(r + 1) * w]
        cr, sr = tw_ref[:, 2 * r - 2:2 * r - 1], tw_ref[:, 2 * r - 1:2 * r]
        a.append(u_r * cr + v_r * sr)
        b.append(None if r == 2 else v_r * cr - u_r * sr)
    o_ref[0] = ((a[0] + a[1] + a[2] + a[3]) * scale).astype(o_ref.dtype)
    o_ref[1] = ((a[0] + b[1] - a[2] - b[3]) * scale).astype(o_ref.dtype)
    o_ref[2] = ((a[0] - a[1] + a[2] - a[3]) * scale).astype(o_ref.dtype)
    o_ref[3] = ((a[0] - b[1] - a[2] + b[3]) * scale).astype(o_ref.dtype)


def _dft4(pc4, ps4, tables, n, tk, first_block, name):
    w = FG * FGD
    nq = n // 4
    c_q, s_q, tw = tables
    return pl.pallas_call(
        functools.partial(_dft4_kernel, (n * FGD) ** -0.5),
        grid=(BATCH, nq // tk),
        in_specs=[
            pl.BlockSpec((tk, nq), lambda b, k: (k, 0)),
            pl.BlockSpec((tk, nq), lambda b, k: (k, 0)),
            pl.BlockSpec((tk, 8), lambda b, k: (k, 0)),
            pl.BlockSpec((nq, 4 * w), lambda b, k: (first_block + b, 0)),
            pl.BlockSpec((nq, 4 * w), lambda b, k: (first_block + b, 0)),
        ],
        out_specs=pl.BlockSpec((None, 4, tk, w), lambda b, k: (b, 0, k, 0)),
        out_shape=jax.ShapeDtypeStruct((BATCH, 4, nq, w), BF16),
        compiler_params=_cparams(("arbitrary", "arbitrary")),
        name=name,
    )(c_q, s_q, tw, pc4, ps4).reshape(BATCH * n, w)


def _fourier(pc4, ps4, dft_lat, dft_ctx, with_ctx):
    out = _dft4(pc4, ps4, dft_lat, SEQ, TK_DFT, 0, "dft_lat")
    if not with_ctx:
        return out, None
    return out, _dft4(pc4, ps4, dft_ctx, CTX, CTX // 4, T_LAT // CTX, "dft_ctx")


def _merge_kernel(n_lat_tiles, with_ctx, h_ref, gain_ref, sh_ref, sc_ref, gt_ref, *refs):
    lat_refs, refs = refs[:3], refs[3:]
    if with_ctx:
        ctx_refs, refs = refs[:3], refs[3:]
    (cbo_ref, u_ref, up_ref, un_ref, cw_ref, cbias_ref, wg_ref, bg_ref, wb_ref, wo_ref,
     o_ref, xn_ref, conv_ref) = refs
    i = pl.program_id(0)
    n = pl.program_id(1)
    last = pl.num_programs(1) - 1
    tm = h_ref.shape[0]
    is_lat = i < n_lat_tiles

    def conv_branch():
        u = u_ref[...]
        seq_len = jnp.where(is_lat, SEQ, CTX)
        ridx = lax.broadcasted_iota(jnp.int32, (tm, 1), 0)
        pos = (i * tm + ridx) & (seq_len - 1)
        prev = jnp.where(ridx == 0, up_ref[7:8, :], pltpu.roll(u, 1, 0))
        prev = jnp.where(pos == 0, 0.0, prev)
        nxt = jnp.where(ridx == tm - 1, un_ref[0:1, :], pltpu.roll(u, tm - 1, 0))
        nxt = jnp.where(pos == seq_len - 1, 0.0, nxt)
        y = prev * cw_ref[0:1, :] + u * cw_ref[1:2, :] + nxt * cw_ref[2:3, :] + cbias_ref[...]
        return (cbo_ref[...] * y).astype(BF16)

    def chunk(first=False):
        if first:
            xn = _norm_mod(h_ref[...], gain_ref[...], sh_ref[...], sc_ref[...]).astype(BF16)
            xn_ref[...] = xn
            conv = conv_branch()
            conv_ref[...] = conv
        else:
            xn = xn_ref[...]
            conv = conv_ref[...]

        if with_ctx:
            oa, ob, of = (jnp.where(is_lat, lr[...], cr[...]) for lr, cr in zip(lat_refs, ctx_refs))
        else:
            oa, ob, of = (lr[...] for lr in lat_refs)

        merged = None
        for k, br in enumerate((oa, ob, conv, of)):
            gate = jax.nn.sigmoid(jnp.dot(xn, wg_ref[k], preferred_element_type=F32) + bg_ref[k])
            term = gate * jnp.dot(br, wb_ref[k], preferred_element_type=F32)
            merged = term if merged is None else merged + term
        return jnp.dot(merged.astype(BF16), wo_ref[...], preferred_element_type=F32)

    @pl.when(n == 0)
    def _():
        o_ref[...] = chunk(first=True)

    @pl.when(jnp.logical_and(n > 0, n < last))
    def _():
        o_ref[...] += chunk()

    @pl.when(n == last)
    def _():
        o_ref[...] = h_ref[...] + gt_ref[...] * (o_ref[...] + chunk())


def _merge(h, mod, layer, br_lat, br_ctx, cbo, u, lw, n_rows):
    tm, tn = TM_MERGE, TN_MERGE
    tpb = SEQ // tm
    n_lat = T_LAT // tm
    with_ctx = br_ctx is not None
    last_blk = T_ALL // 8 - 1
    br_specs = [pl.BlockSpec((tm, BR_W), lambda i, n: (jnp.minimum(i, n_lat - 1), 0))] * 3
    if with_ctx:
        br_specs += [pl.BlockSpec((tm, BR_W), lambda i, n: (jnp.maximum(i - n_lat, 0), 0))] * 3

    def row_spec(w):
        return pl.BlockSpec((tm, w), lambda i, n: (i, 0))

    return pl.pallas_call(
        functools.partial(_merge_kernel, n_lat, with_ctx),
        grid=(n_rows // tm, D // tn),
        in_specs=[
            row_spec(D),
            pl.BlockSpec((None, 1, D), lambda i, n: (layer, 0, 0)),
            _mod_spec(layer, 3, tpb),
            _mod_spec(layer, 4, tpb),
            _mod_spec(layer, 5, tpb),
            *br_specs,
            row_spec(CONV_W), row_spec(CONV_W),
            pl.BlockSpec((8, CONV_W), lambda i, n: (jnp.maximum(i * (tm // 8) - 1, 0), 0)),
            pl.BlockSpec((8, CONV_W), lambda i, n: (jnp.minimum((i + 1) * (tm // 8), last_blk), 0)),
            pl.BlockSpec((None, 3, CONV_W), lambda i, n: (layer, 0, 0)),
            pl.BlockSpec((None, 1, CONV_W), lambda i, n: (layer, 0, 0)),
            pl.BlockSpec((None, N_BR, D, tn), lambda i, n: (layer, 0, 0, n)),
            pl.BlockSpec((None, N_BR, 1, tn), lambda i, n: (layer, 0, 0, n)),
            pl.BlockSpec((None, N_BR, BR_W, tn), lambda i, n: (layer, 0, 0, n)),
            pl.BlockSpec((None, tn, D), lambda i, n: (layer, n, 0)),
        ],
        out_specs=row_spec(D),
        out_shape=jax.ShapeDtypeStruct((n_rows, D), F32),
        scratch_shapes=[pltpu.VMEM((tm, D), BF16), pltpu.VMEM((tm, CONV_W), BF16)],
        compiler_params=_cparams(("arbitrary", "arbitrary")),
        name="merge",
    )(h, lw["norm_mix"], mod, mod, mod, *br_lat, *(br_ctx or ()), cbo, u, u, u, lw["conv_w"],
      lw["conv_b"], lw["w_gate"], lw["b_gate"], lw["w_branch"], lw["w_o"])


def _rope_tables():
    t = jnp.arange(SEQ, dtype=jnp.int32)
    pos_row = (t // GRID_W).astype(F32)
    pos_col = (t % GRID_W).astype(F32)

    def ang(rot_dim):
        n = rot_dim // 4
        inv_freq = THETA ** (-jnp.arange(n, dtype=F32) / n)
        a = jnp.concatenate([pos_row[:, None] * inv_freq, pos_col[:, None] * inv_freq], axis=-1)
        return jnp.cos(a), jnp.sin(a)

    ca, sa = ang(ROPE_A)
    z32 = jnp.zeros_like(ca)
    z64 = jnp.zeros((SEQ, 64), F32)
    tab_c = jnp.concatenate([ca, ca, z64], axis=-1)
    tab_s1 = jnp.concatenate([-sa, z32, z64], axis=-1)
    tab_s2 = jnp.concatenate([z32, sa, z64], axis=-1)
    cb, sb = ang(HD_B)
    tab_cb = jnp.concatenate([cb, cb], axis=-1)
    tab_sb = jnp.concatenate([-sb, sb], axis=-1)
    ones = jnp.ones((TM_PROJ, 128), F32)
    zeros = jnp.zeros((TM_PROJ, 128), F32)
    return tuple(jnp.concatenate([tab, ident], axis=0)
                 for tab, ident in ((tab_c, ones), (tab_s1, zeros), (tab_s2, zeros),
                                    (tab_cb, ones), (tab_sb, zeros)))


def _dft_mats(n):
    j = jnp.arange(n, dtype=jnp.int32)
    a = ((j[:, None] * j[None, :]) % n).astype(F32) * (2.0 * math.pi / n)
    return jnp.cos(a), jnp.sin(a)


def _dft4_tables(n):
    nq = n // 4
    c_q, s_q = _dft_mats(nq)
    k = jnp.arange(nq, dtype=jnp.int32)[:, None]
    ang = (k * jnp.arange(1, 4, dtype=jnp.int32)[None, :]).astype(F32) * (2.0 * math.pi / n)
    tw = jnp.stack([jnp.cos(ang), jnp.sin(ang)], axis=-1).reshape(nq, 6)
    return c_q.astype(BF16), s_q.astype(BF16), _pad_to(tw, 1, 8)


def _pad_to(a, axis, size):
    pad = [(0, 0)] * a.ndim
    pad[axis] = (0, size - a.shape[axis])
    return jnp.pad(a, pad)


def _prep_weights(p):
    w_uq = p["w_uq"].reshape(DEPTH, Q_LORA, HEADS, QK_A)
    w_uq = _pad_to(_pad_to(w_uq, 3, QK_A_PAD), 1, Q_LORA_PAD).reshape(DEPTH, Q_LORA_PAD, HEADS * QK_A_PAD)
    w_ukv = p["w_ukv"].reshape(DEPTH, KV_LORA, HEADS, NOPE + V_A)
    w_ukv = jnp.concatenate([w_ukv[..., :NOPE].reshape(DEPTH, KV_LORA, HEADS * NOPE),
                             w_ukv[..., NOPE:].reshape(DEPTH, KV_LORA, HEADS * V_A)], axis=2)
    g_ka = p["g_ka"]

    return {
        "w_in": p["w_in"].astype(BF16),
        "g_cq": _pad_to(p["g_cq"], 1, Q_LORA_PAD)[:, None, :],
        "w_uq": w_uq.astype(BF16),
        "g_ckv": p["g_ckv"][:, None, :],
        "w_ukv": w_ukv.astype(BF16),
        "g_qa": _pad_to(p["g_qa"], 1, QK_A_PAD)[:, None, :],
        "g_ka_nope": g_ka[:, None, :NOPE],
        "g_ka_pe": _pad_to(g_ka[:, NOPE:], 1, 128)[:, None, :],
        "g_qb": p["g_qb"][:, None, :],
        "g_kb": p["g_kb"][:, None, :],
        "conv_w": p["conv_w"],
        "conv_b": p["conv_b"][:, None, :],
        "w_gate": p["w_gate"].astype(BF16),
        "b_gate": p["b_gate"][:, :, None, :],
        "w_branch": p["w_branch"].astype(BF16),
        "w_o": p["w_o"].astype(BF16),
        "norm_ffn1": p["norm_ffn1"][:, None, :],
        "norm_mix": p["norm_mix"][:, None, :],
        "norm_ffn2": p["norm_ffn2"][:, None, :],
        "ffn1_wi": p["ffn1_wi"].astype(BF16), "ffn1_wo": p["ffn1_wo"].astype(BF16),
        "ffn2_wi": p["ffn2_wi"].astype(BF16), "ffn2_wo": p["ffn2_wo"].astype(BF16),
    }


def kernel(x, c, ctx, c_ctx, w_ada, b_ada, norm_ffn1, ffn1_wi, ffn1_wo, norm_mix, w_in, g_cq, w_uq, g_ckv, w_ukv, g_qa, g_ka, g_qb, g_kb, conv_w, conv_b, w_branch, w_gate, b_gate, w_o, norm_ffn2, ffn2_wi, ffn2_wo):
    p = dict(norm_ffn1=norm_ffn1, ffn1_wi=ffn1_wi, ffn1_wo=ffn1_wo, norm_mix=norm_mix, w_in=w_in,
             g_cq=g_cq, w_uq=w_uq, g_ckv=g_ckv, w_ukv=w_ukv, g_qa=g_qa, g_ka=g_ka, g_qb=g_qb,
             g_kb=g_kb, conv_w=conv_w, conv_b=conv_b, w_branch=w_branch, w_gate=w_gate,
             b_gate=b_gate, w_o=w_o, norm_ffn2=norm_ffn2, ffn2_wi=ffn2_wi, ffn2_wo=ffn2_wo)

    cond8 = jnp.concatenate([c, c_ctx[None, :], jnp.zeros((8 - BATCH - 1, D), F32)], axis=0)
    mod = _ada(cond8, w_ada, b_ada).reshape(DEPTH, 8, N_MOD, 1, D)

    tabs = _rope_tables()
    cc, sc = _dft_mats(FGD)
    ccs = jnp.concatenate([cc, sc], axis=1).astype(BF16)
    dft_lat = _dft4_tables(SEQ)
    dft_ctx = _dft4_tables(CTX)

    lw = _prep_weights(p)
    h = jnp.concatenate([x.reshape(T_LAT, D), ctx.reshape(T_CTX, D)], axis=0)
    for l in range(DEPTH):
        last = l == DEPTH - 1
        h = _ffn(h, mod, l, 0, lw["norm_ffn1"], lw["ffn1_wi"], lw["ffn1_wo"], T_ALL)
        qa, ka, va, qb, kb, vb, cbo, u, pc, ps = _proj(h, mod, l, lw, tabs, ccs)
        oa, oa_c = _attention(qa, ka, va, QK_A_PAD, V_A, 1, not last)
        ob, ob_c = _attention(qb, kb, vb, HD_B, HD_B, GROUP_B, not last)
        of, of_c = _fourier(pc, ps, dft_lat, dft_ctx, not last)
        n_rows = T_LAT if last else T_ALL
        h = _merge(h, mod, l, (oa, ob, of), None if last else (oa_c, ob_c, of_c), cbo, u, lw, n_rows)
        h = _ffn(h, mod, l, 6, lw["norm_ffn2"], lw["ffn2_wi"], lw["ffn2_wo"], n_rows)
    return h.reshape(BATCH, SEQ, D)
```

```python
import functools
import math

import jax
import jax.numpy as jnp
from jax import lax
from jax.experimental import pallas as pl
from jax.experimental.pallas import tpu as pltpu

F32 = jnp.float32
BF16 = jnp.bfloat16

D = 2048
BATCH = 4
SEQ = 4096
DEPTH = 4
GRID_W = 64
CTX = 256
N_MOD = 9
F_FF = 5504
HEADS = 4
Q_LORA = 448
Q_LORA_PAD = 512
KV_LORA = 128
NOPE = 128
ROPE_A = 64
QK_A = NOPE + ROPE_A
QK_A_PAD = 256
V_A = 128
HD_B = 128
KVH_B = 2
GROUP_B = 2
CONV_W = 512
FG = 4
FGD = 128
BR_W = 512
N_BR = 4
THETA = 10000.0
EPS = 1e-6
LOG2_E = math.log2(math.e)

T_LAT = BATCH * SEQ
T_CTX = BATCH * CTX
T_ALL = T_LAT + T_CTX

W_IN = 3712
OFF_A = 0
OFF_B = 640
OFF_C = 1664
OFF_F = 3200

LANE = 128
VMEM_PHYS_V7X = 64 * 1024 * 1024
VMEM_LIMIT = VMEM_PHYS_V7X - 4 * 1024 * 1024

TM_FFN = 1024
TF_FFN = 512
TM_PROJ = 512
TM_MERGE = 512
TN_MERGE = 512
TQ_ATT = 1024
KC_ATT = 2048
TK_DFT = 512
TN_ADA = 1024


def _cparams(sem):
    return pltpu.CompilerParams(dimension_semantics=sem, vmem_limit_bytes=VMEM_LIMIT)


def _mod_row(i, tiles_per_batch):
    return jnp.minimum(i // tiles_per_batch, BATCH)


def _row_rsqrt(x):
    return lax.rsqrt(jnp.mean(x * x, axis=-1, keepdims=True) + EPS)


def _scale_mod(x, r, gain, shift, scale):
    return (x * r * gain) * (1.0 + scale) + shift


def _norm_mod(x, gain, shift, scale):
    return _scale_mod(x, _row_rsqrt(x), gain, shift, scale)


def _silu(x):
    return x * jax.nn.sigmoid(x)


def _ada_kernel(cond_ref, w_ref, b_ref, o_ref):
    s = _silu(cond_ref[...]).astype(BF16)
    o_ref[...] = jnp.dot(s, w_ref[...].astype(BF16), preferred_element_type=F32) + b_ref[...]


def _ada(cond8, w_ada, b_ada):
    n = N_MOD * D
    return pl.pallas_call(
        _ada_kernel,
        grid=(DEPTH, n // TN_ADA),
        in_specs=[
            pl.BlockSpec((8, D), lambda l, j: (0, 0)),
            pl.BlockSpec((None, D, TN_ADA), lambda l, j: (l, 0, j)),
            pl.BlockSpec((None, 1, TN_ADA), lambda l, j: (l, 0, j)),
        ],
        out_specs=pl.BlockSpec((None, 8, TN_ADA), lambda l, j: (l, 0, j)),
        out_shape=jax.ShapeDtypeStruct((DEPTH, 8, n), F32),
        compiler_params=_cparams(("arbitrary", "arbitrary")),
        name="ada",
    )(cond8, w_ada, b_ada.reshape(DEPTH, 1, n))


def _mod_spec(layer, j, tiles_per_batch):
    return pl.BlockSpec((None, None, None, 1, D),
                        lambda i, *_: (layer, _mod_row(i, tiles_per_batch), j, 0, 0))


def _ffn_kernel(overlap, x_ref, gain_ref, sh_ref, sc_ref, gt_ref, wg_ref, wu_ref, wo_ref, o_ref, xn_ref):
    f = pl.program_id(1)
    last = pl.num_programs(1) - 1

    def chunk(skip=0, first=False):
        if first:
            xn = _norm_mod(x_ref[...], gain_ref[...], sh_ref[...], sc_ref[...]).astype(BF16)
            xn_ref[...] = xn
        else:
            xn = xn_ref[...]
        g = jnp.dot(xn, wg_ref[:, skip:], preferred_element_type=F32)
        u = jnp.dot(xn, wu_ref[:, skip:], preferred_element_type=F32)
        a = (_silu(g) * u).astype(BF16)
        return jnp.dot(a, wo_ref[skip:, :], preferred_element_type=F32)

    @pl.when(f == 0)
    def _():
        o_ref[...] = chunk(first=True)

    @pl.when(jnp.logical_and(f > 0, f < last))
    def _():
        o_ref[...] += chunk()

    @pl.when(f == last)
    def _():
        o_ref[...] = x_ref[...] + (0.5 * gt_ref[...]) * (o_ref[...] + chunk(overlap))


def _ffn(h, mod, layer, j0, gain, wi, wo, n_rows):
    tm, tf = TM_FFN, TF_FFN
    tpb = SEQ // tm
    nf = pl.cdiv(F_FF, tf)

    def hid(f, base=0):
        return (base // LANE + jnp.minimum(f * (tf // LANE), (F_FF - tf) // LANE)) * LANE

    return pl.pallas_call(
        functools.partial(_ffn_kernel, nf * tf - F_FF),
        grid=(n_rows // tm, nf),
        in_specs=[
            pl.BlockSpec((tm, D), lambda i, f: (i, 0)),
            pl.BlockSpec((None, 1, D), lambda i, f: (layer, 0, 0)),
            _mod_spec(layer, j0, tpb),
            _mod_spec(layer, j0 + 1, tpb),
            _mod_spec(layer, j0 + 2, tpb),
            pl.BlockSpec((None, pl.Element(D), pl.Element(tf)), lambda i, f: (layer, 0, hid(f))),
            pl.BlockSpec((None, pl.Element(D), pl.Element(tf)), lambda i, f: (layer, 0, hid(f, F_FF))),
            pl.BlockSpec((None, pl.Element(tf), pl.Element(D)), lambda i, f: (layer, hid(f), 0)),
        ],
        out_specs=pl.BlockSpec((tm, D), lambda i, f: (i, 0)),
        out_shape=jax.ShapeDtypeStruct((n_rows, D), F32),
        scratch_shapes=[pltpu.VMEM((tm, D), BF16)],
        compiler_params=_cparams(("arbitrary", "arbitrary")),
        name="ffn",
    )(h, gain, mod, mod, mod, wi, wi, wo)


def _rope_a(v, c, s1, s2):
    return v * c + pltpu.roll(v, 96, 1) * s1 + pltpu.roll(v, 32, 1) * s2


def _rope_b(v, c, s):
    return v * c + pltpu.roll(v, 64, 1) * s


def _proj_kernel(h_ref, gain_ref, sh_ref, sc_ref, w_ref, gcq_ref, wuq_ref, gckv_ref, wukv_ref,
                 gqa_ref, gkan_ref, gkap_ref, gqb_ref, gkb_ref, ca_ref, s1a_ref, s2a_ref,
                 cb_ref, sb_ref, ccs_ref,
                 qa_ref, ka_ref, va_ref, qb_ref, kb_ref, vb_ref, cbo_ref, u_ref, pc_ref, ps_ref,
                 pp_ref):
    xn = _norm_mod(h_ref[...], gain_ref[...], sh_ref[...], sc_ref[...]).astype(BF16)
    ca, s1a, s2a = ca_ref[...], s1a_ref[...], s2a_ref[...]
    cb, sb = cb_ref[...], sb_ref[...]
    scale_a = QK_A ** -0.5 * LOG2_E
    scale_b = HD_B ** -0.5 * LOG2_E

    za = jnp.dot(xn, w_ref[:, OFF_A:OFF_B], preferred_element_type=F32)
    zb = jnp.dot(xn, w_ref[:, OFF_B:OFF_C], preferred_element_type=F32)

    hi = lax.broadcasted_iota(jnp.int32, (1, LANE), 1) >= LANE // 2
    s3, s4 = za[:, 3 * LANE:4 * LANE], za[:, 4 * LANE:]
    aq = jnp.concatenate([za[:, :3 * LANE], jnp.where(hi, 0.0, s3)], axis=1)
    akv = pltpu.roll(jnp.where(hi, s3, s4), LANE // 2, 1)
    ape = jnp.where(hi, 0.0, pltpu.roll(s4, LANE // 2, 1))
    r = lax.rsqrt(jnp.sum(aq * aq, axis=-1, keepdims=True) * (1.0 / Q_LORA) + EPS)
    cq = (aq * r * gcq_ref[...]).astype(BF16)
    q = jnp.dot(cq, wuq_ref[...], preferred_element_type=F32)
    gqa = gqa_ref[...]
    for hd in range(HEADS):
        lo = hd * QK_A_PAD
        qh = q[:, lo:lo + QK_A_PAD]
        r = lax.rsqrt(jnp.sum(qh * qh, axis=-1, keepdims=True) * (1.0 / QK_A) + EPS)
        qh = qh * r * gqa
        qa_ref[:, lo:lo + NOPE] = (qh[:, :NOPE] * scale_a).astype(BF16)
        qa_ref[:, lo + NOPE:lo + QK_A_PAD] = (_rope_a(qh[:, NOPE:], ca, s1a, s2a) * scale_a).astype(BF16)
    r = lax.rsqrt(jnp.mean(akv * akv, axis=-1, keepdims=True) + EPS)
    ckv = (akv * r * gckv_ref[...]).astype(BF16)
    kv = jnp.dot(ckv, wukv_ref[...], preferred_element_type=F32)
    ss_pe = jnp.sum(ape * ape, axis=-1, keepdims=True)
    ones_blk = jnp.ones((h_ref.shape[0], LANE), BF16)
    gkan, gkap = gkan_ref[...], gkap_ref[...]
    for hd in range(HEADS):
        kn = kv[:, hd * NOPE:(hd + 1) * NOPE]
        r = lax.rsqrt((jnp.sum(kn * kn, axis=-1, keepdims=True) + ss_pe) * (1.0 / QK_A) + EPS)
        lo = hd * QK_A_PAD
        ka_ref[:, lo:lo + NOPE] = (kn * r * gkan).astype(BF16)
        ka_ref[:, lo + NOPE:lo + QK_A_PAD] = _rope_a(ape * r * gkap, ca, s1a, s2a).astype(BF16)
        va_ref[:, 2 * hd * V_A:(2 * hd + 1) * V_A] = kv[:, HEADS * NOPE + hd * V_A:HEADS * NOPE + (hd + 1) * V_A].astype(BF16)
        va_ref[:, (2 * hd + 1) * V_A:(2 * hd + 2) * V_A] = ones_blk

    zc = jnp.dot(xn, w_ref[:, OFF_C:OFF_F], preferred_element_type=F32)
    gqb, gkb = gqb_ref[...], gkb_ref[...]
    for hd in range(HEADS):
        qh = zb[:, hd * HD_B:(hd + 1) * HD_B]
        r = lax.rsqrt(jnp.mean(qh * qh, axis=-1, keepdims=True) + EPS)
        qb_ref[:, hd * HD_B:(hd + 1) * HD_B] = (_rope_b(qh * r * gqb, cb, sb) * scale_b).astype(BF16)
    for hd in range(KVH_B):
        lo = HEADS * HD_B + hd * HD_B
        kh = zb[:, lo:lo + HD_B]
        r = lax.rsqrt(jnp.mean(kh * kh, axis=-1, keepdims=True) + EPS)
        kb_ref[:, hd * HD_B:(hd + 1) * HD_B] = _rope_b(kh * r * gkb, cb, sb).astype(BF16)
    for hd in range(KVH_B):
        lo = (HEADS + KVH_B + hd) * HD_B
        vb_ref[:, 2 * hd * HD_B:(2 * hd + 1) * HD_B] = zb[:, lo:lo + HD_B].astype(BF16)
        vb_ref[:, (2 * hd + 1) * HD_B:(2 * hd + 2) * HD_B] = ones_blk

    zf = jnp.dot(xn, w_ref[:, OFF_F:], preferred_element_type=F32).astype(BF16)
    cbo_ref[...] = zc[:, :CONV_W]
    u_ref[...] = zc[:, CONV_W:2 * CONV_W] * zc[:, 2 * CONV_W:]

    ccs = ccs_ref[...]
    tm = h_ref.shape[0]
    w = FG * FGD
    for g in range(FG):
        pp = jnp.dot(zf[:, g * FGD:(g + 1) * FGD], ccs, preferred_element_type=F32)
        pp_ref[g] = pp[:, :FGD]
        pp_ref[FG + g] = pp[:, FGD:]
    for r in range(4):
        for g in range(FG):
            lo = r * w + g * FGD
            pc_ref[:, lo:lo + FGD] = pp_ref[g, pl.ds(r, tm // 4, stride=4), :].astype(BF16)
            ps_ref[:, lo:lo + FGD] = pp_ref[FG + g, pl.ds(r, tm // 4, stride=4), :].astype(BF16)


def _const_spec(shape, layer=None):
    nd = len(shape)
    if layer is None:
        return pl.BlockSpec(shape, lambda i, *_: (0,) * nd, pipeline_mode=pl.Buffered(1))
    return pl.BlockSpec((None,) + shape, lambda i, *_: (layer,) + (0,) * nd, pipeline_mode=pl.Buffered(1))


def _proj(h, mod, layer, lw, tabs, ccs):
    tm = TM_PROJ
    tpb = SEQ // tm
    n_lat = T_LAT // tm

    def tab_spec():
        return pl.BlockSpec((tm, 128), lambda i: (jnp.where(i < n_lat, i % tpb, tpb), 0))

    def row_spec(w):
        return pl.BlockSpec((tm, w), lambda i: (i, 0))

    widths = (HEADS * QK_A_PAD, HEADS * QK_A_PAD, HEADS * 2 * V_A, HEADS * HD_B, KVH_B * HD_B,
              KVH_B * 2 * HD_B, CONV_W, CONV_W)
    dtypes = (BF16, BF16, BF16, BF16, BF16, BF16, F32, F32)
    dft_spec = pl.BlockSpec((tm // 4, 4 * FG * FGD), lambda i: (i, 0))
    dft_shape = jax.ShapeDtypeStruct((T_ALL // 4, 4 * FG * FGD), BF16)
    return pl.pallas_call(
        _proj_kernel,
        grid=(T_ALL // tm,),
        in_specs=[
            row_spec(D),
            _const_spec((1, D), layer),
            _mod_spec(layer, 3, tpb),
            _mod_spec(layer, 4, tpb),
            _const_spec((D, W_IN), layer),
            _const_spec((1, Q_LORA_PAD), layer),
            _const_spec((Q_LORA_PAD, HEADS * QK_A_PAD), layer),
            _const_spec((1, KV_LORA), layer),
            _const_spec((KV_LORA, HEADS * (NOPE + V_A)), layer),
            _const_spec((1, QK_A_PAD), layer),
            _const_spec((1, NOPE), layer),
            _const_spec((1, 128), layer),
            _const_spec((1, HD_B), layer),
            _const_spec((1, HD_B), layer),
            tab_spec(), tab_spec(), tab_spec(), tab_spec(), tab_spec(),
            _const_spec((FGD, 2 * FGD)),
        ],
        out_specs=[row_spec(w) for w in widths] + [dft_spec, dft_spec],
        out_shape=[jax.ShapeDtypeStruct((T_ALL, w), dt) for w, dt in zip(widths, dtypes)]
        + [dft_shape, dft_shape],
        scratch_shapes=[pltpu.VMEM((2 * FG, tm, FGD), F32)],
        compiler_params=_cparams(("arbitrary",)),
        name="proj",
    )(h, lw["norm_mix"], mod, mod, lw["w_in"], lw["g_cq"], lw["w_uq"], lw["g_ckv"], lw["w_ukv"],
      lw["g_qa"], lw["g_ka_nope"], lw["g_ka_pe"], lw["g_qb"], lw["g_kb"], *tabs, ccs)


_QK_DIMS = (((1,), (1,)), ((), ()))


def _att_lat_kernel(q_ref, k_ref, kc_ref, v_ref, vc_ref, o_ref):
    q = q_ref[...]
    chunks = [(k_ref, v_ref, j * KC_ATT, KC_ATT) for j in range(SEQ // KC_ATT)] + [(kc_ref, vc_ref, 0, CTX)]

    def scores(chunk):
        kr, _, lo, n = chunk
        return lax.dot_general(q, kr[lo:lo + n, :], _QK_DIMS, preferred_element_type=F32)

    dv = o_ref.shape[-1]
    s = scores(chunks[0])
    m = acc = None
    for idx, (_, vr, lo, n) in enumerate(chunks):
        s_next = scores(chunks[idx + 1]) if idx + 1 < len(chunks) else None
        m_chunk = jnp.max(s, axis=-1, keepdims=True)
        m_new = m_chunk if m is None else jnp.maximum(m, m_chunk)
        p = _exp2_bf16(s - m_new)
        pv = jnp.dot(p, vr[lo:lo + n, :], preferred_element_type=F32)
        acc = pv if m is None else jnp.exp2(m - m_new) * acc + pv
        m, s = m_new, s_next
    o_ref[...] = (acc[:, :dv] / acc[:, dv:]).astype(o_ref.dtype)


def _exp2_bf16(x):
    return jnp.exp2(x.astype(BF16))


def _att_ctx_kernel(group, q_ref, k_ref, v_ref, o_ref):
    dv = o_ref.shape[-1] // HEADS
    dq = q_ref.shape[-1] // HEADS
    for hd in range(HEADS):
        kv = hd // group
        s = lax.dot_general(q_ref[:, hd * dq:(hd + 1) * dq], k_ref[:, kv * dq:(kv + 1) * dq], _QK_DIMS,
                            preferred_element_type=F32)
        p = _exp2_bf16(s - jnp.max(s, axis=-1, keepdims=True))
        o = jnp.dot(p, v_ref[:, 2 * kv * dv:2 * (kv + 1) * dv], preferred_element_type=F32)
        o_ref[:, hd * dv:(hd + 1) * dv] = (o[:, :dv] / o[:, dv:]).astype(o_ref.dtype)


def _attention(q, k, v, dq, dv, group, with_ctx):
    tq = TQ_ATT
    nq = SEQ // tq
    ctx0 = T_LAT // CTX
    out = pl.pallas_call(
        _att_lat_kernel,
        grid=(BATCH, HEADS, nq),
        in_specs=[
            pl.BlockSpec((tq, dq), lambda b, h, i: (b * nq + i, h)),
            pl.BlockSpec((SEQ, dq), lambda b, h, i: (b, h // group)),
            pl.BlockSpec((CTX, dq), lambda b, h, i: (ctx0 + b, h // group)),
            pl.BlockSpec((SEQ, 2 * dv), lambda b, h, i: (b, h // group)),
            pl.BlockSpec((CTX, 2 * dv), lambda b, h, i: (ctx0 + b, h // group)),
        ],
        out_specs=pl.BlockSpec((tq, dv), lambda b, h, i: (b * nq + i, h)),
        out_shape=jax.ShapeDtypeStruct((T_LAT, HEADS * dv), BF16),
        compiler_params=_cparams(("arbitrary", "arbitrary", "arbitrary")),
        name="att_lat",
    )(q, k, k, v, v)
    if not with_ctx:
        return out, None
    kvh = HEADS // group
    out_ctx = pl.pallas_call(
        functools.partial(_att_ctx_kernel, group),
        grid=(BATCH,),
        in_specs=[
            pl.BlockSpec((CTX, HEADS * dq), lambda b: (ctx0 + b, 0)),
            pl.BlockSpec((CTX, kvh * dq), lambda b: (ctx0 + b, 0)),
            pl.BlockSpec((CTX, kvh * 2 * dv), lambda b: (ctx0 + b, 0)),
        ],
        out_specs=pl.BlockSpec((CTX, HEADS * dv), lambda b: (b, 0)),
        out_shape=jax.ShapeDtypeStruct((T_CTX, HEADS * dv), BF16),
        compiler_params=_cparams(("arbitrary",)),
        name="att_ctx",
    )(q, k, v)
    return out, out_ctx


def _dft4_kernel(scale, c_ref, s_ref, tw_ref, pc_ref, ps_ref, o_ref):
    w = o_ref.shape[-1]
    c, s = c_ref[...], s_ref[...]
    pc, ps = pc_ref[...], ps_ref[...]
    ur = jnp.dot(c, pc, preferred_element_type=F32) - jnp.dot(s, ps, preferred_element_type=F32)
    ui = -(jnp.dot(c, ps, preferred_element_type=F32) + jnp.dot(s, pc, preferred_element_type=F32))
    a = [ur[:, :w]]
    b = [None]
    for r in range(1, 4):
        u_r, v_r = ur[:, r * w:(r + 1) * w], ui[:, r * w:(r + 1) * w]
        cr, sr = tw_ref[:, 2 * r - 2:2 * r - 1], tw_ref[:, 2 * r - 1:2 * r]
        a.append(u_r * cr + v_r * sr)
        b.append(None if r == 2 else v_r * cr - u_r * sr)
    o_ref[0] = ((a[0] + a[1] + a[2] + a[3]) * scale).astype(o_ref.dtype)
    o_ref[1] = ((a[0] + b[1] - a[2] - b[3]) * scale).astype(o_ref.dtype)
    o_ref[2] = ((a[0] - a[1] + a[2] - a[3]) * scale).astype(o_ref.dtype)
    o_ref[3] = ((a[0] - b[1] - a[2] + b[3]) * scale).astype(o_ref.dtype)


def _dft4(pc4, ps4, tables, n, tk, first_block, name):
    w = FG * FGD
    nq = n // 4
    c_q, s_q, tw = tables
    return pl.pallas_call(
        functools.partial(_dft4_kernel, (n * FGD) ** -0.5),
        grid=(BATCH, nq // tk),
        in_specs=[
            pl.BlockSpec((tk, nq), lambda b, k: (k, 0)),
            pl.BlockSpec((tk, nq), lambda b, k: (k, 0)),
            pl.BlockSpec((tk, 8), lambda b, k: (k, 0)),
            pl.BlockSpec((nq, 4 * w), lambda b, k: (first_block + b, 0)),
            pl.BlockSpec((nq, 4 * w), lambda b, k: (first_block + b, 0)),
        ],
        out_specs=pl.BlockSpec((None, 4, tk, w), lambda b, k: (b, 0, k, 0)),
        out_shape=jax.ShapeDtypeStruct((BATCH, 4, nq, w), BF16),
        compiler_params=_cparams(("arbitrary", "arbitrary")),
        name=name,
    )(c_q, s_q, tw, pc4, ps4).reshape(BATCH * n, w)


def _fourier(pc4, ps4, dft_lat, dft_ctx, with_ctx):
    out = _dft4(pc4, ps4, dft_lat, SEQ, TK_DFT, 0, "dft_lat")
    if not with_ctx:
        return out, None
    return out, _dft4(pc4, ps4, dft_ctx, CTX, CTX // 4, T_LAT // CTX, "dft_ctx")


def _merge_kernel(n_lat_tiles, with_ctx, h_ref, gain_ref, sh_ref, sc_ref, gt_ref, *refs):
    lat_refs, refs = refs[:3], refs[3:]
    if with_ctx:
        ctx_refs, refs = refs[:3], refs[3:]
    (cbo_ref, u_ref, up_ref, un_ref, cw_ref, cbias_ref, wg_ref, bg_ref, wb_ref, wo_ref,
     o_ref, xn_ref, conv_ref) = refs
    i = pl.program_id(0)
    n = pl.program_id(1)
    last = pl.num_programs(1) - 1
    tm = h_ref.shape[0]
    is_lat = i < n_lat_tiles

    def conv_branch():
        u = u_ref[...]
        seq_len = jnp.where(is_lat, SEQ, CTX)
        ridx = lax.broadcasted_iota(jnp.int32, (tm, 1), 0)
        pos = (i * tm + ridx) & (seq_len - 1)
        prev = jnp.where(ridx == 0, up_ref[7:8, :], pltpu.roll(u, 1, 0))
        prev = jnp.where(pos == 0, 0.0, prev)
        nxt = jnp.where(ridx == tm - 1, un_ref[0:1, :], pltpu.roll(u, tm - 1, 0))
        nxt = jnp.where(pos == seq_len - 1, 0.0, nxt)
        y = prev * cw_ref[0:1, :] + u * cw_ref[1:2, :] + nxt * cw_ref[2:3, :] + cbias_ref[...]
        return (cbo_ref[...] * y).astype(BF16)

    def chunk(first=False):
        if first:
            xn = _norm_mod(h_ref[...], gain_ref[...], sh_ref[...], sc_ref[...]).astype(BF16)
            xn_ref[...] = xn
            conv = conv_branch()
            conv_ref[...] = conv
        else:
            xn = xn_ref[...]
            conv = conv_ref[...]

        if with_ctx:
            oa, ob, of = (jnp.where(is_lat, lr[...], cr[...]) for lr, cr in zip(lat_refs, ctx_refs))
        else:
            oa, ob, of = (lr[...] for lr in lat_refs)

        merged = None
        for k, br in enumerate((oa, ob, conv, of)):
            gate = jax.nn.sigmoid(jnp.dot(xn, wg_ref[k], preferred_element_type=F32) + bg_ref[k])
            term = gate * jnp.dot(br, wb_ref[k], preferred_element_type=F32)
            merged = term if merged is None else merged + term
        return jnp.dot(merged.astype(BF16), wo_ref[...], preferred_element_type=F32)

    @pl.when(n == 0)
    def _():
        o_ref[...] = chunk(first=True)

    @pl.when(jnp.logical_and(n > 0, n < last))
    def _():
        o_ref[...] += chunk()

    @pl.when(n == last)
    def _():
        o_ref[...] = h_ref[...] + gt_ref[...] * (o_ref[...] + chunk())


def _merge(h, mod, layer, br_lat, br_ctx, cbo, u, lw, n_rows):
    tm, tn = TM_MERGE, TN_MERGE
    tpb = SEQ // tm
    n_lat = T_LAT // tm
    with_ctx = br_ctx is not None
    last_blk = T_ALL // 8 - 1
    br_specs = [pl.BlockSpec((tm, BR_W), lambda i, n: (jnp.minimum(i, n_lat - 1), 0))] * 3
    if with_ctx:
        br_specs += [pl.BlockSpec((tm, BR_W), lambda i, n: (jnp.maximum(i - n_lat, 0), 0))] * 3

    def row_spec(w):
        return pl.BlockSpec((tm, w), lambda i, n: (i, 0))

    return pl.pallas_call(
        functools.partial(_merge_kernel, n_lat, with_ctx),
        grid=(n_rows // tm, D // tn),
        in_specs=[
            row_spec(D),
            pl.BlockSpec((None, 1, D), lambda i, n: (layer, 0, 0)),
            _mod_spec(layer, 3, tpb),
            _mod_spec(layer, 4, tpb),
            _mod_spec(layer, 5, tpb),
            *br_specs,
            row_spec(CONV_W), row_spec(CONV_W),
            pl.BlockSpec((8, CONV_W), lambda i, n: (jnp.maximum(i * (tm // 8) - 1, 0), 0)),
            pl.BlockSpec((8, CONV_W), lambda i, n: (jnp.minimum((i + 1) * (tm // 8), last_blk), 0)),
            pl.BlockSpec((None, 3, CONV_W), lambda i, n: (layer, 0, 0)),
            pl.BlockSpec((None, 1, CONV_W), lambda i, n: (layer, 0, 0)),
            pl.BlockSpec((None, N_BR, D, tn), lambda i, n: (layer, 0, 0, n)),
            pl.BlockSpec((None, N_BR, 1, tn), lambda i, n: (layer, 0, 0, n)),
            pl.BlockSpec((None, N_BR, BR_W, tn), lambda i, n: (layer, 0, 0, n)),
            pl.BlockSpec((None, tn, D), lambda i, n: (layer, n, 0)),
        ],
        out_specs=row_spec(D),
        out_shape=jax.ShapeDtypeStruct((n_rows, D), F32),
        scratch_shapes=[pltpu.VMEM((tm, D), BF16), pltpu.VMEM((tm, CONV_W), BF16)],
        compiler_params=_cparams(("arbitrary", "arbitrary")),
        name="merge",
    )(h, lw["norm_mix"], mod, mod, mod, *br_lat, *(br_ctx or ()), cbo, u, u, u, lw["conv_w"],
      lw["conv_b"], lw["w_gate"], lw["b_gate"], lw["w_branch"], lw["w_o"])


def _rope_tables():
    t = jnp.arange(SEQ, dtype=jnp.int32)
    pos_row = (t // GRID_W).astype(F32)
    pos_col = (t % GRID_W).astype(F32)

    def ang(rot_dim):
        n = rot_dim // 4
        inv_freq = THETA ** (-jnp.arange(n, dtype=F32) / n)
        a = jnp.concatenate([pos_row[:, None] * inv_freq, pos_col[:, None] * inv_freq], axis=-1)
        return jnp.cos(a), jnp.sin(a)

    ca, sa = ang(ROPE_A)
    z32 = jnp.zeros_like(ca)
    z64 = jnp.zeros((SEQ, 64), F32)
    tab_c = jnp.concatenate([ca, ca, z64], axis=-1)
    tab_s1 = jnp.concatenate([-sa, z32, z64], axis=-1)
    tab_s2 = jnp.concatenate([z32, sa, z64], axis=-1)
    cb, sb = ang(HD_B)
    tab_cb = jnp.concatenate([cb, cb], axis=-1)
    tab_sb = jnp.concatenate([-sb, sb], axis=-1)
    ones = jnp.ones((TM_PROJ, 128), F32)
    zeros = jnp.zeros((TM_PROJ, 128), F32)
    return tuple(jnp.concatenate([tab, ident], axis=0)
                 for tab, ident in ((tab_c, ones), (tab_s1, zeros), (tab_s2, zeros),
                                    (tab_cb, ones), (tab_sb, zeros)))


def _dft_mats(n):
    j = jnp.arange(n, dtype=jnp.int32)
    a = ((j[:, None] * j[None, :]) % n).astype(F32) * (2.0 * math.pi / n)
    return jnp.cos(a), jnp.sin(a)


def _dft4_tables(n):
    nq = n // 4
    c_q, s_q = _dft_mats(nq)
    k = jnp.arange(nq, dtype=jnp.int32)[:, None]
    ang = (k * jnp.arange(1, 4, dtype=jnp.int32)[None, :]).astype(F32) * (2.0 * math.pi / n)
    tw = jnp.stack([jnp.cos(ang), jnp.sin(ang)], axis=-1).reshape(nq, 6)
    return c_q.astype(BF16), s_q.astype(BF16), _pad_to(tw, 1, 8)


def _pad_to(a, axis, size):
    pad = [(0, 0)] * a.ndim
    pad[axis] = (0, size - a.shape[axis])
    return jnp.pad(a, pad)


def _prep_weights(p):
    w_uq = p["w_uq"].reshape(DEPTH, Q_LORA, HEADS, QK_A)
    w_uq = _pad_to(_pad_to(w_uq, 3, QK_A_PAD), 1, Q_LORA_PAD).reshape(DEPTH, Q_LORA_PAD, HEADS * QK_A_PAD)
    w_ukv = p["w_ukv"].reshape(DEPTH, KV_LORA, HEADS, NOPE + V_A)
    w_ukv = jnp.concatenate([w_ukv[..., :NOPE].reshape(DEPTH, KV_LORA, HEADS * NOPE),
                             w_ukv[..., NOPE:].reshape(DEPTH, KV_LORA, HEADS * V_A)], axis=2)
    g_ka = p["g_ka"]

    return {
        "w_in": p["w_in"].astype(BF16),
        "g_cq": _pad_to(p["g_cq"], 1, Q_LORA_PAD)[:, None, :],
        "w_uq": w_uq.astype(BF16),
        "g_ckv": p["g_ckv"][:, None, :],
        "w_ukv": w_ukv.astype(BF16),
        "g_qa": _pad_to(p["g_qa"], 1, QK_A_PAD)[:, None, :],
        "g_ka_nope": g_ka[:, None, :NOPE],
        "g_ka_pe": _pad_to(g_ka[:, NOPE:], 1, 128)[:, None, :],
        "g_qb": p["g_qb"][:, None, :],
        "g_kb": p["g_kb"][:, None, :],
        "conv_w": p["conv_w"],
        "conv_b": p["conv_b"][:, None, :],
        "w_gate": p["w_gate"].astype(BF16),
        "b_gate": p["b_gate"][:, :, None, :],
        "w_branch": p["w_branch"].astype(BF16),
        "w_o": p["w_o"].astype(BF16),
        "norm_ffn1": p["norm_ffn1"][:, None, :],
        "norm_mix": p["norm_mix"][:, None, :],
        "norm_ffn2": p["norm_ffn2"][:, None, :],
        "ffn1_wi": p["ffn1_wi"].astype(BF16), "ffn1_wo": p["ffn1_wo"].astype(BF16),
        "ffn2_wi": p["ffn2_wi"].astype(BF16), "ffn2_wo": p["ffn2_wo"].astype(BF16),
    }


def kernel(x, c, ctx, c_ctx, w_ada, b_ada, norm_ffn1, ffn1_wi, ffn1_wo, norm_mix, w_in, g_cq, w_uq, g_ckv, w_ukv, g_qa, g_ka, g_qb, g_kb, conv_w, conv_b, w_branch, w_gate, b_gate, w_o, norm_ffn2, ffn2_wi, ffn2_wo):
    p = dict(norm_ffn1=norm_ffn1, ffn1_wi=ffn1_wi, ffn1_wo=ffn1_wo, norm_mix=norm_mix, w_in=w_in,
             g_cq=g_cq, w_uq=w_uq, g_ckv=g_ckv, w_ukv=w_ukv, g_qa=g_qa, g_ka=g_ka, g_qb=g_qb,
             g_kb=g_kb, conv_w=conv_w, conv_b=conv_b, w_branch=w_branch, w_gate=w_gate,
             b_gate=b_gate, w_o=w_o, norm_ffn2=norm_ffn2, ffn2_wi=ffn2_wi, ffn2_wo=ffn2_wo)

    cond8 = jnp.concatenate([c, c_ctx[None, :], jnp.zeros((8 - BATCH - 1, D), F32)], axis=0)
    mod = _ada(cond8, w_ada, b_ada).reshape(DEPTH, 8, N_MOD, 1, D)

    tabs = _rope_tables()
    cc, sc = _dft_mats(FGD)
    ccs = jnp.concatenate([cc, sc], axis=1).astype(BF16)
    dft_lat = _dft4_tables(SEQ)
    dft_ctx = _dft4_tables(CTX)

    lw = _prep_weights(p)
    h = jnp.concatenate([x.reshape(T_LAT, D), ctx.reshape(T_CTX, D)], axis=0)
    for l in range(DEPTH):
        last = l == DEPTH - 1
        h = _ffn(h, mod, l, 0, lw["norm_ffn1"], lw["ffn1_wi"], lw["ffn1_wo"], T_ALL)
        qa, ka, va, qb, kb, vb, cbo, u, pc, ps = _proj(h, mod, l, lw, tabs, ccs)
        oa, oa_c = _attention(qa, ka, va, QK_A_PAD, V_A, 1, not last)
        ob, ob_c = _attention(qb, kb, vb, HD_B, HD_B, GROUP_B, not last)
        of, of_c = _fourier(pc, ps, dft_lat, dft_ctx, not last)
        n_rows = T_LAT if last else T_ALL
        h = _merge(h, mod, l, (oa, ob, of), None if last else (oa_c, ob_c, of_c), cbo, u, lw, n_rows)
        h = _ffn(h, mod, l, 6, lw["norm_ffn2"], lw["ffn2_wi"], lw["ffn2_wo"], n_rows)
    return h.reshape(BATCH, SEQ, D)
```

```python
import functools
import math

import jax
import jax.numpy as jnp
from jax import lax
from jax.experimental import pallas as pl
from jax.experimental.pallas import tpu as pltpu

F32 = jnp.float32
BF16 = jnp.bfloat16

D = 2048
BATCH = 4
SEQ = 4096
DEPTH = 4
GRID_W = 64
CTX = 256
N_MOD = 9
F_FF = 5504
HEADS = 4
Q_LORA = 448
Q_LORA_PAD = 512
KV_LORA = 128
NOPE = 128
ROPE_A = 64
QK_A = NOPE + ROPE_A
QK_A_PAD = 256
V_A = 128
HD_B = 128
KVH_B = 2
GROUP_B = 2
CONV_W = 512
FG = 4
FGD = 128
BR_W = 512
N_BR = 4
THETA = 10000.0
EPS = 1e-6
LOG2_E = math.log2(math.e)

T_LAT = BATCH * SEQ
T_CTX = BATCH * CTX
T_ALL = T_LAT + T_CTX

W_IN = 3712
OFF_A = 0
OFF_B = 640
OFF_C = 1664
OFF_F = 3200

LANE = 128
VMEM_PHYS_V7X = 64 * 1024 * 1024
VMEM_LIMIT = VMEM_PHYS_V7X - 4 * 1024 * 1024

TM_FFN = 1024
TF_FFN = 512
TM_PROJ = 512
TM_MERGE = 512
TN_MERGE = 512
TQ_ATT = 1024
KC_ATT = 2048
TK_DFT = 512
TN_ADA = 1024


def _cparams(sem):
    return pltpu.CompilerParams(dimension_semantics=sem, vmem_limit_bytes=VMEM_LIMIT)


def _mod_row(i, tiles_per_batch):
    return jnp.minimum(i // tiles_per_batch, BATCH)


def _row_rsqrt(x):
    return lax.rsqrt(jnp.mean(x * x, axis=-1, keepdims=True) + EPS)


def _scale_mod(x, r, gain, shift, scale):
    return (x * r * gain) * (1.0 + scale) + shift


def _norm_mod(x, gain, shift, scale):
    return _scale_mod(x, _row_rsqrt(x), gain, shift, scale)


def _silu(x):
    return x * jax.nn.sigmoid(x)


def _ada_kernel(cond_ref, w_ref, b_ref, o_ref):
    s = _silu(cond_ref[...]).astype(BF16)
    o_ref[...] = jnp.dot(s, w_ref[...].astype(BF16), preferred_element_type=F32) + b_ref[...]


def _ada(cond8, w_ada, b_ada):
    n = N_MOD * D
    return pl.pallas_call(
        _ada_kernel,
        grid=(DEPTH, n // TN_ADA),
        in_specs=[
            pl.BlockSpec((8, D), lambda l, j: (0, 0)),
            pl.BlockSpec((None, D, TN_ADA), lambda l, j: (l, 0, j)),
            pl.BlockSpec((None, 1, TN_ADA), lambda l, j: (l, 0, j)),
        ],
        out_specs=pl.BlockSpec((None, 8, TN_ADA), lambda l, j: (l, 0, j)),
        out_shape=jax.ShapeDtypeStruct((DEPTH, 8, n), F32),
        compiler_params=_cparams(("arbitrary", "arbitrary")),
        name="ada",
    )(cond8, w_ada, b_ada.reshape(DEPTH, 1, n))


def _mod_spec(layer, j, tiles_per_batch):
    return pl.BlockSpec((None, None, None, 1, D),
                        lambda i, *_: (layer, _mod_row(i, tiles_per_batch), j, 0, 0))


def _ffn_kernel(overlap, x_ref, gain_ref, sh_ref, sc_ref, gt_ref, wg_ref, wu_ref, wo_ref, o_ref, xn_ref):
    f = pl.program_id(1)
    last = pl.num_programs(1) - 1

    def chunk(skip=0, first=False):
        if first:
            xn = _norm_mod(x_ref[...], gain_ref[...], sh_ref[...], sc_ref[...]).astype(BF16)
            xn_ref[...] = xn
        else:
            xn = xn_ref[...]
        tf = wg_ref.shape[1]
        halves = [(skip, tf // 2), (tf // 2, tf)]
        gu = [(jnp.dot(xn, wg_ref[:, lo:hi], preferred_element_type=F32),
               jnp.dot(xn, wu_ref[:, lo:hi], preferred_element_type=F32)) for lo, hi in halves]
        out = None
        for (lo, hi), (g, u) in zip(halves, gu):
            a = (_silu(g) * u).astype(BF16)
            part = jnp.dot(a, wo_ref[lo:hi, :], preferred_element_type=F32)
            out = part if out is None else out + part
        return out

    @pl.when(f == 0)
    def _():
        o_ref[...] = chunk(first=True)

    @pl.when(jnp.logical_and(f > 0, f < last))
    def _():
        o_ref[...] += chunk()

    @pl.when(f == last)
    def _():
        o_ref[...] = x_ref[...] + (0.5 * gt_ref[...]) * (o_ref[...] + chunk(overlap))


def _ffn(h, mod, layer, j0, gain, wi, wo, n_rows):
    tm, tf = TM_FFN, TF_FFN
    tpb = SEQ // tm
    nf = pl.cdiv(F_FF, tf)

    def hid(f, base=0):
        return (base // LANE + jnp.minimum(f * (tf // LANE), (F_FF - tf) // LANE)) * LANE

    return pl.pallas_call(
        functools.partial(_ffn_kernel, nf * tf - F_FF),
        grid=(n_rows // tm, nf),
        in_specs=[
            pl.BlockSpec((tm, D), lambda i, f: (i, 0)),
            pl.BlockSpec((None, 1, D), lambda i, f: (layer, 0, 0)),
            _mod_spec(layer, j0, tpb),
            _mod_spec(layer, j0 + 1, tpb),
            _mod_spec(layer, j0 + 2, tpb),
            pl.BlockSpec((None, pl.Element(D), pl.Element(tf)), lambda i, f: (layer, 0, hid(f))),
            pl.BlockSpec((None, pl.Element(D), pl.Element(tf)), lambda i, f: (layer, 0, hid(f, F_FF))),
            pl.BlockSpec((None, pl.Element(tf), pl.Element(D)), lambda i, f: (layer, hid(f), 0)),
        ],
        out_specs=pl.BlockSpec((tm, D), lambda i, f: (i, 0)),
        out_shape=jax.ShapeDtypeStruct((n_rows, D), F32),
        scratch_shapes=[pltpu.VMEM((tm, D), BF16)],
        compiler_params=_cparams(("arbitrary", "arbitrary")),
        name="ffn",
    )(h, gain, mod, mod, mod, wi, wi, wo)


def _rope_a(v, c, s1, s2):
    return v * c + pltpu.roll(v, 96, 1) * s1 + pltpu.roll(v, 32, 1) * s2


def _rope_b(v, c, s):
    return v * c + pltpu.roll(v, 64, 1) * s


def _proj_kernel(h_ref, gain_ref, sh_ref, sc_ref, w_ref, gcq_ref, wuq_ref, gckv_ref, wukv_ref,
                 gqa_ref, gkan_ref, gkap_ref, gqb_ref, gkb_ref, ca_ref, s1a_ref, s2a_ref,
                 cb_ref, sb_ref, ccs_ref,
                 qa_ref, ka_ref, va_ref, qb_ref, kb_ref, vb_ref, cbo_ref, u_ref, pc_ref, ps_ref,
                 pp_ref):
    xn = _norm_mod(h_ref[...], gain_ref[...], sh_ref[...], sc_ref[...]).astype(BF16)
    ca, s1a, s2a = ca_ref[...], s1a_ref[...], s2a_ref[...]
    cb, sb = cb_ref[...], sb_ref[...]
    scale_a = QK_A ** -0.5 * LOG2_E
    scale_b = HD_B ** -0.5 * LOG2_E

    za = jnp.dot(xn, w_ref[:, OFF_A:OFF_B], preferred_element_type=F32)
    zb = jnp.dot(xn, w_ref[:, OFF_B:OFF_C], preferred_element_type=F32)

    hi = lax.broadcasted_iota(jnp.int32, (1, LANE), 1) >= LANE // 2
    s3, s4 = za[:, 3 * LANE:4 * LANE], za[:, 4 * LANE:]
    aq = jnp.concatenate([za[:, :3 * LANE], jnp.where(hi, 0.0, s3)], axis=1)
    akv = pltpu.roll(jnp.where(hi, s3, s4), LANE // 2, 1)
    ape = jnp.where(hi, 0.0, pltpu.roll(s4, LANE // 2, 1))
    r = lax.rsqrt(jnp.sum(aq * aq, axis=-1, keepdims=True) * (1.0 / Q_LORA) + EPS)
    cq = (aq * r * gcq_ref[...]).astype(BF16)
    q = jnp.dot(cq, wuq_ref[...], preferred_element_type=F32)
    gqa = gqa_ref[...]
    for hd in range(HEADS):
        lo = hd * QK_A_PAD
        qh = q[:, lo:lo + QK_A_PAD]
        r = lax.rsqrt(jnp.sum(qh * qh, axis=-1, keepdims=True) * (1.0 / QK_A) + EPS)
        qh = qh * r * gqa
        qa_ref[:, lo:lo + NOPE] = (qh[:, :NOPE] * scale_a).astype(BF16)
        qa_ref[:, lo + NOPE:lo + QK_A_PAD] = (_rope_a(qh[:, NOPE:], ca, s1a, s2a) * scale_a).astype(BF16)
    r = lax.rsqrt(jnp.mean(akv * akv, axis=-1, keepdims=True) + EPS)
    ckv = (akv * r * gckv_ref[...]).astype(BF16)
    kv = jnp.dot(ckv, wukv_ref[...], preferred_element_type=F32)
    ss_pe = jnp.sum(ape * ape, axis=-1, keepdims=True)
    ones_blk = jnp.ones((h_ref.shape[0], LANE), BF16)
    gkan, gkap = gkan_ref[...], gkap_ref[...]
    for hd in range(HEADS):
        kn = kv[:, hd * NOPE:(hd + 1) * NOPE]
        r = lax.rsqrt((jnp.sum(kn * kn, axis=-1, keepdims=True) + ss_pe) * (1.0 / QK_A) + EPS)
        lo = hd * QK_A_PAD
        ka_ref[:, lo:lo + NOPE] = (kn * r * gkan).astype(BF16)
        ka_ref[:, lo + NOPE:lo + QK_A_PAD] = _rope_a(ape * r * gkap, ca, s1a, s2a).astype(BF16)
        va_ref[:, 2 * hd * V_A:(2 * hd + 1) * V_A] = kv[:, HEADS * NOPE + hd * V_A:HEADS * NOPE + (hd + 1) * V_A].astype(BF16)
        va_ref[:, (2 * hd + 1) * V_A:(2 * hd + 2) * V_A] = ones_blk

    zc = jnp.dot(xn, w_ref[:, OFF_C:OFF_F], preferred_element_type=F32)
    gqb, gkb = gqb_ref[...], gkb_ref[...]
    for hd in range(HEADS):
        qh = zb[:, hd * HD_B:(hd + 1) * HD_B]
        r = lax.rsqrt(jnp.mean(qh * qh, axis=-1, keepdims=True) + EPS)
        qb_ref[:, hd * HD_B:(hd + 1) * HD_B] = (_rope_b(qh * r * gqb, cb, sb) * scale_b).astype(BF16)
    for hd in range(KVH_B):
        lo = HEADS * HD_B + hd * HD_B
        kh = zb[:, lo:lo + HD_B]
        r = lax.rsqrt(jnp.mean(kh * kh, axis=-1, keepdims=True) + EPS)
        kb_ref[:, hd * HD_B:(hd + 1) * HD_B] = _rope_b(kh * r * gkb, cb, sb).astype(BF16)
    for hd in range(KVH_B):
        lo = (HEADS + KVH_B + hd) * HD_B
        vb_ref[:, 2 * hd * HD_B:(2 * hd + 1) * HD_B] = zb[:, lo:lo + HD_B].astype(BF16)
        vb_ref[:, (2 * hd + 1) * HD_B:(2 * hd + 2) * HD_B] = ones_blk

    zf = jnp.dot(xn, w_ref[:, OFF_F:], preferred_element_type=F32).astype(BF16)
    cbo_ref[...] = zc[:, :CONV_W]
    u_ref[...] = zc[:, CONV_W:2 * CONV_W] * zc[:, 2 * CONV_W:]

    ccs = ccs_ref[...]
    tm = h_ref.shape[0]
    w = FG * FGD
    for g in range(FG):
        pp = jnp.dot(zf[:, g * FGD:(g + 1) * FGD], ccs, preferred_element_type=F32)
        pp_ref[g] = pp[:, :FGD]
        pp_ref[FG + g] = pp[:, FGD:]
    for r in range(4):
        for g in range(FG):
            lo = r * w + g * FGD
            pc_ref[:, lo:lo + FGD] = pp_ref[g, pl.ds(r, tm // 4, stride=4), :].astype(BF16)
            ps_ref[:, lo:lo + FGD] = pp_ref[FG + g, pl.ds(r, tm // 4, stride=4), :].astype(BF16)


def _const_spec(shape, layer=None):
    nd = len(shape)
    if layer is None:
        return pl.BlockSpec(shape, lambda i, *_: (0,) * nd, pipeline_mode=pl.Buffered(1))
    return pl.BlockSpec((None,) + shape, lambda i, *_: (layer,) + (0,) * nd, pipeline_mode=pl.Buffered(1))


def _proj(h, mod, layer, lw, tabs, ccs):
    tm = TM_PROJ
    tpb = SEQ // tm
    n_lat = T_LAT // tm

    def tab_spec():
        return pl.BlockSpec((tm, 128), lambda i: (jnp.where(i < n_lat, i % tpb, tpb), 0))

    def row_spec(w):
        return pl.BlockSpec((tm, w), lambda i: (i, 0))

    widths = (HEADS * QK_A_PAD, HEADS * QK_A_PAD, HEADS * 2 * V_A, HEADS * HD_B, KVH_B * HD_B,
              KVH_B * 2 * HD_B, CONV_W, CONV_W)
    dtypes = (BF16, BF16, BF16, BF16, BF16, BF16, F32, F32)
    dft_spec = pl.BlockSpec((tm // 4, 4 * FG * FGD), lambda i: (i, 0))
    dft_shape = jax.ShapeDtypeStruct((T_ALL // 4, 4 * FG * FGD), BF16)
    return pl.pallas_call(
        _proj_kernel,
        grid=(T_ALL // tm,),
        in_specs=[
            row_spec(D),
            _const_spec((1, D), layer),
            _mod_spec(layer, 3, tpb),
            _mod_spec(layer, 4, tpb),
            _const_spec((D, W_IN), layer),
            _const_spec((1, Q_LORA_PAD), layer),
            _const_spec((Q_LORA_PAD, HEADS * QK_A_PAD), layer),
            _const_spec((1, KV_LORA), layer),
            _const_spec((KV_LORA, HEADS * (NOPE + V_A)), layer),
            _const_spec((1, QK_A_PAD), layer),
            _const_spec((1, NOPE), layer),
            _const_spec((1, 128), layer),
            _const_spec((1, HD_B), layer),
            _const_spec((1, HD_B), layer),
            tab_spec(), tab_spec(), tab_spec(), tab_spec(), tab_spec(),
            _const_spec((FGD, 2 * FGD)),
        ],
        out_specs=[row_spec(w) for w in widths] + [dft_spec, dft_spec],
        out_shape=[jax.ShapeDtypeStruct((T_ALL, w), dt) for w, dt in zip(widths, dtypes)]
        + [dft_shape, dft_shape],
        scratch_shapes=[pltpu.VMEM((2 * FG, tm, FGD), F32)],
        compiler_params=_cparams(("arbitrary",)),
        name="proj",
    )(h, lw["norm_mix"], mod, mod, lw["w_in"], lw["g_cq"], lw["w_uq"], lw["g_ckv"], lw["w_ukv"],
      lw["g_qa"], lw["g_ka_nope"], lw["g_ka_pe"], lw["g_qb"], lw["g_kb"], *tabs, ccs)


_QK_DIMS = (((1,), (1,)), ((), ()))


def _att_lat_kernel(q_ref, k_ref, kc_ref, v_ref, vc_ref, o_ref):
    q = q_ref[...]
    chunks = [(k_ref, v_ref, j * KC_ATT, KC_ATT) for j in range(SEQ // KC_ATT)] + [(kc_ref, vc_ref, 0, CTX)]

    def scores(chunk):
        kr, _, lo, n = chunk
        return lax.dot_general(q, kr[lo:lo + n, :], _QK_DIMS, preferred_element_type=F32)

    dv = o_ref.shape[-1]
    s = scores(chunks[0])
    m = acc = None
    for idx, (_, vr, lo, n) in enumerate(chunks):
        s_next = scores(chunks[idx + 1]) if idx + 1 < len(chunks) else None
        m_chunk = jnp.max(s, axis=-1, keepdims=True)
        m_new = m_chunk if m is None else jnp.maximum(m, m_chunk)
        p = _exp2_bf16(s - m_new)
        pv = jnp.dot(p, vr[lo:lo + n, :], preferred_element_type=F32)
        acc = pv if m is None else jnp.exp2(m - m_new) * acc + pv
        m, s = m_new, s_next
    o_ref[...] = (acc[:, :dv] / acc[:, dv:]).astype(o_ref.dtype)


def _exp2_bf16(x):
    return jnp.exp2(x.astype(BF16))


def _att_ctx_kernel(group, q_ref, k_ref, v_ref, o_ref):
    dv = o_ref.shape[-1] // HEADS
    dq = q_ref.shape[-1] // HEADS
    for hd in range(HEADS):
        kv = hd // group
        s = lax.dot_general(q_ref[:, hd * dq:(hd + 1) * dq], k_ref[:, kv * dq:(kv + 1) * dq], _QK_DIMS,
                            preferred_element_type=F32)
        p = _exp2_bf16(s - jnp.max(s, axis=-1, keepdims=True))
        o = jnp.dot(p, v_ref[:, 2 * kv * dv:2 * (kv + 1) * dv], preferred_element_type=F32)
        o_ref[:, hd * dv:(hd + 1) * dv] = (o[:, :dv] / o[:, dv:]).astype(o_ref.dtype)


def _attention(q, k, v, dq, dv, group, with_ctx):
    tq = TQ_ATT
    nq = SEQ // tq
    ctx0 = T_LAT // CTX
    out = pl.pallas_call(
        _att_lat_kernel,
        grid=(BATCH, HEADS, nq),
        in_specs=[
            pl.BlockSpec((tq, dq), lambda b, h, i: (b * nq + i, h)),
            pl.BlockSpec((SEQ, dq), lambda b, h, i: (b, h // group)),
            pl.BlockSpec((CTX, dq), lambda b, h, i: (ctx0 + b, h // group)),
            pl.BlockSpec((SEQ, 2 * dv), lambda b, h, i: (b, h // group)),
            pl.BlockSpec((CTX, 2 * dv), lambda b, h, i: (ctx0 + b, h // group)),
        ],
        out_specs=pl.BlockSpec((tq, dv), lambda b, h, i: (b * nq + i, h)),
        out_shape=jax.ShapeDtypeStruct((T_LAT, HEADS * dv), BF16),
        compiler_params=_cparams(("arbitrary", "arbitrary", "arbitrary")),
        name="att_lat",
    )(q, k, k, v, v)
    if not with_ctx:
        return out, None
    kvh = HEADS // group
    out_ctx = pl.pallas_call(
        functools.partial(_att_ctx_kernel, group),
        grid=(BATCH,),
        in_specs=[
            pl.BlockSpec((CTX, HEADS * dq), lambda b: (ctx0 + b, 0)),
            pl.BlockSpec((CTX, kvh * dq), lambda b: (ctx0 + b, 0)),
            pl.BlockSpec((CTX, kvh * 2 * dv), lambda b: (ctx0 + b, 0)),
        ],
        out_specs=pl.BlockSpec((CTX, HEADS * dv), lambda b: (b, 0)),
        out_shape=jax.ShapeDtypeStruct((T_CTX, HEADS * dv), BF16),
        compiler_params=_cparams(("arbitrary",)),
        name="att_ctx",
    )(q, k, v)
    return out, out_ctx


def _dft4_kernel(scale, c_ref, s_ref, tw_ref, pc_ref, ps_ref, o_ref):
    w = o_ref.shape[-1]
    c, s = c_ref[...], s_ref[...]
    pc, ps = pc_ref[...], ps_ref[...]
    ur = jnp.dot(c, pc, preferred_element_type=F32) - jnp.dot(s, ps, preferred_element_type=F32)
    ui = -(jnp.dot(c, ps, preferred_element_type=F32) + jnp.dot(s, pc, preferred_element_type=F32))
    a = [ur[:, :w]]
    b = [None]
    for r in range(1, 4):
        u_r, v_r = ur[:, r * w:(r + 1) * w], ui[:, r * w:(r + 1) * w]
        cr, sr = tw_ref[:, 2 * r - 2:2 * r - 1], tw_ref[:, 2 * r - 1:2 * r]
        a.append(u_r * cr + v_r * sr)
        b.append(None if r == 2 else v_r * cr - u_r * sr)
    o_ref[0] = ((a[0] + a[1] + a[2] + a[3]) * scale).astype(o_ref.dtype)
    o_ref[1] = ((a[0] + b[1] - a[2] - b[3]) * scale).astype(o_ref.dtype)
    o_ref[2] = ((a[0] - a[1] + a[2] - a[3]) * scale).astype(o_ref.dtype)
    o_ref[3] = ((a[0] - b[1] - a[2] + b[3]) * scale).astype(o_ref.dtype)


def _dft4(pc4, ps4, tables, n, tk, first_block, name):
    w = FG * FGD
    nq = n // 4
    c_q, s_q, tw = tables
    return pl.pallas_call(
        functools.partial(_dft4_kernel, (n * FGD) ** -0.5),
        grid=(BATCH, nq // tk),
        in_specs=[
            pl.BlockSpec((tk, nq), lambda b, k: (k, 0)),
            pl.BlockSpec((tk, nq), lambda b, k: (k, 0)),
            pl.BlockSpec((tk, 8), lambda b, k: (k, 0)),
            pl.BlockSpec((nq, 4 * w), lambda b, k: (first_block + b, 0)),
            pl.BlockSpec((nq, 4 * w), lambda b, k: (first_block + b, 0)),
        ],
        out_specs=pl.BlockSpec((None, 4, tk, w), lambda b, k: (b, 0, k, 0)),
        out_shape=jax.ShapeDtypeStruct((BATCH, 4, nq, w), BF16),
        compiler_params=_cparams(("arbitrary", "arbitrary")),
        name=name,
    )(c_q, s_q, tw, pc4, ps4).reshape(BATCH * n, w)


def _fourier(pc4, ps4, dft_lat, dft_ctx, with_ctx):
    out = _dft4(pc4, ps4, dft_lat, SEQ, TK_DFT, 0, "dft_lat")
    if not with_ctx:
        return out, None
    return out, _dft4(pc4, ps4, dft_ctx, CTX, CTX // 4, T_LAT // CTX, "dft_ctx")


def _merge_kernel(n_lat_tiles, with_ctx, h_ref, gain_ref, sh_ref, sc_ref, gt_ref, *refs):
    lat_refs, refs = refs[:3], refs[3:]
    if with_ctx:
        ctx_refs, refs = refs[:3], refs[3:]
    (cbo_ref, u_ref, up_ref, un_ref, cw_ref, cbias_ref, wg_ref, bg_ref, wb_ref, wo_ref,
     o_ref, xn_ref, conv_ref) = refs
    i = pl.program_id(0)
    n = pl.program_id(1)
    last = pl.num_programs(1) - 1
    tm = h_ref.shape[0]
    is_lat = i < n_lat_tiles

    def conv_branch():
        u = u_ref[...]
        seq_len = jnp.where(is_lat, SEQ, CTX)
        ridx = lax.broadcasted_iota(jnp.int32, (tm, 1), 0)
        pos = (i * tm + ridx) & (seq_len - 1)
        prev = jnp.where(ridx == 0, up_ref[7:8, :], pltpu.roll(u, 1, 0))
        prev = jnp.where(pos == 0, 0.0, prev)
        nxt = jnp.where(ridx == tm - 1, un_ref[0:1, :], pltpu.roll(u, tm - 1, 0))
        nxt = jnp.where(pos == seq_len - 1, 0.0, nxt)
        y = prev * cw_ref[0:1, :] + u * cw_ref[1:2, :] + nxt * cw_ref[2:3, :] + cbias_ref[...]
        return (cbo_ref[...] * y).astype(BF16)

    def chunk(first=False):
        if first:
            xn = _norm_mod(h_ref[...], gain_ref[...], sh_ref[...], sc_ref[...]).astype(BF16)
            xn_ref[...] = xn
            conv = conv_branch()
            conv_ref[...] = conv
        else:
            xn = xn_ref[...]
            conv = conv_ref[...]

        if with_ctx:
            oa, ob, of = (jnp.where(is_lat, lr[...], cr[...]) for lr, cr in zip(lat_refs, ctx_refs))
        else:
            oa, ob, of = (lr[...] for lr in lat_refs)

        merged = None
        for k, br in enumerate((oa, ob, conv, of)):
            gate = jax.nn.sigmoid(jnp.dot(xn, wg_ref[k], preferred_element_type=F32) + bg_ref[k])
            term = gate * jnp.dot(br, wb_ref[k], preferred_element_type=F32)
            merged = term if merged is None else merged + term
        return jnp.dot(merged.astype(BF16), wo_ref[...], preferred_element_type=F32)

    @pl.when(n == 0)
    def _():
        o_ref[...] = chunk(first=True)

    @pl.when(jnp.logical_and(n > 0, n < last))
    def _():
        o_ref[...] += chunk()

    @pl.when(n == last)
    def _():
        o_ref[...] = h_ref[...] + gt_ref[...] * (o_ref[...] + chunk())


def _merge(h, mod, layer, br_lat, br_ctx, cbo, u, lw, n_rows):
    tm, tn = TM_MERGE, TN_MERGE
    tpb = SEQ // tm
    n_lat = T_LAT // tm
    with_ctx = br_ctx is not None
    last_blk = T_ALL // 8 - 1
    br_specs = [pl.BlockSpec((tm, BR_W), lambda i, n: (jnp.minimum(i, n_lat - 1), 0))] * 3
    if with_ctx:
        br_specs += [pl.BlockSpec((tm, BR_W), lambda i, n: (jnp.maximum(i - n_lat, 0), 0))] * 3

    def row_spec(w):
        return pl.BlockSpec((tm, w), lambda i, n: (i, 0))

    return pl.pallas_call(
        functools.partial(_merge_kernel, n_lat, with_ctx),
        grid=(n_rows // tm, D // tn),
        in_specs=[
            row_spec(D),
            pl.BlockSpec((None, 1, D), lambda i, n: (layer, 0, 0)),
            _mod_spec(layer, 3, tpb),
            _mod_spec(layer, 4, tpb),
            _mod_spec(layer, 5, tpb),
            *br_specs,
            row_spec(CONV_W), row_spec(CONV_W),
            pl.BlockSpec((8, CONV_W), lambda i, n: (jnp.maximum(i * (tm // 8) - 1, 0), 0)),
            pl.BlockSpec((8, CONV_W), lambda i, n: (jnp.minimum((i + 1) * (tm // 8), last_blk), 0)),
            pl.BlockSpec((None, 3, CONV_W), lambda i, n: (layer, 0, 0)),
            pl.BlockSpec((None, 1, CONV_W), lambda i, n: (layer, 0, 0)),
            pl.BlockSpec((None, N_BR, D, tn), lambda i, n: (layer, 0, 0, n)),
            pl.BlockSpec((None, N_BR, 1, tn), lambda i, n: (layer, 0, 0, n)),
            pl.BlockSpec((None, N_BR, BR_W, tn), lambda i, n: (layer, 0, 0, n)),
            pl.BlockSpec((None, tn, D), lambda i, n: (layer, n, 0)),
        ],
        out_specs=row_spec(D),
        out_shape=jax.ShapeDtypeStruct((n_rows, D), F32),
        scratch_shapes=[pltpu.VMEM((tm, D), BF16), pltpu.VMEM((tm, CONV_W), BF16)],
        compiler_params=_cparams(("arbitrary", "arbitrary")),
        name="merge",
    )(h, lw["norm_mix"], mod, mod, mod, *br_lat, *(br_ctx or ()), cbo, u, u, u, lw["conv_w"],
      lw["conv_b"], lw["w_gate"], lw["b_gate"], lw["w_branch"], lw["w_o"])


def _rope_tables():
    t = jnp.arange(SEQ, dtype=jnp.int32)
    pos_row = (t // GRID_W).astype(F32)
    pos_col = (t % GRID_W).astype(F32)

    def ang(rot_dim):
        n = rot_dim // 4
        inv_freq = THETA ** (-jnp.arange(n, dtype=F32) / n)
        a = jnp.concatenate([pos_row[:, None] * inv_freq, pos_col[:, None] * inv_freq], axis=-1)
        return jnp.cos(a), jnp.sin(a)

    ca, sa = ang(ROPE_A)
    z32 = jnp.zeros_like(ca)
    z64 = jnp.zeros((SEQ, 64), F32)
    tab_c = jnp.concatenate([ca, ca, z64], axis=-1)
    tab_s1 = jnp.concatenate([-sa, z32, z64], axis=-1)
    tab_s2 = jnp.concatenate([z32, sa, z64], axis=-1)
    cb, sb = ang(HD_B)
    tab_cb = jnp.concatenate([cb, cb], axis=-1)
    tab_sb = jnp.concatenate([-sb, sb], axis=-1)
    ones = jnp.ones((TM_PROJ, 128), F32)
    zeros = jnp.zeros((TM_PROJ, 128), F32)
    return tuple(jnp.concatenate([tab, ident], axis=0)
                 for tab, ident in ((tab_c, ones), (tab_s1, zeros), (tab_s2, zeros),
                                    (tab_cb, ones), (tab_sb, zeros)))


def _dft_mats(n):
    j = jnp.arange(n, dtype=jnp.int32)
    a = ((j[:, None] * j[None, :]) % n).astype(F32) * (2.0 * math.pi / n)
    return jnp.cos(a), jnp.sin(a)


def _dft4_tables(n):
    nq = n // 4
    c_q, s_q = _dft_mats(nq)
    k = jnp.arange(nq, dtype=jnp.int32)[:, None]
    ang = (k * jnp.arange(1, 4, dtype=jnp.int32)[None, :]).astype(F32) * (2.0 * math.pi / n)
    tw = jnp.stack([jnp.cos(ang), jnp.sin(ang)], axis=-1).reshape(nq, 6)
    return c_q.astype(BF16), s_q.astype(BF16), _pad_to(tw, 1, 8)


def _pad_to(a, axis, size):
    pad = [(0, 0)] * a.ndim
    pad[axis] = (0, size - a.shape[axis])
    return jnp.pad(a, pad)


def _prep_weights(p):
    w_uq = p["w_uq"].reshape(DEPTH, Q_LORA, HEADS, QK_A)
    w_uq = _pad_to(_pad_to(w_uq, 3, QK_A_PAD), 1, Q_LORA_PAD).reshape(DEPTH, Q_LORA_PAD, HEADS * QK_A_PAD)
    w_ukv = p["w_ukv"].reshape(DEPTH, KV_LORA, HEADS, NOPE + V_A)
    w_ukv = jnp.concatenate([w_ukv[..., :NOPE].reshape(DEPTH, KV_LORA, HEADS * NOPE),
                             w_ukv[..., NOPE:].reshape(DEPTH, KV_LORA, HEADS * V_A)], axis=2)
    g_ka = p["g_ka"]

    return {
        "w_in": p["w_in"].astype(BF16),
        "g_cq": _pad_to(p["g_cq"], 1, Q_LORA_PAD)[:, None, :],
        "w_uq": w_uq.astype(BF16),
        "g_ckv": p["g_ckv"][:, None, :],
        "w_ukv": w_ukv.astype(BF16),
        "g_qa": _pad_to(p["g_qa"], 1, QK_A_PAD)[:, None, :],
        "g_ka_nope": g_ka[:, None, :NOPE],
        "g_ka_pe": _pad_to(g_ka[:, NOPE:], 1, 128)[:, None, :],
        "g_qb": p["g_qb"][:, None, :],
        "g_kb": p["g_kb"][:, None, :],
        "conv_w": p["conv_w"],
        "conv_b": p["conv_b"][:, None, :],
        "w_gate": p["w_gate"].astype(BF16),
        "b_gate": p["b_gate"][:, :, None, :],
        "w_branch": p["w_branch"].astype(BF16),
        "w_o": p["w_o"].astype(BF16),
        "norm_ffn1": p["norm_ffn1"][:, None, :],
        "norm_mix": p["norm_mix"][:, None, :],
        "norm_ffn2": p["norm_ffn2"][:, None, :],
        "ffn1_wi": p["ffn1_wi"].astype(BF16), "ffn1_wo": p["ffn1_wo"].astype(BF16),
        "ffn2_wi": p["ffn2_wi"].astype(BF16), "ffn2_wo": p["ffn2_wo"].astype(BF16),
    }


def kernel(x, c, ctx, c_ctx, w_ada, b_ada, norm_ffn1, ffn1_wi, ffn1_wo, norm_mix, w_in, g_cq, w_uq, g_ckv, w_ukv, g_qa, g_ka, g_qb, g_kb, conv_w, conv_b, w_branch, w_gate, b_gate, w_o, norm_ffn2, ffn2_wi, ffn2_wo):
    p = dict(norm_ffn1=norm_ffn1, ffn1_wi=ffn1_wi, ffn1_wo=ffn1_wo, norm_mix=norm_mix, w_in=w_in,
             g_cq=g_cq, w_uq=w_uq, g_ckv=g_ckv, w_ukv=w_ukv, g_qa=g_qa, g_ka=g_ka, g_qb=g_qb,
             g_kb=g_kb, conv_w=conv_w, conv_b=conv_b, w_branch=w_branch, w_gate=w_gate,
             b_gate=b_gate, w_o=w_o, norm_ffn2=norm_ffn2, ffn2_wi=ffn2_wi, ffn2_wo=ffn2_wo)

    cond8 = jnp.concatenate([c, c_ctx[None, :], jnp.zeros((8 - BATCH - 1, D), F32)], axis=0)
    mod = _ada(cond8, w_ada, b_ada).reshape(DEPTH, 8, N_MOD, 1, D)

    tabs = _rope_tables()
    cc, sc = _dft_mats(FGD)
    ccs = jnp.concatenate([cc, sc], axis=1).astype(BF16)
    dft_lat = _dft4_tables(SEQ)
    dft_ctx = _dft4_tables(CTX)

    lw = _prep_weights(p)
    h = jnp.concatenate([x.reshape(T_LAT, D), ctx.reshape(T_CTX, D)], axis=0)
    for l in range(DEPTH):
        last = l == DEPTH - 1
        h = _ffn(h, mod, l, 0, lw["norm_ffn1"], lw["ffn1_wi"], lw["ffn1_wo"], T_ALL)
        qa, ka, va, qb, kb, vb, cbo, u, pc, ps = _proj(h, mod, l, lw, tabs, ccs)
        oa, oa_c = _attention(qa, ka, va, QK_A_PAD, V_A, 1, not last)
        ob, ob_c = _attention(qb, kb, vb, HD_B, HD_B, GROUP_B, not last)
        of, of_c = _fourier(pc, ps, dft_lat, dft_ctx, not last)
        n_rows = T_LAT if last else T_ALL
        h = _merge(h, mod, l, (oa, ob, of), None if last else (oa_c, ob_c, of_c), cbo, u, lw, n_rows)
        h = _ffn(h, mod, l, 6, lw["norm_ffn2"], lw["ffn2_wi"], lw["ffn2_wo"], n_rows)
    return h.reshape(BATCH, SEQ, D)
```
